```python
import math
import jax, jax.numpy as jnp
from jax import lax
import numpy as np

D_MODEL = 1024
BATCH = 2
SEQ = 8192
DEPTH = 1

CONV_DIM = D_MODEL
CONV_WIDTH = 3
HEAD_DIM = 64
ATTN_SLOTS = 8
WINDOWS = (128, 512, 2048)
DILATIONS = (1, 4, 16)
N_GROUPS = len(WINDOWS)
ATTN_HEADS = N_GROUPS * ATTN_SLOTS
ATTN_DIM = ATTN_HEADS * HEAD_DIM
ATTN_OUT = ATTN_SLOTS * HEAD_DIM
Q_BLOCK = 128
EPS = 1e-6
NEG_INF = -1e30

IN_SIZES = (CONV_DIM, CONV_DIM, CONV_DIM, CONV_DIM,
            ATTN_DIM, ATTN_DIM, ATTN_DIM,
            ATTN_OUT,
            D_MODEL, D_MODEL)
IN_TOTAL = sum(IN_SIZES)
IN_SPLITS = tuple(int(v) for v in np.cumsum(IN_SIZES)[:-1])

kernel_name = "hybrid_shortconv_dilated_swa_gated_merge"


def rms_norm(t, w):
    tf = t.astype(jnp.float32)
    tf = tf * lax.rsqrt(jnp.mean(tf * tf, axis=-1, keepdims=True) + EPS)
    return (tf * w.astype(jnp.float32)).astype(t.dtype)


def window_attend(q, k, v, w_sub):
    n, length, hd = q.shape
    blk = math.gcd(Q_BLOCK, length)
    nblk = length // blk
    span = blk + w_sub
    kp = jnp.pad(k, ((0, 0), (w_sub, 0), (0, 0)))
    vp = jnp.pad(v, ((0, 0), (w_sub, 0), (0, 0)))
    idx = jnp.arange(nblk)[:, None] * blk + jnp.arange(span)[None, :]
    kb = kp[:, idx]
    vb = vp[:, idx]
    qb = q.reshape(n, nblk, blk, hd)
    scores = jnp.einsum('nbqd,nbkd->nbqk', qb, kb, preferred_element_type=jnp.float32) * (hd ** -0.5)
    dist = jnp.arange(blk)[:, None] + w_sub - jnp.arange(span)[None, :]
    valid = (dist >= 0) & (dist <= w_sub)
    valid = valid[None, :, :] & (idx >= w_sub)[:, None, :]
    scores = jnp.where(valid, scores, NEG_INF)
    mx = jnp.max(scores, axis=-1)
    p = jnp.exp(scores - mx[..., None])
    s = jnp.sum(p, axis=-1)
    o = jnp.einsum('nbqk,nbkd->nbqd', p, vb.astype(jnp.float32))
    return o.reshape(n, length, hd), mx.reshape(n, length), s.reshape(n, length)


def dilated_attend(q, k, v, window, dilation):
    b, s_len, h, hd = q.shape
    length = s_len // dilation
    w_sub = window // dilation

    def to_sub(t):
        return t.reshape(b, length, dilation, h, hd).transpose(0, 2, 3, 1, 4).reshape(b * dilation * h, length, hd)

    o, m, s = window_attend(to_sub(q), to_sub(k), to_sub(v), w_sub)
    o = o.reshape(b, dilation, h, length, hd).transpose(0, 3, 1, 2, 4).reshape(b, s_len, h, hd)
    m = m.reshape(b, dilation, h, length).transpose(0, 3, 1, 2).reshape(b, s_len, h)
    s = s.reshape(b, dilation, h, length).transpose(0, 3, 1, 2).reshape(b, s_len, h)
    return o, m, s


def setup_inputs(seed: int = 0) -> dict:
    key = jax.random.key(seed)
    ks = jax.random.split(key, 14)
    f32 = jnp.float32
    nrm = lambda k, shape: jax.random.normal(k, shape, f32)
    x = nrm(ks[0], (BATCH, SEQ, D_MODEL))
    c = nrm(ks[1], (BATCH, D_MODEL))
    w_ada = nrm(ks[2], (DEPTH, D_MODEL, 3 * D_MODEL)) * D_MODEL ** -0.5
    b_ada = 0.02 * nrm(ks[3], (DEPTH, 3 * D_MODEL))
    norm_w = 1.0 + 0.1 * nrm(ks[4], (DEPTH, D_MODEL))
    w_in = nrm(ks[5], (DEPTH, D_MODEL, IN_TOTAL)) * D_MODEL ** -0.5
    conv_w = nrm(ks[6], (DEPTH, CONV_WIDTH, CONV_DIM)) * CONV_WIDTH ** -0.5
    q_norm_w = 1.0 + 0.1 * nrm(ks[7], (DEPTH, HEAD_DIM))
    k_norm_w = 1.0 + 0.1 * nrm(ks[8], (DEPTH, HEAD_DIM))
    w_br_conv = nrm(ks[9], (DEPTH, CONV_DIM, D_MODEL)) * CONV_DIM ** -0.5
    w_br_attn = nrm(ks[10], (DEPTH, ATTN_OUT, D_MODEL)) * ATTN_OUT ** -0.5
    w_out = nrm(ks[11], (DEPTH, D_MODEL, D_MODEL)) * D_MODEL ** -0.5
    return {"x": x, "c": c, "w_ada": w_ada, "b_ada": b_ada, "norm_w": norm_w, "w_in": w_in,
            "conv_w": conv_w, "q_norm_w": q_norm_w, "k_norm_w": k_norm_w,
            "w_br_conv": w_br_conv, "w_br_attn": w_br_attn, "w_out": w_out}


def reference(x, c, w_ada, b_ada, norm_w, w_in, conv_w, q_norm_w, k_norm_w, w_br_conv, w_br_attn, w_out):
    b, s_len, _ = x.shape
    for l in range(DEPTH):
        mod = jax.nn.silu(c) @ w_ada[l] + b_ada[l]
        shift, scale, gate = jnp.split(mod[:, None, :], 3, axis=-1)
        h = rms_norm(x, norm_w[l]) * (1 + scale) + shift

        proj = h @ w_in[l]
        b_a, c_a, x_a, z_a, q, k, v, z_b, g_a, g_b = jnp.split(proj, IN_SPLITS, axis=-1)

        u = c_a * x_a
        conv = lax.conv_general_dilated(u, conv_w[l][:, None, :], window_strides=(1,),
                                        padding=[(CONV_WIDTH - 1, 0)],
                                        dimension_numbers=('NWC', 'WIO', 'NWC'),
                                        feature_group_count=CONV_DIM)
        y_a = b_a * conv * jax.nn.silu(z_a)

        q = rms_norm(q.reshape(b, s_len, N_GROUPS, ATTN_SLOTS, HEAD_DIM), q_norm_w[l])
        k = rms_norm(k.reshape(b, s_len, N_GROUPS, ATTN_SLOTS, HEAD_DIM), k_norm_w[l])
        v = v.reshape(b, s_len, N_GROUPS, ATTN_SLOTS, HEAD_DIM)
        outs, maxes, sums = [], [], []
        for g in range(N_GROUPS):
            o_g, m_g, s_g = dilated_attend(q[:, :, g], k[:, :, g], v[:, :, g], WINDOWS[g], DILATIONS[g])
            outs.append(o_g)
            maxes.append(m_g)
            sums.append(s_g)
        o_all = jnp.stack(outs, axis=0)
        m_all = jnp.stack(maxes, axis=0)
        s_all = jnp.stack(sums, axis=0)
        wgt = jnp.exp(m_all - jnp.max(m_all, axis=0, keepdims=True))
        attn = jnp.sum(wgt[..., None] * o_all, axis=0) / jnp.sum(wgt * s_all, axis=0)[..., None]
        y_b = attn.reshape(b, s_len, ATTN_OUT).astype(x.dtype) * jax.nn.silu(z_b)

        merged = jax.nn.sigmoid(g_a) * (y_a @ w_br_conv[l]) + jax.nn.sigmoid(g_b) * (y_b @ w_br_attn[l])
        x = x + gate * (merged @ w_out[l])
    return x
```

```python
import functools

import jax
import jax.numpy as jnp
from jax import lax
from jax.experimental import pallas as pl
from jax.experimental.pallas import tpu as pltpu

F32 = jnp.float32
BF16 = jnp.bfloat16

HEAD_DIM = 64
ATTN_SLOTS = 8
WINDOWS = (128, 512, 2048)
DILATIONS = (1, 4, 16)
N_GROUPS = len(WINDOWS)
GROUP_DIM = ATTN_SLOTS * HEAD_DIM
Q_BLOCK = 128
CONV_WIDTH = 3
EPS = 1e-6
NEG_INF = -1e30
LANES = 128
SUBLANES = 8
VMEM_LIMIT = 56 * 1024 * 1024

ROW_TILE = 512
OUT_ROW_TILE = 256


def _params(n_axes):
    return pltpu.CompilerParams(dimension_semantics=("arbitrary",) * n_axes,
                                vmem_limit_bytes=VMEM_LIMIT)


def _silu(t):
    return t * jax.nn.sigmoid(t)


def _modulated_norm(x, norm_w, mod):
    ms = jnp.mean(x * x, axis=-1, keepdims=True)
    xn = x * lax.rsqrt(ms + EPS) * norm_w
    return xn * (1.0 + mod[1:2, :]) + mod[0:1, :]


def _mod_kernel(c_ref, w_ref, b_ref, o_ref):
    c = c_ref[...]
    o_ref[...] = jnp.dot(_silu(c), w_ref[...], preferred_element_type=F32) + b_ref[...]


def _modulation(c, w_ada, b_ada):
    bsz, d = c.shape
    n = w_ada.shape[1]
    bn = 512
    return pl.pallas_call(
        _mod_kernel,
        grid=(n // bn,),
        in_specs=[pl.BlockSpec((bsz, d), lambda j: (0, 0)),
                  pl.BlockSpec((d, bn), lambda j: (0, j)),
                  pl.BlockSpec((1, bn), lambda j: (0, j))],
        out_specs=pl.BlockSpec((bsz, bn), lambda j: (0, j)),
        out_shape=jax.ShapeDtypeStruct((bsz, n), F32),
        compiler_params=_params(1),
        name="adaln_mod",
    )(c, w_ada, b_ada.reshape(1, n))


def _mixer_a_kernel(x_ref, halo_ref, mod_ref, nw_ref, w_ref, cw_ref, y_ref):
    d = x_ref.shape[-1]
    mod = mod_ref[...]
    nw = nw_ref[...]
    h = _modulated_norm(x_ref[...], nw, mod).astype(BF16)
    proj = [jnp.dot(h, w_ref[:, j * d:(j + 1) * d], preferred_element_type=F32) for j in range(4)]
    b_a, c_a, x_a, z_a = proj
    u = c_a * x_a

    hh = _modulated_norm(halo_ref[...], nw, mod).astype(BF16)
    cx = jnp.dot(hh, w_ref[:, d:3 * d], preferred_element_type=F32)
    u_prev = cx[:, :d] * cx[:, d:]
    u_prev = jnp.where(pl.program_id(1) > 0, u_prev, 0.0)
    prev1 = u_prev[SUBLANES - 1:SUBLANES, :]
    prev2 = u_prev[SUBLANES - 2:SUBLANES - 1, :]

    row = lax.broadcasted_iota(jnp.int32, u.shape, 0)
    u_m1 = jnp.where(row == 0, prev1, pltpu.roll(u, 1, axis=0))
    u_m2 = jnp.where(row == 0, prev2, jnp.where(row == 1, prev1, pltpu.roll(u, 2, axis=0)))
    cw = cw_ref[...]
    conv = cw[0:1, :] * u_m2 + cw[1:2, :] * u_m1 + cw[2:3, :] * u
    y_ref[...] = (b_a * conv * _silu(z_a)).astype(y_ref.dtype)


def _mixer_a(x, mod3, norm_w, w_a, conv_w):
    bsz, s, d = x.shape
    tm = ROW_TILE
    halo_blocks = tm // SUBLANES
    return pl.pallas_call(
        _mixer_a_kernel,
        grid=(bsz, s // tm),
        in_specs=[pl.BlockSpec((None, tm, d), lambda b, i: (b, i, 0)),
                  pl.BlockSpec((None, SUBLANES, d),
                               lambda b, i: (b, jnp.maximum(i * halo_blocks - 1, 0), 0)),
                  pl.BlockSpec((None, 3, d), lambda b, i: (b, 0, 0)),
                  pl.BlockSpec((1, d), lambda b, i: (0, 0)),
                  pl.BlockSpec(w_a.shape, lambda b, i: (0, 0)),
                  pl.BlockSpec(conv_w.shape, lambda b, i: (0, 0))],
        out_specs=pl.BlockSpec((None, tm, d), lambda b, i: (b, i, 0)),
        out_shape=jax.ShapeDtypeStruct((bsz, s, d), BF16),
        compiler_params=_params(2),
        name="mixer_a",
    )(x, x, mod3, norm_w, w_a, conv_w)


def _head_rms(t, ones_blk, w):
    sq = (t * t).astype(BF16)
    width = ones_blk.shape[0]
    parts = [jnp.dot(sq[:, j:j + width], ones_blk, preferred_element_type=F32)
             for j in range(0, t.shape[-1], width)]
    ssq = jnp.concatenate(parts, axis=-1)
    return t * lax.rsqrt(ssq * (1.0 / HEAD_DIM) + EPS) * w


def _attn_proj_kernel(x_ref, mod_ref, nw_ref, w_ref, ones_ref, qw_ref, kw_ref,
                      q_ref, k_ref, v_ref, zs_ref):
    h = _modulated_norm(x_ref[...], nw_ref[...], mod_ref[...]).astype(BF16)
    ones_blk = ones_ref[...]
    ad = N_GROUPS * GROUP_DIM
    for g in range(N_GROUPS):
        lo = g * GROUP_DIM
        q = jnp.dot(h, w_ref[:, lo:lo + GROUP_DIM], preferred_element_type=F32)
        q_ref[g] = _head_rms(q, ones_blk, qw_ref[...]).astype(q_ref.dtype)
        k = jnp.dot(h, w_ref[:, ad + lo:ad + lo + GROUP_DIM], preferred_element_type=F32)
        k_ref[g] = _head_rms(k, ones_blk, kw_ref[...]).astype(k_ref.dtype)
        v = jnp.dot(h, w_ref[:, 2 * ad + lo:2 * ad + lo + GROUP_DIM], preferred_element_type=F32)
        v_ref[g] = v.astype(v_ref.dtype)
    z = jnp.dot(h, w_ref[:, 3 * ad:], preferred_element_type=F32)
    zs_ref[...] = _silu(z)


def _attn_proj(x, mod3, norm_w, w_b, q_norm_w, k_norm_w):
    bsz, s, d = x.shape
    tm = ROW_TILE
    width = 2 * LANES
    lane = jnp.arange(width)
    ones_blk = (lane[:, None] // HEAD_DIM == lane[None, :] // HEAD_DIM).astype(BF16)
    qw = jnp.tile(q_norm_w * (HEAD_DIM ** -0.5), ATTN_SLOTS).reshape(1, GROUP_DIM)
    kw = jnp.tile(k_norm_w, ATTN_SLOTS).reshape(1, GROUP_DIM)
    qkv_shape = jax.ShapeDtypeStruct((N_GROUPS, bsz, s, GROUP_DIM), BF16)
    qkv_spec = pl.BlockSpec((N_GROUPS, None, tm, GROUP_DIM), lambda b, i: (0, b, i, 0))
    return pl.pallas_call(
        _attn_proj_kernel,
        grid=(bsz, s // tm),
        in_specs=[pl.BlockSpec((None, tm, d), lambda b, i: (b, i, 0)),
                  pl.BlockSpec((None, 3, d), lambda b, i: (b, 0, 0)),
                  pl.BlockSpec((1, d), lambda b, i: (0, 0)),
                  pl.BlockSpec(w_b.shape, lambda b, i: (0, 0)),
                  pl.BlockSpec(ones_blk.shape, lambda b, i: (0, 0)),
                  pl.BlockSpec((1, GROUP_DIM), lambda b, i: (0, 0)),
                  pl.BlockSpec((1, GROUP_DIM), lambda b, i: (0, 0))],
        out_specs=[qkv_spec, qkv_spec, qkv_spec,
                   pl.BlockSpec((None, tm, GROUP_DIM), lambda b, i: (b, i, 0))],
        out_shape=[qkv_shape, qkv_shape, qkv_shape,
                   jax.ShapeDtypeStruct((bsz, s, GROUP_DIM), F32)],
        compiler_params=_params(2),
        name="attn_proj",
    )(x, mod3, norm_w, w_b, ones_blk, qw, kw)


def _attn_kernel(q_ref, kp_ref, kc_ref, vp_ref, vc_ref, o_ref, lse_ref):
    blk = q_ref.shape[0]
    span = 2 * blk
    row = lax.broadcasted_iota(jnp.int32, (blk, span), 0)
    col = lax.broadcasted_iota(jnp.int32, (blk, span), 1)
    has_prev = pl.program_id(2) > 0
    valid = (col >= row) & (col <= row + blk) & ((col >= blk) | has_prev)
    lane_lo = lax.broadcasted_iota(jnp.int32, (blk, LANES), 1) < HEAD_DIM

    for p in range(GROUP_DIM // LANES):
        sl = slice(p * LANES, (p + 1) * LANES)
        q2 = q_ref[:, sl]
        k2 = jnp.concatenate([kp_ref[:, sl], kc_ref[:, sl]], axis=0)
        v2 = jnp.concatenate([vp_ref[:, sl], vc_ref[:, sl]], axis=0)
        o_half, lse_half = [], []
        for keep in (lane_lo, ~lane_lo):
            qm = jnp.where(keep, q2, jnp.zeros_like(q2))
            s = lax.dot_general(qm, k2, (((1,), (1,)), ((), ())), preferred_element_type=F32)
            s = jnp.where(valid, s, NEG_INF)
            m = jnp.max(s, axis=-1, keepdims=True)
            e = jnp.exp(s - m)
            l = jnp.sum(e, axis=-1, keepdims=True)
            o = jnp.dot(e.astype(BF16), v2, preferred_element_type=F32)
            o_half.append(o / l)
            lse_half.append(m + jnp.log(l))
        o_ref[:, sl] = jnp.where(lane_lo, o_half[0], o_half[1])
        lse_ref[:, sl] = jnp.where(lane_lo, lse_half[0], lse_half[1])


def _dilated_attention(q_all, k_all, v_all, g):
    _, bsz, s, gd = q_all.shape
    dil = DILATIONS[g]
    assert WINDOWS[g] // dil == Q_BLOCK
    length = s // dil
    view = lambda t: t.reshape(N_GROUPS, bsz, length, dil * gd)
    cur = pl.BlockSpec((None, None, Q_BLOCK, gd), lambda b, r, i: (g, b, i, r))
    prev = pl.BlockSpec((None, None, Q_BLOCK, gd), lambda b, r, i: (g, b, jnp.maximum(i - 1, 0), r))
    out_spec = pl.BlockSpec((None, Q_BLOCK, gd), lambda b, r, i: (b, i, r))
    out_shape = jax.ShapeDtypeStruct((bsz, length, dil * gd), F32)
    o, lse = pl.pallas_call(
        _attn_kernel,
        grid=(bsz, dil, length // Q_BLOCK),
        in_specs=[cur, prev, cur, prev, cur],
        out_specs=[out_spec, out_spec],
        out_shape=[out_shape, out_shape],
        compiler_params=_params(3),
        name=f"dilated_attn_g{g}",
    )(view(q_all), view(k_all), view(k_all), view(v_all), view(v_all))
    return o.reshape(bsz, s, gd), lse.reshape(bsz, s, gd)


def _merge_kernel(x_ref, mod_ref, nw_ref, ya_ref, zs_ref,
                  o0_ref, o1_ref, o2_ref, l0_ref, l1_ref, l2_ref,
                  wg_ref, pa_ref, pb_ref, wo_ref, out_ref):
    d = x_ref.shape[-1]
    x = x_ref[...]
    mod = mod_ref[...]
    h = _modulated_norm(x, nw_ref[...], mod).astype(BF16)
    gates = jnp.dot(h, wg_ref[...], preferred_element_type=F32)
    g_a = jax.nn.sigmoid(gates[:, :d])
    g_b = jax.nn.sigmoid(gates[:, d:])

    lses = (l0_ref[...], l1_ref[...], l2_ref[...])
    outs = (o0_ref[...], o1_ref[...], o2_ref[...])
    top = jnp.maximum(jnp.maximum(lses[0], lses[1]), lses[2])
    wts = [jnp.exp(t - top) for t in lses]
    attn = (wts[0] * outs[0] + wts[1] * outs[1] + wts[2] * outs[2]) / (wts[0] + wts[1] + wts[2])
    y_b = (attn * zs_ref[...]).astype(BF16)

    merged = (g_a * jnp.dot(ya_ref[...], pa_ref[...], preferred_element_type=F32)
              + g_b * jnp.dot(y_b, pb_ref[...], preferred_element_type=F32))
    upd = jnp.dot(merged.astype(BF16), wo_ref[...], preferred_element_type=F32)
    out_ref[...] = x + mod[2:3, :] * upd


def _merge(x, mod3, norm_w, y_a, zs, outs, lses, w_g, p_a, p_b, w_o):
    bsz, s, d = x.shape
    tm = OUT_ROW_TILE
    row_d = pl.BlockSpec((None, tm, d), lambda b, i: (b, i, 0))
    row_g = pl.BlockSpec((None, tm, GROUP_DIM), lambda b, i: (b, i, 0))
    full = lambda a: pl.BlockSpec(a.shape, lambda b, i: (0,) * a.ndim)
    return pl.pallas_call(
        _merge_kernel,
        grid=(bsz, s // tm),
        in_specs=[row_d, pl.BlockSpec((None, 3, d), lambda b, i: (b, 0, 0)),
                  pl.BlockSpec((1, d), lambda b, i: (0, 0)), row_d, row_g,
                  row_g, row_g, row_g, row_g, row_g, row_g,
                  full(w_g), full(p_a), full(p_b), full(w_o)],
        out_specs=row_d,
        out_shape=jax.ShapeDtypeStruct((bsz, s, d), F32),
        compiler_params=_params(2),
        name="merge_out",
    )(x, mod3, norm_w, y_a, zs, *outs, *lses, w_g, p_a, p_b, w_o)


def _layer(x, c, w_ada, b_ada, norm_w, w_in, conv_w, q_norm_w, k_norm_w, w_br_conv, w_br_attn, w_out):
    bsz, s, d = x.shape
    conv_dim = conv_w.shape[-1]
    attn_dim = N_GROUPS * GROUP_DIM
    assert conv_dim == d and conv_w.shape[0] == CONV_WIDTH
    assert w_in.shape[1] == 4 * conv_dim + 3 * attn_dim + GROUP_DIM + 2 * d
    assert s % (max(DILATIONS) * Q_BLOCK) == 0 and s % ROW_TILE == 0

    mod3 = _modulation(c, w_ada, b_ada).reshape(bsz, 3, d)
    nw = norm_w.reshape(1, d)

    o_a = 4 * conv_dim
    o_b = o_a + 3 * attn_dim + GROUP_DIM
    w_a = w_in[:, :o_a].astype(BF16)
    w_b = w_in[:, o_a:o_b].astype(BF16)
    w_g = w_in[:, o_b:].astype(BF16)

    y_a = _mixer_a(x, mod3, nw, w_a, conv_w)
    q_all, k_all, v_all, zs = _attn_proj(x, mod3, nw, w_b, q_norm_w, k_norm_w)
    outs, lses = zip(*[_dilated_attention(q_all, k_all, v_all, g) for g in range(N_GROUPS)])
    return _merge(x, mod3, nw, y_a, zs, outs, lses, w_g,
                  w_br_conv.astype(BF16), w_br_attn.astype(BF16), w_out.astype(BF16))


@jax.jit
def kernel(x, c, w_ada, b_ada, norm_w, w_in, conv_w, q_norm_w, k_norm_w, w_br_conv, w_br_attn, w_out):
    depth = w_ada.shape[0]
    for l in range(depth):
        x = _layer(x, c, w_ada[l], b_ada[l], norm_w[l], w_in[l], conv_w[l], q_norm_w[l],
                   k_norm_w[l], w_br_conv[l], w_br_attn[l], w_out[l])
    return x
```

```python
import jax
import jax.numpy as jnp
from jax import lax
from jax.experimental import pallas as pl
from jax.experimental.pallas import tpu as pltpu

F32 = jnp.float32
BF16 = jnp.bfloat16

HEAD_DIM = 64
ATTN_SLOTS = 8
WINDOWS = (128, 512, 2048)
DILATIONS = (1, 4, 16)
N_GROUPS = len(WINDOWS)
GROUP_DIM = ATTN_SLOTS * HEAD_DIM
Q_BLOCK = 128
CONV_WIDTH = 3
EPS = 1e-6
NEG_INF = -1e30
LANES = 128
SUBLANES = 8
LSE_LANES = LANES // ATTN_SLOTS
VMEM_LIMIT = 56 * 1024 * 1024

ROW_TILE = 512
OUT_ROW_TILE = 256


def _params(n_axes):
    return pltpu.CompilerParams(dimension_semantics=("arbitrary",) * n_axes,
                                vmem_limit_bytes=VMEM_LIMIT)


def _silu(t):
    return t * jax.nn.sigmoid(t)


def _modulated_norm(x, norm_w, mod):
    ms = jnp.mean(x * x, axis=-1, keepdims=True)
    xn = x * lax.rsqrt(ms + EPS) * norm_w
    return xn * (1.0 + mod[1:2, :]) + mod[0:1, :]


def _to_dilated(slab_ref, n_rows, dil):
    per = n_rows // dil
    return jnp.concatenate(
        [jnp.concatenate([slab_ref[j, pl.ds(r, per, stride=dil), :] for r in range(dil)], axis=0)
         for j in range(slab_ref.shape[0])], axis=-1)


def _mod_kernel(c_ref, w_ref, b_ref, o_ref):
    c = c_ref[...]
    o_ref[...] = jnp.dot(_silu(c), w_ref[...], preferred_element_type=F32) + b_ref[...]


def _modulation(c, w_ada, b_ada):
    bsz, d = c.shape
    n = w_ada.shape[1]
    bn = 512
    return pl.pallas_call(
        _mod_kernel,
        grid=(n // bn,),
        in_specs=[pl.BlockSpec((bsz, d), lambda j: (0, 0)),
                  pl.BlockSpec((d, bn), lambda j: (0, j)),
                  pl.BlockSpec((1, bn), lambda j: (0, j))],
        out_specs=pl.BlockSpec((bsz, bn), lambda j: (0, j)),
        out_shape=jax.ShapeDtypeStruct((bsz, n), F32),
        compiler_params=_params(1),
        name="adaln_mod",
    )(c, w_ada, b_ada.reshape(1, n))


def _mixer_a_kernel(x_ref, halo_ref, mod_ref, nw_ref, w_ref, cw_ref, y_ref):
    d = x_ref.shape[-1]
    mod = mod_ref[...]
    nw = nw_ref[...]
    h = _modulated_norm(x_ref[...], nw, mod).astype(BF16)
    proj = [jnp.dot(h, w_ref[:, j * d:(j + 1) * d], preferred_element_type=F32) for j in range(4)]
    b_a, c_a, x_a, z_a = proj
    u = c_a * x_a

    hh = _modulated_norm(halo_ref[...], nw, mod).astype(BF16)
    cx = jnp.dot(hh, w_ref[:, d:3 * d], preferred_element_type=F32)
    u_prev = cx[:, :d] * cx[:, d:]
    u_prev = jnp.where(pl.program_id(1) > 0, u_prev, 0.0)
    prev1 = u_prev[SUBLANES - 1:SUBLANES, :]
    prev2 = u_prev[SUBLANES - 2:SUBLANES - 1, :]

    row = lax.broadcasted_iota(jnp.int32, u.shape, 0)
    u_m1 = jnp.where(row == 0, prev1, pltpu.roll(u, 1, axis=0))
    u_m2 = jnp.where(row == 0, prev2, jnp.where(row == 1, prev1, pltpu.roll(u, 2, axis=0)))
    cw = cw_ref[...]
    conv = cw[0:1, :] * u_m2 + cw[1:2, :] * u_m1 + cw[2:3, :] * u
    y_ref[...] = (b_a * conv * _silu(z_a)).astype(y_ref.dtype)


def _mixer_a(x, mod3, norm_w, w_a, conv_w):
    bsz, s, d = x.shape
    tm = ROW_TILE
    halo_blocks = tm // SUBLANES
    return pl.pallas_call(
        _mixer_a_kernel,
        grid=(bsz, s // tm),
        in_specs=[pl.BlockSpec((None, tm, d), lambda b, i: (b, i, 0)),
                  pl.BlockSpec((None, SUBLANES, d),
                               lambda b, i: (b, jnp.maximum(i * halo_blocks - 1, 0), 0)),
                  pl.BlockSpec((None, 3, d), lambda b, i: (b, 0, 0)),
                  pl.BlockSpec((1, d), lambda b, i: (0, 0)),
                  pl.BlockSpec(w_a.shape, lambda b, i: (0, 0)),
                  pl.BlockSpec(conv_w.shape, lambda b, i: (0, 0))],
        out_specs=pl.BlockSpec((None, tm, d), lambda b, i: (b, i, 0)),
        out_shape=jax.ShapeDtypeStruct((bsz, s, d), BF16),
        compiler_params=_params(2),
        name="mixer_a",
    )(x, x, mod3, norm_w, w_a, conv_w)


def _head_rms(t, ones_blk, w):
    sq = (t * t).astype(BF16)
    width = ones_blk.shape[0]
    parts = [jnp.dot(sq[:, j:j + width], ones_blk, preferred_element_type=F32)
             for j in range(0, t.shape[-1], width)]
    ssq = jnp.concatenate(parts, axis=-1)
    return t * lax.rsqrt(ssq * (1.0 / HEAD_DIM) + EPS) * w


def _attn_proj_kernel(x_ref, mod_ref, nw_ref, w_ref, ones_ref, qw_ref, kw_ref,
                      qkv0_ref, qkv1_ref, qkv2_ref, zs_ref, slab_ref):
    tm, d = x_ref.shape
    h = _modulated_norm(x_ref[...], nw_ref[...], mod_ref[...])
    for j in range(d // LANES):
        slab_ref[j] = h[:, j * LANES:(j + 1) * LANES]
    ones_blk = ones_ref[...]
    gw = 3 * GROUP_DIM
    for g, out_ref in enumerate((qkv0_ref, qkv1_ref, qkv2_ref)):
        hg = h if DILATIONS[g] == 1 else _to_dilated(slab_ref, tm, DILATIONS[g])
        hg = hg.astype(BF16)
        w_g = w_ref.at[:, g * gw:(g + 1) * gw]
        q = jnp.dot(hg, w_g[:, :GROUP_DIM], preferred_element_type=F32)
        k = jnp.dot(hg, w_g[:, GROUP_DIM:2 * GROUP_DIM], preferred_element_type=F32)
        v = jnp.dot(hg, w_g[:, 2 * GROUP_DIM:], preferred_element_type=F32)
        qkv = jnp.concatenate([_head_rms(q, ones_blk, qw_ref[...]),
                               _head_rms(k, ones_blk, kw_ref[...]), v], axis=-1)
        out_ref[...] = qkv.astype(out_ref.dtype).reshape(out_ref.shape)
        if g == 0:
            z = jnp.dot(hg, w_ref[:, N_GROUPS * gw:], preferred_element_type=F32)
            zs_ref[...] = _silu(z)


def _attn_proj(x, mod3, norm_w, w_b, q_norm_w, k_norm_w):
    bsz, s, d = x.shape
    tm = ROW_TILE
    width = 2 * LANES
    lane = jnp.arange(width)
    ones_blk = (lane[:, None] // HEAD_DIM == lane[None, :] // HEAD_DIM).astype(BF16)
    qw = jnp.tile(q_norm_w * (HEAD_DIM ** -0.5), ATTN_SLOTS).reshape(1, GROUP_DIM)
    kw = jnp.tile(k_norm_w, ATTN_SLOTS).reshape(1, GROUP_DIM)
    gw = 3 * GROUP_DIM
    out_shapes, out_specs = [], []
    for dil in DILATIONS:
        tile = Q_BLOCK * dil
        assert dil == 1 or tile % tm == 0
        if tile <= tm:
            nblk = tm // Q_BLOCK
            out_shapes.append(jax.ShapeDtypeStruct((bsz, s // Q_BLOCK, Q_BLOCK, gw), BF16))
            out_specs.append(pl.BlockSpec((None, nblk, Q_BLOCK, gw), lambda b, i: (b, i, 0, 0)))
        else:
            per = tm // dil
            steps = tile // tm
            out_shapes.append(jax.ShapeDtypeStruct((bsz, s // tile, dil, Q_BLOCK, gw), BF16))
            out_specs.append(pl.BlockSpec((None, None, dil, per, gw),
                                          lambda b, i, steps=steps: (b, i // steps, 0, i % steps, 0)))
    out_shapes.append(jax.ShapeDtypeStruct((bsz, s, GROUP_DIM), F32))
    out_specs.append(pl.BlockSpec((None, tm, GROUP_DIM), lambda b, i: (b, i, 0)))
    outs = pl.pallas_call(
        _attn_proj_kernel,
        grid=(bsz, s // tm),
        in_specs=[pl.BlockSpec((None, tm, d), lambda b, i: (b, i, 0)),
                  pl.BlockSpec((None, 3, d), lambda b, i: (b, 0, 0)),
                  pl.BlockSpec((1, d), lambda b, i: (0, 0)),
                  pl.BlockSpec(w_b.shape, lambda b, i: (0, 0)),
                  pl.BlockSpec(ones_blk.shape, lambda b, i: (0, 0)),
                  pl.BlockSpec((1, GROUP_DIM), lambda b, i: (0, 0)),
                  pl.BlockSpec((1, GROUP_DIM), lambda b, i: (0, 0))],
        out_specs=out_specs,
        out_shape=out_shapes,
        scratch_shapes=[pltpu.VMEM((d // LANES, tm, LANES), F32)],
        compiler_params=_params(2),
        name="attn_proj",
    )(x, mod3, norm_w, w_b, ones_blk, qw, kw)
    qkv = [t.reshape(bsz, s // Q_BLOCK, Q_BLOCK, gw) for t in outs[:N_GROUPS]]
    return qkv, outs[N_GROUPS]


def _attn_kernel(q_ref, kp_ref, kc_ref, vp_ref, vc_ref, o_ref, lse_ref, *, dil):
    blk = q_ref.shape[0]
    span = 2 * blk
    row = lax.broadcasted_iota(jnp.int32, (blk, span), 0)
    col = lax.broadcasted_iota(jnp.int32, (blk, span), 1)
    has_prev = pl.program_id(1) >= dil
    valid = (col >= row) & (col <= row + blk) & ((col >= blk) | has_prev)
    lane = lax.broadcasted_iota(jnp.int32, (blk, LANES), 1)
    lane_lo = lane < HEAD_DIM

    lse_tile = jnp.zeros((blk, LANES), F32)
    for p in range(GROUP_DIM // LANES):
        sl = slice(p * LANES, (p + 1) * LANES)
        q2 = q_ref[:, sl]
        k2 = jnp.concatenate([kp_ref[:, sl], kc_ref[:, sl]], axis=0)
        v2 = jnp.concatenate([vp_ref[:, sl], vc_ref[:, sl]], axis=0)
        o_half = []
        for half, keep in enumerate((lane_lo, ~lane_lo)):
            qm = jnp.where(keep, q2, jnp.zeros_like(q2))
            s = lax.dot_general(qm, k2, (((1,), (1,)), ((), ())), preferred_element_type=F32)
            s = jnp.where(valid, s, NEG_INF)
            m = jnp.max(s, axis=-1, keepdims=True)
            e = jnp.exp(s - m)
            l = jnp.sum(e, axis=-1, keepdims=True)
            o = jnp.dot(e.astype(BF16), v2, preferred_element_type=F32)
            o_half.append(o / l)
            head = 2 * p + half
            lse_tile = jnp.where(lane // LSE_LANES == head, m + jnp.log(l), lse_tile)
        o_ref[:, sl] = jnp.where(lane_lo, o_half[0], o_half[1])
    lse_ref[...] = lse_tile


def _dilated_attention(qkv, dil):
    bsz, nb, blk, _ = qkv.shape
    gd = GROUP_DIM
    cur = lambda part: pl.BlockSpec((None, None, blk, gd), lambda b, n: (b, n, 0, part))
    prev = lambda part: pl.BlockSpec((None, None, blk, gd),
                                     lambda b, n: (b, jnp.maximum(n - dil, 0), 0, part))
    return pl.pallas_call(
        lambda *refs: _attn_kernel(*refs, dil=dil),
        grid=(bsz, nb),
        in_specs=[cur(0), prev(1), cur(1), prev(2), cur(2)],
        out_specs=[pl.BlockSpec((None, None, blk, gd), lambda b, n: (b, n, 0, 0)),
                   pl.BlockSpec((None, None, blk, LANES), lambda b, n: (b, n, 0, 0))],
        out_shape=[jax.ShapeDtypeStruct((bsz, nb, blk, gd), F32),
                   jax.ShapeDtypeStruct((bsz, nb, blk, LANES), F32)],
        compiler_params=_params(2),
        name=f"dilated_attn_d{dil}",
    )(qkv, qkv, qkv, qkv, qkv)


def _to_sequence(src_ref, slab_ref, dil):
    per = src_ref.shape[1]
    n_slabs = src_ref.shape[2] // LANES
    for j in range(n_slabs):
        for r in range(dil):
            slab_ref[j, pl.ds(r, per, stride=dil), :] = src_ref[r, :, j * LANES:(j + 1) * LANES]
    return jnp.concatenate([slab_ref[j] for j in range(n_slabs)], axis=-1)


def _merge_kernel(x_ref, mod_ref, nw_ref, ya_ref, zs_ref,
                  o0_ref, o1_ref, o2_ref, l0_ref, l1_ref, l2_ref,
                  expand_ref, wg_ref, pa_ref, pb_ref, wo_ref, out_ref, slab_ref):
    d = x_ref.shape[-1]
    x = x_ref[...]
    mod = mod_ref[...]
    h = _modulated_norm(x, nw_ref[...], mod).astype(BF16)
    gates = jnp.dot(h, wg_ref[...], preferred_element_type=F32)
    g_a = jax.nn.sigmoid(gates[:, :d])
    g_b = jax.nn.sigmoid(gates[:, d:])

    outs = [o0_ref[...]] + [_to_sequence(r, slab_ref, dil)
                            for r, dil in zip((o1_ref, o2_ref), DILATIONS[1:])]
    lses = [l0_ref[...]] + [_to_sequence(r, slab_ref, dil)
                            for r, dil in zip((l1_ref, l2_ref), DILATIONS[1:])]
    top = jnp.maximum(jnp.maximum(lses[0], lses[1]), lses[2])
    wts = [jnp.exp(t - top) for t in lses]
    den = wts[0] + wts[1] + wts[2]
    attn = jnp.zeros_like(outs[0])
    for w, o in zip(wts, outs):
        wn = w / den
        hi = wn.astype(BF16)
        lo = (wn - hi.astype(F32)).astype(BF16)
        wide = (jnp.dot(hi, expand_ref[...], preferred_element_type=F32)
                + jnp.dot(lo, expand_ref[...], preferred_element_type=F32))
        attn = attn + wide * o
    y_b = (attn * zs_ref[...]).astype(BF16)

    merged = (g_a * jnp.dot(ya_ref[...], pa_ref[...], preferred_element_type=F32)
              + g_b * jnp.dot(y_b, pb_ref[...], preferred_element_type=F32))
    upd = jnp.dot(merged.astype(BF16), wo_ref[...], preferred_element_type=F32)
    out_ref[...] = x + mod[2:3, :] * upd


def _merge(x, mod3, norm_w, y_a, zs, outs, lses, w_g, p_a, p_b, w_o):
    bsz, s, d = x.shape
    tm = OUT_ROW_TILE
    row_d = pl.BlockSpec((None, tm, d), lambda b, i: (b, i, 0))
    row_g = pl.BlockSpec((None, tm, GROUP_DIM), lambda b, i: (b, i, 0))
    full = lambda a: pl.BlockSpec(a.shape, lambda b, i: (0,) * a.ndim)

    def dilated(t, dil):
        width = t.shape[-1]
        tile = Q_BLOCK * dil
        per = tm // dil
        if tile <= tm:
            raise NotImplementedError("merge tile must not exceed a dilated-order tile")
        steps = tile // tm
        view = t.reshape(bsz, s // tile, dil, Q_BLOCK, width)
        spec = pl.BlockSpec((None, None, dil, per, width),
                            lambda b, i: (b, i // steps, 0, i % steps, 0))
        return view, spec

    expand = (jnp.arange(LANES)[:, None] == (jnp.arange(GROUP_DIM)[None, :] // HEAD_DIM) * LSE_LANES)
    expand = expand.astype(BF16)

    o_views, o_specs, l_views, l_specs = [], [], [], []
    for t, lse, dil in zip(outs, lses, DILATIONS):
        if dil == 1:
            o_views.append(t.reshape(bsz, s, GROUP_DIM)); o_specs.append(row_g)
            l_views.append(lse.reshape(bsz, s, LANES))
            l_specs.append(pl.BlockSpec((None, tm, LANES), lambda b, i: (b, i, 0)))
        else:
            v, sp = dilated(t, dil); o_views.append(v); o_specs.append(sp)
            v, sp = dilated(lse, dil); l_views.append(v); l_specs.append(sp)

    return pl.pallas_call(
        _merge_kernel,
        grid=(bsz, s // tm),
        in_specs=[row_d, pl.BlockSpec((None, 3, d), lambda b, i: (b, 0, 0)),
                  pl.BlockSpec((1, d), lambda b, i: (0, 0)), row_d, row_g,
                  *o_specs, *l_specs,
                  full(expand), full(w_g), full(p_a), full(p_b), full(w_o)],
        out_specs=row_d,
        out_shape=jax.ShapeDtypeStruct((bsz, s, d), F32),
        scratch_shapes=[pltpu.VMEM((GROUP_DIM // LANES, tm, LANES), F32)],
        compiler_params=_params(2),
        name="merge_out",
    )(x, mod3, norm_w, y_a, zs, *o_views, *l_views, expand, w_g, p_a, p_b, w_o)


def _layer(x, c, w_ada, b_ada, norm_w, w_in, conv_w, q_norm_w, k_norm_w, w_br_conv, w_br_attn, w_out):
    bsz, s, d = x.shape
    conv_dim = conv_w.shape[-1]
    attn_dim = N_GROUPS * GROUP_DIM
    assert conv_dim == d and conv_w.shape[0] == CONV_WIDTH and d % LANES == 0
    assert w_in.shape[1] == 4 * conv_dim + 3 * attn_dim + GROUP_DIM + 2 * d
    assert s % (max(DILATIONS) * Q_BLOCK) == 0 and s % ROW_TILE == 0
    assert all(w // dil == Q_BLOCK for w, dil in zip(WINDOWS, DILATIONS))

    mod3 = _modulation(c, w_ada, b_ada).reshape(bsz, 3, d)
    nw = norm_w.reshape(1, d)

    o_a = 4 * conv_dim
    o_z = o_a + 3 * attn_dim
    o_g = o_z + GROUP_DIM
    w_a = w_in[:, :o_a].astype(BF16)
    cols = [w_in[:, o_a + part * attn_dim + g * GROUP_DIM:o_a + part * attn_dim + (g + 1) * GROUP_DIM]
            for g in range(N_GROUPS) for part in range(3)]
    w_b = jnp.concatenate(cols + [w_in[:, o_z:o_g]], axis=1).astype(BF16)
    w_g = w_in[:, o_g:].astype(BF16)

    y_a = _mixer_a(x, mod3, nw, w_a, conv_w)
    qkv, zs = _attn_proj(x, mod3, nw, w_b, q_norm_w, k_norm_w)
    outs, lses = zip(*[_dilated_attention(t, dil) for t, dil in zip(qkv, DILATIONS)])
    return _merge(x, mod3, nw, y_a, zs, outs, lses, w_g,
                  w_br_conv.astype(BF16), w_br_attn.astype(BF16), w_out.astype(BF16))


@jax.jit
def kernel(x, c, w_ada, b_ada, norm_w, w_in, conv_w, q_norm_w, k_norm_w, w_br_conv, w_br_attn, w_out):
    depth = w_ada.shape[0]
    for l in range(depth):
        x = _layer(x, c, w_ada[l], b_ada[l], norm_w[l], w_in[l], conv_w[l], q_norm_w[l],
                   k_norm_w[l], w_br_conv[l], w_br_attn[l], w_out[l])
    return x
```

```python
import jax
import jax.numpy as jnp
from jax import lax
from jax.experimental import pallas as pl
from jax.experimental.pallas import tpu as pltpu

F32 = jnp.float32
BF16 = jnp.bfloat16

HEAD_DIM = 64
ATTN_SLOTS = 8
WINDOWS = (128, 512, 2048)
DILATIONS = (1, 4, 16)
N_GROUPS = len(WINDOWS)
GROUP_DIM = ATTN_SLOTS * HEAD_DIM
Q_BLOCK = 128
CONV_WIDTH = 3
EPS = 1e-6
NEG_INF = -1e30
LOG2_E = 1.4426950408889634
LANES = 128
SUBLANES = 8
LSE_LANES = LANES // ATTN_SLOTS
VMEM_LIMIT = 56 * 1024 * 1024

ROW_TILE = 512
OUT_ROW_TILE = 256
ATTN_RUN = 8


def _params(n_axes):
    return pltpu.CompilerParams(dimension_semantics=("arbitrary",) * n_axes,
                                vmem_limit_bytes=VMEM_LIMIT)


def _silu(t):
    return t * jax.nn.sigmoid(t)


def _modulated_norm(x, norm_w, mod):
    ms = jnp.mean(x * x, axis=-1, keepdims=True)
    xn = x * lax.rsqrt(ms + EPS) * norm_w
    return xn * (1.0 + mod[1:2, :]) + mod[0:1, :]


def _to_dilated(slab_ref, n_rows, dil):
    per = n_rows // dil
    return jnp.concatenate(
        [jnp.concatenate([slab_ref[j, pl.ds(r, per, stride=dil), :] for r in range(dil)], axis=0)
         for j in range(slab_ref.shape[0])], axis=-1)


def _mod_kernel(c_ref, w_ref, b_ref, o_ref):
    c = c_ref[...]
    o_ref[...] = jnp.dot(_silu(c), w_ref[...], preferred_element_type=F32) + b_ref[...]


def _modulation(c, w_ada, b_ada):
    bsz, d = c.shape
    n = w_ada.shape[1]
    bn = 512
    return pl.pallas_call(
        _mod_kernel,
        grid=(n // bn,),
        in_specs=[pl.BlockSpec((bsz, d), lambda j: (0, 0)),
                  pl.BlockSpec((d, bn), lambda j: (0, j)),
                  pl.BlockSpec((1, bn), lambda j: (0, j))],
        out_specs=pl.BlockSpec((bsz, bn), lambda j: (0, j)),
        out_shape=jax.ShapeDtypeStruct((bsz, n), F32),
        compiler_params=_params(1),
        name="adaln_mod",
    )(c, w_ada, b_ada.reshape(1, n))


def _mixer_a_kernel(x_ref, halo_ref, mod_ref, nw_ref, w_ref, cw_ref, y_ref):
    d = x_ref.shape[-1]
    mod = mod_ref[...]
    nw = nw_ref[...]
    h = _modulated_norm(x_ref[...], nw, mod).astype(BF16)
    proj = [jnp.dot(h, w_ref[:, j * d:(j + 1) * d], preferred_element_type=F32) for j in range(4)]
    b_a, c_a, x_a, z_a = proj
    u = c_a * x_a

    hh = _modulated_norm(halo_ref[...], nw, mod).astype(BF16)
    cx = jnp.dot(hh, w_ref[:, d:3 * d], preferred_element_type=F32)
    u_prev = cx[:, :d] * cx[:, d:]
    u_prev = jnp.where(pl.program_id(1) > 0, u_prev, 0.0)
    prev1 = u_prev[SUBLANES - 1:SUBLANES, :]
    prev2 = u_prev[SUBLANES - 2:SUBLANES - 1, :]

    row = lax.broadcasted_iota(jnp.int32, u.shape, 0)
    u_m1 = jnp.where(row == 0, prev1, pltpu.roll(u, 1, axis=0))
    u_m2 = jnp.where(row == 0, prev2, jnp.where(row == 1, prev1, pltpu.roll(u, 2, axis=0)))
    cw = cw_ref[...]
    conv = cw[0:1, :] * u_m2 + cw[1:2, :] * u_m1 + cw[2:3, :] * u
    y_ref[...] = (b_a * conv * _silu(z_a)).astype(y_ref.dtype)


def _mixer_a(x, mod3, norm_w, w_a, conv_w):
    bsz, s, d = x.shape
    tm = ROW_TILE
    halo_blocks = tm // SUBLANES
    return pl.pallas_call(
        _mixer_a_kernel,
        grid=(bsz, s // tm),
        in_specs=[pl.BlockSpec((None, tm, d), lambda b, i: (b, i, 0)),
                  pl.BlockSpec((None, SUBLANES, d),
                               lambda b, i: (b, jnp.maximum(i * halo_blocks - 1, 0), 0)),
                  pl.BlockSpec((None, 3, d), lambda b, i: (b, 0, 0)),
                  pl.BlockSpec((1, d), lambda b, i: (0, 0)),
                  pl.BlockSpec(w_a.shape, lambda b, i: (0, 0)),
                  pl.BlockSpec(conv_w.shape, lambda b, i: (0, 0))],
        out_specs=pl.BlockSpec((None, tm, d), lambda b, i: (b, i, 0)),
        out_shape=jax.ShapeDtypeStruct((bsz, s, d), BF16),
        compiler_params=_params(2),
        name="mixer_a",
    )(x, x, mod3, norm_w, w_a, conv_w)


def _head_rms(t, ones_blk, w):
    sq = (t * t).astype(BF16)
    width = ones_blk.shape[0]
    parts = [jnp.dot(sq[:, j:j + width], ones_blk, preferred_element_type=F32)
             for j in range(0, t.shape[-1], width)]
    ssq = jnp.concatenate(parts, axis=-1)
    return t * lax.rsqrt(ssq * (1.0 / HEAD_DIM) + EPS) * w


def _attn_proj_kernel(x_ref, mod_ref, nw_ref, w_ref, ones_ref, qw_ref, kw_ref,
                      qkv0_ref, qkv1_ref, qkv2_ref, zs_ref, slab_ref):
    tm, d = x_ref.shape
    h = _modulated_norm(x_ref[...], nw_ref[...], mod_ref[...])
    for j in range(d // LANES):
        slab_ref[j] = h[:, j * LANES:(j + 1) * LANES]
    ones_blk = ones_ref[...]
    gw = 3 * GROUP_DIM
    for g, out_ref in enumerate((qkv0_ref, qkv1_ref, qkv2_ref)):
        hg = h if DILATIONS[g] == 1 else _to_dilated(slab_ref, tm, DILATIONS[g])
        hg = hg.astype(BF16)
        w_g = w_ref.at[:, g * gw:(g + 1) * gw]
        q = jnp.dot(hg, w_g[:, :GROUP_DIM], preferred_element_type=F32)
        k = jnp.dot(hg, w_g[:, GROUP_DIM:2 * GROUP_DIM], preferred_element_type=F32)
        v = jnp.dot(hg, w_g[:, 2 * GROUP_DIM:], preferred_element_type=F32)
        qkv = jnp.concatenate([_head_rms(q, ones_blk, qw_ref[...]),
                               _head_rms(k, ones_blk, kw_ref[...]), v], axis=-1)
        out_ref[...] = qkv.astype(out_ref.dtype).reshape(out_ref.shape)
        if g == 0:
            z = jnp.dot(hg, w_ref[:, N_GROUPS * gw:], preferred_element_type=F32)
            zs_ref[...] = _silu(z)


def _attn_proj(x, mod3, norm_w, w_b, q_norm_w, k_norm_w):
    bsz, s, d = x.shape
    tm = ROW_TILE
    width = 2 * LANES
    lane = jnp.arange(width)
    ones_blk = (lane[:, None] // HEAD_DIM == lane[None, :] // HEAD_DIM).astype(BF16)
    qw = jnp.tile(q_norm_w * (HEAD_DIM ** -0.5 * LOG2_E), ATTN_SLOTS).reshape(1, GROUP_DIM)
    kw = jnp.tile(k_norm_w, ATTN_SLOTS).reshape(1, GROUP_DIM)
    gw = 3 * GROUP_DIM
    out_shapes, out_specs = [], []
    for dil in DILATIONS:
        tile = Q_BLOCK * dil
        assert dil == 1 or tile % tm == 0
        if tile <= tm:
            nblk = tm // Q_BLOCK
            out_shapes.append(jax.ShapeDtypeStruct((bsz, s // Q_BLOCK, Q_BLOCK, gw), BF16))
            out_specs.append(pl.BlockSpec((None, nblk, Q_BLOCK, gw), lambda b, i: (b, i, 0, 0)))
        else:
            per = tm // dil
            steps = tile // tm
            out_shapes.append(jax.ShapeDtypeStruct((bsz, s // tile, dil, Q_BLOCK, gw), BF16))
            out_specs.append(pl.BlockSpec((None, None, dil, per, gw),
                                          lambda b, i, steps=steps: (b, i // steps, 0, i % steps, 0)))
    out_shapes.append(jax.ShapeDtypeStruct((bsz, s, GROUP_DIM), F32))
    out_specs.append(pl.BlockSpec((None, tm, GROUP_DIM), lambda b, i: (b, i, 0)))
    outs = pl.pallas_call(
        _attn_proj_kernel,
        grid=(bsz, s // tm),
        in_specs=[pl.BlockSpec((None, tm, d), lambda b, i: (b, i, 0)),
                  pl.BlockSpec((None, 3, d), lambda b, i: (b, 0, 0)),
                  pl.BlockSpec((1, d), lambda b, i: (0, 0)),
                  pl.BlockSpec(w_b.shape, lambda b, i: (0, 0)),
                  pl.BlockSpec(ones_blk.shape, lambda b, i: (0, 0)),
                  pl.BlockSpec((1, GROUP_DIM), lambda b, i: (0, 0)),
                  pl.BlockSpec((1, GROUP_DIM), lambda b, i: (0, 0))],
        out_specs=out_specs,
        out_shape=out_shapes,
        scratch_shapes=[pltpu.VMEM((d // LANES, tm, LANES), F32)],
        compiler_params=_params(2),
        name="attn_proj",
    )(x, mod3, norm_w, w_b, ones_blk, qw, kw)
    qkv = [t.reshape(bsz, s // Q_BLOCK, Q_BLOCK, gw) for t in outs[:N_GROUPS]]
    return qkv, outs[N_GROUPS]


def _attn_block(q, kp, kc, vp, vc, bias, o_ref, lse_ref, j):
    blk = q.shape[0]
    lane = lax.broadcasted_iota(jnp.int32, (blk, LANES), 1)
    lane_lo = lane < HEAD_DIM
    zero = jnp.zeros((blk, LANES), q.dtype)
    lse_tile = jnp.zeros((blk, LANES), F32)
    for p in range(GROUP_DIM // LANES):
        sl = slice(p * LANES, (p + 1) * LANES)
        q2 = q[:, sl]
        qs = jnp.concatenate([jnp.where(lane_lo, q2, zero), jnp.where(lane_lo, zero, q2)], axis=0)
        k2 = jnp.concatenate([kp[:, sl], kc[:, sl]], axis=0)
        v2 = jnp.concatenate([vp[:, sl], vc[:, sl]], axis=0)
        s = lax.dot_general(qs, k2, (((1,), (1,)), ((), ())), preferred_element_type=F32) + bias
        m = jnp.max(s, axis=-1, keepdims=True)
        e = jnp.exp2(s - m)
        l = jnp.sum(e, axis=-1, keepdims=True)
        o = jnp.dot(e.astype(BF16), v2, preferred_element_type=F32) * (1.0 / l)
        o_ref[j, :, sl] = jnp.where(lane_lo, o[:blk], o[blk:])
        lse = m + jnp.log2(l)
        lse_tile = jnp.where(lane // LSE_LANES == 2 * p, lse[:blk], lse_tile)
        lse_tile = jnp.where(lane // LSE_LANES == 2 * p + 1, lse[blk:], lse_tile)
    lse_ref[j] = lse_tile


def _attn_kernel(q_ref, kprev_ref, k_ref, vprev_ref, v_ref, o_ref, lse_ref):
    n_blocks, blk, _ = q_ref.shape
    span = 2 * blk
    row = lax.broadcasted_iota(jnp.int32, (span, span), 0) % blk
    col = lax.broadcasted_iota(jnp.int32, (span, span), 1)
    in_window = (col >= row) & (col <= row + blk)
    bias = jnp.where(in_window, 0.0, NEG_INF)
    run_has_prev = pl.program_id(2) > 0
    bias_first = jnp.where(in_window & ((col >= blk) | run_has_prev), 0.0, NEG_INF)

    _attn_block(q_ref[0], kprev_ref[...], k_ref[0], vprev_ref[...], v_ref[0], bias_first,
                o_ref, lse_ref, 0)

    def body(j, carry):
        _attn_block(q_ref[j], k_ref[j - 1], k_ref[j], v_ref[j - 1], v_ref[j], bias, o_ref, lse_ref, j)
        return carry

    lax.fori_loop(1, n_blocks, body, 0)


def _dilated_attention(qkv, dil):
    bsz, nb, blk, width = qkv.shape
    gd = GROUP_DIM
    tiles = nb // dil
    run = min(ATTN_RUN, tiles)
    assert tiles % run == 0
    view = qkv.reshape(bsz, tiles, dil, blk, width)
    cur = lambda part: pl.BlockSpec((None, run, None, blk, gd), lambda b, r, c: (b, c, r, 0, part))
    prev = lambda part: pl.BlockSpec((None, None, None, blk, gd),
                                     lambda b, r, c: (b, jnp.maximum(c * run - 1, 0), r, 0, part))
    o, lse = pl.pallas_call(
        _attn_kernel,
        grid=(bsz, dil, tiles // run),
        in_specs=[cur(0), prev(1), cur(1), prev(2), cur(2)],
        out_specs=[pl.BlockSpec((None, run, None, blk, gd), lambda b, r, c: (b, c, r, 0, 0)),
                   pl.BlockSpec((None, run, None, blk, LANES), lambda b, r, c: (b, c, r, 0, 0))],
        out_shape=[jax.ShapeDtypeStruct((bsz, tiles, dil, blk, gd), F32),
                   jax.ShapeDtypeStruct((bsz, tiles, dil, blk, LANES), F32)],
        compiler_params=_params(3),
        name=f"dilated_attn_d{dil}",
    )(view, view, view, view, view)
    return o.reshape(bsz, nb, blk, gd), lse.reshape(bsz, nb, blk, LANES)


def _to_sequence(src_ref, slab_ref, dil):
    per = src_ref.shape[1]
    n_slabs = src_ref.shape[2] // LANES
    for j in range(n_slabs):
        for r in range(dil):
            slab_ref[j, pl.ds(r, per, stride=dil), :] = src_ref[r, :, j * LANES:(j + 1) * LANES]
    return jnp.concatenate([slab_ref[j] for j in range(n_slabs)], axis=-1)


def _merge_kernel(x_ref, mod_ref, nw_ref, ya_ref, zs_ref,
                  o0_ref, o1_ref, o2_ref, l0_ref, l1_ref, l2_ref,
                  expand_ref, wg_ref, pa_ref, pb_ref, wo_ref, out_ref, slab_ref):
    d = x_ref.shape[-1]
    x = x_ref[...]
    mod = mod_ref[...]
    h = _modulated_norm(x, nw_ref[...], mod).astype(BF16)
    gates = jnp.dot(h, wg_ref[...], preferred_element_type=F32)
    g_a = jax.nn.sigmoid(gates[:, :d])
    g_b = jax.nn.sigmoid(gates[:, d:])

    outs = [o0_ref[...]] + [_to_sequence(r, slab_ref, dil)
                            for r, dil in zip((o1_ref, o2_ref), DILATIONS[1:])]
    lses = [l0_ref[...]] + [_to_sequence(r, slab_ref, dil)
                            for r, dil in zip((l1_ref, l2_ref), DILATIONS[1:])]
    top = jnp.maximum(jnp.maximum(lses[0], lses[1]), lses[2])
    wts = [jnp.exp2(t - top) for t in lses]
    den = wts[0] + wts[1] + wts[2]
    attn = jnp.zeros_like(outs[0])
    for w, o in zip(wts, outs):
        wn = w / den
        hi = wn.astype(BF16)
        lo = (wn - hi.astype(F32)).astype(BF16)
        wide = (jnp.dot(hi, expand_ref[...], preferred_element_type=F32)
                + jnp.dot(lo, expand_ref[...], preferred_element_type=F32))
        attn = attn + wide * o
    y_b = (attn * zs_ref[...]).astype(BF16)

    merged = (g_a * jnp.dot(ya_ref[...], pa_ref[...], preferred_element_type=F32)
              + g_b * jnp.dot(y_b, pb_ref[...], preferred_element_type=F32))
    upd = jnp.dot(merged.astype(BF16), wo_ref[...], preferred_element_type=F32)
    out_ref[...] = x + mod[2:3, :] * upd


def _merge(x, mod3, norm_w, y_a, zs, outs, lses, w_g, p_a, p_b, w_o):
    bsz, s, d = x.shape
    tm = OUT_ROW_TILE
    row_d = pl.BlockSpec((None, tm, d), lambda b, i: (b, i, 0))
    row_g = pl.BlockSpec((None, tm, GROUP_DIM), lambda b, i: (b, i, 0))
    full = lambda a: pl.BlockSpec(a.shape, lambda b, i: (0,) * a.ndim)

    def dilated(t, dil):
        width = t.shape[-1]
        tile = Q_BLOCK * dil
        per = tm // dil
        if tile <= tm:
            raise NotImplementedError("merge tile must not exceed a dilated-order tile")
        steps = tile // tm
        view = t.reshape(bsz, s // tile, dil, Q_BLOCK, width)
        spec = pl.BlockSpec((None, None, dil, per, width),
                            lambda b, i: (b, i // steps, 0, i % steps, 0))
        return view, spec

    expand = (jnp.arange(LANES)[:, None] == (jnp.arange(GROUP_DIM)[None, :] // HEAD_DIM) * LSE_LANES)
    expand = expand.astype(BF16)

    o_views, o_specs, l_views, l_specs = [], [], [], []
    for t, lse, dil in zip(outs, lses, DILATIONS):
        if dil == 1:
            o_views.append(t.reshape(bsz, s, GROUP_DIM)); o_specs.append(row_g)
            l_views.append(lse.reshape(bsz, s, LANES))
            l_specs.append(pl.BlockSpec((None, tm, LANES), lambda b, i: (b, i, 0)))
        else:
            v, sp = dilated(t, dil); o_views.append(v); o_specs.append(sp)
            v, sp = dilated(lse, dil); l_views.append(v); l_specs.append(sp)

    return pl.pallas_call(
        _merge_kernel,
        grid=(bsz, s // tm),
        in_specs=[row_d, pl.BlockSpec((None, 3, d), lambda b, i: (b, 0, 0)),
                  pl.BlockSpec((1, d), lambda b, i: (0, 0)), row_d, row_g,
                  *o_specs, *l_specs,
                  full(expand), full(w_g), full(p_a), full(p_b), full(w_o)],
        out_specs=row_d,
        out_shape=jax.ShapeDtypeStruct((bsz, s, d), F32),
        scratch_shapes=[pltpu.VMEM((GROUP_DIM // LANES, tm, LANES), F32)],
        compiler_params=_params(2),
        name="merge_out",
    )(x, mod3, norm_w, y_a, zs, *o_views, *l_views, expand, w_g, p_a, p_b, w_o)


def _layer(x, c, w_ada, b_ada, norm_w, w_in, conv_w, q_norm_w, k_norm_w, w_br_conv, w_br_attn, w_out):
    bsz, s, d = x.shape
    conv_dim = conv_w.shape[-1]
    attn_dim = N_GROUPS * GROUP_DIM
    assert conv_dim == d and conv_w.shape[0] == CONV_WIDTH and d % LANES == 0
    assert w_in.shape[1] == 4 * conv_dim + 3 * attn_dim + GROUP_DIM + 2 * d
    assert s % (max(DILATIONS) * Q_BLOCK) == 0 and s % ROW_TILE == 0
    assert all(w // dil == Q_BLOCK for w, dil in zip(WINDOWS, DILATIONS))

    mod3 = _modulation(c, w_ada, b_ada).reshape(bsz, 3, d)
    nw = norm_w.reshape(1, d)

    o_a = 4 * conv_dim
    o_z = o_a + 3 * attn_dim
    o_g = o_z + GROUP_DIM
    w_a = w_in[:, :o_a].astype(BF16)
    cols = [w_in[:, o_a + part * attn_dim + g * GROUP_DIM:o_a + part * attn_dim + (g + 1) * GROUP_DIM]
            for g in range(N_GROUPS) for part in range(3)]
    w_b = jnp.concatenate(cols + [w_in[:, o_z:o_g]], axis=1).astype(BF16)
    w_g = w_in[:, o_g:].astype(BF16)

    y_a = _mixer_a(x, mod3, nw, w_a, conv_w)
    qkv, zs = _attn_proj(x, mod3, nw, w_b, q_norm_w, k_norm_w)
    outs, lses = zip(*[_dilated_attention(t, dil) for t, dil in zip(qkv, DILATIONS)])
    return _merge(x, mod3, nw, y_a, zs, outs, lses, w_g,
                  w_br_conv.astype(BF16), w_br_attn.astype(BF16), w_out.astype(BF16))


@jax.jit
def kernel(x, c, w_ada, b_ada, norm_w, w_in, conv_w, q_norm_w, k_norm_w, w_br_conv, w_br_attn, w_out):
    depth = w_ada.shape[0]
    for l in range(depth):
        x = _layer(x, c, w_ada[l], b_ada[l], norm_w[l], w_in[l], conv_w[l], q_norm_w[l],
                   k_norm_w[l], w_br_conv[l], w_br_attn[l], w_out[l])
    return x
```

```python
import jax
import jax.numpy as jnp
from jax import lax
from jax.experimental import pallas as pl
from jax.experimental.pallas import tpu as pltpu

F32 = jnp.float32
BF16 = jnp.bfloat16

HEAD_DIM = 64
ATTN_SLOTS = 8
WINDOWS = (128, 512, 2048)
DILATIONS = (1, 4, 16)
N_GROUPS = len(WINDOWS)
GROUP_DIM = ATTN_SLOTS * HEAD_DIM
Q_BLOCK = 128
CONV_WIDTH = 3
EPS = 1e-6
NEG_INF = -1e30
LOG2_E = 1.4426950408889634
LANES = 128
SUBLANES = 8
LSE_LANES = LANES // ATTN_SLOTS
VMEM_LIMIT = 56 * 1024 * 1024

ROW_TILE = 512
OUT_ROW_TILE = 256
ATTN_STEP_BLOCKS = 8


def _params(n_axes):
    return pltpu.CompilerParams(dimension_semantics=("arbitrary",) * n_axes,
                                vmem_limit_bytes=VMEM_LIMIT)


def _silu(t):
    return t * jax.nn.sigmoid(t)


def _modulated_norm(x, norm_w, mod):
    ms = jnp.mean(x * x, axis=-1, keepdims=True)
    xn = x * lax.rsqrt(ms + EPS) * norm_w
    return xn * (1.0 + mod[1:2, :]) + mod[0:1, :]


def _to_dilated(slab_ref, n_rows, dil):
    per = n_rows // dil
    return jnp.concatenate(
        [jnp.concatenate([slab_ref[j, pl.ds(r, per, stride=dil), :] for r in range(dil)], axis=0)
         for j in range(slab_ref.shape[0])], axis=-1)


def _mod_kernel(c_ref, w_ref, b_ref, o_ref):
    c = c_ref[...]
    o_ref[...] = jnp.dot(_silu(c), w_ref[...], preferred_element_type=F32) + b_ref[...]


def _modulation(c, w_ada, b_ada):
    bsz, d = c.shape
    n = w_ada.shape[1]
    bn = 512
    return pl.pallas_call(
        _mod_kernel,
        grid=(n // bn,),
        in_specs=[pl.BlockSpec((bsz, d), lambda j: (0, 0)),
                  pl.BlockSpec((d, bn), lambda j: (0, j)),
                  pl.BlockSpec((1, bn), lambda j: (0, j))],
        out_specs=pl.BlockSpec((bsz, bn), lambda j: (0, j)),
        out_shape=jax.ShapeDtypeStruct((bsz, n), F32),
        compiler_params=_params(1),
        name="adaln_mod",
    )(c, w_ada, b_ada.reshape(1, n))


def _mixer_a_kernel(x_ref, halo_ref, mod_ref, nw_ref, w_ref, cw_ref, y_ref):
    d = x_ref.shape[-1]
    mod = mod_ref[...]
    nw = nw_ref[...]
    h = _modulated_norm(x_ref[...], nw, mod).astype(BF16)
    proj = [jnp.dot(h, w_ref[:, j * d:(j + 1) * d], preferred_element_type=F32) for j in range(4)]
    b_a, c_a, x_a, z_a = proj
    u = c_a * x_a

    hh = _modulated_norm(halo_ref[...], nw, mod).astype(BF16)
    cx = jnp.dot(hh, w_ref[:, d:3 * d], preferred_element_type=F32)
    u_prev = cx[:, :d] * cx[:, d:]
    u_prev = jnp.where(pl.program_id(1) > 0, u_prev, 0.0)
    prev1 = u_prev[SUBLANES - 1:SUBLANES, :]
    prev2 = u_prev[SUBLANES - 2:SUBLANES - 1, :]

    row = lax.broadcasted_iota(jnp.int32, u.shape, 0)
    u_m1 = jnp.where(row == 0, prev1, pltpu.roll(u, 1, axis=0))
    u_m2 = jnp.where(row == 0, prev2, jnp.where(row == 1, prev1, pltpu.roll(u, 2, axis=0)))
    cw = cw_ref[...]
    conv = cw[0:1, :] * u_m2 + cw[1:2, :] * u_m1 + cw[2:3, :] * u
    y_ref[...] = (b_a * conv * _silu(z_a)).astype(y_ref.dtype)


def _mixer_a(x, mod3, norm_w, w_a, conv_w):
    bsz, s, d = x.shape
    tm = ROW_TILE
    halo_blocks = tm // SUBLANES
    return pl.pallas_call(
        _mixer_a_kernel,
        grid=(bsz, s // tm),
        in_specs=[pl.BlockSpec((None, tm, d), lambda b, i: (b, i, 0)),
                  pl.BlockSpec((None, SUBLANES, d),
                               lambda b, i: (b, jnp.maximum(i * halo_blocks - 1, 0), 0)),
                  pl.BlockSpec((None, 3, d), lambda b, i: (b, 0, 0)),
                  pl.BlockSpec((1, d), lambda b, i: (0, 0)),
                  pl.BlockSpec(w_a.shape, lambda b, i: (0, 0)),
                  pl.BlockSpec(conv_w.shape, lambda b, i: (0, 0))],
        out_specs=pl.BlockSpec((None, tm, d), lambda b, i: (b, i, 0)),
        out_shape=jax.ShapeDtypeStruct((bsz, s, d), BF16),
        compiler_params=_params(2),
        name="mixer_a",
    )(x, x, mod3, norm_w, w_a, conv_w)


def _head_rms(t, ones_blk, w):
    sq = (t * t).astype(BF16)
    width = ones_blk.shape[0]
    parts = [jnp.dot(sq[:, j:j + width], ones_blk, preferred_element_type=F32)
             for j in range(0, t.shape[-1], width)]
    ssq = jnp.concatenate(parts, axis=-1)
    return t * lax.rsqrt(ssq * (1.0 / HEAD_DIM) + EPS) * w


def _attn_proj_kernel(x_ref, mod_ref, nw_ref, w_ref, ones_ref, qw_ref, kw_ref,
                      qkv0_ref, qkv1_ref, qkv2_ref, zs_ref, slab_ref):
    tm, d = x_ref.shape
    h = _modulated_norm(x_ref[...], nw_ref[...], mod_ref[...])
    for j in range(d // LANES):
        slab_ref[j] = h[:, j * LANES:(j + 1) * LANES]
    ones_blk = ones_ref[...]
    gw = 3 * GROUP_DIM
    for g, out_ref in enumerate((qkv0_ref, qkv1_ref, qkv2_ref)):
        hg = h if DILATIONS[g] == 1 else _to_dilated(slab_ref, tm, DILATIONS[g])
        hg = hg.astype(BF16)
        w_g = w_ref.at[:, g * gw:(g + 1) * gw]
        q = jnp.dot(hg, w_g[:, :GROUP_DIM], preferred_element_type=F32)
        k = jnp.dot(hg, w_g[:, GROUP_DIM:2 * GROUP_DIM], preferred_element_type=F32)
        v = jnp.dot(hg, w_g[:, 2 * GROUP_DIM:], preferred_element_type=F32)
        qkv = jnp.concatenate([_head_rms(q, ones_blk, qw_ref[...]),
                               _head_rms(k, ones_blk, kw_ref[...]), v], axis=-1)
        out_ref[...] = qkv.astype(out_ref.dtype).reshape(out_ref.shape)
        if g == 0:
            z = jnp.dot(hg, w_ref[:, N_GROUPS * gw:], preferred_element_type=F32)
            zs_ref[...] = _silu(z)


def _attn_proj(x, mod3, norm_w, w_b, q_norm_w, k_norm_w):
    bsz, s, d = x.shape
    tm = ROW_TILE
    width = 2 * LANES
    lane = jnp.arange(width)
    ones_blk = (lane[:, None] // HEAD_DIM == lane[None, :] // HEAD_DIM).astype(BF16)
    qw = jnp.tile(q_norm_w * (HEAD_DIM ** -0.5 * LOG2_E), ATTN_SLOTS).reshape(1, GROUP_DIM)
    kw = jnp.tile(k_norm_w, ATTN_SLOTS).reshape(1, GROUP_DIM)
    gw = 3 * GROUP_DIM
    out_shapes, out_specs = [], []
    for dil in DILATIONS:
        tile = Q_BLOCK * dil
        assert dil == 1 or tile % tm == 0
        if tile <= tm:
            nblk = tm // Q_BLOCK
            out_shapes.append(jax.ShapeDtypeStruct((bsz, s // Q_BLOCK, Q_BLOCK, gw), BF16))
            out_specs.append(pl.BlockSpec((None, nblk, Q_BLOCK, gw), lambda b, i: (b, i, 0, 0)))
        else:
            per = tm // dil
            steps = tile // tm
            out_shapes.append(jax.ShapeDtypeStruct((bsz, s // tile, dil, Q_BLOCK, gw), BF16))
            out_specs.append(pl.BlockSpec((None, None, dil, per, gw),
                                          lambda b, i, steps=steps: (b, i // steps, 0, i % steps, 0)))
    out_shapes.append(jax.ShapeDtypeStruct((bsz, s, GROUP_DIM), F32))
    out_specs.append(pl.BlockSpec((None, tm, GROUP_DIM), lambda b, i: (b, i, 0)))
    outs = pl.pallas_call(
        _attn_proj_kernel,
        grid=(bsz, s // tm),
        in_specs=[pl.BlockSpec((None, tm, d), lambda b, i: (b, i, 0)),
                  pl.BlockSpec((None, 3, d), lambda b, i: (b, 0, 0)),
                  pl.BlockSpec((1, d), lambda b, i: (0, 0)),
                  pl.BlockSpec(w_b.shape, lambda b, i: (0, 0)),
                  pl.BlockSpec(ones_blk.shape, lambda b, i: (0, 0)),
                  pl.BlockSpec((1, GROUP_DIM), lambda b, i: (0, 0)),
                  pl.BlockSpec((1, GROUP_DIM), lambda b, i: (0, 0))],
        out_specs=out_specs,
        out_shape=out_shapes,
        scratch_shapes=[pltpu.VMEM((d // LANES, tm, LANES), F32)],
        compiler_params=_params(2),
        name="attn_proj",
    )(x, mod3, norm_w, w_b, ones_blk, qw, kw)
    qkv = [t.reshape(bsz, s // Q_BLOCK, Q_BLOCK, gw) for t in outs[:N_GROUPS]]
    return qkv, outs[N_GROUPS]


def _stat_lane(head):
    pair, odd = divmod(head, 2)
    return pair * LSE_LANES + (0 if odd else HEAD_DIM)


def _attn_block(q, kp, kc, vp, vc, bias, o_ref, m_ref, l_ref, j):
    blk = q.shape[0]
    lane = lax.broadcasted_iota(jnp.int32, (blk, LANES), 1)
    lane_lo = lane < HEAD_DIM
    lane2_lo = lax.broadcasted_iota(jnp.int32, (2 * blk, LANES), 1) < HEAD_DIM
    zero = jnp.zeros((blk, LANES), q.dtype)
    one = jnp.ones((2 * blk, LANES), q.dtype)
    m_tile = jnp.zeros((blk, LANES), F32)
    l_tile = jnp.ones((blk, LANES), F32)
    for p in range(GROUP_DIM // LANES):
        sl = slice(p * LANES, (p + 1) * LANES)
        q2 = q[:, sl]
        qs = jnp.concatenate([jnp.where(lane_lo, q2, zero), jnp.where(lane_lo, zero, q2)], axis=0)
        k2 = jnp.concatenate([kp[:, sl], kc[:, sl]], axis=0)
        v2 = jnp.concatenate([vp[:, sl], vc[:, sl]], axis=0)
        s = lax.dot_general(qs, k2, (((1,), (1,)), ((), ())), preferred_element_type=F32) + bias
        m = jnp.max(s, axis=-1, keepdims=True)
        e = jnp.exp2(s - m).astype(BF16)
        o_even = jnp.dot(e[:blk], jnp.where(lane2_lo, v2, one), preferred_element_type=F32)
        o_odd = jnp.dot(e[blk:], jnp.where(lane2_lo, one, v2), preferred_element_type=F32)
        o_ref[j + (slice(None), sl)] = jnp.where(lane_lo, o_even, o_odd)
        for head, m_h, o_h in ((2 * p, m[:blk], o_even), (2 * p + 1, m[blk:], o_odd)):
            slot = (lane >= _stat_lane(head)) & (lane < _stat_lane(head) + LSE_LANES)
            m_tile = jnp.where(slot, m_h, m_tile)
            l_tile = jnp.where(slot, o_h, l_tile)
    m_ref[j] = m_tile
    l_ref[j] = l_tile


def _attn_kernel(q_ref, kprev_ref, k_ref, vprev_ref, v_ref, o_ref, m_ref, l_ref):
    n_blocks, n_res, blk, _ = q_ref.shape
    span = 2 * blk
    row = lax.broadcasted_iota(jnp.int32, (span, span), 0) % blk
    col = lax.broadcasted_iota(jnp.int32, (span, span), 1)
    in_window = (col >= row) & (col <= row + blk)
    bias = jnp.where(in_window, 0.0, NEG_INF)
    run_has_prev = pl.program_id(2) > 0
    bias_first = jnp.where(in_window & ((col >= blk) | run_has_prev), 0.0, NEG_INF)

    for r in range(n_res):
        _attn_block(q_ref[0, r], kprev_ref[r], k_ref[0, r], vprev_ref[r], v_ref[0, r], bias_first,
                    o_ref, m_ref, l_ref, (0, r))
        for j in range(1, n_blocks):
            _attn_block(q_ref[j, r], k_ref[j - 1, r], k_ref[j, r], v_ref[j - 1, r], v_ref[j, r], bias,
                        o_ref, m_ref, l_ref, (j, r))


def _dilated_attention(qkv, dil):
    bsz, nb, blk, width = qkv.shape
    gd = GROUP_DIM
    tiles = nb // dil
    run = min(ATTN_STEP_BLOCKS, tiles)
    res = min(ATTN_STEP_BLOCKS // run, dil)
    assert tiles % run == 0 and dil % res == 0
    view = qkv.reshape(bsz, tiles, dil, blk, width)
    cur = lambda part: pl.BlockSpec((None, run, res, blk, gd), lambda b, r, c: (b, c, r, 0, part))
    prev = lambda part: pl.BlockSpec((None, None, res, blk, gd),
                                     lambda b, r, c: (b, jnp.maximum(c * run - 1, 0), r, 0, part))
    stat_spec = pl.BlockSpec((None, run, res, blk, LANES), lambda b, r, c: (b, c, r, 0, 0))
    stat_shape = jax.ShapeDtypeStruct((bsz, tiles, dil, blk, LANES), F32)
    o, m, l = pl.pallas_call(
        _attn_kernel,
        grid=(bsz, dil // res, tiles // run),
        in_specs=[cur(0), prev(1), cur(1), prev(2), cur(2)],
        out_specs=[pl.BlockSpec((None, run, res, blk, gd), lambda b, r, c: (b, c, r, 0, 0)),
                   stat_spec, stat_spec],
        out_shape=[jax.ShapeDtypeStruct((bsz, tiles, dil, blk, gd), F32), stat_shape, stat_shape],
        compiler_params=_params(3),
        name=f"dilated_attn_d{dil}",
    )(view, view, view, view, view)
    return (o.reshape(bsz, nb, blk, gd), m.reshape(bsz, nb, blk, LANES),
            l.reshape(bsz, nb, blk, LANES))


def _to_sequence(src_ref, slab_ref, dil):
    per = src_ref.shape[1]
    n_slabs = src_ref.shape[2] // LANES
    for j in range(n_slabs):
        for r in range(dil):
            slab_ref[j, pl.ds(r, per, stride=dil), :] = src_ref[r, :, j * LANES:(j + 1) * LANES]
    return jnp.concatenate([slab_ref[j] for j in range(n_slabs)], axis=-1)


def _merge_kernel(x_ref, mod_ref, nw_ref, ya_ref, zs_ref,
                  o0_ref, o1_ref, o2_ref, m0_ref, m1_ref, m2_ref, l0_ref, l1_ref, l2_ref,
                  expand_ref, wg_ref, pa_ref, pb_ref, wo_ref, out_ref, slab_ref):
    d = x_ref.shape[-1]
    x = x_ref[...]
    mod = mod_ref[...]
    h = _modulated_norm(x, nw_ref[...], mod).astype(BF16)
    gates = jnp.dot(h, wg_ref[...], preferred_element_type=F32)
    g_a = jax.nn.sigmoid(gates[:, :d])
    g_b = jax.nn.sigmoid(gates[:, d:])

    def in_sequence(refs):
        return [refs[0][...]] + [_to_sequence(r, slab_ref, dil) for r, dil in zip(refs[1:], DILATIONS[1:])]

    outs = in_sequence((o0_ref, o1_ref, o2_ref))
    maxes = in_sequence((m0_ref, m1_ref, m2_ref))
    sums = in_sequence((l0_ref, l1_ref, l2_ref))
    top = jnp.maximum(jnp.maximum(maxes[0], maxes[1]), maxes[2])
    wts = [jnp.exp2(t - top) for t in maxes]
    den = wts[0] * sums[0] + wts[1] * sums[1] + wts[2] * sums[2]
    attn = jnp.zeros_like(outs[0])
    for w, o in zip(wts, outs):
        wn = w / den
        hi = wn.astype(BF16)
        lo = (wn - hi.astype(F32)).astype(BF16)
        wide = (jnp.dot(hi, expand_ref[...], preferred_element_type=F32)
                + jnp.dot(lo, expand_ref[...], preferred_element_type=F32))
        attn = attn + wide * o
    y_b = (attn * zs_ref[...]).astype(BF16)

    merged = (g_a * jnp.dot(ya_ref[...], pa_ref[...], preferred_element_type=F32)
              + g_b * jnp.dot(y_b, pb_ref[...], preferred_element_type=F32))
    upd = jnp.dot(merged.astype(BF16), wo_ref[...], preferred_element_type=F32)
    out_ref[...] = x + mod[2:3, :] * upd


def _merge(x, mod3, norm_w, y_a, zs, outs, maxes, sums, w_g, p_a, p_b, w_o):
    bsz, s, d = x.shape
    tm = OUT_ROW_TILE
    row_d = pl.BlockSpec((None, tm, d), lambda b, i: (b, i, 0))
    row_g = pl.BlockSpec((None, tm, GROUP_DIM), lambda b, i: (b, i, 0))
    full = lambda a: pl.BlockSpec(a.shape, lambda b, i: (0,) * a.ndim)

    def dilated(t, dil):
        width = t.shape[-1]
        tile = Q_BLOCK * dil
        per = tm // dil
        if tile <= tm:
            raise NotImplementedError("merge tile must not exceed a dilated-order tile")
        steps = tile // tm
        view = t.reshape(bsz, s // tile, dil, Q_BLOCK, width)
        spec = pl.BlockSpec((None, None, dil, per, width),
                            lambda b, i: (b, i // steps, 0, i % steps, 0))
        return view, spec

    src_lane = jnp.array([_stat_lane(hd) for hd in range(ATTN_SLOTS)])
    expand = (jnp.arange(LANES)[:, None] == src_lane[jnp.arange(GROUP_DIM) // HEAD_DIM][None, :])
    expand = expand.astype(BF16)

    views, specs = [], []
    for group in (outs, maxes, sums):
        for t, dil in zip(group, DILATIONS):
            width = t.shape[-1]
            if dil == 1:
                views.append(t.reshape(bsz, s, width))
                specs.append(pl.BlockSpec((None, tm, width), lambda b, i: (b, i, 0)))
            else:
                v, sp = dilated(t, dil)
                views.append(v)
                specs.append(sp)

    return pl.pallas_call(
        _merge_kernel,
        grid=(bsz, s // tm),
        in_specs=[row_d, pl.BlockSpec((None, 3, d), lambda b, i: (b, 0, 0)),
                  pl.BlockSpec((1, d), lambda b, i: (0, 0)), row_d, row_g,
                  *specs,
                  full(expand), full(w_g), full(p_a), full(p_b), full(w_o)],
        out_specs=row_d,
        out_shape=jax.ShapeDtypeStruct((bsz, s, d), F32),
        scratch_shapes=[pltpu.VMEM((GROUP_DIM // LANES, tm, LANES), F32)],
        compiler_params=_params(2),
        name="merge_out",
    )(x, mod3, norm_w, y_a, zs, *views, expand, w_g, p_a, p_b, w_o)


def _layer(x, c, w_ada, b_ada, norm_w, w_in, conv_w, q_norm_w, k_norm_w, w_br_conv, w_br_attn, w_out):
    bsz, s, d = x.shape
    conv_dim = conv_w.shape[-1]
    attn_dim = N_GROUPS * GROUP_DIM
    assert conv_dim == d and conv_w.shape[0] == CONV_WIDTH and d % LANES == 0
    assert w_in.shape[1] == 4 * conv_dim + 3 * attn_dim + GROUP_DIM + 2 * d
    assert s % (max(DILATIONS) * Q_BLOCK) == 0 and s % ROW_TILE == 0
    assert all(w // dil == Q_BLOCK for w, dil in zip(WINDOWS, DILATIONS))

    mod3 = _modulation(c, w_ada, b_ada).reshape(bsz, 3, d)
    nw = norm_w.reshape(1, d)

    o_a = 4 * conv_dim
    o_z = o_a + 3 * attn_dim
    o_g = o_z + GROUP_DIM
    w_a = w_in[:, :o_a].astype(BF16)
    cols = [w_in[:, o_a + part * attn_dim + g * GROUP_DIM:o_a + part * attn_dim + (g + 1) * GROUP_DIM]
            for g in range(N_GROUPS) for part in range(3)]
    w_b = jnp.concatenate(cols + [w_in[:, o_z:o_g]], axis=1).astype(BF16)
    w_g = w_in[:, o_g:].astype(BF16)

    y_a = _mixer_a(x, mod3, nw, w_a, conv_w)
    qkv, zs = _attn_proj(x, mod3, nw, w_b, q_norm_w, k_norm_w)
    outs, maxes, sums = zip(*[_dilated_attention(t, dil) for t, dil in zip(qkv, DILATIONS)])
    return _merge(x, mod3, nw, y_a, zs, outs, maxes, sums, w_g,
                  w_br_conv.astype(BF16), w_br_attn.astype(BF16), w_out.astype(BF16))


@jax.jit
def kernel(x, c, w_ada, b_ada, norm_w, w_in, conv_w, q_norm_w, k_norm_w, w_br_conv, w_br_attn, w_out):
    depth = w_ada.shape[0]
    for l in range(depth):
        x = _layer(x, c, w_ada[l], b_ada[l], norm_w[l], w_in[l], conv_w[l], q_norm_w[l],
                   k_norm_w[l], w_br_conv[l], w_br_attn[l], w_out[l])
    return x
```

```python
import jax
import jax.numpy as jnp
from jax import lax
from jax.experimental import pallas as pl
from jax.experimental.pallas import tpu as pltpu

F32 = jnp.float32
BF16 = jnp.bfloat16

HEAD_DIM = 64
ATTN_SLOTS = 8
WINDOWS = (128, 512, 2048)
DILATIONS = (1, 4, 16)
N_GROUPS = len(WINDOWS)
GROUP_DIM = ATTN_SLOTS * HEAD_DIM
Q_BLOCK = 128
CONV_WIDTH = 3
EPS = 1e-6
NEG_INF = -1e30
LOG2_E = 1.4426950408889634
LANES = 128
SUBLANES = 8
LSE_LANES = LANES // ATTN_SLOTS
VMEM_LIMIT = 56 * 1024 * 1024

ROW_TILE = 512
OUT_ROW_TILE = 512
ATTN_STEP_BLOCKS = 8


def _params(n_axes):
    return pltpu.CompilerParams(dimension_semantics=("arbitrary",) * n_axes,
                                vmem_limit_bytes=VMEM_LIMIT)


def _silu(t):
    return t * jax.nn.sigmoid(t)


def _modulated_norm(x, norm_w, mod):
    ms = jnp.mean(x * x, axis=-1, keepdims=True)
    xn = x * lax.rsqrt(ms + EPS) * norm_w
    return xn * (1.0 + mod[1:2, :]) + mod[0:1, :]


def _to_dilated(slab_ref, n_rows, dil):
    per = n_rows // dil
    return jnp.concatenate(
        [jnp.concatenate([slab_ref[j, pl.ds(r, per, stride=dil), :] for r in range(dil)], axis=0)
         for j in range(slab_ref.shape[0])], axis=-1)


def _mod_kernel(c_ref, w_ref, b_ref, o_ref):
    c = c_ref[...]
    o_ref[...] = jnp.dot(_silu(c), w_ref[...], preferred_element_type=F32) + b_ref[...]


def _modulation(c, w_ada, b_ada):
    bsz, d = c.shape
    n = w_ada.shape[1]
    bn = 512
    return pl.pallas_call(
        _mod_kernel,
        grid=(n // bn,),
        in_specs=[pl.BlockSpec((bsz, d), lambda j: (0, 0)),
                  pl.BlockSpec((d, bn), lambda j: (0, j)),
                  pl.BlockSpec((1, bn), lambda j: (0, j))],
        out_specs=pl.BlockSpec((bsz, bn), lambda j: (0, j)),
        out_shape=jax.ShapeDtypeStruct((bsz, n), F32),
        compiler_params=_params(1),
        name="adaln_mod",
    )(c, w_ada, b_ada.reshape(1, n))


def _mixer_a_kernel(x_ref, mod_ref, nw_ref, w_ref, cw_ref, y_ref, tail_ref):
    tm, d = x_ref.shape
    mod = mod_ref[...]
    nw = nw_ref[...]
    cw = cw_ref[...]

    @pl.when(pl.program_id(1) == 0)
    def _():
        tail_ref[...] = jnp.zeros_like(tail_ref)

    prev1 = tail_ref[SUBLANES - 1:SUBLANES, :]
    prev2 = tail_ref[SUBLANES - 2:SUBLANES - 1, :]

    h = _modulated_norm(x_ref[...], nw, mod).astype(BF16)
    b_a, c_a, x_a, z_a = [jnp.dot(h, w_ref[:, j * d:(j + 1) * d], preferred_element_type=F32)
                          for j in range(4)]
    u = c_a * x_a
    row = lax.broadcasted_iota(jnp.int32, (tm, d), 0)
    u_m1 = jnp.where(row == 0, prev1, pltpu.roll(u, 1, axis=0))
    u_m2 = jnp.where(row == 0, prev2, jnp.where(row == 1, prev1, pltpu.roll(u, 2, axis=0)))
    conv = cw[0:1, :] * u_m2 + cw[1:2, :] * u_m1 + cw[2:3, :] * u
    y_ref[...] = (b_a * conv * _silu(z_a)).astype(y_ref.dtype)
    tail_ref[...] = u[tm - SUBLANES:, :]


def _mixer_a(x, mod3, norm_w, w_a, conv_w):
    bsz, s, d = x.shape
    tm = ROW_TILE
    return pl.pallas_call(
        _mixer_a_kernel,
        grid=(bsz, s // tm),
        in_specs=[pl.BlockSpec((None, tm, d), lambda b, i: (b, i, 0)),
                  pl.BlockSpec((None, 3, d), lambda b, i: (b, 0, 0)),
                  pl.BlockSpec((1, d), lambda b, i: (0, 0)),
                  pl.BlockSpec(w_a.shape, lambda b, i: (0, 0)),
                  pl.BlockSpec(conv_w.shape, lambda b, i: (0, 0))],
        out_specs=pl.BlockSpec((None, tm, d), lambda b, i: (b, i, 0)),
        out_shape=jax.ShapeDtypeStruct((bsz, s, d), BF16),
        scratch_shapes=[pltpu.VMEM((SUBLANES, d), F32)],
        compiler_params=_params(2),
        name="mixer_a",
    )(x, mod3, norm_w, w_a, conv_w)


def _head_rms(t, ones_blk, w):
    sq = (t * t).astype(BF16)
    width = ones_blk.shape[0]
    parts = [jnp.dot(sq[:, j:j + width], ones_blk, preferred_element_type=F32)
             for j in range(0, t.shape[-1], width)]
    ssq = jnp.concatenate(parts, axis=-1)
    return t * lax.rsqrt(ssq * (1.0 / HEAD_DIM) + EPS) * w


def _attn_proj_kernel(x_ref, mod_ref, nw_ref, w_ref, ones_ref, qw_ref, kw_ref,
                      qkv0_ref, qkv1_ref, qkv2_ref, zs_ref, slab_ref):
    tm, d = x_ref.shape
    h = _modulated_norm(x_ref[...], nw_ref[...], mod_ref[...])
    for j in range(d // LANES):
        slab_ref[j] = h[:, j * LANES:(j + 1) * LANES]
    ones_blk = ones_ref[...]
    gw = 3 * GROUP_DIM
    for g, out_ref in enumerate((qkv0_ref, qkv1_ref, qkv2_ref)):
        hg = h if DILATIONS[g] == 1 else _to_dilated(slab_ref, tm, DILATIONS[g])
        hg = hg.astype(BF16)
        w_g = w_ref.at[:, g * gw:(g + 1) * gw]
        q = jnp.dot(hg, w_g[:, :GROUP_DIM], preferred_element_type=F32)
        k = jnp.dot(hg, w_g[:, GROUP_DIM:2 * GROUP_DIM], preferred_element_type=F32)
        v = jnp.dot(hg, w_g[:, 2 * GROUP_DIM:], preferred_element_type=F32)
        qkv = jnp.concatenate([_head_rms(q, ones_blk, qw_ref[...]),
                               _head_rms(k, ones_blk, kw_ref[...]), v], axis=-1)
        out_ref[...] = qkv.astype(out_ref.dtype).reshape(out_ref.shape)
        if g == 0:
            z = jnp.dot(hg, w_ref[:, N_GROUPS * gw:], preferred_element_type=F32)
            zs_ref[...] = _silu(z)


def _attn_proj(x, mod3, norm_w, w_b, q_norm_w, k_norm_w):
    bsz, s, d = x.shape
    tm = ROW_TILE
    width = 2 * LANES
    lane = jnp.arange(width)
    ones_blk = (lane[:, None] // HEAD_DIM == lane[None, :] // HEAD_DIM).astype(BF16)
    qw = jnp.tile(q_norm_w * (HEAD_DIM ** -0.5 * LOG2_E), ATTN_SLOTS).reshape(1, GROUP_DIM)
    kw = jnp.tile(k_norm_w, ATTN_SLOTS).reshape(1, GROUP_DIM)
    gw = 3 * GROUP_DIM
    out_shapes, out_specs = [], []
    for dil in DILATIONS:
        tile = Q_BLOCK * dil
        assert dil == 1 or tile % tm == 0
        if tile <= tm:
            nblk = tm // Q_BLOCK
            out_shapes.append(jax.ShapeDtypeStruct((bsz, s // Q_BLOCK, Q_BLOCK, gw), BF16))
            out_specs.append(pl.BlockSpec((None, nblk, Q_BLOCK, gw), lambda b, i: (b, i, 0, 0)))
        else:
            per = tm // dil
            steps = tile // tm
            out_shapes.append(jax.ShapeDtypeStruct((bsz, s // tile, dil, Q_BLOCK, gw), BF16))
            out_specs.append(pl.BlockSpec((None, None, dil, per, gw),
                                          lambda b, i, steps=steps: (b, i // steps, 0, i % steps, 0)))
    out_shapes.append(jax.ShapeDtypeStruct((bsz, s, GROUP_DIM), F32))
    out_specs.append(pl.BlockSpec((None, tm, GROUP_DIM), lambda b, i: (b, i, 0)))
    outs = pl.pallas_call(
        _attn_proj_kernel,
        grid=(bsz, s // tm),
        in_specs=[pl.BlockSpec((None, tm, d), lambda b, i: (b, i, 0)),
                  pl.BlockSpec((None, 3, d), lambda b, i: (b, 0, 0)),
                  pl.BlockSpec((1, d), lambda b, i: (0, 0)),
                  pl.BlockSpec(w_b.shape, lambda b, i: (0, 0)),
                  pl.BlockSpec(ones_blk.shape, lambda b, i: (0, 0)),
                  pl.BlockSpec((1, GROUP_DIM), lambda b, i: (0, 0)),
                  pl.BlockSpec((1, GROUP_DIM), lambda b, i: (0, 0))],
        out_specs=out_specs,
        out_shape=out_shapes,
        scratch_shapes=[pltpu.VMEM((d // LANES, tm, LANES), F32)],
        compiler_params=_params(2),
        name="attn_proj",
    )(x, mod3, norm_w, w_b, ones_blk, qw, kw)
    qkv = [t.reshape(bsz, s // Q_BLOCK, Q_BLOCK, gw) for t in outs[:N_GROUPS]]
    return qkv, outs[N_GROUPS]


def _stat_lane(head):
    pair, odd = divmod(head, 2)
    return pair * LSE_LANES + (0 if odd else HEAD_DIM)


def _attn_block(q, kp, kc, vp, vc, bias, o_ref, m_ref, l_ref, j):
    blk = q.shape[0]
    lane = lax.broadcasted_iota(jnp.int32, (blk, LANES), 1)
    lane_lo = lane < HEAD_DIM
    lane2_lo = lax.broadcasted_iota(jnp.int32, (2 * blk, LANES), 1) < HEAD_DIM
    zero = jnp.zeros((blk, LANES), q.dtype)
    one = jnp.ones((2 * blk, LANES), q.dtype)
    m_tile = jnp.zeros((blk, LANES), F32)
    l_tile = jnp.ones((blk, LANES), F32)
    for p in range(GROUP_DIM // LANES):
        sl = slice(p * LANES, (p + 1) * LANES)
        q2 = q[:, sl]
        qs = jnp.concatenate([jnp.where(lane_lo, q2, zero), jnp.where(lane_lo, zero, q2)], axis=0)
        k2 = jnp.concatenate([kp[:, sl], kc[:, sl]], axis=0)
        v2 = jnp.concatenate([vp[:, sl], vc[:, sl]], axis=0)
        s = lax.dot_general(qs, k2, (((1,), (1,)), ((), ())), preferred_element_type=F32) + bias
        m = jnp.max(s, axis=-1, keepdims=True)
        e = jnp.exp2(s - m).astype(BF16)
        o_even = jnp.dot(e[:blk], jnp.where(lane2_lo, v2, one), preferred_element_type=F32)
        o_odd = jnp.dot(e[blk:], jnp.where(lane2_lo, one, v2), preferred_element_type=F32)
        o_ref[j + (slice(None), sl)] = jnp.where(lane_lo, o_even, o_odd)
        for head, m_h, o_h in ((2 * p, m[:blk], o_even), (2 * p + 1, m[blk:], o_odd)):
            slot = (lane >= _stat_lane(head)) & (lane < _stat_lane(head) + LSE_LANES)
            m_tile = jnp.where(slot, m_h, m_tile)
            l_tile = jnp.where(slot, o_h, l_tile)
    m_ref[j] = m_tile
    l_ref[j] = l_tile


def _attn_kernel(q_ref, kprev_ref, k_ref, vprev_ref, v_ref, o_ref, m_ref, l_ref):
    n_blocks, n_res, blk, _ = q_ref.shape
    span = 2 * blk
    row = lax.broadcasted_iota(jnp.int32, (span, span), 0) % blk
    col = lax.broadcasted_iota(jnp.int32, (span, span), 1)
    in_window = (col >= row) & (col <= row + blk)
    bias = jnp.where(in_window, 0.0, NEG_INF)
    run_has_prev = pl.program_id(2) > 0
    bias_first = jnp.where(in_window & ((col >= blk) | run_has_prev), 0.0, NEG_INF)

    for r in range(n_res):
        _attn_block(q_ref[0, r], kprev_ref[r], k_ref[0, r], vprev_ref[r], v_ref[0, r], bias_first,
                    o_ref, m_ref, l_ref, (0, r))
        for j in range(1, n_blocks):
            _attn_block(q_ref[j, r], k_ref[j - 1, r], k_ref[j, r], v_ref[j - 1, r], v_ref[j, r], bias,
                        o_ref, m_ref, l_ref, (j, r))


def _dilated_attention(qkv, dil):
    bsz, nb, blk, width = qkv.shape
    gd = GROUP_DIM
    tiles = nb // dil
    run = min(ATTN_STEP_BLOCKS, tiles)
    res = min(ATTN_STEP_BLOCKS // run, dil)
    assert tiles % run == 0 and dil % res == 0
    view = qkv.reshape(bsz, tiles, dil, blk, width)
    cur = lambda part: pl.BlockSpec((None, run, res, blk, gd), lambda b, r, c: (b, c, r, 0, part))
    prev = lambda part: pl.BlockSpec((None, None, res, blk, gd),
                                     lambda b, r, c: (b, jnp.maximum(c * run - 1, 0), r, 0, part))
    stat_spec = pl.BlockSpec((None, run, res, blk, LANES), lambda b, r, c: (b, c, r, 0, 0))
    stat_shape = jax.ShapeDtypeStruct((bsz, tiles, dil, blk, LANES), F32)
    o, m, l = pl.pallas_call(
        _attn_kernel,
        grid=(bsz, dil // res, tiles // run),
        in_specs=[cur(0), prev(1), cur(1), prev(2), cur(2)],
        out_specs=[pl.BlockSpec((None, run, res, blk, gd), lambda b, r, c: (b, c, r, 0, 0)),
                   stat_spec, stat_spec],
        out_shape=[jax.ShapeDtypeStruct((bsz, tiles, dil, blk, gd), F32), stat_shape, stat_shape],
        compiler_params=_params(3),
        name=f"dilated_attn_d{dil}",
    )(view, view, view, view, view)
    return (o.reshape(bsz, nb, blk, gd), m.reshape(bsz, nb, blk, LANES),
            l.reshape(bsz, nb, blk, LANES))


def _to_sequence(src_ref, slab_ref, dil):
    per = src_ref.shape[1]
    n_slabs = src_ref.shape[2] // LANES
    for j in range(n_slabs):
        for r in range(dil):
            slab_ref[j, pl.ds(r, per, stride=dil), :] = src_ref[r, :, j * LANES:(j + 1) * LANES]
    return jnp.concatenate([slab_ref[j] for j in range(n_slabs)], axis=-1)


def _merge_kernel(x_ref, mod_ref, nw_ref, ya_ref, zs_ref,
                  o0_ref, o1_ref, o2_ref, m0_ref, m1_ref, m2_ref, l0_ref, l1_ref, l2_ref,
                  expand_ref, wg_ref, pa_ref, pb_ref, wo_ref, out_ref, slab_ref):
    d = x_ref.shape[-1]
    x = x_ref[...]
    mod = mod_ref[...]
    h = _modulated_norm(x, nw_ref[...], mod).astype(BF16)
    gates = jnp.dot(h, wg_ref[...], preferred_element_type=F32)
    g_a = jax.nn.sigmoid(gates[:, :d])
    g_b = jax.nn.sigmoid(gates[:, d:])

    def in_sequence(refs):
        return [refs[0][...]] + [_to_sequence(r, slab_ref, dil) for r, dil in zip(refs[1:], DILATIONS[1:])]

    outs = in_sequence((o0_ref, o1_ref, o2_ref))
    maxes = in_sequence((m0_ref, m1_ref, m2_ref))
    sums = in_sequence((l0_ref, l1_ref, l2_ref))
    top = jnp.maximum(jnp.maximum(maxes[0], maxes[1]), maxes[2])
    wts = [jnp.exp2(t - top) for t in maxes]
    den = wts[0] * sums[0] + wts[1] * sums[1] + wts[2] * sums[2]
    attn = jnp.zeros_like(outs[0])
    for w, o in zip(wts, outs):
        wn = w / den
        hi = wn.astype(BF16)
        lo = (wn - hi.astype(F32)).astype(BF16)
        wide = (jnp.dot(hi, expand_ref[...], preferred_element_type=F32)
                + jnp.dot(lo, expand_ref[...], preferred_element_type=F32))
        attn = attn + wide * o
    y_b = (attn * zs_ref[...]).astype(BF16)

    merged = (g_a * jnp.dot(ya_ref[...], pa_ref[...], preferred_element_type=F32)
              + g_b * jnp.dot(y_b, pb_ref[...], preferred_element_type=F32))
    upd = jnp.dot(merged.astype(BF16), wo_ref[...], preferred_element_type=F32)
    out_ref[...] = x + mod[2:3, :] * upd


def _merge(x, mod3, norm_w, y_a, zs, outs, maxes, sums, w_g, p_a, p_b, w_o):
    bsz, s, d = x.shape
    tm = OUT_ROW_TILE
    row_d = pl.BlockSpec((None, tm, d), lambda b, i: (b, i, 0))
    row_g = pl.BlockSpec((None, tm, GROUP_DIM), lambda b, i: (b, i, 0))
    full = lambda a: pl.BlockSpec(a.shape, lambda b, i: (0,) * a.ndim, pipeline_mode=pl.Buffered(1))

    def dilated(t, dil):
        width = t.shape[-1]
        tile = Q_BLOCK * dil
        per = tm // dil
        if tile % tm:
            raise NotImplementedError("a dilated-order tile must be a whole number of merge tiles")
        steps = tile // tm
        view = t.reshape(bsz, s // tile, dil, Q_BLOCK, width)
        spec = pl.BlockSpec((None, None, dil, per, width),
                            lambda b, i: (b, i // steps, 0, i % steps, 0))
        return view, spec

    src_lane = jnp.array([_stat_lane(hd) for hd in range(ATTN_SLOTS)])
    expand = (jnp.arange(LANES)[:, None] == src_lane[jnp.arange(GROUP_DIM) // HEAD_DIM][None, :])
    expand = expand.astype(BF16)

    views, specs = [], []
    for group in (outs, maxes, sums):
        for t, dil in zip(group, DILATIONS):
            width = t.shape[-1]
            if dil == 1:
                views.append(t.reshape(bsz, s, width))
                specs.append(pl.BlockSpec((None, tm, width), lambda b, i: (b, i, 0)))
            else:
                v, sp = dilated(t, dil)
                views.append(v)
                specs.append(sp)

    return pl.pallas_call(
        _merge_kernel,
        grid=(bsz, s // tm),
        in_specs=[row_d, pl.BlockSpec((None, 3, d), lambda b, i: (b, 0, 0)),
                  pl.BlockSpec((1, d), lambda b, i: (0, 0)), row_d, row_g,
                  *specs,
                  full(expand), full(w_g), full(p_a), full(p_b), full(w_o)],
        out_specs=row_d,
        out_shape=jax.ShapeDtypeStruct((bsz, s, d), F32),
        scratch_shapes=[pltpu.VMEM((GROUP_DIM // LANES, tm, LANES), F32)],
        compiler_params=_params(2),
        name="merge_out",
    )(x, mod3, norm_w, y_a, zs, *views, expand, w_g, p_a, p_b, w_o)


def _layer(x, c, w_ada, b_ada, norm_w, w_in, conv_w, q_norm_w, k_norm_w, w_br_conv, w_br_attn, w_out):
    bsz, s, d = x.shape
    conv_dim = conv_w.shape[-1]
    attn_dim = N_GROUPS * GROUP_DIM
    assert conv_dim == d and conv_w.shape[0] == CONV_WIDTH and d % LANES == 0
    assert w_in.shape[1] == 4 * conv_dim + 3 * attn_dim + GROUP_DIM + 2 * d
    assert s % (max(DILATIONS) * Q_BLOCK) == 0 and s % ROW_TILE == 0
    assert all(w // dil == Q_BLOCK for w, dil in zip(WINDOWS, DILATIONS))

    mod3 = _modulation(c, w_ada, b_ada).reshape(bsz, 3, d)
    nw = norm_w.reshape(1, d)

    o_a = 4 * conv_dim
    o_z = o_a + 3 * attn_dim
    o_g = o_z + GROUP_DIM
    w_a = w_in[:, :o_a].astype(BF16)
    cols = [w_in[:, o_a + part * attn_dim + g * GROUP_DIM:o_a + part * attn_dim + (g + 1) * GROUP_DIM]
            for g in range(N_GROUPS) for part in range(3)]
    w_b = jnp.concatenate(cols + [w_in[:, o_z:o_g]], axis=1).astype(BF16)
    w_g = w_in[:, o_g:].astype(BF16)

    y_a = _mixer_a(x, mod3, nw, w_a, conv_w)
    qkv, zs = _attn_proj(x, mod3, nw, w_b, q_norm_w, k_norm_w)
    outs, maxes, sums = zip(*[_dilated_attention(t, dil) for t, dil in zip(qkv, DILATIONS)])
    return _merge(x, mod3, nw, y_a, zs, outs, maxes, sums, w_g,
                  w_br_conv.astype(BF16), w_br_attn.astype(BF16), w_out.astype(BF16))


@jax.jit
def kernel(x, c, w_ada, b_ada, norm_w, w_in, conv_w, q_norm_w, k_norm_w, w_br_conv, w_br_attn, w_out):
    depth = w_ada.shape[0]
    for l in range(depth):
        x = _layer(x, c, w_ada[l], b_ada[l], norm_w[l], w_in[l], conv_w[l], q_norm_w[l],
                   k_norm_w[l], w_br_conv[l], w_br_attn[l], w_out[l])
    return x
```

```python
import functools
import math

import jax
import jax.numpy as jnp
from jax import lax
from jax.experimental import pallas as pl
from jax.experimental.pallas import tpu as pltpu

F32 = jnp.float32
BF16 = jnp.bfloat16

HEAD_DIM = 64
ATTN_SLOTS = 8
WINDOWS = (128, 512, 2048)
DILATIONS = (1, 4, 16)
N_GROUPS = len(WINDOWS)
GROUP_DIM = ATTN_SLOTS * HEAD_DIM
Q_BLOCK = 128
CONV_WIDTH = 3
EPS = 1e-6
NEG_INF = -1e30
LOG2_E = 1.4426950408889634
LANES = 128
SUBLANES = 8
LSE_LANES = LANES // ATTN_SLOTS
VMEM_LIMIT = 56 * 1024 * 1024

ROW_TILE = 512
OUT_ROW_TILE = 512
ATTN_STEP_BLOCKS = 16


def _params(n_axes):
    return pltpu.CompilerParams(dimension_semantics=("arbitrary",) * n_axes,
                                vmem_limit_bytes=VMEM_LIMIT)


def _silu(t):
    return t * jax.nn.sigmoid(t)


def _modulated_norm(x, norm_w, mod):
    ms = jnp.mean(x * x, axis=-1, keepdims=True)
    xn = x * lax.rsqrt(ms + EPS) * norm_w
    return xn * (1.0 + mod[1:2, :]) + mod[0:1, :]


def _to_dilated(slab_ref, n_rows, dil):
    per = n_rows // dil
    return jnp.concatenate(
        [jnp.concatenate([slab_ref[j, pl.ds(r, per, stride=dil), :] for r in range(dil)], axis=0)
         for j in range(slab_ref.shape[0])], axis=-1)


def _column_window(w, start, width):
    unit = math.gcd(start, width)
    specs = [pl.BlockSpec((w.shape[0], unit), lambda *_, j=start // unit + j: (0, j),
                          pipeline_mode=pl.Buffered(1)) for j in range(width // unit)]
    return specs, unit


def _columns(w_refs, unit, off, width):
    assert off // unit == (off + width - 1) // unit
    return w_refs[off // unit].at[:, off % unit:off % unit + width]


def _mod_kernel(c_ref, w_ref, b_ref, o_ref):
    c = c_ref[...]
    o_ref[...] = jnp.dot(_silu(c), w_ref[...], preferred_element_type=F32) + b_ref[...]


def _modulation(c, w_ada, b_ada):
    bsz, d = c.shape
    n = w_ada.shape[1]
    bn = 512
    return pl.pallas_call(
        _mod_kernel,
        grid=(n // bn,),
        in_specs=[pl.BlockSpec((bsz, d), lambda j: (0, 0)),
                  pl.BlockSpec((d, bn), lambda j: (0, j)),
                  pl.BlockSpec((1, bn), lambda j: (0, j))],
        out_specs=pl.BlockSpec((bsz, bn), lambda j: (0, j)),
        out_shape=jax.ShapeDtypeStruct((bsz, n), F32),
        compiler_params=_params(1),
        name="adaln_mod",
    )(c, w_ada, b_ada.reshape(1, n))


def _mixer_a_kernel(x_ref, mod_ref, nw_ref, w_ref, cw_ref, y_ref, tail_ref):
    tm, d = x_ref.shape
    mod = mod_ref[...]
    nw = nw_ref[...]
    cw = cw_ref[...]

    @pl.when(pl.program_id(1) == 0)
    def _():
        tail_ref[...] = jnp.zeros_like(tail_ref)

    prev1 = tail_ref[SUBLANES - 1:SUBLANES, :]
    prev2 = tail_ref[SUBLANES - 2:SUBLANES - 1, :]

    h = _modulated_norm(x_ref[...], nw, mod).astype(BF16)
    b_a, c_a, x_a, z_a = [jnp.dot(h, w_ref[:, j * d:(j + 1) * d], preferred_element_type=F32)
                          for j in range(4)]
    u = c_a * x_a
    row = lax.broadcasted_iota(jnp.int32, (tm, d), 0)
    u_m1 = jnp.where(row == 0, prev1, pltpu.roll(u, 1, axis=0))
    u_m2 = jnp.where(row == 0, prev2, jnp.where(row == 1, prev1, pltpu.roll(u, 2, axis=0)))
    conv = cw[0:1, :] * u_m2 + cw[1:2, :] * u_m1 + cw[2:3, :] * u
    y_ref[...] = (b_a * conv * _silu(z_a)).astype(y_ref.dtype)
    tail_ref[...] = u[tm - SUBLANES:, :]


def _mixer_a(x, mod3, norm_w, w_in, conv_w):
    bsz, s, d = x.shape
    tm = ROW_TILE
    (w_spec,), _ = _column_window(w_in, 0, 4 * d)
    return pl.pallas_call(
        _mixer_a_kernel,
        grid=(bsz, s // tm),
        in_specs=[pl.BlockSpec((None, tm, d), lambda b, i: (b, i, 0)),
                  pl.BlockSpec((None, 3, d), lambda b, i: (b, 0, 0)),
                  pl.BlockSpec((1, d), lambda b, i: (0, 0)),
                  w_spec,
                  pl.BlockSpec(conv_w.shape, lambda b, i: (0, 0))],
        out_specs=pl.BlockSpec((None, tm, d), lambda b, i: (b, i, 0)),
        out_shape=jax.ShapeDtypeStruct((bsz, s, d), BF16),
        scratch_shapes=[pltpu.VMEM((SUBLANES, d), F32)],
        compiler_params=_params(2),
        name="mixer_a",
    )(x, mod3, norm_w, w_in, conv_w)


def _head_rms(t, ones_blk, w):
    sq = (t * t).astype(BF16)
    width = ones_blk.shape[0]
    parts = [jnp.dot(sq[:, j:j + width], ones_blk, preferred_element_type=F32)
             for j in range(0, t.shape[-1], width)]
    ssq = jnp.concatenate(parts, axis=-1)
    return t * lax.rsqrt(ssq * (1.0 / HEAD_DIM) + EPS) * w


def _attn_proj_kernel(x_ref, mod_ref, nw_ref, *refs, w_unit):
    n_w = (3 * N_GROUPS + 1) * GROUP_DIM // w_unit
    w_refs = refs[:n_w]
    ones_ref, qw_ref, kw_ref, qkv0_ref, qkv1_ref, qkv2_ref, zs_ref, slab_ref = refs[n_w:]
    tm, d = x_ref.shape
    h = _modulated_norm(x_ref[...], nw_ref[...], mod_ref[...])
    for j in range(d // LANES):
        slab_ref[j] = h[:, j * LANES:(j + 1) * LANES]
    ones_blk = ones_ref[...]
    ad = N_GROUPS * GROUP_DIM
    for g, out_ref in enumerate((qkv0_ref, qkv1_ref, qkv2_ref)):
        hg = h if DILATIONS[g] == 1 else _to_dilated(slab_ref, tm, DILATIONS[g])
        hg = hg.astype(BF16)
        q, k, v = [jnp.dot(hg, _columns(w_refs, w_unit, part * ad + g * GROUP_DIM, GROUP_DIM)[...],
                           preferred_element_type=F32) for part in range(3)]
        qkv = jnp.concatenate([_head_rms(q, ones_blk, qw_ref[...]),
                               _head_rms(k, ones_blk, kw_ref[...]), v], axis=-1)
        out_ref[...] = qkv.astype(out_ref.dtype).reshape(out_ref.shape)
        if g == 0:
            z = jnp.dot(hg, _columns(w_refs, w_unit, 3 * ad, GROUP_DIM)[...], preferred_element_type=F32)
            zs_ref[...] = _silu(z)


def _attn_proj(x, mod3, norm_w, w_in, w_start, q_norm_w, k_norm_w):
    bsz, s, d = x.shape
    tm = ROW_TILE
    width = 2 * LANES
    lane = jnp.arange(width)
    ones_blk = (lane[:, None] // HEAD_DIM == lane[None, :] // HEAD_DIM).astype(BF16)
    qw = jnp.tile(q_norm_w * (HEAD_DIM ** -0.5 * LOG2_E), ATTN_SLOTS).reshape(1, GROUP_DIM)
    kw = jnp.tile(k_norm_w, ATTN_SLOTS).reshape(1, GROUP_DIM)
    gw = 3 * GROUP_DIM
    w_specs, w_unit = _column_window(w_in, w_start, (3 * N_GROUPS + 1) * GROUP_DIM)
    out_shapes, out_specs = [], []
    for dil in DILATIONS:
        tile = Q_BLOCK * dil
        assert dil == 1 or tile % tm == 0
        if tile <= tm:
            nblk = tm // Q_BLOCK
            out_shapes.append(jax.ShapeDtypeStruct((bsz, s // Q_BLOCK, Q_BLOCK, gw), BF16))
            out_specs.append(pl.BlockSpec((None, nblk, Q_BLOCK, gw), lambda b, i: (b, i, 0, 0)))
        else:
            per = tm // dil
            steps = tile // tm
            out_shapes.append(jax.ShapeDtypeStruct((bsz, s // tile, dil, Q_BLOCK, gw), BF16))
            out_specs.append(pl.BlockSpec((None, None, dil, per, gw),
                                          lambda b, i, steps=steps: (b, i // steps, 0, i % steps, 0)))
    out_shapes.append(jax.ShapeDtypeStruct((bsz, s, GROUP_DIM), F32))
    out_specs.append(pl.BlockSpec((None, tm, GROUP_DIM), lambda b, i: (b, i, 0)))
    outs = pl.pallas_call(
        functools.partial(_attn_proj_kernel, w_unit=w_unit),
        grid=(bsz, s // tm),
        in_specs=[pl.BlockSpec((None, tm, d), lambda b, i: (b, i, 0)),
                  pl.BlockSpec((None, 3, d), lambda b, i: (b, 0, 0)),
                  pl.BlockSpec((1, d), lambda b, i: (0, 0)),
                  *w_specs,
                  pl.BlockSpec(ones_blk.shape, lambda b, i: (0, 0)),
                  pl.BlockSpec((1, GROUP_DIM), lambda b, i: (0, 0)),
                  pl.BlockSpec((1, GROUP_DIM), lambda b, i: (0, 0))],
        out_specs=out_specs,
        out_shape=out_shapes,
        scratch_shapes=[pltpu.VMEM((d // LANES, tm, LANES), F32)],
        compiler_params=_params(2),
        name="attn_proj",
    )(x, mod3, norm_w, *([w_in] * len(w_specs)), ones_blk, qw, kw)
    qkv = [t.reshape(bsz, s // Q_BLOCK, Q_BLOCK, gw) for t in outs[:N_GROUPS]]
    return qkv, outs[N_GROUPS]


def _stat_lane(head):
    pair, odd = divmod(head, 2)
    return pair * LSE_LANES + (0 if odd else HEAD_DIM)


def _attn_block(q, kp, kc, vp, vc, bias, o_ref, m_ref, l_ref, j):
    blk = q.shape[0]
    lane = lax.broadcasted_iota(jnp.int32, (blk, LANES), 1)
    lane_lo = lane < HEAD_DIM
    lane2_lo = lax.broadcasted_iota(jnp.int32, (2 * blk, LANES), 1) < HEAD_DIM
    zero = jnp.zeros((blk, LANES), q.dtype)
    one = jnp.ones((2 * blk, LANES), q.dtype)
    m_tile = jnp.zeros((blk, LANES), F32)
    l_tile = jnp.ones((blk, LANES), F32)
    for p in range(GROUP_DIM // LANES):
        sl = slice(p * LANES, (p + 1) * LANES)
        q2 = q[:, sl]
        qs = jnp.concatenate([jnp.where(lane_lo, q2, zero), jnp.where(lane_lo, zero, q2)], axis=0)
        k2 = jnp.concatenate([kp[:, sl], kc[:, sl]], axis=0)
        v2 = jnp.concatenate([vp[:, sl], vc[:, sl]], axis=0)
        s = lax.dot_general(qs, k2, (((1,), (1,)), ((), ())), preferred_element_type=F32) + bias
        m = jnp.max(s, axis=-1, keepdims=True)
        e = jnp.exp2(s - m).astype(BF16)
        o_even = jnp.dot(e[:blk], jnp.where(lane2_lo, v2, one), preferred_element_type=F32)
        o_odd = jnp.dot(e[blk:], jnp.where(lane2_lo, one, v2), preferred_element_type=F32)
        o_ref[j + (slice(None), sl)] = jnp.where(lane_lo, o_even, o_odd)
        for head, m_h, o_h in ((2 * p, m[:blk], o_even), (2 * p + 1, m[blk:], o_odd)):
            slot = (lane >= _stat_lane(head)) & (lane < _stat_lane(head) + LSE_LANES)
            m_tile = jnp.where(slot, m_h, m_tile)
            l_tile = jnp.where(slot, o_h, l_tile)
    m_ref[j] = m_tile
    l_ref[j] = l_tile


def _attn_kernel(q_ref, kprev_ref, k_ref, vprev_ref, v_ref, o_ref, m_ref, l_ref):
    n_blocks, n_res, blk, _ = q_ref.shape
    span = 2 * blk
    row = lax.broadcasted_iota(jnp.int32, (span, span), 0) % blk
    col = lax.broadcasted_iota(jnp.int32, (span, span), 1)
    in_window = (col >= row) & (col <= row + blk)
    bias = jnp.where(in_window, 0.0, NEG_INF)
    run_has_prev = pl.program_id(2) > 0
    bias_first = jnp.where(in_window & ((col >= blk) | run_has_prev), 0.0, NEG_INF)

    for r in range(n_res):
        _attn_block(q_ref[0, r], kprev_ref[r], k_ref[0, r], vprev_ref[r], v_ref[0, r], bias_first,
                    o_ref, m_ref, l_ref, (0, r))
        for j in range(1, n_blocks):
            _attn_block(q_ref[j, r], k_ref[j - 1, r], k_ref[j, r], v_ref[j - 1, r], v_ref[j, r], bias,
                        o_ref, m_ref, l_ref, (j, r))


def _dilated_attention(qkv, dil):
    bsz, nb, blk, width = qkv.shape
    gd = GROUP_DIM
    tiles = nb // dil
    run = min(ATTN_STEP_BLOCKS, tiles)
    res = min(ATTN_STEP_BLOCKS // run, dil)
    assert tiles % run == 0 and dil % res == 0
    view = qkv.reshape(bsz, tiles, dil, blk, width)
    cur = lambda part: pl.BlockSpec((None, run, res, blk, gd), lambda b, r, c: (b, c, r, 0, part))
    prev = lambda part: pl.BlockSpec((None, None, res, blk, gd),
                                     lambda b, r, c: (b, jnp.maximum(c * run - 1, 0), r, 0, part))
    stat_spec = pl.BlockSpec((None, run, res, blk, LANES), lambda b, r, c: (b, c, r, 0, 0))
    stat_shape = jax.ShapeDtypeStruct((bsz, tiles, dil, blk, LANES), F32)
    o, m, l = pl.pallas_call(
        _attn_kernel,
        grid=(bsz, dil // res, tiles // run),
        in_specs=[cur(0), prev(1), cur(1), prev(2), cur(2)],
        out_specs=[pl.BlockSpec((None, run, res, blk, gd), lambda b, r, c: (b, c, r, 0, 0)),
                   stat_spec, stat_spec],
        out_shape=[jax.ShapeDtypeStruct((bsz, tiles, dil, blk, gd), F32), stat_shape, stat_shape],
        compiler_params=_params(3),
        name=f"dilated_attn_d{dil}",
    )(view, view, view, view, view)
    return (o.reshape(bsz, nb, blk, gd), m.reshape(bsz, nb, blk, LANES),
            l.reshape(bsz, nb, blk, LANES))


def _to_sequence(src_ref, slab_ref, dil):
    per = src_ref.shape[1]
    n_slabs = src_ref.shape[2] // LANES
    for j in range(n_slabs):
        for r in range(dil):
            slab_ref[j, pl.ds(r, per, stride=dil), :] = src_ref[r, :, j * LANES:(j + 1) * LANES]
    return jnp.concatenate([slab_ref[j] for j in range(n_slabs)], axis=-1)


def _merge_kernel(x_ref, mod_ref, nw_ref, ya_ref, zs_ref,
                  o0_ref, o1_ref, o2_ref, m0_ref, m1_ref, m2_ref, l0_ref, l1_ref, l2_ref,
                  expand_ref, wga_ref, wgb_ref, pa_ref, pb_ref, wo_ref, out_ref, slab_ref):
    x = x_ref[...]
    mod = mod_ref[...]
    h = _modulated_norm(x, nw_ref[...], mod).astype(BF16)
    g_a = jax.nn.sigmoid(jnp.dot(h, wga_ref[...], preferred_element_type=F32))
    g_b = jax.nn.sigmoid(jnp.dot(h, wgb_ref[...], preferred_element_type=F32))

    def in_sequence(refs):
        return [refs[0][...]] + [_to_sequence(r, slab_ref, dil) for r, dil in zip(refs[1:], DILATIONS[1:])]

    outs = in_sequence((o0_ref, o1_ref, o2_ref))
    maxes = in_sequence((m0_ref, m1_ref, m2_ref))
    sums = in_sequence((l0_ref, l1_ref, l2_ref))
    top = jnp.maximum(jnp.maximum(maxes[0], maxes[1]), maxes[2])
    wts = [jnp.exp2(t - top) for t in maxes]
    den = wts[0] * sums[0] + wts[1] * sums[1] + wts[2] * sums[2]
    attn = jnp.zeros_like(outs[0])
    for w, o in zip(wts, outs):
        wn = w / den
        hi = wn.astype(BF16)
        lo = (wn - hi.astype(F32)).astype(BF16)
        wide = (jnp.dot(hi, expand_ref[...], preferred_element_type=F32)
                + jnp.dot(lo, expand_ref[...], preferred_element_type=F32))
        attn = attn + wide * o
    y_b = (attn * zs_ref[...]).astype(BF16)

    merged = (g_a * jnp.dot(ya_ref[...], pa_ref[...], preferred_element_type=F32)
              + g_b * jnp.dot(y_b, pb_ref[...], preferred_element_type=F32))
    upd = jnp.dot(merged.astype(BF16), wo_ref[...], preferred_element_type=F32)
    out_ref[...] = x + mod[2:3, :] * upd


def _merge(x, mod3, norm_w, y_a, zs, outs, maxes, sums, w_in, w_start, p_a, p_b, w_o):
    bsz, s, d = x.shape
    tm = OUT_ROW_TILE
    row_d = pl.BlockSpec((None, tm, d), lambda b, i: (b, i, 0))
    row_g = pl.BlockSpec((None, tm, GROUP_DIM), lambda b, i: (b, i, 0))
    full = lambda a: pl.BlockSpec(a.shape, lambda b, i: (0,) * a.ndim, pipeline_mode=pl.Buffered(1))
    gate_specs, gate_unit = _column_window(w_in, w_start, 2 * d)
    assert gate_unit == d

    def dilated(t, dil):
        width = t.shape[-1]
        tile = Q_BLOCK * dil
        per = tm // dil
        if tile % tm:
            raise NotImplementedError("a dilated-order tile must be a whole number of merge tiles")
        steps = tile // tm
        view = t.reshape(bsz, s // tile, dil, Q_BLOCK, width)
        spec = pl.BlockSpec((None, None, dil, per, width),
                            lambda b, i: (b, i // steps, 0, i % steps, 0))
        return view, spec

    src_lane = jnp.array([_stat_lane(hd) for hd in range(ATTN_SLOTS)])
    expand = (jnp.arange(LANES)[:, None] == src_lane[jnp.arange(GROUP_DIM) // HEAD_DIM][None, :])
    expand = expand.astype(BF16)

    views, specs = [], []
    for group in (outs, maxes, sums):
        for t, dil in zip(group, DILATIONS):
            width = t.shape[-1]
            if dil == 1:
                views.append(t.reshape(bsz, s, width))
                specs.append(pl.BlockSpec((None, tm, width), lambda b, i: (b, i, 0)))
            else:
                v, sp = dilated(t, dil)
                views.append(v)
                specs.append(sp)

    return pl.pallas_call(
        _merge_kernel,
        grid=(bsz, s // tm),
        in_specs=[row_d, pl.BlockSpec((None, 3, d), lambda b, i: (b, 0, 0)),
                  pl.BlockSpec((1, d), lambda b, i: (0, 0)), row_d, row_g,
                  *specs,
                  full(expand), *gate_specs, full(p_a), full(p_b), full(w_o)],
        out_specs=row_d,
        out_shape=jax.ShapeDtypeStruct((bsz, s, d), F32),
        scratch_shapes=[pltpu.VMEM((GROUP_DIM // LANES, tm, LANES), F32)],
        compiler_params=_params(2),
        name="merge_out",
    )(x, mod3, norm_w, y_a, zs, *views, expand, w_in, w_in, p_a, p_b, w_o)


def _layer(x, c, w_ada, b_ada, norm_w, w_in, conv_w, q_norm_w, k_norm_w, w_br_conv, w_br_attn, w_out):
    bsz, s, d = x.shape
    conv_dim = conv_w.shape[-1]
    attn_dim = N_GROUPS * GROUP_DIM
    assert conv_dim == d and conv_w.shape[0] == CONV_WIDTH and d % LANES == 0
    assert w_in.shape[1] == 4 * conv_dim + 3 * attn_dim + GROUP_DIM + 2 * d
    assert s % (max(DILATIONS) * Q_BLOCK) == 0 and s % ROW_TILE == 0
    assert all(w // dil == Q_BLOCK for w, dil in zip(WINDOWS, DILATIONS))

    mod3 = _modulation(c, w_ada, b_ada).reshape(bsz, 3, d)
    nw = norm_w.reshape(1, d)

    w_in = w_in.astype(BF16)
    o_attn = 4 * conv_dim
    o_gate = o_attn + 3 * attn_dim + GROUP_DIM

    y_a = _mixer_a(x, mod3, nw, w_in, conv_w)
    qkv, zs = _attn_proj(x, mod3, nw, w_in, o_attn, q_norm_w, k_norm_w)
    outs, maxes, sums = zip(*[_dilated_attention(t, dil) for t, dil in zip(qkv, DILATIONS)])
    return _merge(x, mod3, nw, y_a, zs, outs, maxes, sums, w_in, o_gate,
                  w_br_conv.astype(BF16), w_br_attn.astype(BF16), w_out.astype(BF16))


@jax.jit
def kernel(x, c, w_ada, b_ada, norm_w, w_in, conv_w, q_norm_w, k_norm_w, w_br_conv, w_br_attn, w_out):
    depth = w_ada.shape[0]
    for l in range(depth):
        x = _layer(x, c, w_ada[l], b_ada[l], norm_w[l], w_in[l], conv_w[l], q_norm_w[l],
                   k_norm_w[l], w_br_conv[l], w_br_attn[l], w_out[l])
    return x
```

```python
import functools
import math

import jax
import jax.numpy as jnp
from jax import lax
from jax.experimental import pallas as pl
from jax.experimental.pallas import tpu as pltpu

F32 = jnp.float32
BF16 = jnp.bfloat16

HEAD_DIM = 64
ATTN_SLOTS = 8
WINDOWS = (128, 512, 2048)
DILATIONS = (1, 4, 16)
N_GROUPS = len(WINDOWS)
GROUP_DIM = ATTN_SLOTS * HEAD_DIM
Q_BLOCK = 128
CONV_WIDTH = 3
EPS = 1e-6
NEG_INF = -1e30
LOG2_E = 1.4426950408889634
LANES = 128
SUBLANES = 8
LSE_LANES = LANES // ATTN_SLOTS
VMEM_LIMIT = 56 * 1024 * 1024

ROW_TILE = 512
OUT_ROW_TILE = 512
ATTN_STEP_BLOCKS = 16


def _params(n_axes):
    return pltpu.CompilerParams(dimension_semantics=("arbitrary",) * n_axes,
                                vmem_limit_bytes=VMEM_LIMIT)


def _silu(t):
    return t * jax.nn.sigmoid(t)


def _modulated_norm(x, norm_w, mod):
    ms = jnp.mean(x * x, axis=-1, keepdims=True)
    xn = x * lax.rsqrt(ms + EPS) * norm_w
    return xn * (1.0 + mod[1:2, :]) + mod[0:1, :]


def _to_dilated(slab_ref, n_rows, dil):
    per = n_rows // dil
    return jnp.concatenate(
        [jnp.concatenate([slab_ref[j, pl.ds(r, per, stride=dil), :] for r in range(dil)], axis=0)
         for j in range(slab_ref.shape[0])], axis=-1)


def _column_window(w, start, width):
    unit = math.gcd(start, width)
    specs = [pl.BlockSpec((w.shape[0], unit), lambda *_, j=start // unit + j: (0, j),
                          pipeline_mode=pl.Buffered(1)) for j in range(width // unit)]
    return specs, unit


def _columns(w_refs, unit, off, width):
    assert off // unit == (off + width - 1) // unit
    return w_refs[off // unit].at[:, off % unit:off % unit + width]


def _mod_kernel(c_ref, w_ref, b_ref, o_ref):
    c = c_ref[...]
    o_ref[...] = jnp.dot(_silu(c), w_ref[...], preferred_element_type=F32) + b_ref[...]


def _modulation(c, w_ada, b_ada):
    bsz, d = c.shape
    n = w_ada.shape[1]
    bn = 512
    return pl.pallas_call(
        _mod_kernel,
        grid=(n // bn,),
        in_specs=[pl.BlockSpec((bsz, d), lambda j: (0, 0)),
                  pl.BlockSpec((d, bn), lambda j: (0, j)),
                  pl.BlockSpec((1, bn), lambda j: (0, j))],
        out_specs=pl.BlockSpec((bsz, bn), lambda j: (0, j)),
        out_shape=jax.ShapeDtypeStruct((bsz, n), F32),
        compiler_params=_params(1),
        name="adaln_mod",
    )(c, w_ada, b_ada.reshape(1, n))


def _mixer_a_kernel(x_ref, mod_ref, nw_ref, w_ref, cw_ref, y_ref, tail_ref):
    tm, d = x_ref.shape
    mod = mod_ref[...]
    nw = nw_ref[...]
    cw = cw_ref[...]

    @pl.when(pl.program_id(1) == 0)
    def _():
        tail_ref[...] = jnp.zeros_like(tail_ref)

    prev1 = tail_ref[SUBLANES - 1:SUBLANES, :]
    prev2 = tail_ref[SUBLANES - 2:SUBLANES - 1, :]

    h = _modulated_norm(x_ref[...], nw, mod).astype(BF16)
    b_a, c_a, x_a, z_a = [jnp.dot(h, w_ref[:, j * d:(j + 1) * d], preferred_element_type=F32)
                          for j in range(4)]
    u = c_a * x_a
    row = lax.broadcasted_iota(jnp.int32, (tm, d), 0)
    u_m1 = jnp.where(row == 0, prev1, pltpu.roll(u, 1, axis=0))
    u_m2 = jnp.where(row == 0, prev2, jnp.where(row == 1, prev1, pltpu.roll(u, 2, axis=0)))
    conv = cw[0:1, :] * u_m2 + cw[1:2, :] * u_m1 + cw[2:3, :] * u
    y_ref[...] = (b_a * conv * _silu(z_a)).astype(y_ref.dtype)
    tail_ref[...] = u[tm - SUBLANES:, :]


def _mixer_a(x, mod3, norm_w, w_in, conv_w):
    bsz, s, d = x.shape
    tm = ROW_TILE
    (w_spec,), _ = _column_window(w_in, 0, 4 * d)
    return pl.pallas_call(
        _mixer_a_kernel,
        grid=(bsz, s // tm),
        in_specs=[pl.BlockSpec((None, tm, d), lambda b, i: (b, i, 0)),
                  pl.BlockSpec((None, 3, d), lambda b, i: (b, 0, 0)),
                  pl.BlockSpec((1, d), lambda b, i: (0, 0)),
                  w_spec,
                  pl.BlockSpec(conv_w.shape, lambda b, i: (0, 0))],
        out_specs=pl.BlockSpec((None, tm, d), lambda b, i: (b, i, 0)),
        out_shape=jax.ShapeDtypeStruct((bsz, s, d), BF16),
        scratch_shapes=[pltpu.VMEM((SUBLANES, d), F32)],
        compiler_params=_params(2),
        name="mixer_a",
    )(x, mod3, norm_w, w_in, conv_w)


def _head_rms(t, w):
    sq = t * t
    lane_lo = lax.broadcasted_iota(jnp.int32, (t.shape[0], LANES), 1) < HEAD_DIM
    parts = []
    for j in range(0, t.shape[-1], LANES):
        tile = sq[:, j:j + LANES]
        even = jnp.sum(jnp.where(lane_lo, tile, 0.0), axis=-1, keepdims=True)
        odd = jnp.sum(jnp.where(lane_lo, 0.0, tile), axis=-1, keepdims=True)
        parts.append(jnp.where(lane_lo, even, odd))
    ssq = jnp.concatenate(parts, axis=-1)
    return t * lax.rsqrt(ssq * (1.0 / HEAD_DIM) + EPS) * w


def _attn_proj_kernel(x_ref, mod_ref, nw_ref, *refs, w_unit):
    n_w = (3 * N_GROUPS + 1) * GROUP_DIM // w_unit
    w_refs = refs[:n_w]
    qw_ref, kw_ref, qkv0_ref, qkv1_ref, qkv2_ref, zs_ref, slab_ref = refs[n_w:]
    tm, d = x_ref.shape
    h = _modulated_norm(x_ref[...], nw_ref[...], mod_ref[...])
    for j in range(d // LANES):
        slab_ref[j] = h[:, j * LANES:(j + 1) * LANES]
    ad = N_GROUPS * GROUP_DIM
    for g, out_ref in enumerate((qkv0_ref, qkv1_ref, qkv2_ref)):
        hg = h if DILATIONS[g] == 1 else _to_dilated(slab_ref, tm, DILATIONS[g])
        hg = hg.astype(BF16)
        q, k, v = [jnp.dot(hg, _columns(w_refs, w_unit, part * ad + g * GROUP_DIM, GROUP_DIM)[...],
                           preferred_element_type=F32) for part in range(3)]
        qkv = jnp.concatenate([_head_rms(q, qw_ref[...]), _head_rms(k, kw_ref[...]), v], axis=-1)
        out_ref[...] = qkv.astype(out_ref.dtype).reshape(out_ref.shape)
        if g == 0:
            z = jnp.dot(hg, _columns(w_refs, w_unit, 3 * ad, GROUP_DIM)[...], preferred_element_type=F32)
            zs_ref[...] = _silu(z)


def _attn_proj(x, mod3, norm_w, w_in, w_start, q_norm_w, k_norm_w):
    bsz, s, d = x.shape
    tm = ROW_TILE
    qw = jnp.tile(q_norm_w * (HEAD_DIM ** -0.5 * LOG2_E), ATTN_SLOTS).reshape(1, GROUP_DIM)
    kw = jnp.tile(k_norm_w, ATTN_SLOTS).reshape(1, GROUP_DIM)
    gw = 3 * GROUP_DIM
    w_specs, w_unit = _column_window(w_in, w_start, (3 * N_GROUPS + 1) * GROUP_DIM)
    out_shapes, out_specs = [], []
    for dil in DILATIONS:
        tile = Q_BLOCK * dil
        assert dil == 1 or tile % tm == 0
        if tile <= tm:
            nblk = tm // Q_BLOCK
            out_shapes.append(jax.ShapeDtypeStruct((bsz, s // Q_BLOCK, Q_BLOCK, gw), BF16))
            out_specs.append(pl.BlockSpec((None, nblk, Q_BLOCK, gw), lambda b, i: (b, i, 0, 0)))
        else:
            per = tm // dil
            steps = tile // tm
            out_shapes.append(jax.ShapeDtypeStruct((bsz, s // tile, dil, Q_BLOCK, gw), BF16))
            out_specs.append(pl.BlockSpec((None, None, dil, per, gw),
                                          lambda b, i, steps=steps: (b, i // steps, 0, i % steps, 0)))
    out_shapes.append(jax.ShapeDtypeStruct((bsz, s, GROUP_DIM), F32))
    out_specs.append(pl.BlockSpec((None, tm, GROUP_DIM), lambda b, i: (b, i, 0)))
    outs = pl.pallas_call(
        functools.partial(_attn_proj_kernel, w_unit=w_unit),
        grid=(bsz, s // tm),
        in_specs=[pl.BlockSpec((None, tm, d), lambda b, i: (b, i, 0)),
                  pl.BlockSpec((None, 3, d), lambda b, i: (b, 0, 0)),
                  pl.BlockSpec((1, d), lambda b, i: (0, 0)),
                  *w_specs,
                  pl.BlockSpec((1, GROUP_DIM), lambda b, i: (0, 0)),
                  pl.BlockSpec((1, GROUP_DIM), lambda b, i: (0, 0))],
        out_specs=out_specs,
        out_shape=out_shapes,
        scratch_shapes=[pltpu.VMEM((d // LANES, tm, LANES), F32)],
        compiler_params=_params(2),
        name="attn_proj",
    )(x, mod3, norm_w, *([w_in] * len(w_specs)), qw, kw)
    qkv = [t.reshape(bsz, s // Q_BLOCK, Q_BLOCK, gw) for t in outs[:N_GROUPS]]
    return qkv, outs[N_GROUPS]


def _stat_lane(head):
    pair, odd = divmod(head, 2)
    return pair * LSE_LANES + (0 if odd else HEAD_DIM)


def _attn_block(q, kp, kc, vp, vc, bias, o_ref, m_ref, l_ref, j):
    blk = q.shape[0]
    lane = lax.broadcasted_iota(jnp.int32, (blk, LANES), 1)
    lane_lo = lane < HEAD_DIM
    lane2_lo = lax.broadcasted_iota(jnp.int32, (2 * blk, LANES), 1) < HEAD_DIM
    zero = jnp.zeros((blk, LANES), q.dtype)
    one = jnp.ones((2 * blk, LANES), q.dtype)
    m_tile = jnp.zeros((blk, LANES), F32)
    l_tile = jnp.ones((blk, LANES), F32)
    for p in range(GROUP_DIM // LANES):
        sl = slice(p * LANES, (p + 1) * LANES)
        q2 = q[:, sl]
        qs = jnp.concatenate([jnp.where(lane_lo, q2, zero), jnp.where(lane_lo, zero, q2)], axis=0)
        k2 = jnp.concatenate([kp[:, sl], kc[:, sl]], axis=0)
        v2 = jnp.concatenate([vp[:, sl], vc[:, sl]], axis=0)
        s = lax.dot_general(qs, k2, (((1,), (1,)), ((), ())), preferred_element_type=F32) + bias
        m = jnp.max(s, axis=-1, keepdims=True)
        e = jnp.exp2(s - m).astype(BF16)
        o_even = jnp.dot(e[:blk], jnp.where(lane2_lo, v2, one), preferred_element_type=F32)
        o_odd = jnp.dot(e[blk:], jnp.where(lane2_lo, one, v2), preferred_element_type=F32)
        o_ref[j + (slice(None), sl)] = jnp.where(lane_lo, o_even, o_odd)
        for head, m_h, o_h in ((2 * p, m[:blk], o_even), (2 * p + 1, m[blk:], o_odd)):
            slot = (lane >= _stat_lane(head)) & (lane < _stat_lane(head) + LSE_LANES)
            m_tile = jnp.where(slot, m_h, m_tile)
            l_tile = jnp.where(slot, o_h, l_tile)
    m_ref[j] = m_tile
    l_ref[j] = l_tile


def _attn_kernel(q_ref, kprev_ref, k_ref, vprev_ref, v_ref, o_ref, m_ref, l_ref):
    n_blocks, n_res, blk, _ = q_ref.shape
    span = 2 * blk
    row = lax.broadcasted_iota(jnp.int32, (span, span), 0) % blk
    col = lax.broadcasted_iota(jnp.int32, (span, span), 1)
    in_window = (col >= row) & (col <= row + blk)
    bias = jnp.where(in_window, 0.0, NEG_INF)
    run_has_prev = pl.program_id(2) > 0
    bias_first = jnp.where(in_window & ((col >= blk) | run_has_prev), 0.0, NEG_INF)

    for r in range(n_res):
        _attn_block(q_ref[0, r], kprev_ref[r], k_ref[0, r], vprev_ref[r], v_ref[0, r], bias_first,
                    o_ref, m_ref, l_ref, (0, r))
        for j in range(1, n_blocks):
            _attn_block(q_ref[j, r], k_ref[j - 1, r], k_ref[j, r], v_ref[j - 1, r], v_ref[j, r], bias,
                        o_ref, m_ref, l_ref, (j, r))


def _dilated_attention(qkv, dil):
    bsz, nb, blk, width = qkv.shape
    gd = GROUP_DIM
    tiles = nb // dil
    run = min(ATTN_STEP_BLOCKS, tiles)
    res = min(ATTN_STEP_BLOCKS // run, dil)
    assert tiles % run == 0 and dil % res == 0
    view = qkv.reshape(bsz, tiles, dil, blk, width)
    cur = lambda part: pl.BlockSpec((None, run, res, blk, gd), lambda b, r, c: (b, c, r, 0, part))
    prev = lambda part: pl.BlockSpec((None, None, res, blk, gd),
                                     lambda b, r, c: (b, jnp.maximum(c * run - 1, 0), r, 0, part))
    stat_spec = pl.BlockSpec((None, run, res, blk, LANES), lambda b, r, c: (b, c, r, 0, 0))
    stat_shape = jax.ShapeDtypeStruct((bsz, tiles, dil, blk, LANES), F32)
    o, m, l = pl.pallas_call(
        _attn_kernel,
        grid=(bsz, dil // res, tiles // run),
        in_specs=[cur(0), prev(1), cur(1), prev(2), cur(2)],
        out_specs=[pl.BlockSpec((None, run, res, blk, gd), lambda b, r, c: (b, c, r, 0, 0)),
                   stat_spec, stat_spec],
        out_shape=[jax.ShapeDtypeStruct((bsz, tiles, dil, blk, gd), F32), stat_shape, stat_shape],
        compiler_params=_params(3),
        name=f"dilated_attn_d{dil}",
    )(view, view, view, view, view)
    return (o.reshape(bsz, nb, blk, gd), m.reshape(bsz, nb, blk, LANES),
            l.reshape(bsz, nb, blk, LANES))


def _to_sequence(src_ref, slab_ref, dil):
    per = src_ref.shape[1]
    n_slabs = src_ref.shape[2] // LANES
    for j in range(n_slabs):
        for r in range(dil):
            slab_ref[j, pl.ds(r, per, stride=dil), :] = src_ref[r, :, j * LANES:(j + 1) * LANES]
    return jnp.concatenate([slab_ref[j] for j in range(n_slabs)], axis=-1)


def _merge_kernel(x_ref, mod_ref, nw_ref, ya_ref, zs_ref,
                  o0_ref, o1_ref, o2_ref, m0_ref, m1_ref, m2_ref, l0_ref, l1_ref, l2_ref,
                  wga_ref, wgb_ref, pa_ref, pb_ref, wo_ref, out_ref, slab_ref):
    x = x_ref[...]
    mod = mod_ref[...]
    h = _modulated_norm(x, nw_ref[...], mod).astype(BF16)
    g_a = jax.nn.sigmoid(jnp.dot(h, wga_ref[...], preferred_element_type=F32))
    g_b = jax.nn.sigmoid(jnp.dot(h, wgb_ref[...], preferred_element_type=F32))

    def in_sequence(refs):
        return [refs[0][...]] + [_to_sequence(r, slab_ref, dil) for r, dil in zip(refs[1:], DILATIONS[1:])]

    outs = in_sequence((o0_ref, o1_ref, o2_ref))
    maxes = in_sequence((m0_ref, m1_ref, m2_ref))
    sums = in_sequence((l0_ref, l1_ref, l2_ref))
    top = jnp.maximum(jnp.maximum(maxes[0], maxes[1]), maxes[2])
    wts = [jnp.exp2(t - top) for t in maxes]
    den = wts[0] * sums[0] + wts[1] * sums[1] + wts[2] * sums[2]
    attn = jnp.zeros_like(outs[0])
    lane_lo = lax.broadcasted_iota(jnp.int32, (x.shape[0], LANES), 1) < HEAD_DIM
    for w, o in zip(wts, outs):
        wn = w / den
        wide = jnp.concatenate(
            [jnp.where(lane_lo, wn[:, _stat_lane(2 * p):_stat_lane(2 * p) + 1],
                       wn[:, _stat_lane(2 * p + 1):_stat_lane(2 * p + 1) + 1])
             for p in range(GROUP_DIM // LANES)], axis=-1)
        attn = attn + wide * o
    y_b = (attn * zs_ref[...]).astype(BF16)

    merged = (g_a * jnp.dot(ya_ref[...], pa_ref[...], preferred_element_type=F32)
              + g_b * jnp.dot(y_b, pb_ref[...], preferred_element_type=F32))
    upd = jnp.dot(merged.astype(BF16), wo_ref[...], preferred_element_type=F32)
    out_ref[...] = x + mod[2:3, :] * upd


def _merge(x, mod3, norm_w, y_a, zs, outs, maxes, sums, w_in, w_start, p_a, p_b, w_o):
    bsz, s, d = x.shape
    tm = OUT_ROW_TILE
    row_d = pl.BlockSpec((None, tm, d), lambda b, i: (b, i, 0))
    row_g = pl.BlockSpec((None, tm, GROUP_DIM), lambda b, i: (b, i, 0))
    full = lambda a: pl.BlockSpec(a.shape, lambda b, i: (0,) * a.ndim, pipeline_mode=pl.Buffered(1))
    gate_specs, gate_unit = _column_window(w_in, w_start, 2 * d)
    assert gate_unit == d

    def dilated(t, dil):
        width = t.shape[-1]
        tile = Q_BLOCK * dil
        per = tm // dil
        if tile % tm:
            raise NotImplementedError("a dilated-order tile must be a whole number of merge tiles")
        steps = tile // tm
        view = t.reshape(bsz, s // tile, dil, Q_BLOCK, width)
        spec = pl.BlockSpec((None, None, dil, per, width),
                            lambda b, i: (b, i // steps, 0, i % steps, 0))
        return view, spec

    views, specs = [], []
    for group in (outs, maxes, sums):
        for t, dil in zip(group, DILATIONS):
            width = t.shape[-1]
            if dil == 1:
                views.append(t.reshape(bsz, s, width))
                specs.append(pl.BlockSpec((None, tm, width), lambda b, i: (b, i, 0)))
            else:
                v, sp = dilated(t, dil)
                views.append(v)
                specs.append(sp)

    return pl.pallas_call(
        _merge_kernel,
        grid=(bsz, s // tm),
        in_specs=[row_d, pl.BlockSpec((None, 3, d), lambda b, i: (b, 0, 0)),
                  pl.BlockSpec((1, d), lambda b, i: (0, 0)), row_d, row_g,
                  *specs,
                  *gate_specs, full(p_a), full(p_b), full(w_o)],
        out_specs=row_d,
        out_shape=jax.ShapeDtypeStruct((bsz, s, d), F32),
        scratch_shapes=[pltpu.VMEM((GROUP_DIM // LANES, tm, LANES), F32)],
        compiler_params=_params(2),
        name="merge_out",
    )(x, mod3, norm_w, y_a, zs, *views, w_in, w_in, p_a, p_b, w_o)


def _layer(x, c, w_ada, b_ada, norm_w, w_in, conv_w, q_norm_w, k_norm_w, w_br_conv, w_br_attn, w_out):
    bsz, s, d = x.shape
    conv_dim = conv_w.shape[-1]
    attn_dim = N_GROUPS * GROUP_DIM
    assert conv_dim == d and conv_w.shape[0] == CONV_WIDTH and d % LANES == 0
    assert w_in.shape[1] == 4 * conv_dim + 3 * attn_dim + GROUP_DIM + 2 * d
    assert s % (max(DILATIONS) * Q_BLOCK) == 0 and s % ROW_TILE == 0
    assert all(w // dil == Q_BLOCK for w, dil in zip(WINDOWS, DILATIONS))

    mod3 = _modulation(c, w_ada, b_ada).reshape(bsz, 3, d)
    nw = norm_w.reshape(1, d)

    w_in = w_in.astype(BF16)
    o_attn = 4 * conv_dim
    o_gate = o_attn + 3 * attn_dim + GROUP_DIM

    y_a = _mixer_a(x, mod3, nw, w_in, conv_w)
    qkv, zs = _attn_proj(x, mod3, nw, w_in, o_attn, q_norm_w, k_norm_w)
    outs, maxes, sums = zip(*[_dilated_attention(t, dil) for t, dil in zip(qkv, DILATIONS)])
    return _merge(x, mod3, nw, y_a, zs, outs, maxes, sums, w_in, o_gate,
                  w_br_conv.astype(BF16), w_br_attn.astype(BF16), w_out.astype(BF16))


@jax.jit
def kernel(x, c, w_ada, b_ada, norm_w, w_in, conv_w, q_norm_w, k_norm_w, w_br_conv, w_br_attn, w_out):
    depth = w_ada.shape[0]
    for l in range(depth):
        x = _layer(x, c, w_ada[l], b_ada[l], norm_w[l], w_in[l], conv_w[l], q_norm_w[l],
                   k_norm_w[l], w_br_conv[l], w_br_attn[l], w_out[l])
    return x
```

```python
import math

import jax
import jax.numpy as jnp
from jax import lax
from jax.experimental import pallas as pl
from jax.experimental.pallas import tpu as pltpu

F32 = jnp.float32
BF16 = jnp.bfloat16

HEAD_DIM = 64
ATTN_SLOTS = 8
WINDOWS = (128, 512, 2048)
DILATIONS = (1, 4, 16)
N_GROUPS = len(WINDOWS)
GROUP_DIM = ATTN_SLOTS * HEAD_DIM
Q_BLOCK = 128
CONV_WIDTH = 3
EPS = 1e-6
NEG_INF = -1e30
LOG2_E = 1.4426950408889634
LANES = 128
SUBLANES = 8
LSE_LANES = LANES // ATTN_SLOTS
VMEM_LIMIT = 56 * 1024 * 1024

ROW_TILE = 512
OUT_ROW_TILE = 512
ATTN_STEP_BLOCKS = 16


def _params(n_axes):
    return pltpu.CompilerParams(dimension_semantics=("arbitrary",) * n_axes,
                                vmem_limit_bytes=VMEM_LIMIT)


def _silu(t):
    return t * jax.nn.sigmoid(t)


def _modulated_norm(x, norm_w, mod):
    ms = jnp.mean(x * x, axis=-1, keepdims=True)
    xn = x * lax.rsqrt(ms + EPS) * norm_w
    return xn * (1.0 + mod[1:2, :]) + mod[0:1, :]


def _to_dilated(slab_ref, n_rows, dil):
    per = n_rows // dil
    return jnp.concatenate(
        [jnp.concatenate([slab_ref[j, pl.ds(r, per, stride=dil), :] for r in range(dil)], axis=0)
         for j in range(slab_ref.shape[0])], axis=-1)


def _column_window(w, start, width):
    unit = math.gcd(start, width)
    specs = [pl.BlockSpec((w.shape[0], unit), lambda *_, j=start // unit + j: (0, j),
                          pipeline_mode=pl.Buffered(1)) for j in range(width // unit)]
    return specs, unit


def _mod_kernel(c_ref, w_ref, b_ref, o_ref):
    c = c_ref[...]
    o_ref[...] = jnp.dot(_silu(c), w_ref[...], preferred_element_type=F32) + b_ref[...]


def _modulation(c, w_ada, b_ada):
    bsz, d = c.shape
    n = w_ada.shape[1]
    bn = 512
    return pl.pallas_call(
        _mod_kernel,
        grid=(n // bn,),
        in_specs=[pl.BlockSpec((bsz, d), lambda j: (0, 0)),
                  pl.BlockSpec((d, bn), lambda j: (0, j)),
                  pl.BlockSpec((1, bn), lambda j: (0, j))],
        out_specs=pl.BlockSpec((bsz, bn), lambda j: (0, j)),
        out_shape=jax.ShapeDtypeStruct((bsz, n), F32),
        compiler_params=_params(1),
        name="adaln_mod",
    )(c, w_ada, b_ada.reshape(1, n))


def _head_rms(t, w):
    sq = t * t
    lane_lo = lax.broadcasted_iota(jnp.int32, (t.shape[0], LANES), 1) < HEAD_DIM
    parts = []
    for j in range(0, t.shape[-1], LANES):
        tile = sq[:, j:j + LANES]
        even = jnp.sum(jnp.where(lane_lo, tile, 0.0), axis=-1, keepdims=True)
        odd = jnp.sum(jnp.where(lane_lo, 0.0, tile), axis=-1, keepdims=True)
        parts.append(jnp.where(lane_lo, even, odd))
    ssq = jnp.concatenate(parts, axis=-1)
    return t * lax.rsqrt(ssq * (1.0 / HEAD_DIM) + EPS) * w


def _proj_kernel(x_ref, mod_ref, nw_ref, w_ref, cw_ref, qw_ref, kw_ref,
                 y_ref, qkv0_ref, qkv1_ref, qkv2_ref, zs_ref, tail_ref, slab_ref):
    tm, d = x_ref.shape
    cw = cw_ref[...]

    @pl.when(pl.program_id(1) == 0)
    def _():
        tail_ref[...] = jnp.zeros_like(tail_ref)

    prev1 = tail_ref[SUBLANES - 1:SUBLANES, :]
    prev2 = tail_ref[SUBLANES - 2:SUBLANES - 1, :]

    h = _modulated_norm(x_ref[...], nw_ref[...], mod_ref[...])
    for j in range(d // LANES):
        slab_ref[j] = h[:, j * LANES:(j + 1) * LANES]
    hb = h.astype(BF16)

    b_a, c_a, x_a, z_a = [jnp.dot(hb, w_ref[:, j * d:(j + 1) * d], preferred_element_type=F32)
                          for j in range(4)]
    u = c_a * x_a
    row = lax.broadcasted_iota(jnp.int32, (tm, d), 0)
    u_m1 = jnp.where(row == 0, prev1, pltpu.roll(u, 1, axis=0))
    u_m2 = jnp.where(row == 0, prev2, jnp.where(row == 1, prev1, pltpu.roll(u, 2, axis=0)))
    conv = cw[0:1, :] * u_m2 + cw[1:2, :] * u_m1 + cw[2:3, :] * u
    y_ref[...] = (b_a * conv * _silu(z_a)).astype(y_ref.dtype)
    tail_ref[...] = u[tm - SUBLANES:, :]

    base = 4 * d
    ad = N_GROUPS * GROUP_DIM
    for g, out_ref in enumerate((qkv0_ref, qkv1_ref, qkv2_ref)):
        hg = hb if DILATIONS[g] == 1 else _to_dilated(slab_ref, tm, DILATIONS[g]).astype(BF16)
        q, k, v = [jnp.dot(hg, w_ref[:, base + part * ad + g * GROUP_DIM:
                                     base + part * ad + (g + 1) * GROUP_DIM],
                           preferred_element_type=F32) for part in range(3)]
        qkv = jnp.concatenate([_head_rms(q, qw_ref[...]), _head_rms(k, kw_ref[...]), v], axis=-1)
        out_ref[...] = qkv.astype(out_ref.dtype).reshape(out_ref.shape)
    z = jnp.dot(hb, w_ref[:, base + 3 * ad:], preferred_element_type=F32)
    zs_ref[...] = _silu(z)


def _projections(x, mod3, norm_w, w_in, conv_w, q_norm_w, k_norm_w):
    bsz, s, d = x.shape
    tm = ROW_TILE
    qw = jnp.tile(q_norm_w * (HEAD_DIM ** -0.5 * LOG2_E), ATTN_SLOTS).reshape(1, GROUP_DIM)
    kw = jnp.tile(k_norm_w, ATTN_SLOTS).reshape(1, GROUP_DIM)
    gw = 3 * GROUP_DIM
    (w_spec,), _ = _column_window(w_in, 0, 4 * d + (3 * N_GROUPS + 1) * GROUP_DIM)
    row_spec = lambda width: pl.BlockSpec((None, tm, width), lambda b, i: (b, i, 0))
    out_shapes = [jax.ShapeDtypeStruct((bsz, s, d), BF16)]
    out_specs = [row_spec(d)]
    for dil in DILATIONS:
        tile = Q_BLOCK * dil
        assert dil == 1 or tile % tm == 0
        if tile <= tm:
            nblk = tm // Q_BLOCK
            out_shapes.append(jax.ShapeDtypeStruct((bsz, s // Q_BLOCK, Q_BLOCK, gw), BF16))
            out_specs.append(pl.BlockSpec((None, nblk, Q_BLOCK, gw), lambda b, i: (b, i, 0, 0)))
        else:
            per = tm // dil
            steps = tile // tm
            out_shapes.append(jax.ShapeDtypeStruct((bsz, s // tile, dil, Q_BLOCK, gw), BF16))
            out_specs.append(pl.BlockSpec((None, None, dil, per, gw),
                                          lambda b, i, steps=steps: (b, i // steps, 0, i % steps, 0)))
    out_shapes.append(jax.ShapeDtypeStruct((bsz, s, GROUP_DIM), F32))
    out_specs.append(row_spec(GROUP_DIM))
    outs = pl.pallas_call(
        _proj_kernel,
        grid=(bsz, s // tm),
        in_specs=[row_spec(d),
                  pl.BlockSpec((None, 3, d), lambda b, i: (b, 0, 0)),
                  pl.BlockSpec((1, d), lambda b, i: (0, 0)),
                  w_spec,
                  pl.BlockSpec(conv_w.shape, lambda b, i: (0, 0)),
                  pl.BlockSpec((1, GROUP_DIM), lambda b, i: (0, 0)),
                  pl.BlockSpec((1, GROUP_DIM), lambda b, i: (0, 0))],
        out_specs=out_specs,
        out_shape=out_shapes,
        scratch_shapes=[pltpu.VMEM((SUBLANES, d), F32), pltpu.VMEM((d // LANES, tm, LANES), F32)],
        compiler_params=_params(2),
        name="projections",
    )(x, mod3, norm_w, w_in, conv_w, qw, kw)
    qkv = [t.reshape(bsz, s // Q_BLOCK, Q_BLOCK, gw) for t in outs[1:1 + N_GROUPS]]
    return outs[0], qkv, outs[1 + N_GROUPS]


def _stat_lane(head):
    pair, odd = divmod(head, 2)
    return pair * LSE_LANES + (0 if odd else HEAD_DIM)


def _attn_block(q, kp, kc, vp, vc, bias, o_ref, m_ref, l_ref, j):
    blk = q.shape[0]
    lane = lax.broadcasted_iota(jnp.int32, (blk, LANES), 1)
    lane_lo = lane < HEAD_DIM
    lane2_lo = lax.broadcasted_iota(jnp.int32, (2 * blk, LANES), 1) < HEAD_DIM
    zero = jnp.zeros((blk, LANES), q.dtype)
    one = jnp.ones((2 * blk, LANES), q.dtype)
    m_tile = jnp.zeros((blk, LANES), F32)
    l_tile = jnp.ones((blk, LANES), F32)
    for p in range(GROUP_DIM // LANES):
        sl = slice(p * LANES, (p + 1) * LANES)
        q2 = q[:, sl]
        qs = jnp.concatenate([jnp.where(lane_lo, q2, zero), jnp.where(lane_lo, zero, q2)], axis=0)
        k2 = jnp.concatenate([kp[:, sl], kc[:, sl]], axis=0)
        v2 = jnp.concatenate([vp[:, sl], vc[:, sl]], axis=0)
        s = lax.dot_general(qs, k2, (((1,), (1,)), ((), ())), preferred_element_type=F32) + bias
        m = jnp.max(s, axis=-1, keepdims=True)
        e = jnp.exp2(s - m).astype(BF16)
        o_even = jnp.dot(e[:blk], jnp.where(lane2_lo, v2, one), preferred_element_type=F32)
        o_odd = jnp.dot(e[blk:], jnp.where(lane2_lo, one, v2), preferred_element_type=F32)
        o_ref[j + (slice(None), sl)] = jnp.where(lane_lo, o_even, o_odd)
        for head, m_h, o_h in ((2 * p, m[:blk], o_even), (2 * p + 1, m[blk:], o_odd)):
            slot = (lane >= _stat_lane(head)) & (lane < _stat_lane(head) + LSE_LANES)
            m_tile = jnp.where(slot, m_h, m_tile)
            l_tile = jnp.where(slot, o_h, l_tile)
    m_ref[j] = m_tile
    l_ref[j] = l_tile


def _attn_kernel(q_ref, kprev_ref, k_ref, vprev_ref, v_ref, o_ref, m_ref, l_ref):
    n_blocks, n_res, blk, _ = q_ref.shape
    span = 2 * blk
    row = lax.broadcasted_iota(jnp.int32, (span, span), 0) % blk
    col = lax.broadcasted_iota(jnp.int32, (span, span), 1)
    in_window = (col >= row) & (col <= row + blk)
    bias = jnp.where(in_window, 0.0, NEG_INF)
    run_has_prev = pl.program_id(2) > 0
    bias_first = jnp.where(in_window & ((col >= blk) | run_has_prev), 0.0, NEG_INF)

    for r in range(n_res):
        _attn_block(q_ref[0, r], kprev_ref[r], k_ref[0, r], vprev_ref[r], v_ref[0, r], bias_first,
                    o_ref, m_ref, l_ref, (0, r))
        for j in range(1, n_blocks):
            _attn_block(q_ref[j, r], k_ref[j - 1, r], k_ref[j, r], v_ref[j - 1, r], v_ref[j, r], bias,
                        o_ref, m_ref, l_ref, (j, r))


def _dilated_attention(qkv, dil):
    bsz, nb, blk, width = qkv.shape
    gd = GROUP_DIM
    tiles = nb // dil
    run = min(ATTN_STEP_BLOCKS, tiles)
    res = min(ATTN_STEP_BLOCKS // run, dil)
    assert tiles % run == 0 and dil % res == 0
    view = qkv.reshape(bsz, tiles, dil, blk, width)
    cur = lambda part: pl.BlockSpec((None, run, res, blk, gd), lambda b, r, c: (b, c, r, 0, part))
    prev = lambda part: pl.BlockSpec((None, None, res, blk, gd),
                                     lambda b, r, c: (b, jnp.maximum(c * run - 1, 0), r, 0, part))
    stat_spec = pl.BlockSpec((None, run, res, blk, LANES), lambda b, r, c: (b, c, r, 0, 0))
    stat_shape = jax.ShapeDtypeStruct((bsz, tiles, dil, blk, LANES), F32)
    o, m, l = pl.pallas_call(
        _attn_kernel,
        grid=(bsz, dil // res, tiles // run),
        in_specs=[cur(0), prev(1), cur(1), prev(2), cur(2)],
        out_specs=[pl.BlockSpec((None, run, res, blk, gd), lambda b, r, c: (b, c, r, 0, 0)),
                   stat_spec, stat_spec],
        out_shape=[jax.ShapeDtypeStruct((bsz, tiles, dil, blk, gd), F32), stat_shape, stat_shape],
        compiler_params=_params(3),
        name=f"dilated_attn_d{dil}",
    )(view, view, view, view, view)
    return (o.reshape(bsz, nb, blk, gd), m.reshape(bsz, nb, blk, LANES),
            l.reshape(bsz, nb, blk, LANES))


def _to_sequence(src_ref, slab_ref, dil):
    per = src_ref.shape[1]
    n_slabs = src_ref.shape[2] // LANES
    for j in range(n_slabs):
        for r in range(dil):
            slab_ref[j, pl.ds(r, per, stride=dil), :] = src_ref[r, :, j * LANES:(j + 1) * LANES]
    return jnp.concatenate([slab_ref[j] for j in range(n_slabs)], axis=-1)


def _merge_kernel(x_ref, mod_ref, nw_ref, ya_ref, zs_ref,
                  o0_ref, o1_ref, o2_ref, m0_ref, m1_ref, m2_ref, l0_ref, l1_ref, l2_ref,
                  wga_ref, wgb_ref, pa_ref, pb_ref, wo_ref, out_ref, slab_ref):
    x = x_ref[...]
    mod = mod_ref[...]
    h = _modulated_norm(x, nw_ref[...], mod).astype(BF16)
    g_a = jax.nn.sigmoid(jnp.dot(h, wga_ref[...], preferred_element_type=F32))
    g_b = jax.nn.sigmoid(jnp.dot(h, wgb_ref[...], preferred_element_type=F32))

    def in_sequence(refs):
        return [refs[0][...]] + [_to_sequence(r, slab_ref, dil) for r, dil in zip(refs[1:], DILATIONS[1:])]

    outs = in_sequence((o0_ref, o1_ref, o2_ref))
    maxes = in_sequence((m0_ref, m1_ref, m2_ref))
    sums = in_sequence((l0_ref, l1_ref, l2_ref))
    top = jnp.maximum(jnp.maximum(maxes[0], maxes[1]), maxes[2])
    wts = [jnp.exp2(t - top) for t in maxes]
    den = wts[0] * sums[0] + wts[1] * sums[1] + wts[2] * sums[2]
    attn = jnp.zeros_like(outs[0])
    lane_lo = lax.broadcasted_iota(jnp.int32, (x.shape[0], LANES), 1) < HEAD_DIM
    for w, o in zip(wts, outs):
        wn = w / den
        wide = jnp.concatenate(
            [jnp.where(lane_lo, wn[:, _stat_lane(2 * p):_stat_lane(2 * p) + 1],
                       wn[:, _stat_lane(2 * p + 1):_stat_lane(2 * p + 1) + 1])
             for p in range(GROUP_DIM // LANES)], axis=-1)
        attn = attn + wide * o
    y_b = (attn * zs_ref[...]).astype(BF16)

    merged = (g_a * jnp.dot(ya_ref[...], pa_ref[...], preferred_element_type=F32)
              + g_b * jnp.dot(y_b, pb_ref[...], preferred_element_type=F32))
    upd = jnp.dot(merged.astype(BF16), wo_ref[...], preferred_element_type=F32)
    out_ref[...] = x + mod[2:3, :] * upd


def _merge(x, mod3, norm_w, y_a, zs, outs, maxes, sums, w_in, w_start, p_a, p_b, w_o):
    bsz, s, d = x.shape
    tm = OUT_ROW_TILE
    row_d = pl.BlockSpec((None, tm, d), lambda b, i: (b, i, 0))
    row_g = pl.BlockSpec((None, tm, GROUP_DIM), lambda b, i: (b, i, 0))
    full = lambda a: pl.BlockSpec(a.shape, lambda b, i: (0,) * a.ndim, pipeline_mode=pl.Buffered(1))
    gate_specs, gate_unit = _column_window(w_in, w_start, 2 * d)
    assert gate_unit == d

    def dilated(t, dil):
        width = t.shape[-1]
        tile = Q_BLOCK * dil
        per = tm // dil
        if tile % tm:
            raise NotImplementedError("a dilated-order tile must be a whole number of merge tiles")
        steps = tile // tm
        view = t.reshape(bsz, s // tile, dil, Q_BLOCK, width)
        spec = pl.BlockSpec((None, None, dil, per, width),
                            lambda b, i: (b, i // steps, 0, i % steps, 0))
        return view, spec

    views, specs = [], []
    for group in (outs, maxes, sums):
        for t, dil in zip(group, DILATIONS):
            width = t.shape[-1]
            if dil == 1:
                views.append(t.reshape(bsz, s, width))
                specs.append(pl.BlockSpec((None, tm, width), lambda b, i: (b, i, 0)))
            else:
                v, sp = dilated(t, dil)
                views.append(v)
                specs.append(sp)

    return pl.pallas_call(
        _merge_kernel,
        grid=(bsz, s // tm),
        in_specs=[row_d, pl.BlockSpec((None, 3, d), lambda b, i: (b, 0, 0)),
                  pl.BlockSpec((1, d), lambda b, i: (0, 0)), row_d, row_g,
                  *specs,
                  *gate_specs, full(p_a), full(p_b), full(w_o)],
        out_specs=row_d,
        out_shape=jax.ShapeDtypeStruct((bsz, s, d), F32),
        scratch_shapes=[pltpu.VMEM((GROUP_DIM // LANES, tm, LANES), F32)],
        compiler_params=_params(2),
        name="merge_out",
    )(x, mod3, norm_w, y_a, zs, *views, w_in, w_in, p_a, p_b, w_o)


def _layer(x, c, w_ada, b_ada, norm_w, w_in, conv_w, q_norm_w, k_norm_w, w_br_conv, w_br_attn, w_out):
    bsz, s, d = x.shape
    conv_dim = conv_w.shape[-1]
    attn_dim = N_GROUPS * GROUP_DIM
    assert conv_dim == d and conv_w.shape[0] == CONV_WIDTH and d % LANES == 0
    assert w_in.shape[1] == 4 * conv_dim + 3 * attn_dim + GROUP_DIM + 2 * d
    assert s % (max(DILATIONS) * Q_BLOCK) == 0 and s % ROW_TILE == 0
    assert all(w // dil == Q_BLOCK for w, dil in zip(WINDOWS, DILATIONS))

    mod3 = _modulation(c, w_ada, b_ada).reshape(bsz, 3, d)
    nw = norm_w.reshape(1, d)

    w_in = w_in.astype(BF16)
    o_gate = 4 * conv_dim + 3 * attn_dim + GROUP_DIM

    y_a, qkv, zs = _projections(x, mod3, nw, w_in, conv_w, q_norm_w, k_norm_w)
    outs, maxes, sums = zip(*[_dilated_attention(t, dil) for t, dil in zip(qkv, DILATIONS)])
    return _merge(x, mod3, nw, y_a, zs, outs, maxes, sums, w_in, o_gate,
                  w_br_conv.astype(BF16), w_br_attn.astype(BF16), w_out.astype(BF16))


@jax.jit
def kernel(x, c, w_ada, b_ada, norm_w, w_in, conv_w, q_norm_w, k_norm_w, w_br_conv, w_br_attn, w_out):
    depth = w_ada.shape[0]
    for l in range(depth):
        x = _layer(x, c, w_ada[l], b_ada[l], norm_w[l], w_in[l], conv_w[l], q_norm_w[l],
                   k_norm_w[l], w_br_conv[l], w_br_attn[l], w_out[l])
    return x
```

```python
import functools
import math

import jax
import jax.numpy as jnp
from jax import lax
from jax.experimental import pallas as pl
from jax.experimental.pallas import tpu as pltpu

F32 = jnp.float32
BF16 = jnp.bfloat16

HEAD_DIM = 64
ATTN_SLOTS = 8
WINDOWS = (128, 512, 2048)
DILATIONS = (1, 4, 16)
N_GROUPS = len(WINDOWS)
GROUP_DIM = ATTN_SLOTS * HEAD_DIM
Q_BLOCK = 128
CONV_WIDTH = 3
EPS = 1e-6
NEG_INF = -1e30
LOG2_E = 1.4426950408889634
LANES = 128
SUBLANES = 8
LSE_LANES = LANES // ATTN_SLOTS
VMEM_LIMIT = 56 * 1024 * 1024

MIXER_ROW_TILE = 1024
ROW_TILE = 512
OUT_ROW_TILE = 512
ATTN_STEP_BLOCKS = 16


def _params(n_axes):
    return pltpu.CompilerParams(dimension_semantics=("arbitrary",) * n_axes,
                                vmem_limit_bytes=VMEM_LIMIT)


def _silu(t):
    return t * jax.nn.sigmoid(t)


def _modulated_norm(x, norm_w, mod):
    ms = jnp.mean(x * x, axis=-1, keepdims=True)
    xn = x * lax.rsqrt(ms + EPS) * norm_w
    return xn * (1.0 + mod[1:2, :]) + mod[0:1, :]


def _to_dilated(slab_ref, n_rows, dil):
    per = n_rows // dil
    return jnp.concatenate(
        [jnp.concatenate([slab_ref[j, pl.ds(r, per, stride=dil), :] for r in range(dil)], axis=0)
         for j in range(slab_ref.shape[0])], axis=-1)


def _column_window(w, start, width):
    unit = math.gcd(start, width)
    specs = [pl.BlockSpec((w.shape[0], unit), lambda *_, j=start // unit + j: (0, j),
                          pipeline_mode=pl.Buffered(1)) for j in range(width // unit)]
    return specs, unit


def _columns(w_refs, unit, off, width):
    assert off // unit == (off + width - 1) // unit
    return w_refs[off // unit].at[:, off % unit:off % unit + width]


def _mod_kernel(c_ref, w_ref, b_ref, o_ref):
    c = c_ref[...]
    o_ref[...] = jnp.dot(_silu(c), w_ref[...], preferred_element_type=F32) + b_ref[...]


def _modulation(c, w_ada, b_ada):
    bsz, d = c.shape
    n = w_ada.shape[1]
    bn = 512
    return pl.pallas_call(
        _mod_kernel,
        grid=(n // bn,),
        in_specs=[pl.BlockSpec((bsz, d), lambda j: (0, 0)),
                  pl.BlockSpec((d, bn), lambda j: (0, j)),
                  pl.BlockSpec((1, bn), lambda j: (0, j))],
        out_specs=pl.BlockSpec((bsz, bn), lambda j: (0, j)),
        out_shape=jax.ShapeDtypeStruct((bsz, n), F32),
        compiler_params=_params(1),
        name="adaln_mod",
    )(c, w_ada, b_ada.reshape(1, n))


def _mixer_a_kernel(x_ref, mod_ref, nw_ref, w_ref, cw_ref, y_ref, h_ref, tail_ref):
    tm, d = x_ref.shape
    mod = mod_ref[...]
    nw = nw_ref[...]
    cw = cw_ref[...]

    @pl.when(pl.program_id(1) == 0)
    def _():
        tail_ref[...] = jnp.zeros_like(tail_ref)

    prev1 = tail_ref[SUBLANES - 1:SUBLANES, :]
    prev2 = tail_ref[SUBLANES - 2:SUBLANES - 1, :]

    h = _modulated_norm(x_ref[...], nw, mod).astype(BF16)
    h_ref[...] = h
    b_a, c_a, x_a, z_a = [jnp.dot(h, w_ref[:, j * d:(j + 1) * d], preferred_element_type=F32)
                          for j in range(4)]
    u = c_a * x_a
    row = lax.broadcasted_iota(jnp.int32, (tm, d), 0)
    u_m1 = jnp.where(row == 0, prev1, pltpu.roll(u, 1, axis=0))
    u_m2 = jnp.where(row == 0, prev2, jnp.where(row == 1, prev1, pltpu.roll(u, 2, axis=0)))
    conv = cw[0:1, :] * u_m2 + cw[1:2, :] * u_m1 + cw[2:3, :] * u
    y_ref[...] = (b_a * conv * _silu(z_a)).astype(y_ref.dtype)
    tail_ref[...] = u[tm - SUBLANES:, :]


def _mixer_a(x, mod3, norm_w, w_in, conv_w):
    bsz, s, d = x.shape
    tm = MIXER_ROW_TILE
    (w_spec,), _ = _column_window(w_in, 0, 4 * d)
    rows = pl.BlockSpec((None, tm, d), lambda b, i: (b, i, 0))
    return pl.pallas_call(
        _mixer_a_kernel,
        grid=(bsz, s // tm),
        in_specs=[rows,
                  pl.BlockSpec((None, 3, d), lambda b, i: (b, 0, 0)),
                  pl.BlockSpec((1, d), lambda b, i: (0, 0)),
                  w_spec,
                  pl.BlockSpec(conv_w.shape, lambda b, i: (0, 0))],
        out_specs=[rows, rows],
        out_shape=[jax.ShapeDtypeStruct((bsz, s, d), BF16)] * 2,
        scratch_shapes=[pltpu.VMEM((SUBLANES, d), F32)],
        compiler_params=_params(2),
        name="mixer_a",
    )(x, mod3, norm_w, w_in, conv_w)


def _head_rms(t, w):
    sq = t * t
    lane_lo = lax.broadcasted_iota(jnp.int32, (t.shape[0], LANES), 1) < HEAD_DIM
    parts = []
    for j in range(0, t.shape[-1], LANES):
        tile = sq[:, j:j + LANES]
        even = jnp.sum(jnp.where(lane_lo, tile, 0.0), axis=-1, keepdims=True)
        odd = jnp.sum(jnp.where(lane_lo, 0.0, tile), axis=-1, keepdims=True)
        parts.append(jnp.where(lane_lo, even, odd))
    ssq = jnp.concatenate(parts, axis=-1)
    return t * lax.rsqrt(ssq * (1.0 / HEAD_DIM) + EPS) * w


def _attn_proj_kernel(h_ref, *refs, w_unit):
    n_w = (3 * N_GROUPS + 1) * GROUP_DIM // w_unit
    w_refs = refs[:n_w]
    qw_ref, kw_ref, qkv0_ref, qkv1_ref, qkv2_ref, zs_ref, slab_ref = refs[n_w:]
    tm, d = h_ref.shape
    h = h_ref[...]
    for j in range(d // LANES):
        slab_ref[j] = h[:, j * LANES:(j + 1) * LANES].astype(F32)
    ad = N_GROUPS * GROUP_DIM
    for g, out_ref in enumerate((qkv0_ref, qkv1_ref, qkv2_ref)):
        hg = h if DILATIONS[g] == 1 else _to_dilated(slab_ref, tm, DILATIONS[g]).astype(BF16)
        q, k, v = [jnp.dot(hg, _columns(w_refs, w_unit, part * ad + g * GROUP_DIM, GROUP_DIM)[...],
                           preferred_element_type=F32) for part in range(3)]
        qkv = jnp.concatenate([_head_rms(q, qw_ref[...]), _head_rms(k, kw_ref[...]), v], axis=-1)
        out_ref[...] = qkv.astype(out_ref.dtype).reshape(out_ref.shape)
        if g == 0:
            z = jnp.dot(hg, _columns(w_refs, w_unit, 3 * ad, GROUP_DIM)[...], preferred_element_type=F32)
            zs_ref[...] = _silu(z)


def _attn_proj(h, w_in, w_start, q_norm_w, k_norm_w):
    bsz, s, d = h.shape
    tm = ROW_TILE
    qw = jnp.tile(q_norm_w * (HEAD_DIM ** -0.5 * LOG2_E), ATTN_SLOTS).reshape(1, GROUP_DIM)
    kw = jnp.tile(k_norm_w, ATTN_SLOTS).reshape(1, GROUP_DIM)
    gw = 3 * GROUP_DIM
    w_specs, w_unit = _column_window(w_in, w_start, (3 * N_GROUPS + 1) * GROUP_DIM)
    out_shapes, out_specs = [], []
    for dil in DILATIONS:
        tile = Q_BLOCK * dil
        assert dil == 1 or tile % tm == 0
        if tile <= tm:
            nblk = tm // Q_BLOCK
            out_shapes.append(jax.ShapeDtypeStruct((bsz, s // Q_BLOCK, Q_BLOCK, gw), BF16))
            out_specs.append(pl.BlockSpec((None, nblk, Q_BLOCK, gw), lambda b, i: (b, i, 0, 0)))
        else:
            per = tm // dil
            steps = tile // tm
            out_shapes.append(jax.ShapeDtypeStruct((bsz, s // tile, dil, Q_BLOCK, gw), BF16))
            out_specs.append(pl.BlockSpec((None, None, dil, per, gw),
                                          lambda b, i, steps=steps: (b, i // steps, 0, i % steps, 0)))
    out_shapes.append(jax.ShapeDtypeStruct((bsz, s, GROUP_DIM), F32))
    out_specs.append(pl.BlockSpec((None, tm, GROUP_DIM), lambda b, i: (b, i, 0)))
    outs = pl.pallas_call(
        functools.partial(_attn_proj_kernel, w_unit=w_unit),
        grid=(bsz, s // tm),
        in_specs=[pl.BlockSpec((None, tm, d), lambda b, i: (b, i, 0)),
                  *w_specs,
                  pl.BlockSpec((1, GROUP_DIM), lambda b, i: (0, 0)),
                  pl.BlockSpec((1, GROUP_DIM), lambda b, i: (0, 0))],
        out_specs=out_specs,
        out_shape=out_shapes,
        scratch_shapes=[pltpu.VMEM((d // LANES, tm, LANES), F32)],
        compiler_params=_params(2),
        name="attn_proj",
    )(h, *([w_in] * len(w_specs)), qw, kw)
    qkv = [t.reshape(bsz, s // Q_BLOCK, Q_BLOCK, gw) for t in outs[:N_GROUPS]]
    return qkv, outs[N_GROUPS]


def _stat_lane(head):
    pair, odd = divmod(head, 2)
    return pair * LSE_LANES + (0 if odd else HEAD_DIM)


def _attn_block(q, kp, kc, vp, vc, bias, o_ref, m_ref, l_ref, j):
    blk = q.shape[0]
    lane = lax.broadcasted_iota(jnp.int32, (blk, LANES), 1)
    lane_lo = lane < HEAD_DIM
    lane2_lo = lax.broadcasted_iota(jnp.int32, (2 * blk, LANES), 1) < HEAD_DIM
    zero = jnp.zeros((blk, LANES), q.dtype)
    one = jnp.ones((2 * blk, LANES), q.dtype)
    m_tile = jnp.zeros((blk, LANES), F32)
    l_tile = jnp.ones((blk, LANES), F32)
    for p in range(GROUP_DIM // LANES):
        sl = slice(p * LANES, (p + 1) * LANES)
        q2 = q[:, sl]
        qs = jnp.concatenate([jnp.where(lane_lo, q2, zero), jnp.where(lane_lo, zero, q2)], axis=0)
        k2 = jnp.concatenate([kp[:, sl], kc[:, sl]], axis=0)
        v2 = jnp.concatenate([vp[:, sl], vc[:, sl]], axis=0)
        s = lax.dot_general(qs, k2, (((1,), (1,)), ((), ())), preferred_element_type=F32) + bias
        m = jnp.max(s, axis=-1, keepdims=True)
        e = jnp.exp2(s - m).astype(BF16)
        o_even = jnp.dot(e[:blk], jnp.where(lane2_lo, v2, one), preferred_element_type=F32)
        o_odd = jnp.dot(e[blk:], jnp.where(lane2_lo, one, v2), preferred_element_type=F32)
        o_ref[j + (slice(None), sl)] = jnp.where(lane_lo, o_even, o_odd)
        for head, m_h, o_h in ((2 * p, m[:blk], o_even), (2 * p + 1, m[blk:], o_odd)):
            slot = (lane >= _stat_lane(head)) & (lane < _stat_lane(head) + LSE_LANES)
            m_tile = jnp.where(slot, m_h, m_tile)
            l_tile = jnp.where(slot, o_h, l_tile)
    m_ref[j] = m_tile
    l_ref[j] = l_tile


def _attn_kernel(q_ref, kprev_ref, k_ref, vprev_ref, v_ref, o_ref, m_ref, l_ref):
    n_blocks, n_res, blk, _ = q_ref.shape
    span = 2 * blk
    row = lax.broadcasted_iota(jnp.int32, (span, span), 0) % blk
    col = lax.broadcasted_iota(jnp.int32, (span, span), 1)
    in_window = (col >= row) & (col <= row + blk)
    bias = jnp.where(in_window, 0.0, NEG_INF)
    run_has_prev = pl.program_id(2) > 0
    bias_first = jnp.where(in_window & ((col >= blk) | run_has_prev), 0.0, NEG_INF)

    for r in range(n_res):
        _attn_block(q_ref[0, r], kprev_ref[r], k_ref[0, r], vprev_ref[r], v_ref[0, r], bias_first,
                    o_ref, m_ref, l_ref, (0, r))
        for j in range(1, n_blocks):
            _attn_block(q_ref[j, r], k_ref[j - 1, r], k_ref[j, r], v_ref[j - 1, r], v_ref[j, r], bias,
                        o_ref, m_ref, l_ref, (j, r))


def _dilated_attention(qkv, dil):
    bsz, nb, blk, width = qkv.shape
    gd = GROUP_DIM
    tiles = nb // dil
    run = min(ATTN_STEP_BLOCKS, tiles)
    res = min(ATTN_STEP_BLOCKS // run, dil)
    assert tiles % run == 0 and dil % res == 0
    view = qkv.reshape(bsz, tiles, dil, blk, width)
    cur = lambda part: pl.BlockSpec((None, run, res, blk, gd), lambda b, r, c: (b, c, r, 0, part))
    prev = lambda part: pl.BlockSpec((None, None, res, blk, gd),
                                     lambda b, r, c: (b, jnp.maximum(c * run - 1, 0), r, 0, part))
    stat_spec = pl.BlockSpec((None, run, res, blk, LANES), lambda b, r, c: (b, c, r, 0, 0))
    stat_shape = jax.ShapeDtypeStruct((bsz, tiles, dil, blk, LANES), F32)
    o, m, l = pl.pallas_call(
        _attn_kernel,
        grid=(bsz, dil // res, tiles // run),
        in_specs=[cur(0), prev(1), cur(1), prev(2), cur(2)],
        out_specs=[pl.BlockSpec((None, run, res, blk, gd), lambda b, r, c: (b, c, r, 0, 0)),
                   stat_spec, stat_spec],
        out_shape=[jax.ShapeDtypeStruct((bsz, tiles, dil, blk, gd), F32), stat_shape, stat_shape],
        compiler_params=_params(3),
        name=f"dilated_attn_d{dil}",
    )(view, view, view, view, view)
    return (o.reshape(bsz, nb, blk, gd), m.reshape(bsz, nb, blk, LANES),
            l.reshape(bsz, nb, blk, LANES))


def _to_sequence(src_ref, slab_ref, dil):
    per = src_ref.shape[1]
    n_slabs = src_ref.shape[2] // LANES
    for j in range(n_slabs):
        for r in range(dil):
            slab_ref[j, pl.ds(r, per, stride=dil), :] = src_ref[r, :, j * LANES:(j + 1) * LANES]
    return jnp.concatenate([slab_ref[j] for j in range(n_slabs)], axis=-1)


def _merge_kernel(x_ref, mod_ref, h_ref, ya_ref, zs_ref,
                  o0_ref, o1_ref, o2_ref, m0_ref, m1_ref, m2_ref, l0_ref, l1_ref, l2_ref,
                  wga_ref, wgb_ref, pa_ref, pb_ref, wo_ref, out_ref, slab_ref):
    h = h_ref[...]
    g_a = jax.nn.sigmoid(jnp.dot(h, wga_ref[...], preferred_element_type=F32))
    g_b = jax.nn.sigmoid(jnp.dot(h, wgb_ref[...], preferred_element_type=F32))

    def in_sequence(refs):
        return [refs[0][...]] + [_to_sequence(r, slab_ref, dil) for r, dil in zip(refs[1:], DILATIONS[1:])]

    outs = in_sequence((o0_ref, o1_ref, o2_ref))
    maxes = in_sequence((m0_ref, m1_ref, m2_ref))
    sums = in_sequence((l0_ref, l1_ref, l2_ref))
    top = jnp.maximum(jnp.maximum(maxes[0], maxes[1]), maxes[2])
    wts = [jnp.exp2(t - top) for t in maxes]
    den = wts[0] * sums[0] + wts[1] * sums[1] + wts[2] * sums[2]
    attn = jnp.zeros_like(outs[0])
    lane_lo = lax.broadcasted_iota(jnp.int32, (h.shape[0], LANES), 1) < HEAD_DIM
    for w, o in zip(wts, outs):
        wn = w / den
        wide = jnp.concatenate(
            [jnp.where(lane_lo, wn[:, _stat_lane(2 * p):_stat_lane(2 * p) + 1],
                       wn[:, _stat_lane(2 * p + 1):_stat_lane(2 * p + 1) + 1])
             for p in range(GROUP_DIM // LANES)], axis=-1)
        attn = attn + wide * o
    y_b = (attn * zs_ref[...]).astype(BF16)

    merged = (g_a * jnp.dot(ya_ref[...], pa_ref[...], preferred_element_type=F32)
              + g_b * jnp.dot(y_b, pb_ref[...], preferred_element_type=F32))
    upd = jnp.dot(merged.astype(BF16), wo_ref[...], preferred_element_type=F32)
    out_ref[...] = x_ref[...] + mod_ref[2:3, :] * upd


def _merge(x, mod3, h, y_a, zs, outs, maxes, sums, w_in, w_start, p_a, p_b, w_o):
    bsz, s, d = x.shape
    tm = OUT_ROW_TILE
    row_d = pl.BlockSpec((None, tm, d), lambda b, i: (b, i, 0))
    row_g = pl.BlockSpec((None, tm, GROUP_DIM), lambda b, i: (b, i, 0))
    full = lambda a: pl.BlockSpec(a.shape, lambda b, i: (0,) * a.ndim, pipeline_mode=pl.Buffered(1))
    gate_specs, gate_unit = _column_window(w_in, w_start, 2 * d)
    assert gate_unit == d

    def dilated(t, dil):
        width = t.shape[-1]
        tile = Q_BLOCK * dil
        per = tm // dil
        if tile % tm:
            raise NotImplementedError("a dilated-order tile must be a whole number of merge tiles")
        steps = tile // tm
        view = t.reshape(bsz, s // tile, dil, Q_BLOCK, width)
        spec = pl.BlockSpec((None, None, dil, per, width),
                            lambda b, i: (b, i // steps, 0, i % steps, 0))
        return view, spec

    views, specs = [], []
    for group in (outs, maxes, sums):
        for t, dil in zip(group, DILATIONS):
            width = t.shape[-1]
            if dil == 1:
                views.append(t.reshape(bsz, s, width))
                specs.append(pl.BlockSpec((None, tm, width), lambda b, i: (b, i, 0)))
            else:
                v, sp = dilated(t, dil)
                views.append(v)
                specs.append(sp)

    return pl.pallas_call(
        _merge_kernel,
        grid=(bsz, s // tm),
        in_specs=[row_d, pl.BlockSpec((None, 3, d), lambda b, i: (b, 0, 0)), row_d, row_d, row_g,
                  *specs,
                  *gate_specs, full(p_a), full(p_b), full(w_o)],
        out_specs=row_d,
        out_shape=jax.ShapeDtypeStruct((bsz, s, d), F32),
        scratch_shapes=[pltpu.VMEM((GROUP_DIM // LANES, tm, LANES), F32)],
        compiler_params=_params(2),
        name="merge_out",
    )(x, mod3, h, y_a, zs, *views, w_in, w_in, p_a, p_b, w_o)


def _layer(x, c, w_ada, b_ada, norm_w, w_in, conv_w, q_norm_w, k_norm_w, w_br_conv, w_br_attn, w_out):
    bsz, s, d = x.shape
    conv_dim = conv_w.shape[-1]
    attn_dim = N_GROUPS * GROUP_DIM
    assert conv_dim == d and conv_w.shape[0] == CONV_WIDTH and d % LANES == 0
    assert w_in.shape[1] == 4 * conv_dim + 3 * attn_dim + GROUP_DIM + 2 * d
    assert s % (max(DILATIONS) * Q_BLOCK) == 0 and s % ROW_TILE == 0 and s % MIXER_ROW_TILE == 0
    assert all(w // dil == Q_BLOCK for w, dil in zip(WINDOWS, DILATIONS))

    mod3 = _modulation(c, w_ada, b_ada).reshape(bsz, 3, d)
    nw = norm_w.reshape(1, d)

    w_in = w_in.astype(BF16)
    o_attn = 4 * conv_dim
    o_gate = o_attn + 3 * attn_dim + GROUP_DIM

    y_a, h = _mixer_a(x, mod3, nw, w_in, conv_w)
    qkv, zs = _attn_proj(h, w_in, o_attn, q_norm_w, k_norm_w)
    outs, maxes, sums = zip(*[_dilated_attention(t, dil) for t, dil in zip(qkv, DILATIONS)])
    return _merge(x, mod3, h, y_a, zs, outs, maxes, sums, w_in, o_gate,
                  w_br_conv.astype(BF16), w_br_attn.astype(BF16), w_out.astype(BF16))


@jax.jit
def kernel(x, c, w_ada, b_ada, norm_w, w_in, conv_w, q_norm_w, k_norm_w, w_br_conv, w_br_attn, w_out):
    depth = w_ada.shape[0]
    for l in range(depth):
        x = _layer(x, c, w_ada[l], b_ada[l], norm_w[l], w_in[l], conv_w[l], q_norm_w[l],
                   k_norm_w[l], w_br_conv[l], w_br_attn[l], w_out[l])
    return x
```

```python
import functools
import math

import jax
import jax.numpy as jnp
from jax import lax
from jax.experimental import pallas as pl
from jax.experimental.pallas import tpu as pltpu

F32 = jnp.float32
BF16 = jnp.bfloat16

HEAD_DIM = 64
ATTN_SLOTS = 8
WINDOWS = (128, 512, 2048)
DILATIONS = (1, 4, 16)
N_GROUPS = len(WINDOWS)
GROUP_DIM = ATTN_SLOTS * HEAD_DIM
Q_BLOCK = 128
CONV_WIDTH = 3
EPS = 1e-6
NEG_INF = -1e30
LOG2_E = 1.4426950408889634
LANES = 128
SUBLANES = 8
LSE_LANES = LANES // ATTN_SLOTS
VMEM_LIMIT = 56 * 1024 * 1024

MOD_COL_BLOCK = 1024
MIXER_ROW_TILE = 1024
ROW_TILE = 512
OUT_ROW_TILE = 512
ATTN_STEP_BLOCKS = 16


def _params(n_axes):
    return pltpu.CompilerParams(dimension_semantics=("arbitrary",) * n_axes,
                                vmem_limit_bytes=VMEM_LIMIT)


def _silu(t):
    return t * jax.nn.sigmoid(t)


def _modulated_norm(x, norm_w, mod):
    ms = jnp.mean(x * x, axis=-1, keepdims=True)
    xn = x * lax.rsqrt(ms + EPS) * norm_w
    return xn * (1.0 + mod[1:2, :]) + mod[0:1, :]


def _to_dilated(slab_ref, n_rows, dil):
    per = n_rows // dil
    return jnp.concatenate(
        [jnp.concatenate([slab_ref[j, pl.ds(r, per, stride=dil), :] for r in range(dil)], axis=0)
         for j in range(slab_ref.shape[0])], axis=-1)


def _column_window(w, start, width):
    unit = math.gcd(start, width)
    specs = [pl.BlockSpec((w.shape[0], unit), lambda *_, j=start // unit + j: (0, j),
                          pipeline_mode=pl.Buffered(1)) for j in range(width // unit)]
    return specs, unit


def _columns(w_refs, unit, off, width):
    assert off // unit == (off + width - 1) // unit
    return w_refs[off // unit].at[:, off % unit:off % unit + width]


def _mod_kernel(c_ref, w_ref, b_ref, o_ref):
    c = c_ref[...]
    o_ref[...] = jnp.dot(_silu(c), w_ref[...], preferred_element_type=F32) + b_ref[...]


def _modulation(c, w_ada, b_ada):
    bsz, d = c.shape
    n = w_ada.shape[1]
    bn = MOD_COL_BLOCK
    return pl.pallas_call(
        _mod_kernel,
        grid=(n // bn,),
        in_specs=[pl.BlockSpec((bsz, d), lambda j: (0, 0)),
                  pl.BlockSpec((d, bn), lambda j: (0, j)),
                  pl.BlockSpec((1, bn), lambda j: (0, j))],
        out_specs=pl.BlockSpec((bsz, bn), lambda j: (0, j)),
        out_shape=jax.ShapeDtypeStruct((bsz, n), F32),
        compiler_params=_params(1),
        name="adaln_mod",
    )(c, w_ada, b_ada.reshape(1, n))


def _mixer_a_kernel(x_ref, mod_ref, nw_ref, w_ref, cw_ref, y_ref, h_ref, tail_ref):
    tm, d = x_ref.shape
    mod = mod_ref[...]
    nw = nw_ref[...]
    cw = cw_ref[...]

    @pl.when(pl.program_id(1) == 0)
    def _():
        tail_ref[...] = jnp.zeros_like(tail_ref)

    prev1 = tail_ref[SUBLANES - 1:SUBLANES, :]
    prev2 = tail_ref[SUBLANES - 2:SUBLANES - 1, :]

    h = _modulated_norm(x_ref[...], nw, mod).astype(BF16)
    h_ref[...] = h
    proj = lambda j: jnp.dot(h, w_ref[:, j * d:(j + 1) * d], preferred_element_type=F32)
    z_a, c_a, x_a = proj(3), proj(1), proj(2)
    u = c_a * x_a
    row = lax.broadcasted_iota(jnp.int32, (tm, d), 0)
    u_m1 = jnp.where(row == 0, prev1, pltpu.roll(u, 1, axis=0))
    u_m2 = jnp.where(row == 0, prev2, jnp.where(row == 1, prev1, pltpu.roll(u, 2, axis=0)))
    conv = cw[0:1, :] * u_m2 + cw[1:2, :] * u_m1 + cw[2:3, :] * u
    gated = conv * _silu(z_a)
    y_ref[...] = (proj(0) * gated).astype(y_ref.dtype)
    tail_ref[...] = u[tm - SUBLANES:, :]


def _mixer_a(x, mod3, norm_w, w_in, conv_w):
    bsz, s, d = x.shape
    tm = MIXER_ROW_TILE
    (w_spec,), _ = _column_window(w_in, 0, 4 * d)
    rows = pl.BlockSpec((None, tm, d), lambda b, i: (b, i, 0))
    return pl.pallas_call(
        _mixer_a_kernel,
        grid=(bsz, s // tm),
        in_specs=[rows,
                  pl.BlockSpec((None, 3, d), lambda b, i: (b, 0, 0)),
                  pl.BlockSpec((1, d), lambda b, i: (0, 0)),
                  w_spec,
                  pl.BlockSpec(conv_w.shape, lambda b, i: (0, 0))],
        out_specs=[rows, rows],
        out_shape=[jax.ShapeDtypeStruct((bsz, s, d), BF16)] * 2,
        scratch_shapes=[pltpu.VMEM((SUBLANES, d), F32)],
        compiler_params=_params(2),
        name="mixer_a",
    )(x, mod3, norm_w, w_in, conv_w)


def _head_rms(t, w):
    sq = t * t
    lane_lo = lax.broadcasted_iota(jnp.int32, (t.shape[0], LANES), 1) < HEAD_DIM
    parts = []
    for j in range(0, t.shape[-1], LANES):
        tile = sq[:, j:j + LANES]
        even = jnp.sum(jnp.where(lane_lo, tile, 0.0), axis=-1, keepdims=True)
        odd = jnp.sum(jnp.where(lane_lo, 0.0, tile), axis=-1, keepdims=True)
        parts.append(jnp.where(lane_lo, even, odd))
    ssq = jnp.concatenate(parts, axis=-1)
    return t * lax.rsqrt(ssq * (1.0 / HEAD_DIM) + EPS) * w


def _attn_proj_kernel(h_ref, *refs, w_unit):
    n_w = (3 * N_GROUPS + 1) * GROUP_DIM // w_unit
    w_refs = refs[:n_w]
    qw_ref, kw_ref, qkv0_ref, qkv1_ref, qkv2_ref, zs_ref, slab_ref = refs[n_w:]
    tm, d = h_ref.shape
    h = h_ref[...]
    for j in range(d // LANES):
        slab_ref[j] = h[:, j * LANES:(j + 1) * LANES].astype(F32)
    ad = N_GROUPS * GROUP_DIM
    for g, out_ref in enumerate((qkv0_ref, qkv1_ref, qkv2_ref)):
        hg = h if DILATIONS[g] == 1 else _to_dilated(slab_ref, tm, DILATIONS[g]).astype(BF16)
        q, k, v = [jnp.dot(hg, _columns(w_refs, w_unit, part * ad + g * GROUP_DIM, GROUP_DIM)[...],
                           preferred_element_type=F32) for part in range(3)]
        qkv = jnp.concatenate([_head_rms(q, qw_ref[...]), _head_rms(k, kw_ref[...]), v], axis=-1)
        out_ref[...] = qkv.astype(out_ref.dtype).reshape(out_ref.shape)
        if g == 0:
            z = jnp.dot(hg, _columns(w_refs, w_unit, 3 * ad, GROUP_DIM)[...], preferred_element_type=F32)
            zs_ref[...] = _silu(z)


def _attn_proj(h, w_in, w_start, q_norm_w, k_norm_w):
    bsz, s, d = h.shape
    tm = ROW_TILE
    qw = jnp.tile(q_norm_w * (HEAD_DIM ** -0.5 * LOG2_E), ATTN_SLOTS).reshape(1, GROUP_DIM)
    kw = jnp.tile(k_norm_w, ATTN_SLOTS).reshape(1, GROUP_DIM)
    gw = 3 * GROUP_DIM
    w_specs, w_unit = _column_window(w_in, w_start, (3 * N_GROUPS + 1) * GROUP_DIM)
    out_shapes, out_specs = [], []
    for dil in DILATIONS:
        tile = Q_BLOCK * dil
        assert dil == 1 or tile % tm == 0
        if tile <= tm:
            nblk = tm // Q_BLOCK
            out_shapes.append(jax.ShapeDtypeStruct((bsz, s // Q_BLOCK, Q_BLOCK, gw), BF16))
            out_specs.append(pl.BlockSpec((None, nblk, Q_BLOCK, gw), lambda b, i: (b, i, 0, 0)))
        else:
            per = tm // dil
            steps = tile // tm
            out_shapes.append(jax.ShapeDtypeStruct((bsz, s // tile, dil, Q_BLOCK, gw), BF16))
            out_specs.append(pl.BlockSpec((None, None, dil, per, gw),
                                          lambda b, i, steps=steps: (b, i // steps, 0, i % steps, 0)))
    out_shapes.append(jax.ShapeDtypeStruct((bsz, s, GROUP_DIM), F32))
    out_specs.append(pl.BlockSpec((None, tm, GROUP_DIM), lambda b, i: (b, i, 0)))
    outs = pl.pallas_call(
        functools.partial(_attn_proj_kernel, w_unit=w_unit),
        grid=(bsz, s // tm),
        in_specs=[pl.BlockSpec((None, tm, d), lambda b, i: (b, i, 0)),
                  *w_specs,
                  pl.BlockSpec((1, GROUP_DIM), lambda b, i: (0, 0)),
                  pl.BlockSpec((1, GROUP_DIM), lambda b, i: (0, 0))],
        out_specs=out_specs,
        out_shape=out_shapes,
        scratch_shapes=[pltpu.VMEM((d // LANES, tm, LANES), F32)],
        compiler_params=_params(2),
        name="attn_proj",
    )(h, *([w_in] * len(w_specs)), qw, kw)
    qkv = [t.reshape(bsz, s // Q_BLOCK, Q_BLOCK, gw) for t in outs[:N_GROUPS]]
    return qkv, outs[N_GROUPS]


def _stat_lane(head):
    pair, odd = divmod(head, 2)
    return pair * LSE_LANES + (0 if odd else HEAD_DIM)


def _attn_block(q, kp, kc, vp, vc, bias, o_ref, m_ref, l_ref, j):
    blk = q.shape[0]
    lane = lax.broadcasted_iota(jnp.int32, (blk, LANES), 1)
    lane_lo = lane < HEAD_DIM
    lane2_lo = lax.broadcasted_iota(jnp.int32, (2 * blk, LANES), 1) < HEAD_DIM
    zero = jnp.zeros((blk, LANES), q.dtype)
    one = jnp.ones((2 * blk, LANES), q.dtype)
    m_tile = jnp.zeros((blk, LANES), F32)
    l_tile = jnp.ones((blk, LANES), F32)
    for p in range(GROUP_DIM // LANES):
        sl = slice(p * LANES, (p + 1) * LANES)
        q2 = q[:, sl]
        qs = jnp.concatenate([jnp.where(lane_lo, q2, zero), jnp.where(lane_lo, zero, q2)], axis=0)
        k2 = jnp.concatenate([kp[:, sl], kc[:, sl]], axis=0)
        v2 = jnp.concatenate([vp[:, sl], vc[:, sl]], axis=0)
        s = lax.dot_general(qs, k2, (((1,), (1,)), ((), ())), preferred_element_type=F32) + bias
        m = jnp.max(s, axis=-1, keepdims=True)
        e = jnp.exp2(s - m).astype(BF16)
        o_even = jnp.dot(e[:blk], jnp.where(lane2_lo, v2, one), preferred_element_type=F32)
        o_odd = jnp.dot(e[blk:], jnp.where(lane2_lo, one, v2), preferred_element_type=F32)
        o_ref[j + (slice(None), sl)] = jnp.where(lane_lo, o_even, o_odd)
        for head, m_h, o_h in ((2 * p, m[:blk], o_even), (2 * p + 1, m[blk:], o_odd)):
            slot = (lane >= _stat_lane(head)) & (lane < _stat_lane(head) + LSE_LANES)
            m_tile = jnp.where(slot, m_h, m_tile)
            l_tile = jnp.where(slot, o_h, l_tile)
    m_ref[j] = m_tile
    l_ref[j] = l_tile


def _attn_kernel(q_ref, kprev_ref, k_ref, vprev_ref, v_ref, o_ref, m_ref, l_ref):
    n_blocks, n_res, blk, _ = q_ref.shape
    span = 2 * blk
    row = lax.broadcasted_iota(jnp.int32, (span, span), 0) % blk
    col = lax.broadcasted_iota(jnp.int32, (span, span), 1)
    in_window = (col >= row) & (col <= row + blk)
    bias = jnp.where(in_window, 0.0, NEG_INF)
    run_has_prev = pl.program_id(2) > 0
    bias_first = jnp.where(in_window & ((col >= blk) | run_has_prev), 0.0, NEG_INF)

    for r in range(n_res):
        _attn_block(q_ref[0, r], kprev_ref[r], k_ref[0, r], vprev_ref[r], v_ref[0, r], bias_first,
                    o_ref, m_ref, l_ref, (0, r))
        for j in range(1, n_blocks):
            _attn_block(q_ref[j, r], k_ref[j - 1, r], k_ref[j, r], v_ref[j - 1, r], v_ref[j, r], bias,
                        o_ref, m_ref, l_ref, (j, r))


def _dilated_attention(qkv, dil):
    bsz, nb, blk, width = qkv.shape
    gd = GROUP_DIM
    tiles = nb // dil
    run = min(ATTN_STEP_BLOCKS, tiles)
    res = min(ATTN_STEP_BLOCKS // run, dil)
    assert tiles % run == 0 and dil % res == 0
    view = qkv.reshape(bsz, tiles, dil, blk, width)
    cur = lambda part: pl.BlockSpec((None, run, res, blk, gd), lambda b, r, c: (b, c, r, 0, part))
    prev = lambda part: pl.BlockSpec((None, None, res, blk, gd),
                                     lambda b, r, c: (b, jnp.maximum(c * run - 1, 0), r, 0, part))
    stat_spec = pl.BlockSpec((None, run, res, blk, LANES), lambda b, r, c: (b, c, r, 0, 0))
    stat_shape = jax.ShapeDtypeStruct((bsz, tiles, dil, blk, LANES), F32)
    o, m, l = pl.pallas_call(
        _attn_kernel,
        grid=(bsz, dil // res, tiles // run),
        in_specs=[cur(0), prev(1), cur(1), prev(2), cur(2)],
        out_specs=[pl.BlockSpec((None, run, res, blk, gd), lambda b, r, c: (b, c, r, 0, 0)),
                   stat_spec, stat_spec],
        out_shape=[jax.ShapeDtypeStruct((bsz, tiles, dil, blk, gd), F32), stat_shape, stat_shape],
        compiler_params=_params(3),
        name=f"dilated_attn_d{dil}",
    )(view, view, view, view, view)
    return (o.reshape(bsz, nb, blk, gd), m.reshape(bsz, nb, blk, LANES),
            l.reshape(bsz, nb, blk, LANES))


def _to_sequence(src_ref, slab_ref, dil):
    per = src_ref.shape[1]
    n_slabs = src_ref.shape[2] // LANES
    for j in range(n_slabs):
        for r in range(dil):
            slab_ref[j, pl.ds(r, per, stride=dil), :] = src_ref[r, :, j * LANES:(j + 1) * LANES]
    return jnp.concatenate([slab_ref[j] for j in range(n_slabs)], axis=-1)


def _merge_kernel(x_ref, mod_ref, h_ref, ya_ref, zs_ref,
                  o0_ref, o1_ref, o2_ref, m0_ref, m1_ref, m2_ref, l0_ref, l1_ref, l2_ref,
                  wga_ref, wgb_ref, pa_ref, pb_ref, wo_ref, out_ref, slab_ref):
    h = h_ref[...]
    g_a = jax.nn.sigmoid(jnp.dot(h, wga_ref[...], preferred_element_type=F32))
    g_b = jax.nn.sigmoid(jnp.dot(h, wgb_ref[...], preferred_element_type=F32))

    def in_sequence(refs):
        return [refs[0][...]] + [_to_sequence(r, slab_ref, dil) for r, dil in zip(refs[1:], DILATIONS[1:])]

    outs = in_sequence((o0_ref, o1_ref, o2_ref))
    maxes = in_sequence((m0_ref, m1_ref, m2_ref))
    sums = in_sequence((l0_ref, l1_ref, l2_ref))
    top = jnp.maximum(jnp.maximum(maxes[0], maxes[1]), maxes[2])
    wts = [jnp.exp2(t - top) for t in maxes]
    den = wts[0] * sums[0] + wts[1] * sums[1] + wts[2] * sums[2]
    attn = jnp.zeros_like(outs[0])
    lane_lo = lax.broadcasted_iota(jnp.int32, (h.shape[0], LANES), 1) < HEAD_DIM
    for w, o in zip(wts, outs):
        wn = w / den
        wide = jnp.concatenate(
            [jnp.where(lane_lo, wn[:, _stat_lane(2 * p):_stat_lane(2 * p) + 1],
                       wn[:, _stat_lane(2 * p + 1):_stat_lane(2 * p + 1) + 1])
             for p in range(GROUP_DIM // LANES)], axis=-1)
        attn = attn + wide * o
    y_b = (attn * zs_ref[...]).astype(BF16)

    merged = (g_a * jnp.dot(ya_ref[...], pa_ref[...], preferred_element_type=F32)
              + g_b * jnp.dot(y_b, pb_ref[...], preferred_element_type=F32))
    upd = jnp.dot(merged.astype(BF16), wo_ref[...], preferred_element_type=F32)
    out_ref[...] = x_ref[...] + mod_ref[2:3, :] * upd


def _merge(x, mod3, h, y_a, zs, outs, maxes, sums, w_in, w_start, p_a, p_b, w_o):
    bsz, s, d = x.shape
    tm = OUT_ROW_TILE
    row_d = pl.BlockSpec((None, tm, d), lambda b, i: (b, i, 0))
    row_g = pl.BlockSpec((None, tm, GROUP_DIM), lambda b, i: (b, i, 0))
    full = lambda a: pl.BlockSpec(a.shape, lambda b, i: (0,) * a.ndim, pipeline_mode=pl.Buffered(1))
    gate_specs, gate_unit = _column_window(w_in, w_start, 2 * d)
    assert gate_unit == d

    def dilated(t, dil):
        width = t.shape[-1]
        tile = Q_BLOCK * dil
        per = tm // dil
        if tile % tm:
            raise NotImplementedError("a dilated-order tile must be a whole number of merge tiles")
        steps = tile // tm
        view = t.reshape(bsz, s // tile, dil, Q_BLOCK, width)
        spec = pl.BlockSpec((None, None, dil, per, width),
                            lambda b, i: (b, i // steps, 0, i % steps, 0))
        return view, spec

    views, specs = [], []
    for group in (outs, maxes, sums):
        for t, dil in zip(group, DILATIONS):
            width = t.shape[-1]
            if dil == 1:
                views.append(t.reshape(bsz, s, width))
                specs.append(pl.BlockSpec((None, tm, width), lambda b, i: (b, i, 0)))
            else:
                v, sp = dilated(t, dil)
                views.append(v)
                specs.append(sp)

    return pl.pallas_call(
        _merge_kernel,
        grid=(bsz, s // tm),
        in_specs=[row_d, pl.BlockSpec((None, 3, d), lambda b, i: (b, 0, 0)), row_d, row_d, row_g,
                  *specs,
                  *gate_specs, full(p_a), full(p_b), full(w_o)],
        out_specs=row_d,
        out_shape=jax.ShapeDtypeStruct((bsz, s, d), F32),
        scratch_shapes=[pltpu.VMEM((GROUP_DIM // LANES, tm, LANES), F32)],
        compiler_params=_params(2),
        name="merge_out",
    )(x, mod3, h, y_a, zs, *views, w_in, w_in, p_a, p_b, w_o)


def _layer(x, c, w_ada, b_ada, norm_w, w_in, conv_w, q_norm_w, k_norm_w, w_br_conv, w_br_attn, w_out):
    bsz, s, d = x.shape
    conv_dim = conv_w.shape[-1]
    attn_dim = N_GROUPS * GROUP_DIM
    assert conv_dim == d and conv_w.shape[0] == CONV_WIDTH and d % LANES == 0
    assert w_in.shape[1] == 4 * conv_dim + 3 * attn_dim + GROUP_DIM + 2 * d
    assert s % (max(DILATIONS) * Q_BLOCK) == 0 and s % ROW_TILE == 0 and s % MIXER_ROW_TILE == 0
    assert all(w // dil == Q_BLOCK for w, dil in zip(WINDOWS, DILATIONS))

    mod3 = _modulation(c, w_ada, b_ada).reshape(bsz, 3, d)
    nw = norm_w.reshape(1, d)

    w_in = w_in.astype(BF16)
    o_attn = 4 * conv_dim
    o_gate = o_attn + 3 * attn_dim + GROUP_DIM

    y_a, h = _mixer_a(x, mod3, nw, w_in, conv_w)
    qkv, zs = _attn_proj(h, w_in, o_attn, q_norm_w, k_norm_w)
    outs, maxes, sums = zip(*[_dilated_attention(t, dil) for t, dil in zip(qkv, DILATIONS)])
    return _merge(x, mod3, h, y_a, zs, outs, maxes, sums, w_in, o_gate,
                  w_br_conv.astype(BF16), w_br_attn.astype(BF16), w_out.astype(BF16))


@jax.jit
def kernel(x, c, w_ada, b_ada, norm_w, w_in, conv_w, q_norm_w, k_norm_w, w_br_conv, w_br_attn, w_out):
    depth = w_ada.shape[0]
    for l in range(depth):
        x = _layer(x, c, w_ada[l], b_ada[l], norm_w[l], w_in[l], conv_w[l], q_norm_w[l],
                   k_norm_w[l], w_br_conv[l], w_br_attn[l], w_out[l])
    return x
```

```python
import functools
import math

import jax
import jax.numpy as jnp
from jax import lax
from jax.experimental import pallas as pl
from jax.experimental.pallas import tpu as pltpu

F32 = jnp.float32
BF16 = jnp.bfloat16

HEAD_DIM = 64
ATTN_SLOTS = 8
WINDOWS = (128, 512, 2048)
DILATIONS = (1, 4, 16)
N_GROUPS = len(WINDOWS)
GROUP_DIM = ATTN_SLOTS * HEAD_DIM
Q_BLOCK = 128
CONV_WIDTH = 3
EPS = 1e-6
NEG_INF = -1e30
LOG2_E = 1.4426950408889634
LANES = 128
SUBLANES = 8
LSE_LANES = LANES // ATTN_SLOTS
VMEM_LIMIT = 56 * 1024 * 1024

MOD_COL_BLOCK = 1024
MIXER_ROW_TILE = 1024
MIXER_COL_BLOCK = 256
ROW_TILE = 512
OUT_ROW_TILE = 512
ATTN_STEP_BLOCKS = 16


def _params(n_axes):
    return pltpu.CompilerParams(dimension_semantics=("arbitrary",) * n_axes,
                                vmem_limit_bytes=VMEM_LIMIT)


def _silu(t):
    return t * jax.nn.sigmoid(t)


def _modulated_norm(x, norm_w, mod):
    ms = jnp.mean(x * x, axis=-1, keepdims=True)
    xn = x * lax.rsqrt(ms + EPS) * norm_w
    return xn * (1.0 + mod[1:2, :]) + mod[0:1, :]


def _to_dilated(slab_ref, n_rows, dil):
    per = n_rows // dil
    return jnp.concatenate(
        [jnp.concatenate([slab_ref[j, pl.ds(r, per, stride=dil), :] for r in range(dil)], axis=0)
         for j in range(slab_ref.shape[0])], axis=-1)


def _column_window(w, start, width):
    unit = math.gcd(start, width)
    specs = [pl.BlockSpec((w.shape[0], unit), lambda *_, j=start // unit + j: (0, j),
                          pipeline_mode=pl.Buffered(1)) for j in range(width // unit)]
    return specs, unit


def _columns(w_refs, unit, off, width):
    assert off // unit == (off + width - 1) // unit
    return w_refs[off // unit].at[:, off % unit:off % unit + width]


def _mod_kernel(c_ref, w_ref, b_ref, o_ref):
    c = c_ref[...]
    o_ref[...] = jnp.dot(_silu(c), w_ref[...], preferred_element_type=F32) + b_ref[...]


def _modulation(c, w_ada, b_ada):
    bsz, d = c.shape
    n = w_ada.shape[1]
    bn = MOD_COL_BLOCK
    return pl.pallas_call(
        _mod_kernel,
        grid=(n // bn,),
        in_specs=[pl.BlockSpec((bsz, d), lambda j: (0, 0)),
                  pl.BlockSpec((d, bn), lambda j: (0, j)),
                  pl.BlockSpec((1, bn), lambda j: (0, j))],
        out_specs=pl.BlockSpec((bsz, bn), lambda j: (0, j)),
        out_shape=jax.ShapeDtypeStruct((bsz, n), F32),
        compiler_params=_params(1),
        name="adaln_mod",
    )(c, w_ada, b_ada.reshape(1, n))


def _mixer_a_kernel(x_ref, mod_ref, nw_ref, w_ref, cw_ref, y_ref, h_ref, tail_ref):
    tm, d = x_ref.shape
    mod = mod_ref[...]
    nw = nw_ref[...]
    cw = cw_ref[...]

    @pl.when(pl.program_id(1) == 0)
    def _():
        tail_ref[...] = jnp.zeros_like(tail_ref)

    prev1 = tail_ref[SUBLANES - 1:SUBLANES, :]
    prev2 = tail_ref[SUBLANES - 2:SUBLANES - 1, :]

    h = _modulated_norm(x_ref[...], nw, mod).astype(BF16)
    h_ref[...] = h
    cb = MIXER_COL_BLOCK
    row = lax.broadcasted_iota(jnp.int32, (tm, cb), 0)
    for c0 in range(0, d, cb):
        proj = lambda j: jnp.dot(h, w_ref[:, j * d + c0:j * d + c0 + cb], preferred_element_type=F32)
        z_a, c_a, x_a = proj(3), proj(1), proj(2)
        u = c_a * x_a
        p1, p2 = prev1[:, c0:c0 + cb], prev2[:, c0:c0 + cb]
        u_m1 = jnp.where(row == 0, p1, pltpu.roll(u, 1, axis=0))
        u_m2 = jnp.where(row == 0, p2, jnp.where(row == 1, p1, pltpu.roll(u, 2, axis=0)))
        conv = (cw[0:1, c0:c0 + cb] * u_m2 + cw[1:2, c0:c0 + cb] * u_m1 + cw[2:3, c0:c0 + cb] * u)
        gated = conv * _silu(z_a)
        y_ref[:, c0:c0 + cb] = (proj(0) * gated).astype(y_ref.dtype)
        tail_ref[:, c0:c0 + cb] = u[tm - SUBLANES:, :]


def _mixer_a(x, mod3, norm_w, w_in, conv_w):
    bsz, s, d = x.shape
    tm = MIXER_ROW_TILE
    (w_spec,), _ = _column_window(w_in, 0, 4 * d)
    rows = pl.BlockSpec((None, tm, d), lambda b, i: (b, i, 0))
    return pl.pallas_call(
        _mixer_a_kernel,
        grid=(bsz, s // tm),
        in_specs=[rows,
                  pl.BlockSpec((None, 3, d), lambda b, i: (b, 0, 0)),
                  pl.BlockSpec((1, d), lambda b, i: (0, 0)),
                  w_spec,
                  pl.BlockSpec(conv_w.shape, lambda b, i: (0, 0))],
        out_specs=[rows, rows],
        out_shape=[jax.ShapeDtypeStruct((bsz, s, d), BF16)] * 2,
        scratch_shapes=[pltpu.VMEM((SUBLANES, d), F32)],
        compiler_params=_params(2),
        name="mixer_a",
    )(x, mod3, norm_w, w_in, conv_w)


def _head_rms(t, w):
    sq = t * t
    lane_lo = lax.broadcasted_iota(jnp.int32, (t.shape[0], LANES), 1) < HEAD_DIM
    parts = []
    for j in range(0, t.shape[-1], LANES):
        tile = sq[:, j:j + LANES]
        even = jnp.sum(jnp.where(lane_lo, tile, 0.0), axis=-1, keepdims=True)
        odd = jnp.sum(jnp.where(lane_lo, 0.0, tile), axis=-1, keepdims=True)
        parts.append(jnp.where(lane_lo, even, odd))
    ssq = jnp.concatenate(parts, axis=-1)
    return t * lax.rsqrt(ssq * (1.0 / HEAD_DIM) + EPS) * w


def _attn_proj_kernel(h_ref, *refs, w_unit):
    n_w = (3 * N_GROUPS + 1) * GROUP_DIM // w_unit
    w_refs = refs[:n_w]
    qw_ref, kw_ref, qkv0_ref, qkv1_ref, qkv2_ref, zs_ref, slab_ref = refs[n_w:]
    tm, d = h_ref.shape
    h = h_ref[...]
    for j in range(d // LANES):
        slab_ref[j] = h[:, j * LANES:(j + 1) * LANES].astype(F32)
    ad = N_GROUPS * GROUP_DIM
    for g, out_ref in enumerate((qkv0_ref, qkv1_ref, qkv2_ref)):
        hg = h if DILATIONS[g] == 1 else _to_dilated(slab_ref, tm, DILATIONS[g]).astype(BF16)
        q, k, v = [jnp.dot(hg, _columns(w_refs, w_unit, part * ad + g * GROUP_DIM, GROUP_DIM)[...],
                           preferred_element_type=F32) for part in range(3)]
        qkv = jnp.concatenate([_head_rms(q, qw_ref[...]), _head_rms(k, kw_ref[...]), v], axis=-1)
        out_ref[...] = qkv.astype(out_ref.dtype).reshape(out_ref.shape)
        if g == 0:
            z = jnp.dot(hg, _columns(w_refs, w_unit, 3 * ad, GROUP_DIM)[...], preferred_element_type=F32)
            zs_ref[...] = _silu(z)


def _attn_proj(h, w_in, w_start, q_norm_w, k_norm_w):
    bsz, s, d = h.shape
    tm = ROW_TILE
    qw = jnp.tile(q_norm_w * (HEAD_DIM ** -0.5 * LOG2_E), ATTN_SLOTS).reshape(1, GROUP_DIM)
    kw = jnp.tile(k_norm_w, ATTN_SLOTS).reshape(1, GROUP_DIM)
    gw = 3 * GROUP_DIM
    w_specs, w_unit = _column_window(w_in, w_start, (3 * N_GROUPS + 1) * GROUP_DIM)
    out_shapes, out_specs = [], []
    for dil in DILATIONS:
        tile = Q_BLOCK * dil
        assert dil == 1 or tile % tm == 0
        if tile <= tm:
            nblk = tm // Q_BLOCK
            out_shapes.append(jax.ShapeDtypeStruct((bsz, s // Q_BLOCK, Q_BLOCK, gw), BF16))
            out_specs.append(pl.BlockSpec((None, nblk, Q_BLOCK, gw), lambda b, i: (b, i, 0, 0)))
        else:
            per = tm // dil
            steps = tile // tm
            out_shapes.append(jax.ShapeDtypeStruct((bsz, s // tile, dil, Q_BLOCK, gw), BF16))
            out_specs.append(pl.BlockSpec((None, None, dil, per, gw),
                                          lambda b, i, steps=steps: (b, i // steps, 0, i % steps, 0)))
    out_shapes.append(jax.ShapeDtypeStruct((bsz, s, GROUP_DIM), F32))
    out_specs.append(pl.BlockSpec((None, tm, GROUP_DIM), lambda b, i: (b, i, 0)))
    outs = pl.pallas_call(
        functools.partial(_attn_proj_kernel, w_unit=w_unit),
        grid=(bsz, s // tm),
        in_specs=[pl.BlockSpec((None, tm, d), lambda b, i: (b, i, 0)),
                  *w_specs,
                  pl.BlockSpec((1, GROUP_DIM), lambda b, i: (0, 0)),
                  pl.BlockSpec((1, GROUP_DIM), lambda b, i: (0, 0))],
        out_specs=out_specs,
        out_shape=out_shapes,
        scratch_shapes=[pltpu.VMEM((d // LANES, tm, LANES), F32)],
        compiler_params=_params(2),
        name="attn_proj",
    )(h, *([w_in] * len(w_specs)), qw, kw)
    qkv = [t.reshape(bsz, s // Q_BLOCK, Q_BLOCK, gw) for t in outs[:N_GROUPS]]
    return qkv, outs[N_GROUPS]


def _stat_lane(head):
    pair, odd = divmod(head, 2)
    return pair * LSE_LANES + (0 if odd else HEAD_DIM)


def _attn_block(q, kp, kc, vp, vc, bias, o_ref, m_ref, l_ref, j):
    blk = q.shape[0]
    lane = lax.broadcasted_iota(jnp.int32, (blk, LANES), 1)
    lane_lo = lane < HEAD_DIM
    lane2_lo = lax.broadcasted_iota(jnp.int32, (2 * blk, LANES), 1) < HEAD_DIM
    zero = jnp.zeros((blk, LANES), q.dtype)
    one = jnp.ones((2 * blk, LANES), q.dtype)
    m_tile = jnp.zeros((blk, LANES), F32)
    l_tile = jnp.ones((blk, LANES), F32)
    for p in range(GROUP_DIM // LANES):
        sl = slice(p * LANES, (p + 1) * LANES)
        q2 = q[:, sl]
        qs = jnp.concatenate([jnp.where(lane_lo, q2, zero), jnp.where(lane_lo, zero, q2)], axis=0)
        k2 = jnp.concatenate([kp[:, sl], kc[:, sl]], axis=0)
        v2 = jnp.concatenate([vp[:, sl], vc[:, sl]], axis=0)
        s = lax.dot_general(qs, k2, (((1,), (1,)), ((), ())), preferred_element_type=F32) + bias
        m = jnp.max(s, axis=-1, keepdims=True)
        e = jnp.exp2(s - m).astype(BF16)
        o_even = jnp.dot(e[:blk], jnp.where(lane2_lo, v2, one), preferred_element_type=F32)
        o_odd = jnp.dot(e[blk:], jnp.where(lane2_lo, one, v2), preferred_element_type=F32)
        o_ref[j + (slice(None), sl)] = jnp.where(lane_lo, o_even, o_odd)
        for head, m_h, o_h in ((2 * p, m[:blk], o_even), (2 * p + 1, m[blk:], o_odd)):
            slot = (lane >= _stat_lane(head)) & (lane < _stat_lane(head) + LSE_LANES)
            m_tile = jnp.where(slot, m_h, m_tile)
            l_tile = jnp.where(slot, o_h, l_tile)
    m_ref[j] = m_tile
    l_ref[j] = l_tile


def _attn_kernel(q_ref, kprev_ref, k_ref, vprev_ref, v_ref, o_ref, m_ref, l_ref):
    n_blocks, n_res, blk, _ = q_ref.shape
    span = 2 * blk
    row = lax.broadcasted_iota(jnp.int32, (span, span), 0) % blk
    col = lax.broadcasted_iota(jnp.int32, (span, span), 1)
    in_window = (col >= row) & (col <= row + blk)
    bias = jnp.where(in_window, 0.0, NEG_INF)
    run_has_prev = pl.program_id(2) > 0
    bias_first = jnp.where(in_window & ((col >= blk) | run_has_prev), 0.0, NEG_INF)

    for r in range(n_res):
        _attn_block(q_ref[0, r], kprev_ref[r], k_ref[0, r], vprev_ref[r], v_ref[0, r], bias_first,
                    o_ref, m_ref, l_ref, (0, r))
        for j in range(1, n_blocks):
            _attn_block(q_ref[j, r], k_ref[j - 1, r], k_ref[j, r], v_ref[j - 1, r], v_ref[j, r], bias,
                        o_ref, m_ref, l_ref, (j, r))


def _dilated_attention(qkv, dil):
    bsz, nb, blk, width = qkv.shape
    gd = GROUP_DIM
    tiles = nb // dil
    run = min(ATTN_STEP_BLOCKS, tiles)
    res = min(ATTN_STEP_BLOCKS // run, dil)
    assert tiles % run == 0 and dil % res == 0
    view = qkv.reshape(bsz, tiles, dil, blk, width)
    cur = lambda part: pl.BlockSpec((None, run, res, blk, gd), lambda b, r, c: (b, c, r, 0, part))
    prev = lambda part: pl.BlockSpec((None, None, res, blk, gd),
                                     lambda b, r, c: (b, jnp.maximum(c * run - 1, 0), r, 0, part))
    stat_spec = pl.BlockSpec((None, run, res, blk, LANES), lambda b, r, c: (b, c, r, 0, 0))
    stat_shape = jax.ShapeDtypeStruct((bsz, tiles, dil, blk, LANES), F32)
    o, m, l = pl.pallas_call(
        _attn_kernel,
        grid=(bsz, dil // res, tiles // run),
        in_specs=[cur(0), prev(1), cur(1), prev(2), cur(2)],
        out_specs=[pl.BlockSpec((None, run, res, blk, gd), lambda b, r, c: (b, c, r, 0, 0)),
                   stat_spec, stat_spec],
        out_shape=[jax.ShapeDtypeStruct((bsz, tiles, dil, blk, gd), F32), stat_shape, stat_shape],
        compiler_params=_params(3),
        name=f"dilated_attn_d{dil}",
    )(view, view, view, view, view)
    return (o.reshape(bsz, nb, blk, gd), m.reshape(bsz, nb, blk, LANES),
            l.reshape(bsz, nb, blk, LANES))


def _to_sequence(src_ref, slab_ref, dil):
    per = src_ref.shape[1]
    n_slabs = src_ref.shape[2] // LANES
    for j in range(n_slabs):
        for r in range(dil):
            slab_ref[j, pl.ds(r, per, stride=dil), :] = src_ref[r, :, j * LANES:(j + 1) * LANES]
    return jnp.concatenate([slab_ref[j] for j in range(n_slabs)], axis=-1)


def _merge_kernel(x_ref, mod_ref, h_ref, ya_ref, zs_ref,
                  o0_ref, o1_ref, o2_ref, m0_ref, m1_ref, m2_ref, l0_ref, l1_ref, l2_ref,
                  wga_ref, wgb_ref, pa_ref, pb_ref, wo_ref, out_ref, slab_ref):
    h = h_ref[...]
    g_a = jax.nn.sigmoid(jnp.dot(h, wga_ref[...], preferred_element_type=F32))
    g_b = jax.nn.sigmoid(jnp.dot(h, wgb_ref[...], preferred_element_type=F32))

    def in_sequence(refs):
        return [refs[0][...]] + [_to_sequence(r, slab_ref, dil) for r, dil in zip(refs[1:], DILATIONS[1:])]

    outs = in_sequence((o0_ref, o1_ref, o2_ref))
    maxes = in_sequence((m0_ref, m1_ref, m2_ref))
    sums = in_sequence((l0_ref, l1_ref, l2_ref))
    top = jnp.maximum(jnp.maximum(maxes[0], maxes[1]), maxes[2])
    wts = [jnp.exp2(t - top) for t in maxes]
    den = wts[0] * sums[0] + wts[1] * sums[1] + wts[2] * sums[2]
    attn = jnp.zeros_like(outs[0])
    lane_lo = lax.broadcasted_iota(jnp.int32, (h.shape[0], LANES), 1) < HEAD_DIM
    for w, o in zip(wts, outs):
        wn = w / den
        wide = jnp.concatenate(
            [jnp.where(lane_lo, wn[:, _stat_lane(2 * p):_stat_lane(2 * p) + 1],
                       wn[:, _stat_lane(2 * p + 1):_stat_lane(2 * p + 1) + 1])
             for p in range(GROUP_DIM // LANES)], axis=-1)
        attn = attn + wide * o
    y_b = (attn * zs_ref[...]).astype(BF16)

    merged = (g_a * jnp.dot(ya_ref[...], pa_ref[...], preferred_element_type=F32)
              + g_b * jnp.dot(y_b, pb_ref[...], preferred_element_type=F32))
    upd = jnp.dot(merged.astype(BF16), wo_ref[...], preferred_element_type=F32)
    out_ref[...] = x_ref[...] + mod_ref[2:3, :] * upd


def _merge(x, mod3, h, y_a, zs, outs, maxes, sums, w_in, w_start, p_a, p_b, w_o):
    bsz, s, d = x.shape
    tm = OUT_ROW_TILE
    row_d = pl.BlockSpec((None, tm, d), lambda b, i: (b, i, 0))
    row_g = pl.BlockSpec((None, tm, GROUP_DIM), lambda b, i: (b, i, 0))
    full = lambda a: pl.BlockSpec(a.shape, lambda b, i: (0,) * a.ndim, pipeline_mode=pl.Buffered(1))
    gate_specs, gate_unit = _column_window(w_in, w_start, 2 * d)
    assert gate_unit == d

    def dilated(t, dil):
        width = t.shape[-1]
        tile = Q_BLOCK * dil
        per = tm // dil
        if tile % tm:
            raise NotImplementedError("a dilated-order tile must be a whole number of merge tiles")
        steps = tile // tm
        view = t.reshape(bsz, s // tile, dil, Q_BLOCK, width)
        spec = pl.BlockSpec((None, None, dil, per, width),
                            lambda b, i: (b, i // steps, 0, i % steps, 0))
        return view, spec

    views, specs = [], []
    for group in (outs, maxes, sums):
        for t, dil in zip(group, DILATIONS):
            width = t.shape[-1]
            if dil == 1:
                views.append(t.reshape(bsz, s, width))
                specs.append(pl.BlockSpec((None, tm, width), lambda b, i: (b, i, 0)))
            else:
                v, sp = dilated(t, dil)
                views.append(v)
                specs.append(sp)

    return pl.pallas_call(
        _merge_kernel,
        grid=(bsz, s // tm),
        in_specs=[row_d, pl.BlockSpec((None, 3, d), lambda b, i: (b, 0, 0)), row_d, row_d, row_g,
                  *specs,
                  *gate_specs, full(p_a), full(p_b), full(w_o)],
        out_specs=row_d,
        out_shape=jax.ShapeDtypeStruct((bsz, s, d), F32),
        scratch_shapes=[pltpu.VMEM((GROUP_DIM // LANES, tm, LANES), F32)],
        compiler_params=_params(2),
        name="merge_out",
    )(x, mod3, h, y_a, zs, *views, w_in, w_in, p_a, p_b, w_o)


def _layer(x, c, w_ada, b_ada, norm_w, w_in, conv_w, q_norm_w, k_norm_w, w_br_conv, w_br_attn, w_out):
    bsz, s, d = x.shape
    conv_dim = conv_w.shape[-1]
    attn_dim = N_GROUPS * GROUP_DIM
    assert conv_dim == d and conv_w.shape[0] == CONV_WIDTH and d % LANES == 0
    assert w_in.shape[1] == 4 * conv_dim + 3 * attn_dim + GROUP_DIM + 2 * d
    assert s % (max(DILATIONS) * Q_BLOCK) == 0 and s % ROW_TILE == 0 and s % MIXER_ROW_TILE == 0
    assert all(w // dil == Q_BLOCK for w, dil in zip(WINDOWS, DILATIONS))

    mod3 = _modulation(c, w_ada, b_ada).reshape(bsz, 3, d)
    nw = norm_w.reshape(1, d)

    w_in = w_in.astype(BF16)
    o_attn = 4 * conv_dim
    o_gate = o_attn + 3 * attn_dim + GROUP_DIM

    y_a, h = _mixer_a(x, mod3, nw, w_in, conv_w)
    qkv, zs = _attn_proj(h, w_in, o_attn, q_norm_w, k_norm_w)
    outs, maxes, sums = zip(*[_dilated_attention(t, dil) for t, dil in zip(qkv, DILATIONS)])
    return _merge(x, mod3, h, y_a, zs, outs, maxes, sums, w_in, o_gate,
                  w_br_conv.astype(BF16), w_br_attn.astype(BF16), w_out.astype(BF16))


@jax.jit
def kernel(x, c, w_ada, b_ada, norm_w, w_in, conv_w, q_norm_w, k_norm_w, w_br_conv, w_br_attn, w_out):
    depth = w_ada.shape[0]
    for l in range(depth):
        x = _layer(x, c, w_ada[l], b_ada[l], norm_w[l], w_in[l], conv_w[l], q_norm_w[l],
                   k_norm_w[l], w_br_conv[l], w_br_attn[l], w_out[l])
    return x
```

```python
import functools
import math

import jax
import jax.numpy as jnp
from jax import lax
from jax.experimental import pallas as pl
from jax.experimental.pallas import tpu as pltpu

F32 = jnp.float32
BF16 = jnp.bfloat16

HEAD_DIM = 64
ATTN_SLOTS = 8
WINDOWS = (128, 512, 2048)
DILATIONS = (1, 4, 16)
N_GROUPS = len(WINDOWS)
GROUP_DIM = ATTN_SLOTS * HEAD_DIM
Q_BLOCK = 128
CONV_WIDTH = 3
EPS = 1e-6
NEG_INF = -1e30
LOG2_E = 1.4426950408889634
LANES = 128
SUBLANES = 8
LSE_LANES = LANES // ATTN_SLOTS
VMEM_LIMIT = 56 * 1024 * 1024

MOD_COL_BLOCK = 1024
MIXER_ROW_TILE = 1024
ROW_TILE = 512
OUT_ROW_TILE = 512
ATTN_STEP_BLOCKS = 16


def _params(n_axes):
    return pltpu.CompilerParams(dimension_semantics=("arbitrary",) * n_axes,
                                vmem_limit_bytes=VMEM_LIMIT)


def _silu(t):
    return t * jax.nn.sigmoid(t)


def _modulated_norm(x, norm_w, mod):
    ms = jnp.mean(x * x, axis=-1, keepdims=True)
    xn = x * lax.rsqrt(ms + EPS) * norm_w
    return xn * (1.0 + mod[1:2, :]) + mod[0:1, :]


def _to_dilated(slab_ref, n_rows, dil):
    per = n_rows // dil
    return jnp.concatenate(
        [jnp.concatenate([slab_ref[j, pl.ds(r, per, stride=dil), :] for r in range(dil)], axis=0)
         for j in range(slab_ref.shape[0])], axis=-1)


def _column_window(w, start, width):
    unit = math.gcd(start, width)
    specs = [pl.BlockSpec((w.shape[0], unit), lambda *_, j=start // unit + j: (0, j),
                          pipeline_mode=pl.Buffered(1)) for j in range(width // unit)]
    return specs, unit


def _columns(w_refs, unit, off, width):
    assert off // unit == (off + width - 1) // unit
    return w_refs[off // unit].at[:, off % unit:off % unit + width]


def _mod_kernel(c_ref, w_ref, b_ref, o_ref):
    c = c_ref[...]
    o_ref[...] = jnp.dot(_silu(c), w_ref[...], preferred_element_type=F32) + b_ref[...]


def _modulation(c, w_ada, b_ada):
    bsz, d = c.shape
    n = w_ada.shape[1]
    bn = MOD_COL_BLOCK
    return pl.pallas_call(
        _mod_kernel,
        grid=(n // bn,),
        in_specs=[pl.BlockSpec((bsz, d), lambda j: (0, 0)),
                  pl.BlockSpec((d, bn), lambda j: (0, j)),
                  pl.BlockSpec((1, bn), lambda j: (0, j))],
        out_specs=pl.BlockSpec((bsz, bn), lambda j: (0, j)),
        out_shape=jax.ShapeDtypeStruct((bsz, n), F32),
        compiler_params=_params(1),
        name="adaln_mod",
    )(c, w_ada, b_ada.reshape(1, n))


def _mixer_a_kernel(x_ref, mod_ref, nw_ref, w32_ref, cw_ref, y_ref, h_ref, tail_ref, w_ref):
    tm, d = x_ref.shape
    mod = mod_ref[...]
    nw = nw_ref[...]
    cw = cw_ref[...]

    @pl.when((pl.program_id(0) == 0) & (pl.program_id(1) == 0))
    def _():
        w_ref[...] = w32_ref[...].astype(w_ref.dtype)

    @pl.when(pl.program_id(1) == 0)
    def _():
        tail_ref[...] = jnp.zeros_like(tail_ref)

    prev1 = tail_ref[SUBLANES - 1:SUBLANES, :]
    prev2 = tail_ref[SUBLANES - 2:SUBLANES - 1, :]

    h = _modulated_norm(x_ref[...], nw, mod).astype(BF16)
    h_ref[...] = h
    proj = lambda j: jnp.dot(h, w_ref[:, j * d:(j + 1) * d], preferred_element_type=F32)
    z_a, c_a, x_a = proj(3), proj(1), proj(2)
    u = c_a * x_a
    row = lax.broadcasted_iota(jnp.int32, (tm, d), 0)
    u_m1 = jnp.where(row == 0, prev1, pltpu.roll(u, 1, axis=0))
    u_m2 = jnp.where(row == 0, prev2, jnp.where(row == 1, prev1, pltpu.roll(u, 2, axis=0)))
    conv = cw[0:1, :] * u_m2 + cw[1:2, :] * u_m1 + cw[2:3, :] * u
    gated = conv * _silu(z_a)
    y_ref[...] = (proj(0) * gated).astype(y_ref.dtype)
    tail_ref[...] = u[tm - SUBLANES:, :]


def _mixer_a(x, mod3, norm_w, w_in, conv_w):
    bsz, s, d = x.shape
    tm = MIXER_ROW_TILE
    (w_spec,), _ = _column_window(w_in, 0, 4 * d)
    rows = pl.BlockSpec((None, tm, d), lambda b, i: (b, i, 0))
    return pl.pallas_call(
        _mixer_a_kernel,
        grid=(bsz, s // tm),
        in_specs=[rows,
                  pl.BlockSpec((None, 3, d), lambda b, i: (b, 0, 0)),
                  pl.BlockSpec((1, d), lambda b, i: (0, 0)),
                  w_spec,
                  pl.BlockSpec(conv_w.shape, lambda b, i: (0, 0))],
        out_specs=[rows, rows],
        out_shape=[jax.ShapeDtypeStruct((bsz, s, d), BF16)] * 2,
        scratch_shapes=[pltpu.VMEM((SUBLANES, d), F32), pltpu.VMEM((d, 4 * d), BF16)],
        compiler_params=_params(2),
        name="mixer_a",
    )(x, mod3, norm_w, w_in, conv_w)


def _head_rms(t, w):
    sq = t * t
    lane_lo = lax.broadcasted_iota(jnp.int32, (t.shape[0], LANES), 1) < HEAD_DIM
    parts = []
    for j in range(0, t.shape[-1], LANES):
        tile = sq[:, j:j + LANES]
        even = jnp.sum(jnp.where(lane_lo, tile, 0.0), axis=-1, keepdims=True)
        odd = jnp.sum(jnp.where(lane_lo, 0.0, tile), axis=-1, keepdims=True)
        parts.append(jnp.where(lane_lo, even, odd))
    ssq = jnp.concatenate(parts, axis=-1)
    return t * lax.rsqrt(ssq * (1.0 / HEAD_DIM) + EPS) * w


def _attn_proj_kernel(h_ref, *refs, w_unit):
    n_w = (3 * N_GROUPS + 1) * GROUP_DIM // w_unit
    w_refs = refs[:n_w]
    qw_ref, kw_ref, qkv0_ref, qkv1_ref, qkv2_ref, zs_ref, slab_ref = refs[n_w:]
    tm, d = h_ref.shape
    h = h_ref[...]
    for j in range(d // LANES):
        slab_ref[j] = h[:, j * LANES:(j + 1) * LANES].astype(F32)
    ad = N_GROUPS * GROUP_DIM
    for g, out_ref in enumerate((qkv0_ref, qkv1_ref, qkv2_ref)):
        hg = h if DILATIONS[g] == 1 else _to_dilated(slab_ref, tm, DILATIONS[g]).astype(BF16)
        q, k, v = [jnp.dot(hg, _columns(w_refs, w_unit, part * ad + g * GROUP_DIM, GROUP_DIM)[...],
                           preferred_element_type=F32) for part in range(3)]
        qkv = jnp.concatenate([_head_rms(q, qw_ref[...]), _head_rms(k, kw_ref[...]), v], axis=-1)
        out_ref[...] = qkv.astype(out_ref.dtype).reshape(out_ref.shape)
        if g == 0:
            z = jnp.dot(hg, _columns(w_refs, w_unit, 3 * ad, GROUP_DIM)[...], preferred_element_type=F32)
            zs_ref[...] = _silu(z)


def _attn_proj(h, w_in, w_start, q_norm_w, k_norm_w):
    bsz, s, d = h.shape
    tm = ROW_TILE
    qw = jnp.tile(q_norm_w * (HEAD_DIM ** -0.5 * LOG2_E), ATTN_SLOTS).reshape(1, GROUP_DIM)
    kw = jnp.tile(k_norm_w, ATTN_SLOTS).reshape(1, GROUP_DIM)
    gw = 3 * GROUP_DIM
    w_specs, w_unit = _column_window(w_in, w_start, (3 * N_GROUPS + 1) * GROUP_DIM)
    out_shapes, out_specs = [], []
    for dil in DILATIONS:
        tile = Q_BLOCK * dil
        assert dil == 1 or tile % tm == 0
        if tile <= tm:
            nblk = tm // Q_BLOCK
            out_shapes.append(jax.ShapeDtypeStruct((bsz, s // Q_BLOCK, Q_BLOCK, gw), BF16))
            out_specs.append(pl.BlockSpec((None, nblk, Q_BLOCK, gw), lambda b, i: (b, i, 0, 0)))
        else:
            per = tm // dil
            steps = tile // tm
            out_shapes.append(jax.ShapeDtypeStruct((bsz, s // tile, dil, Q_BLOCK, gw), BF16))
            out_specs.append(pl.BlockSpec((None, None, dil, per, gw),
                                          lambda b, i, steps=steps: (b, i // steps, 0, i % steps, 0)))
    out_shapes.append(jax.ShapeDtypeStruct((bsz, s, GROUP_DIM), F32))
    out_specs.append(pl.BlockSpec((None, tm, GROUP_DIM), lambda b, i: (b, i, 0)))
    outs = pl.pallas_call(
        functools.partial(_attn_proj_kernel, w_unit=w_unit),
        grid=(bsz, s // tm),
        in_specs=[pl.BlockSpec((None, tm, d), lambda b, i: (b, i, 0)),
                  *w_specs,
                  pl.BlockSpec((1, GROUP_DIM), lambda b, i: (0, 0)),
                  pl.BlockSpec((1, GROUP_DIM), lambda b, i: (0, 0))],
        out_specs=out_specs,
        out_shape=out_shapes,
        scratch_shapes=[pltpu.VMEM((d // LANES, tm, LANES), F32)],
        compiler_params=_params(2),
        name="attn_proj",
    )(h, *([w_in] * len(w_specs)), qw, kw)
    qkv = [t.reshape(bsz, s // Q_BLOCK, Q_BLOCK, gw) for t in outs[:N_GROUPS]]
    return qkv, outs[N_GROUPS]


def _stat_lane(head):
    pair, odd = divmod(head, 2)
    return pair * LSE_LANES + (0 if odd else HEAD_DIM)


def _attn_block(q, kp, kc, vp, vc, bias, o_ref, m_ref, l_ref, j):
    blk = q.shape[0]
    lane = lax.broadcasted_iota(jnp.int32, (blk, LANES), 1)
    lane_lo = lane < HEAD_DIM
    lane2_lo = lax.broadcasted_iota(jnp.int32, (2 * blk, LANES), 1) < HEAD_DIM
    zero = jnp.zeros((blk, LANES), q.dtype)
    one = jnp.ones((2 * blk, LANES), q.dtype)
    m_tile = jnp.zeros((blk, LANES), F32)
    l_tile = jnp.ones((blk, LANES), F32)
    for p in range(GROUP_DIM // LANES):
        sl = slice(p * LANES, (p + 1) * LANES)
        q2 = q[:, sl]
        qs = jnp.concatenate([jnp.where(lane_lo, q2, zero), jnp.where(lane_lo, zero, q2)], axis=0)
        k2 = jnp.concatenate([kp[:, sl], kc[:, sl]], axis=0)
        v2 = jnp.concatenate([vp[:, sl], vc[:, sl]], axis=0)
        s = lax.dot_general(qs, k2, (((1,), (1,)), ((), ())), preferred_element_type=F32) + bias
        m = jnp.max(s, axis=-1, keepdims=True)
        e = jnp.exp2(s - m).astype(BF16)
        o_even = jnp.dot(e[:blk], jnp.where(lane2_lo, v2, one), preferred_element_type=F32)
        o_odd = jnp.dot(e[blk:], jnp.where(lane2_lo, one, v2), preferred_element_type=F32)
        o_ref[j + (slice(None), sl)] = jnp.where(lane_lo, o_even, o_odd)
        for head, m_h, o_h in ((2 * p, m[:blk], o_even), (2 * p + 1, m[blk:], o_odd)):
            slot = (lane >= _stat_lane(head)) & (lane < _stat_lane(head) + LSE_LANES)
            m_tile = jnp.where(slot, m_h, m_tile)
            l_tile = jnp.where(slot, o_h, l_tile)
    m_ref[j] = m_tile
    l_ref[j] = l_tile


def _attn_kernel(q_ref, kprev_ref, k_ref, vprev_ref, v_ref, o_ref, m_ref, l_ref):
    n_blocks, n_res, blk, _ = q_ref.shape
    span = 2 * blk
    row = lax.broadcasted_iota(jnp.int32, (span, span), 0) % blk
    col = lax.broadcasted_iota(jnp.int32, (span, span), 1)
    in_window = (col >= row) & (col <= row + blk)
    bias = jnp.where(in_window, 0.0, NEG_INF)
    run_has_prev = pl.program_id(2) > 0
    bias_first = jnp.where(in_window & ((col >= blk) | run_has_prev), 0.0, NEG_INF)

    for r in range(n_res):
        _attn_block(q_ref[0, r], kprev_ref[r], k_ref[0, r], vprev_ref[r], v_ref[0, r], bias_first,
                    o_ref, m_ref, l_ref, (0, r))
        for j in range(1, n_blocks):
            _attn_block(q_ref[j, r], k_ref[j - 1, r], k_ref[j, r], v_ref[j - 1, r], v_ref[j, r], bias,
                        o_ref, m_ref, l_ref, (j, r))


def _dilated_attention(qkv, dil):
    bsz, nb, blk, width = qkv.shape
    gd = GROUP_DIM
    tiles = nb // dil
    run = min(ATTN_STEP_BLOCKS, tiles)
    res = min(ATTN_STEP_BLOCKS // run, dil)
    assert tiles % run == 0 and dil % res == 0
    view = qkv.reshape(bsz, tiles, dil, blk, width)
    cur = lambda part: pl.BlockSpec((None, run, res, blk, gd), lambda b, r, c: (b, c, r, 0, part))
    prev = lambda part: pl.BlockSpec((None, None, res, blk, gd),
                                     lambda b, r, c: (b, jnp.maximum(c * run - 1, 0), r, 0, part))
    stat_spec = pl.BlockSpec((None, run, res, blk, LANES), lambda b, r, c: (b, c, r, 0, 0))
    stat_shape = jax.ShapeDtypeStruct((bsz, tiles, dil, blk, LANES), F32)
    o, m, l = pl.pallas_call(
        _attn_kernel,
        grid=(bsz, dil // res, tiles // run),
        in_specs=[cur(0), prev(1), cur(1), prev(2), cur(2)],
        out_specs=[pl.BlockSpec((None, run, res, blk, gd), lambda b, r, c: (b, c, r, 0, 0)),
                   stat_spec, stat_spec],
        out_shape=[jax.ShapeDtypeStruct((bsz, tiles, dil, blk, gd), F32), stat_shape, stat_shape],
        compiler_params=_params(3),
        name=f"dilated_attn_d{dil}",
    )(view, view, view, view, view)
    return (o.reshape(bsz, nb, blk, gd), m.reshape(bsz, nb, blk, LANES),
            l.reshape(bsz, nb, blk, LANES))


def _to_sequence(src_ref, slab_ref, dil):
    per = src_ref.shape[1]
    n_slabs = src_ref.shape[2] // LANES
    for j in range(n_slabs):
        for r in range(dil):
            slab_ref[j, pl.ds(r, per, stride=dil), :] = src_ref[r, :, j * LANES:(j + 1) * LANES]
    return jnp.concatenate([slab_ref[j] for j in range(n_slabs)], axis=-1)


def _merge_kernel(x_ref, mod_ref, h_ref, ya_ref, zs_ref,
                  o0_ref, o1_ref, o2_ref, m0_ref, m1_ref, m2_ref, l0_ref, l1_ref, l2_ref,
                  wga_ref, wgb_ref, pa_ref, pb_ref, wo_ref, out_ref, slab_ref):
    h = h_ref[...]
    g_a = jax.nn.sigmoid(jnp.dot(h, wga_ref[...], preferred_element_type=F32))
    g_b = jax.nn.sigmoid(jnp.dot(h, wgb_ref[...], preferred_element_type=F32))

    def in_sequence(refs):
        return [refs[0][...]] + [_to_sequence(r, slab_ref, dil) for r, dil in zip(refs[1:], DILATIONS[1:])]

    outs = in_sequence((o0_ref, o1_ref, o2_ref))
    maxes = in_sequence((m0_ref, m1_ref, m2_ref))
    sums = in_sequence((l0_ref, l1_ref, l2_ref))
    top = jnp.maximum(jnp.maximum(maxes[0], maxes[1]), maxes[2])
    wts = [jnp.exp2(t - top) for t in maxes]
    den = wts[0] * sums[0] + wts[1] * sums[1] + wts[2] * sums[2]
    attn = jnp.zeros_like(outs[0])
    lane_lo = lax.broadcasted_iota(jnp.int32, (h.shape[0], LANES), 1) < HEAD_DIM
    for w, o in zip(wts, outs):
        wn = w / den
        wide = jnp.concatenate(
            [jnp.where(lane_lo, wn[:, _stat_lane(2 * p):_stat_lane(2 * p) + 1],
                       wn[:, _stat_lane(2 * p + 1):_stat_lane(2 * p + 1) + 1])
             for p in range(GROUP_DIM // LANES)], axis=-1)
        attn = attn + wide * o
    y_b = (attn * zs_ref[...]).astype(BF16)

    merged = (g_a * jnp.dot(ya_ref[...], pa_ref[...], preferred_element_type=F32)
              + g_b * jnp.dot(y_b, pb_ref[...], preferred_element_type=F32))
    upd = jnp.dot(merged.astype(BF16), wo_ref[...], preferred_element_type=F32)
    out_ref[...] = x_ref[...] + mod_ref[2:3, :] * upd


def _merge(x, mod3, h, y_a, zs, outs, maxes, sums, w_in, w_start, p_a, p_b, w_o):
    bsz, s, d = x.shape
    tm = OUT_ROW_TILE
    row_d = pl.BlockSpec((None, tm, d), lambda b, i: (b, i, 0))
    row_g = pl.BlockSpec((None, tm, GROUP_DIM), lambda b, i: (b, i, 0))
    full = lambda a: pl.BlockSpec(a.shape, lambda b, i: (0,) * a.ndim, pipeline_mode=pl.Buffered(1))
    gate_specs, gate_unit = _column_window(w_in, w_start, 2 * d)
    assert gate_unit == d

    def dilated(t, dil):
        width = t.shape[-1]
        tile = Q_BLOCK * dil
        per = tm // dil
        if tile % tm:
            raise NotImplementedError("a dilated-order tile must be a whole number of merge tiles")
        steps = tile // tm
        view = t.reshape(bsz, s // tile, dil, Q_BLOCK, width)
        spec = pl.BlockSpec((None, None, dil, per, width),
                            lambda b, i: (b, i // steps, 0, i % steps, 0))
        return view, spec

    views, specs = [], []
    for group in (outs, maxes, sums):
        for t, dil in zip(group, DILATIONS):
            width = t.shape[-1]
            if dil == 1:
                views.append(t.reshape(bsz, s, width))
                specs.append(pl.BlockSpec((None, tm, width), lambda b, i: (b, i, 0)))
            else:
                v, sp = dilated(t, dil)
                views.append(v)
                specs.append(sp)

    return pl.pallas_call(
        _merge_kernel,
        grid=(bsz, s // tm),
        in_specs=[row_d, pl.BlockSpec((None, 3, d), lambda b, i: (b, 0, 0)), row_d, row_d, row_g,
                  *specs,
                  *gate_specs, full(p_a), full(p_b), full(w_o)],
        out_specs=row_d,
        out_shape=jax.ShapeDtypeStruct((bsz, s, d), F32),
        scratch_shapes=[pltpu.VMEM((GROUP_DIM // LANES, tm, LANES), F32)],
        compiler_params=_params(2),
        name="merge_out",
    )(x, mod3, h, y_a, zs, *views, w_in, w_in, p_a, p_b, w_o)


def _layer(x, c, w_ada, b_ada, norm_w, w_in, conv_w, q_norm_w, k_norm_w, w_br_conv, w_br_attn, w_out):
    bsz, s, d = x.shape
    conv_dim = conv_w.shape[-1]
    attn_dim = N_GROUPS * GROUP_DIM
    assert conv_dim == d and conv_w.shape[0] == CONV_WIDTH and d % LANES == 0
    assert w_in.shape[1] == 4 * conv_dim + 3 * attn_dim + GROUP_DIM + 2 * d
    assert s % (max(DILATIONS) * Q_BLOCK) == 0 and s % ROW_TILE == 0 and s % MIXER_ROW_TILE == 0
    assert all(w // dil == Q_BLOCK for w, dil in zip(WINDOWS, DILATIONS))

    mod3 = _modulation(c, w_ada, b_ada).reshape(bsz, 3, d)
    nw = norm_w.reshape(1, d)

    w_rest = w_in[:, 4 * conv_dim:].astype(BF16)
    o_gate = 3 * attn_dim + GROUP_DIM

    y_a, h = _mixer_a(x, mod3, nw, w_in, conv_w)
    qkv, zs = _attn_proj(h, w_rest, 0, q_norm_w, k_norm_w)
    outs, maxes, sums = zip(*[_dilated_attention(t, dil) for t, dil in zip(qkv, DILATIONS)])
    return _merge(x, mod3, h, y_a, zs, outs, maxes, sums, w_rest, o_gate,
                  w_br_conv.astype(BF16), w_br_attn.astype(BF16), w_out.astype(BF16))


@jax.jit
def kernel(x, c, w_ada, b_ada, norm_w, w_in, conv_w, q_norm_w, k_norm_w, w_br_conv, w_br_attn, w_out):
    depth = w_ada.shape[0]
    for l in range(depth):
        x = _layer(x, c, w_ada[l], b_ada[l], norm_w[l], w_in[l], conv_w[l], q_norm_w[l],
                   k_norm_w[l], w_br_conv[l], w_br_attn[l], w_out[l])
    return x
```

```python
import functools

import jax
import jax.numpy as jnp
from jax import lax
from jax.experimental import pallas as pl
from jax.experimental.pallas import tpu as pltpu

F32 = jnp.float32
BF16 = jnp.bfloat16

HEAD_DIM = 64
ATTN_SLOTS = 8
WINDOWS = (128, 512, 2048)
DILATIONS = (1, 4, 16)
N_GROUPS = len(WINDOWS)
GROUP_DIM = ATTN_SLOTS * HEAD_DIM
Q_BLOCK = 128
CONV_WIDTH = 3
EPS = 1e-6
NEG_INF = -1e30
LOG2_E = 1.4426950408889634
LANES = 128
SUBLANES = 8
LSE_LANES = LANES // ATTN_SLOTS
VMEM_LIMIT = 56 * 1024 * 1024

MOD_COL_BLOCK = 1024
MIXER_ROW_TILE = 1024
STAGE_ROWS = 128
ROW_TILE = 512
OUT_ROW_TILE = 512
ATTN_STEP_BLOCKS = 16


def _params(n_axes):
    return pltpu.CompilerParams(dimension_semantics=("arbitrary",) * n_axes,
                                vmem_limit_bytes=VMEM_LIMIT)


def _silu(t):
    return t * jax.nn.sigmoid(t)


def _modulated_norm(x, norm_w, mod):
    ms = jnp.mean(x * x, axis=-1, keepdims=True)
    xn = x * lax.rsqrt(ms + EPS) * norm_w
    return xn * (1.0 + mod[1:2, :]) + mod[0:1, :]


def _to_dilated(slab_ref, n_rows, dil):
    per = n_rows // dil
    return jnp.concatenate(
        [jnp.concatenate([slab_ref[j, pl.ds(r, per, stride=dil), :] for r in range(dil)], axis=0)
         for j in range(slab_ref.shape[0])], axis=-1)


def _fetch_bf16(src_ref, dst_ref, stage_ref, sem):
    k, n = dst_ref.shape
    rows = stage_ref.shape[1]
    n_chunks = k // rows

    def copy(c):
        return pltpu.make_async_copy(src_ref.at[pl.ds(c * rows, rows), :],
                                     stage_ref.at[c % 2, :, pl.ds(0, n)], sem.at[c % 2])

    copy(0).start()
    for c in range(n_chunks):
        if c + 1 < n_chunks:
            copy(c + 1).start()
        copy(c).wait()
        dst_ref[c * rows:(c + 1) * rows, :] = stage_ref[c % 2, :, 0:n].astype(dst_ref.dtype)


def _first_step():
    return (pl.program_id(0) == 0) & (pl.program_id(1) == 0)


_HBM = pl.BlockSpec(memory_space=pl.ANY)


def _mod_kernel(c_ref, w_ref, b_ref, o_ref):
    c = c_ref[...]
    o_ref[...] = jnp.dot(_silu(c), w_ref[...], preferred_element_type=F32) + b_ref[...]


def _modulation(c, w_ada, b_ada):
    bsz, d = c.shape
    n = w_ada.shape[1]
    bn = MOD_COL_BLOCK
    return pl.pallas_call(
        _mod_kernel,
        grid=(n // bn,),
        in_specs=[pl.BlockSpec((bsz, d), lambda j: (0, 0)),
                  pl.BlockSpec((d, bn), lambda j: (0, j)),
                  pl.BlockSpec((1, bn), lambda j: (0, j))],
        out_specs=pl.BlockSpec((bsz, bn), lambda j: (0, j)),
        out_shape=jax.ShapeDtypeStruct((bsz, n), F32),
        compiler_params=_params(1),
        name="adaln_mod",
    )(c, w_ada, b_ada.reshape(1, n))


def _mixer_a_kernel(x_ref, mod_ref, nw_ref, w_hbm, cw_ref, y_ref, h_ref, tail_ref,
                    w_ref, stage_ref, sem, *, layer):
    tm, d = x_ref.shape
    mod = mod_ref[...]
    nw = nw_ref[...]
    cw = cw_ref[...]

    @pl.when(_first_step())
    def _():
        _fetch_bf16(w_hbm.at[layer, :, pl.ds(0, w_ref.shape[1])], w_ref, stage_ref, sem)

    @pl.when(pl.program_id(1) == 0)
    def _():
        tail_ref[...] = jnp.zeros_like(tail_ref)

    prev1 = tail_ref[SUBLANES - 1:SUBLANES, :]
    prev2 = tail_ref[SUBLANES - 2:SUBLANES - 1, :]

    h = _modulated_norm(x_ref[...], nw, mod).astype(BF16)
    h_ref[...] = h
    proj = lambda j: jnp.dot(h, w_ref[:, j * d:(j + 1) * d], preferred_element_type=F32)
    z_a, c_a, x_a = proj(3), proj(1), proj(2)
    u = c_a * x_a
    row = lax.broadcasted_iota(jnp.int32, (tm, d), 0)
    u_m1 = jnp.where(row == 0, prev1, pltpu.roll(u, 1, axis=0))
    u_m2 = jnp.where(row == 0, prev2, jnp.where(row == 1, prev1, pltpu.roll(u, 2, axis=0)))
    conv = cw[0:1, :] * u_m2 + cw[1:2, :] * u_m1 + cw[2:3, :] * u
    gated = conv * _silu(z_a)
    y_ref[...] = (proj(0) * gated).astype(y_ref.dtype)
    tail_ref[...] = u[tm - SUBLANES:, :]


def _mixer_a(x, mod3, norm_w, w_in, layer, conv_w):
    bsz, s, d = x.shape
    tm = MIXER_ROW_TILE
    rows = pl.BlockSpec((None, tm, d), lambda b, i: (b, i, 0))
    return pl.pallas_call(
        functools.partial(_mixer_a_kernel, layer=layer),
        grid=(bsz, s // tm),
        in_specs=[rows,
                  pl.BlockSpec((None, 3, d), lambda b, i: (b, 0, 0)),
                  pl.BlockSpec((1, d), lambda b, i: (0, 0)),
                  _HBM,
                  pl.BlockSpec(conv_w.shape, lambda b, i: (0, 0))],
        out_specs=[rows, rows],
        out_shape=[jax.ShapeDtypeStruct((bsz, s, d), BF16)] * 2,
        scratch_shapes=[pltpu.VMEM((SUBLANES, d), F32),
                        pltpu.VMEM((d, 4 * d), BF16),
                        pltpu.VMEM((2, STAGE_ROWS, 4 * d), F32),
                        pltpu.SemaphoreType.DMA((2,))],
        compiler_params=_params(2),
        name="mixer_a",
    )(x, mod3, norm_w, w_in, conv_w)


def _head_rms(t, w):
    sq = t * t
    lane_lo = lax.broadcasted_iota(jnp.int32, (t.shape[0], LANES), 1) < HEAD_DIM
    parts = []
    for j in range(0, t.shape[-1], LANES):
        tile = sq[:, j:j + LANES]
        even = jnp.sum(jnp.where(lane_lo, tile, 0.0), axis=-1, keepdims=True)
        odd = jnp.sum(jnp.where(lane_lo, 0.0, tile), axis=-1, keepdims=True)
        parts.append(jnp.where(lane_lo, even, odd))
    ssq = jnp.concatenate(parts, axis=-1)
    return t * lax.rsqrt(ssq * (1.0 / HEAD_DIM) + EPS) * w


def _attn_proj_kernel(h_ref, w_hbm, qw_ref, kw_ref, qkv0_ref, qkv1_ref, qkv2_ref, zs_ref,
                      slab_ref, w_ref, stage_ref, sem, *, layer, w_start):
    @pl.when(_first_step())
    def _():
        _fetch_bf16(w_hbm.at[layer, :, pl.ds(w_start, w_ref.shape[1])], w_ref, stage_ref, sem)

    tm, d = h_ref.shape
    h = h_ref[...]
    for j in range(d // LANES):
        slab_ref[j] = h[:, j * LANES:(j + 1) * LANES].astype(F32)
    ad = N_GROUPS * GROUP_DIM
    for g, out_ref in enumerate((qkv0_ref, qkv1_ref, qkv2_ref)):
        hg = h if DILATIONS[g] == 1 else _to_dilated(slab_ref, tm, DILATIONS[g]).astype(BF16)
        q, k, v = [jnp.dot(hg, w_ref[:, part * ad + g * GROUP_DIM:part * ad + (g + 1) * GROUP_DIM],
                           preferred_element_type=F32) for part in range(3)]
        qkv = jnp.concatenate([_head_rms(q, qw_ref[...]), _head_rms(k, kw_ref[...]), v], axis=-1)
        out_ref[...] = qkv.astype(out_ref.dtype).reshape(out_ref.shape)
        if g == 0:
            z = jnp.dot(hg, w_ref[:, 3 * ad:], preferred_element_type=F32)
            zs_ref[...] = _silu(z)


def _attn_proj(h, w_in, layer, w_start, q_norm_w, k_norm_w):
    bsz, s, d = h.shape
    tm = ROW_TILE
    qw = jnp.tile(q_norm_w * (HEAD_DIM ** -0.5 * LOG2_E), ATTN_SLOTS).reshape(1, GROUP_DIM)
    kw = jnp.tile(k_norm_w, ATTN_SLOTS).reshape(1, GROUP_DIM)
    gw = 3 * GROUP_DIM
    w_cols = (3 * N_GROUPS + 1) * GROUP_DIM
    out_shapes, out_specs = [], []
    for dil in DILATIONS:
        tile = Q_BLOCK * dil
        assert dil == 1 or tile % tm == 0
        if tile <= tm:
            nblk = tm // Q_BLOCK
            out_shapes.append(jax.ShapeDtypeStruct((bsz, s // Q_BLOCK, Q_BLOCK, gw), BF16))
            out_specs.append(pl.BlockSpec((None, nblk, Q_BLOCK, gw), lambda b, i: (b, i, 0, 0)))
        else:
            per = tm // dil
            steps = tile // tm
            out_shapes.append(jax.ShapeDtypeStruct((bsz, s // tile, dil, Q_BLOCK, gw), BF16))
            out_specs.append(pl.BlockSpec((None, None, dil, per, gw),
                                          lambda b, i, steps=steps: (b, i // steps, 0, i % steps, 0)))
    out_shapes.append(jax.ShapeDtypeStruct((bsz, s, GROUP_DIM), F32))
    out_specs.append(pl.BlockSpec((None, tm, GROUP_DIM), lambda b, i: (b, i, 0)))
    outs = pl.pallas_call(
        functools.partial(_attn_proj_kernel, layer=layer, w_start=w_start),
        grid=(bsz, s // tm),
        in_specs=[pl.BlockSpec((None, tm, d), lambda b, i: (b, i, 0)),
                  _HBM,
                  pl.BlockSpec((1, GROUP_DIM), lambda b, i: (0, 0)),
                  pl.BlockSpec((1, GROUP_DIM), lambda b, i: (0, 0))],
        out_specs=out_specs,
        out_shape=out_shapes,
        scratch_shapes=[pltpu.VMEM((d // LANES, tm, LANES), F32),
                        pltpu.VMEM((d, w_cols), BF16),
                        pltpu.VMEM((2, STAGE_ROWS, w_cols), F32),
                        pltpu.SemaphoreType.DMA((2,))],
        compiler_params=_params(2),
        name="attn_proj",
    )(h, w_in, qw, kw)
    qkv = [t.reshape(bsz, s // Q_BLOCK, Q_BLOCK, gw) for t in outs[:N_GROUPS]]
    return qkv, outs[N_GROUPS]


def _stat_lane(head):
    pair, odd = divmod(head, 2)
    return pair * LSE_LANES + (0 if odd else HEAD_DIM)


def _attn_block(q, kp, kc, vp, vc, bias, o_ref, m_ref, l_ref, j):
    blk = q.shape[0]
    lane = lax.broadcasted_iota(jnp.int32, (blk, LANES), 1)
    lane_lo = lane < HEAD_DIM
    lane2_lo = lax.broadcasted_iota(jnp.int32, (2 * blk, LANES), 1) < HEAD_DIM
    zero = jnp.zeros((blk, LANES), q.dtype)
    one = jnp.ones((2 * blk, LANES), q.dtype)
    m_tile = jnp.zeros((blk, LANES), F32)
    l_tile = jnp.ones((blk, LANES), F32)
    for p in range(GROUP_DIM // LANES):
        sl = slice(p * LANES, (p + 1) * LANES)
        q2 = q[:, sl]
        qs = jnp.concatenate([jnp.where(lane_lo, q2, zero), jnp.where(lane_lo, zero, q2)], axis=0)
        k2 = jnp.concatenate([kp[:, sl], kc[:, sl]], axis=0)
        v2 = jnp.concatenate([vp[:, sl], vc[:, sl]], axis=0)
        s = lax.dot_general(qs, k2, (((1,), (1,)), ((), ())), preferred_element_type=F32) + bias
        m = jnp.max(s, axis=-1, keepdims=True)
        e = jnp.exp2(s - m).astype(BF16)
        o_even = jnp.dot(e[:blk], jnp.where(lane2_lo, v2, one), preferred_element_type=F32)
        o_odd = jnp.dot(e[blk:], jnp.where(lane2_lo, one, v2), preferred_element_type=F32)
        o_ref[j + (slice(None), sl)] = jnp.where(lane_lo, o_even, o_odd)
        for head, m_h, o_h in ((2 * p, m[:blk], o_even), (2 * p + 1, m[blk:], o_odd)):
            slot = (lane >= _stat_lane(head)) & (lane < _stat_lane(head) + LSE_LANES)
            m_tile = jnp.where(slot, m_h, m_tile)
            l_tile = jnp.where(slot, o_h, l_tile)
    m_ref[j] = m_tile
    l_ref[j] = l_tile


def _attn_kernel(q_ref, kprev_ref, k_ref, vprev_ref, v_ref, o_ref, m_ref, l_ref):
    n_blocks, n_res, blk, _ = q_ref.shape
    span = 2 * blk
    row = lax.broadcasted_iota(jnp.int32, (span, span), 0) % blk
    col = lax.broadcasted_iota(jnp.int32, (span, span), 1)
    in_window = (col >= row) & (col <= row + blk)
    bias = jnp.where(in_window, 0.0, NEG_INF)
    run_has_prev = pl.program_id(2) > 0
    bias_first = jnp.where(in_window & ((col >= blk) | run_has_prev), 0.0, NEG_INF)

    for r in range(n_res):
        _attn_block(q_ref[0, r], kprev_ref[r], k_ref[0, r], vprev_ref[r], v_ref[0, r], bias_first,
                    o_ref, m_ref, l_ref, (0, r))
        for j in range(1, n_blocks):
            _attn_block(q_ref[j, r], k_ref[j - 1, r], k_ref[j, r], v_ref[j - 1, r], v_ref[j, r], bias,
                        o_ref, m_ref, l_ref, (j, r))


def _dilated_attention(qkv, dil):
    bsz, nb, blk, width = qkv.shape
    gd = GROUP_DIM
    tiles = nb // dil
    run = min(ATTN_STEP_BLOCKS, tiles)
    res = min(ATTN_STEP_BLOCKS // run, dil)
    assert tiles % run == 0 and dil % res == 0
    view = qkv.reshape(bsz, tiles, dil, blk, width)
    cur = lambda part: pl.BlockSpec((None, run, res, blk, gd), lambda b, r, c: (b, c, r, 0, part))
    prev = lambda part: pl.BlockSpec((None, None, res, blk, gd),
                                     lambda b, r, c: (b, jnp.maximum(c * run - 1, 0), r, 0, part))
    stat_spec = pl.BlockSpec((None, run, res, blk, LANES), lambda b, r, c: (b, c, r, 0, 0))
    stat_shape = jax.ShapeDtypeStruct((bsz, tiles, dil, blk, LANES), F32)
    o, m, l = pl.pallas_call(
        _attn_kernel,
        grid=(bsz, dil // res, tiles // run),
        in_specs=[cur(0), prev(1), cur(1), prev(2), cur(2)],
        out_specs=[pl.BlockSpec((None, run, res, blk, gd), lambda b, r, c: (b, c, r, 0, 0)),
                   stat_spec, stat_spec],
        out_shape=[jax.ShapeDtypeStruct((bsz, tiles, dil, blk, gd), F32), stat_shape, stat_shape],
        compiler_params=_params(3),
        name=f"dilated_attn_d{dil}",
    )(view, view, view, view, view)
    return (o.reshape(bsz, nb, blk, gd), m.reshape(bsz, nb, blk, LANES),
            l.reshape(bsz, nb, blk, LANES))


def _to_sequence(src_ref, slab_ref, dil):
    per = src_ref.shape[1]
    n_slabs = src_ref.shape[2] // LANES
    for j in range(n_slabs):
        for r in range(dil):
            slab_ref[j, pl.ds(r, per, stride=dil), :] = src_ref[r, :, j * LANES:(j + 1) * LANES]
    return jnp.concatenate([slab_ref[j] for j in range(n_slabs)], axis=-1)


def _merge_kernel(x_ref, mod_ref, h_ref, ya_ref, zs_ref,
                  o0_ref, o1_ref, o2_ref, m0_ref, m1_ref, m2_ref, l0_ref, l1_ref, l2_ref,
                  win_hbm, pa_hbm, pb_hbm, wo_hbm, out_ref,
                  slab_ref, wg_ref, pa_ref, pb_ref, wo_ref, stage_ref, sem, *, layer, w_start):
    d = x_ref.shape[-1]

    @pl.when(_first_step())
    def _():
        _fetch_bf16(win_hbm.at[layer, :, pl.ds(w_start, 2 * d)], wg_ref, stage_ref, sem)
        _fetch_bf16(pa_hbm.at[layer], pa_ref, stage_ref, sem)
        _fetch_bf16(pb_hbm.at[layer], pb_ref, stage_ref, sem)
        _fetch_bf16(wo_hbm.at[layer], wo_ref, stage_ref, sem)

    h = h_ref[...]
    g_a = jax.nn.sigmoid(jnp.dot(h, wg_ref[:, :d], preferred_element_type=F32))
    g_b = jax.nn.sigmoid(jnp.dot(h, wg_ref[:, d:], preferred_element_type=F32))

    def in_sequence(refs):
        return [refs[0][...]] + [_to_sequence(r, slab_ref, dil) for r, dil in zip(refs[1:], DILATIONS[1:])]

    outs = in_sequence((o0_ref, o1_ref, o2_ref))
    maxes = in_sequence((m0_ref, m1_ref, m2_ref))
    sums = in_sequence((l0_ref, l1_ref, l2_ref))
    top = jnp.maximum(jnp.maximum(maxes[0], maxes[1]), maxes[2])
    wts = [jnp.exp2(t - top) for t in maxes]
    den = wts[0] * sums[0] + wts[1] * sums[1] + wts[2] * sums[2]
    attn = jnp.zeros_like(outs[0])
    lane_lo = lax.broadcasted_iota(jnp.int32, (h.shape[0], LANES), 1) < HEAD_DIM
    for w, o in zip(wts, outs):
        wn = w / den
        wide = jnp.concatenate(
            [jnp.where(lane_lo, wn[:, _stat_lane(2 * p):_stat_lane(2 * p) + 1],
                       wn[:, _stat_lane(2 * p + 1):_stat_lane(2 * p + 1) + 1])
             for p in range(GROUP_DIM // LANES)], axis=-1)
        attn = attn + wide * o
    y_b = (attn * zs_ref[...]).astype(BF16)

    merged = (g_a * jnp.dot(ya_ref[...], pa_ref[...], preferred_element_type=F32)
              + g_b * jnp.dot(y_b, pb_ref[...], preferred_element_type=F32))
    upd = jnp.dot(merged.astype(BF16), wo_ref[...], preferred_element_type=F32)
    out_ref[...] = x_ref[...] + mod_ref[2:3, :] * upd


def _merge(x, mod3, h, y_a, zs, outs, maxes, sums, w_in, layer, w_start, p_a, p_b, w_o):
    bsz, s, d = x.shape
    tm = OUT_ROW_TILE
    row_d = pl.BlockSpec((None, tm, d), lambda b, i: (b, i, 0))
    row_g = pl.BlockSpec((None, tm, GROUP_DIM), lambda b, i: (b, i, 0))

    def dilated(t, dil):
        width = t.shape[-1]
        tile = Q_BLOCK * dil
        per = tm // dil
        if tile % tm:
            raise NotImplementedError("a dilated-order tile must be a whole number of merge tiles")
        steps = tile // tm
        view = t.reshape(bsz, s // tile, dil, Q_BLOCK, width)
        spec = pl.BlockSpec((None, None, dil, per, width),
                            lambda b, i: (b, i // steps, 0, i % steps, 0))
        return view, spec

    views, specs = [], []
    for group in (outs, maxes, sums):
        for t, dil in zip(group, DILATIONS):
            width = t.shape[-1]
            if dil == 1:
                views.append(t.reshape(bsz, s, width))
                specs.append(pl.BlockSpec((None, tm, width), lambda b, i: (b, i, 0)))
            else:
                v, sp = dilated(t, dil)
                views.append(v)
                specs.append(sp)

    return pl.pallas_call(
        functools.partial(_merge_kernel, layer=layer, w_start=w_start),
        grid=(bsz, s // tm),
        in_specs=[row_d, pl.BlockSpec((None, 3, d), lambda b, i: (b, 0, 0)), row_d, row_d, row_g,
                  *specs,
                  _HBM, _HBM, _HBM, _HBM],
        out_specs=row_d,
        out_shape=jax.ShapeDtypeStruct((bsz, s, d), F32),
        scratch_shapes=[pltpu.VMEM((GROUP_DIM // LANES, tm, LANES), F32),
                        pltpu.VMEM((d, 2 * d), BF16),
                        pltpu.VMEM(p_a.shape[1:], BF16),
                        pltpu.VMEM(p_b.shape[1:], BF16),
                        pltpu.VMEM(w_o.shape[1:], BF16),
                        pltpu.VMEM((2, STAGE_ROWS, 2 * d), F32),
                        pltpu.SemaphoreType.DMA((2,))],
        compiler_params=_params(2),
        name="merge_out",
    )(x, mod3, h, y_a, zs, *views, w_in, p_a, p_b, w_o)


def _layer(x, c, layer, w_ada, b_ada, norm_w, w_in, conv_w, q_norm_w, k_norm_w, w_br_conv, w_br_attn, w_out):
    bsz, s, d = x.shape
    conv_dim = conv_w.shape[-1]
    attn_dim = N_GROUPS * GROUP_DIM
    assert conv_dim == d and conv_w.shape[0] == CONV_WIDTH and d % LANES == 0 and d % STAGE_ROWS == 0
    assert w_in.shape[2] == 4 * conv_dim + 3 * attn_dim + GROUP_DIM + 2 * d
    assert w_br_attn.shape[1] % STAGE_ROWS == 0
    assert s % (max(DILATIONS) * Q_BLOCK) == 0 and s % ROW_TILE == 0 and s % MIXER_ROW_TILE == 0
    assert all(w // dil == Q_BLOCK for w, dil in zip(WINDOWS, DILATIONS))

    mod3 = _modulation(c, w_ada, b_ada).reshape(bsz, 3, d)
    nw = norm_w.reshape(1, d)
    o_attn = 4 * conv_dim
    o_gate = o_attn + 3 * attn_dim + GROUP_DIM

    y_a, h = _mixer_a(x, mod3, nw, w_in, layer, conv_w)
    qkv, zs = _attn_proj(h, w_in, layer, o_attn, q_norm_w, k_norm_w)
    outs, maxes, sums = zip(*[_dilated_attention(t, dil) for t, dil in zip(qkv, DILATIONS)])
    return _merge(x, mod3, h, y_a, zs, outs, maxes, sums, w_in, layer, o_gate, w_br_conv, w_br_attn, w_out)


@jax.jit
def kernel(x, c, w_ada, b_ada, norm_w, w_in, conv_w, q_norm_w, k_norm_w, w_br_conv, w_br_attn, w_out):
    depth = w_ada.shape[0]
    for l in range(depth):
        x = _layer(x, c, l, w_ada[l], b_ada[l], norm_w[l], w_in, conv_w[l], q_norm_w[l],
                   k_norm_w[l], w_br_conv, w_br_attn, w_out)
    return x
```

```python
import functools

import jax
import jax.numpy as jnp
from jax import lax
from jax.experimental import pallas as pl
from jax.experimental.pallas import tpu as pltpu

F32 = jnp.float32
BF16 = jnp.bfloat16

HEAD_DIM = 64
ATTN_SLOTS = 8
WINDOWS = (128, 512, 2048)
DILATIONS = (1, 4, 16)
N_GROUPS = len(WINDOWS)
GROUP_DIM = ATTN_SLOTS * HEAD_DIM
Q_BLOCK = 128
CONV_WIDTH = 3
EPS = 1e-6
NEG_INF = -1e30
LOG2_E = 1.4426950408889634
LANES = 128
SUBLANES = 8
LSE_LANES = LANES // ATTN_SLOTS
VMEM_LIMIT = 56 * 1024 * 1024

MOD_COL_BLOCK = 1024
MIXER_ROW_TILE = 1024
STAGE_ROWS = 128
STAGE_SLOTS = 4
ROW_TILE = 512
OUT_ROW_TILE = 512
ATTN_STEP_BLOCKS = 16


def _params(n_axes):
    return pltpu.CompilerParams(dimension_semantics=("arbitrary",) * n_axes,
                                vmem_limit_bytes=VMEM_LIMIT)


def _silu(t):
    return t * jax.nn.sigmoid(t)


def _modulated_norm(x, norm_w, mod):
    ms = jnp.mean(x * x, axis=-1, keepdims=True)
    xn = x * lax.rsqrt(ms + EPS) * norm_w
    return xn * (1.0 + mod[1:2, :]) + mod[0:1, :]


def _to_dilated(slab_ref, n_rows, dil):
    per = n_rows // dil
    return jnp.concatenate(
        [jnp.concatenate([slab_ref[j, pl.ds(r, per, stride=dil), :] for r in range(dil)], axis=0)
         for j in range(slab_ref.shape[0])], axis=-1)


def _fetch_bf16(pairs, stage_ref, sem):
    slots, rows, _ = stage_ref.shape
    chunks = [(src, dst, r0) for src, dst in pairs for r0 in range(0, dst.shape[0], rows)]

    def copy(i):
        src, dst, r0 = chunks[i]
        return pltpu.make_async_copy(src.at[pl.ds(r0, rows), :],
                                     stage_ref.at[i % slots, :, pl.ds(0, dst.shape[1])],
                                     sem.at[i % slots])

    for i in range(min(slots - 1, len(chunks))):
        copy(i).start()
    for i, (_, dst, r0) in enumerate(chunks):
        if i + slots - 1 < len(chunks):
            copy(i + slots - 1).start()
        copy(i).wait()
        dst[r0:r0 + rows, :] = stage_ref[i % slots, :, 0:dst.shape[1]].astype(dst.dtype)


def _first_step():
    return (pl.program_id(0) == 0) & (pl.program_id(1) == 0)


_HBM = pl.BlockSpec(memory_space=pl.ANY)


def _mod_kernel(c_ref, w_ref, b_ref, o_ref):
    c = c_ref[...]
    o_ref[...] = jnp.dot(_silu(c), w_ref[...], preferred_element_type=F32) + b_ref[...]


def _modulation(c, w_ada, b_ada):
    bsz, d = c.shape
    n = w_ada.shape[1]
    bn = MOD_COL_BLOCK
    return pl.pallas_call(
        _mod_kernel,
        grid=(n // bn,),
        in_specs=[pl.BlockSpec((bsz, d), lambda j: (0, 0)),
                  pl.BlockSpec((d, bn), lambda j: (0, j)),
                  pl.BlockSpec((1, bn), lambda j: (0, j))],
        out_specs=pl.BlockSpec((bsz, bn), lambda j: (0, j)),
        out_shape=jax.ShapeDtypeStruct((bsz, n), F32),
        compiler_params=_params(1),
        name="adaln_mod",
    )(c, w_ada, b_ada.reshape(1, n))


def _mixer_a_kernel(x_ref, mod_ref, nw_ref, w_hbm, cw_ref, y_ref, h_ref, tail_ref,
                    w_ref, stage_ref, sem, *, layer):
    tm, d = x_ref.shape
    mod = mod_ref[...]
    nw = nw_ref[...]
    cw = cw_ref[...]

    @pl.when(_first_step())
    def _():
        _fetch_bf16([(w_hbm.at[layer, :, pl.ds(0, w_ref.shape[1])], w_ref)], stage_ref, sem)

    @pl.when(pl.program_id(1) == 0)
    def _():
        tail_ref[...] = jnp.zeros_like(tail_ref)

    prev1 = tail_ref[SUBLANES - 1:SUBLANES, :]
    prev2 = tail_ref[SUBLANES - 2:SUBLANES - 1, :]

    h = _modulated_norm(x_ref[...], nw, mod).astype(BF16)
    h_ref[...] = h
    proj = lambda j: jnp.dot(h, w_ref[:, j * d:(j + 1) * d], preferred_element_type=F32)
    z_a, c_a, x_a = proj(3), proj(1), proj(2)
    u = c_a * x_a
    row = lax.broadcasted_iota(jnp.int32, (tm, d), 0)
    u_m1 = jnp.where(row == 0, prev1, pltpu.roll(u, 1, axis=0))
    u_m2 = jnp.where(row == 0, prev2, jnp.where(row == 1, prev1, pltpu.roll(u, 2, axis=0)))
    conv = cw[0:1, :] * u_m2 + cw[1:2, :] * u_m1 + cw[2:3, :] * u
    gated = conv * _silu(z_a)
    y_ref[...] = (proj(0) * gated).astype(y_ref.dtype)
    tail_ref[...] = u[tm - SUBLANES:, :]


def _mixer_a(x, mod3, norm_w, w_in, layer, conv_w):
    bsz, s, d = x.shape
    tm = MIXER_ROW_TILE
    rows = pl.BlockSpec((None, tm, d), lambda b, i: (b, i, 0))
    return pl.pallas_call(
        functools.partial(_mixer_a_kernel, layer=layer),
        grid=(bsz, s // tm),
        in_specs=[rows,
                  pl.BlockSpec((None, 3, d), lambda b, i: (b, 0, 0)),
                  pl.BlockSpec((1, d), lambda b, i: (0, 0)),
                  _HBM,
                  pl.BlockSpec(conv_w.shape, lambda b, i: (0, 0))],
        out_specs=[rows, rows],
        out_shape=[jax.ShapeDtypeStruct((bsz, s, d), BF16)] * 2,
        scratch_shapes=[pltpu.VMEM((SUBLANES, d), F32),
                        pltpu.VMEM((d, 4 * d), BF16),
                        pltpu.VMEM((STAGE_SLOTS, STAGE_ROWS, 4 * d), F32),
                        pltpu.SemaphoreType.DMA((STAGE_SLOTS,))],
        compiler_params=_params(2),
        name="mixer_a",
    )(x, mod3, norm_w, w_in, conv_w)


def _head_rms(t, w):
    sq = t * t
    lane_lo = lax.broadcasted_iota(jnp.int32, (t.shape[0], LANES), 1) < HEAD_DIM
    parts = []
    for j in range(0, t.shape[-1], LANES):
        tile = sq[:, j:j + LANES]
        even = jnp.sum(jnp.where(lane_lo, tile, 0.0), axis=-1, keepdims=True)
        odd = jnp.sum(jnp.where(lane_lo, 0.0, tile), axis=-1, keepdims=True)
        parts.append(jnp.where(lane_lo, even, odd))
    ssq = jnp.concatenate(parts, axis=-1)
    return t * lax.rsqrt(ssq * (1.0 / HEAD_DIM) + EPS) * w


def _attn_proj_kernel(h_ref, w_hbm, qw_ref, kw_ref, qkv0_ref, qkv1_ref, qkv2_ref, zs_ref,
                      slab_ref, w_ref, stage_ref, sem, *, layer, w_start):
    @pl.when(_first_step())
    def _():
        _fetch_bf16([(w_hbm.at[layer, :, pl.ds(w_start, w_ref.shape[1])], w_ref)], stage_ref, sem)

    tm, d = h_ref.shape
    h = h_ref[...]
    for j in range(d // LANES):
        slab_ref[j] = h[:, j * LANES:(j + 1) * LANES].astype(F32)
    ad = N_GROUPS * GROUP_DIM
    for g, out_ref in enumerate((qkv0_ref, qkv1_ref, qkv2_ref)):
        hg = h if DILATIONS[g] == 1 else _to_dilated(slab_ref, tm, DILATIONS[g]).astype(BF16)
        q, k, v = [jnp.dot(hg, w_ref[:, part * ad + g * GROUP_DIM:part * ad + (g + 1) * GROUP_DIM],
                           preferred_element_type=F32) for part in range(3)]
        qkv = jnp.concatenate([_head_rms(q, qw_ref[...]), _head_rms(k, kw_ref[...]), v], axis=-1)
        out_ref[...] = qkv.astype(out_ref.dtype).reshape(out_ref.shape)
        if g == 0:
            z = jnp.dot(hg, w_ref[:, 3 * ad:], preferred_element_type=F32)
            zs_ref[...] = _silu(z)


def _attn_proj(h, w_in, layer, w_start, q_norm_w, k_norm_w):
    bsz, s, d = h.shape
    tm = ROW_TILE
    qw = jnp.tile(q_norm_w * (HEAD_DIM ** -0.5 * LOG2_E), ATTN_SLOTS).reshape(1, GROUP_DIM)
    kw = jnp.tile(k_norm_w, ATTN_SLOTS).reshape(1, GROUP_DIM)
    gw = 3 * GROUP_DIM
    w_cols = (3 * N_GROUPS + 1) * GROUP_DIM
    out_shapes, out_specs = [], []
    for dil in DILATIONS:
        tile = Q_BLOCK * dil
        assert dil == 1 or tile % tm == 0
        if tile <= tm:
            nblk = tm // Q_BLOCK
            out_shapes.append(jax.ShapeDtypeStruct((bsz, s // Q_BLOCK, Q_BLOCK, gw), BF16))
            out_specs.append(pl.BlockSpec((None, nblk, Q_BLOCK, gw), lambda b, i: (b, i, 0, 0)))
        else:
            per = tm // dil
            steps = tile // tm
            out_shapes.append(jax.ShapeDtypeStruct((bsz, s // tile, dil, Q_BLOCK, gw), BF16))
            out_specs.append(pl.BlockSpec((None, None, dil, per, gw),
                                          lambda b, i, steps=steps: (b, i // steps, 0, i % steps, 0)))
    out_shapes.append(jax.ShapeDtypeStruct((bsz, s, GROUP_DIM), F32))
    out_specs.append(pl.BlockSpec((None, tm, GROUP_DIM), lambda b, i: (b, i, 0)))
    outs = pl.pallas_call(
        functools.partial(_attn_proj_kernel, layer=layer, w_start=w_start),
        grid=(bsz, s // tm),
        in_specs=[pl.BlockSpec((None, tm, d), lambda b, i: (b, i, 0)),
                  _HBM,
                  pl.BlockSpec((1, GROUP_DIM), lambda b, i: (0, 0)),
                  pl.BlockSpec((1, GROUP_DIM), lambda b, i: (0, 0))],
        out_specs=out_specs,
        out_shape=out_shapes,
        scratch_shapes=[pltpu.VMEM((d // LANES, tm, LANES), F32),
                        pltpu.VMEM((d, w_cols), BF16),
                        pltpu.VMEM((STAGE_SLOTS, STAGE_ROWS, w_cols), F32),
                        pltpu.SemaphoreType.DMA((STAGE_SLOTS,))],
        compiler_params=_params(2),
        name="attn_proj",
    )(h, w_in, qw, kw)
    qkv = [t.reshape(bsz, s // Q_BLOCK, Q_BLOCK, gw) for t in outs[:N_GROUPS]]
    return qkv, outs[N_GROUPS]


def _stat_lane(head):
    pair, odd = divmod(head, 2)
    return pair * LSE_LANES + (0 if odd else HEAD_DIM)


def _attn_block(q, kp, kc, vp, vc, bias, o_ref, m_ref, l_ref, j):
    blk = q.shape[0]
    lane = lax.broadcasted_iota(jnp.int32, (blk, LANES), 1)
    lane_lo = lane < HEAD_DIM
    lane2_lo = lax.broadcasted_iota(jnp.int32, (2 * blk, LANES), 1) < HEAD_DIM
    zero = jnp.zeros((blk, LANES), q.dtype)
    one = jnp.ones((2 * blk, LANES), q.dtype)
    m_tile = jnp.zeros((blk, LANES), F32)
    l_tile = jnp.ones((blk, LANES), F32)
    for p in range(GROUP_DIM // LANES):
        sl = slice(p * LANES, (p + 1) * LANES)
        q2 = q[:, sl]
        qs = jnp.concatenate([jnp.where(lane_lo, q2, zero), jnp.where(lane_lo, zero, q2)], axis=0)
        k2 = jnp.concatenate([kp[:, sl], kc[:, sl]], axis=0)
        v2 = jnp.concatenate([vp[:, sl], vc[:, sl]], axis=0)
        s = lax.dot_general(qs, k2, (((1,), (1,)), ((), ())), preferred_element_type=F32) + bias
        m = jnp.max(s, axis=-1, keepdims=True)
        e = jnp.exp2(s - m).astype(BF16)
        o_even = jnp.dot(e[:blk], jnp.where(lane2_lo, v2, one), preferred_element_type=F32)
        o_odd = jnp.dot(e[blk:], jnp.where(lane2_lo, one, v2), preferred_element_type=F32)
        o_ref[j + (slice(None), sl)] = jnp.where(lane_lo, o_even, o_odd)
        for head, m_h, o_h in ((2 * p, m[:blk], o_even), (2 * p + 1, m[blk:], o_odd)):
            slot = (lane >= _stat_lane(head)) & (lane < _stat_lane(head) + LSE_LANES)
            m_tile = jnp.where(slot, m_h, m_tile)
            l_tile = jnp.where(slot, o_h, l_tile)
    m_ref[j] = m_tile
    l_ref[j] = l_tile


def _attn_kernel(q_ref, kprev_ref, k_ref, vprev_ref, v_ref, o_ref, m_ref, l_ref):
    n_blocks, n_res, blk, _ = q_ref.shape
    span = 2 * blk
    row = lax.broadcasted_iota(jnp.int32, (span, span), 0) % blk
    col = lax.broadcasted_iota(jnp.int32, (span, span), 1)
    in_window = (col >= row) & (col <= row + blk)
    bias = jnp.where(in_window, 0.0, NEG_INF)
    run_has_prev = pl.program_id(2) > 0
    bias_first = jnp.where(in_window & ((col >= blk) | run_has_prev), 0.0, NEG_INF)

    for r in range(n_res):
        _attn_block(q_ref[0, r], kprev_ref[r], k_ref[0, r], vprev_ref[r], v_ref[0, r], bias_first,
                    o_ref, m_ref, l_ref, (0, r))
        for j in range(1, n_blocks):
            _attn_block(q_ref[j, r], k_ref[j - 1, r], k_ref[j, r], v_ref[j - 1, r], v_ref[j, r], bias,
                        o_ref, m_ref, l_ref, (j, r))


def _dilated_attention(qkv, dil):
    bsz, nb, blk, width = qkv.shape
    gd = GROUP_DIM
    tiles = nb // dil
    run = min(ATTN_STEP_BLOCKS, tiles)
    res = min(ATTN_STEP_BLOCKS // run, dil)
    assert tiles % run == 0 and dil % res == 0
    view = qkv.reshape(bsz, tiles, dil, blk, width)
    cur = lambda part: pl.BlockSpec((None, run, res, blk, gd), lambda b, r, c: (b, c, r, 0, part))
    prev = lambda part: pl.BlockSpec((None, None, res, blk, gd),
                                     lambda b, r, c: (b, jnp.maximum(c * run - 1, 0), r, 0, part))
    stat_spec = pl.BlockSpec((None, run, res, blk, LANES), lambda b, r, c: (b, c, r, 0, 0))
    stat_shape = jax.ShapeDtypeStruct((bsz, tiles, dil, blk, LANES), F32)
    o, m, l = pl.pallas_call(
        _attn_kernel,
        grid=(bsz, dil // res, tiles // run),
        in_specs=[cur(0), prev(1), cur(1), prev(2), cur(2)],
        out_specs=[pl.BlockSpec((None, run, res, blk, gd), lambda b, r, c: (b, c, r, 0, 0)),
                   stat_spec, stat_spec],
        out_shape=[jax.ShapeDtypeStruct((bsz, tiles, dil, blk, gd), F32), stat_shape, stat_shape],
        compiler_params=_params(3),
        name=f"dilated_attn_d{dil}",
    )(view, view, view, view, view)
    return (o.reshape(bsz, nb, blk, gd), m.reshape(bsz, nb, blk, LANES),
            l.reshape(bsz, nb, blk, LANES))


def _to_sequence(src_ref, slab_ref, dil):
    per = src_ref.shape[1]
    n_slabs = src_ref.shape[2] // LANES
    for j in range(n_slabs):
        for r in range(dil):
            slab_ref[j, pl.ds(r, per, stride=dil), :] = src_ref[r, :, j * LANES:(j + 1) * LANES]
    return jnp.concatenate([slab_ref[j] for j in range(n_slabs)], axis=-1)


def _merge_kernel(x_ref, mod_ref, h_ref, ya_ref, zs_ref,
                  o0_ref, o1_ref, o2_ref, m0_ref, m1_ref, m2_ref, l0_ref, l1_ref, l2_ref,
                  win_hbm, pa_hbm, pb_hbm, wo_hbm, out_ref,
                  slab_ref, wg_ref, pa_ref, pb_ref, wo_ref, stage_ref, sem, *, layer, w_start):
    d = x_ref.shape[-1]

    @pl.when(_first_step())
    def _():
        _fetch_bf16([(win_hbm.at[layer, :, pl.ds(w_start, 2 * d)], wg_ref),
                     (pa_hbm.at[layer], pa_ref), (pb_hbm.at[layer], pb_ref),
                     (wo_hbm.at[layer], wo_ref)], stage_ref, sem)

    h = h_ref[...]
    g_a = jax.nn.sigmoid(jnp.dot(h, wg_ref[:, :d], preferred_element_type=F32))
    g_b = jax.nn.sigmoid(jnp.dot(h, wg_ref[:, d:], preferred_element_type=F32))

    def in_sequence(refs):
        return [refs[0][...]] + [_to_sequence(r, slab_ref, dil) for r, dil in zip(refs[1:], DILATIONS[1:])]

    outs = in_sequence((o0_ref, o1_ref, o2_ref))
    maxes = in_sequence((m0_ref, m1_ref, m2_ref))
    sums = in_sequence((l0_ref, l1_ref, l2_ref))
    top = jnp.maximum(jnp.maximum(maxes[0], maxes[1]), maxes[2])
    wts = [jnp.exp2(t - top) for t in maxes]
    den = wts[0] * sums[0] + wts[1] * sums[1] + wts[2] * sums[2]
    attn = jnp.zeros_like(outs[0])
    lane_lo = lax.broadcasted_iota(jnp.int32, (h.shape[0], LANES), 1) < HEAD_DIM
    for w, o in zip(wts, outs):
        wn = w / den
        wide = jnp.concatenate(
            [jnp.where(lane_lo, wn[:, _stat_lane(2 * p):_stat_lane(2 * p) + 1],
                       wn[:, _stat_lane(2 * p + 1):_stat_lane(2 * p + 1) + 1])
             for p in range(GROUP_DIM // LANES)], axis=-1)
        attn = attn + wide * o
    y_b = (attn * zs_ref[...]).astype(BF16)

    merged = (g_a * jnp.dot(ya_ref[...], pa_ref[...], preferred_element_type=F32)
              + g_b * jnp.dot(y_b, pb_ref[...], preferred_element_type=F32))
    upd = jnp.dot(merged.astype(BF16), wo_ref[...], preferred_element_type=F32)
    out_ref[...] = x_ref[...] + mod_ref[2:3, :] * upd


def _merge(x, mod3, h, y_a, zs, outs, maxes, sums, w_in, layer, w_start, p_a, p_b, w_o):
    bsz, s, d = x.shape
    tm = OUT_ROW_TILE
    row_d = pl.BlockSpec((None, tm, d), lambda b, i: (b, i, 0))
    row_g = pl.BlockSpec((None, tm, GROUP_DIM), lambda b, i: (b, i, 0))

    def dilated(t, dil):
        width = t.shape[-1]
        tile = Q_BLOCK * dil
        per = tm // dil
        if tile % tm:
            raise NotImplementedError("a dilated-order tile must be a whole number of merge tiles")
        steps = tile // tm
        view = t.reshape(bsz, s // tile, dil, Q_BLOCK, width)
        spec = pl.BlockSpec((None, None, dil, per, width),
                            lambda b, i: (b, i // steps, 0, i % steps, 0))
        return view, spec

    views, specs = [], []
    for group in (outs, maxes, sums):
        for t, dil in zip(group, DILATIONS):
            width = t.shape[-1]
            if dil == 1:
                views.append(t.reshape(bsz, s, width))
                specs.append(pl.BlockSpec((None, tm, width), lambda b, i: (b, i, 0)))
            else:
                v, sp = dilated(t, dil)
                views.append(v)
                specs.append(sp)

    return pl.pallas_call(
        functools.partial(_merge_kernel, layer=layer, w_start=w_start),
        grid=(bsz, s // tm),
        in_specs=[row_d, pl.BlockSpec((None, 3, d), lambda b, i: (b, 0, 0)), row_d, row_d, row_g,
                  *specs,
                  _HBM, _HBM, _HBM, _HBM],
        out_specs=row_d,
        out_shape=jax.ShapeDtypeStruct((bsz, s, d), F32),
        scratch_shapes=[pltpu.VMEM((GROUP_DIM // LANES, tm, LANES), F32),
                        pltpu.VMEM((d, 2 * d), BF16),
                        pltpu.VMEM(p_a.shape[1:], BF16),
                        pltpu.VMEM(p_b.shape[1:], BF16),
                        pltpu.VMEM(w_o.shape[1:], BF16),
                        pltpu.VMEM((STAGE_SLOTS, STAGE_ROWS, 2 * d), F32),
                        pltpu.SemaphoreType.DMA((STAGE_SLOTS,))],
        compiler_params=_params(2),
        name="merge_out",
    )(x, mod3, h, y_a, zs, *views, w_in, p_a, p_b, w_o)


def _layer(x, c, layer, w_ada, b_ada, norm_w, w_in, conv_w, q_norm_w, k_norm_w, w_br_conv, w_br_attn, w_out):
    bsz, s, d = x.shape
    conv_dim = conv_w.shape[-1]
    attn_dim = N_GROUPS * GROUP_DIM
    assert conv_dim == d and conv_w.shape[0] == CONV_WIDTH and d % LANES == 0 and d % STAGE_ROWS == 0
    assert w_in.shape[2] == 4 * conv_dim + 3 * attn_dim + GROUP_DIM + 2 * d
    assert w_br_attn.shape[1] % STAGE_ROWS == 0
    assert s % (max(DILATIONS) * Q_BLOCK) == 0 and s % ROW_TILE == 0 and s % MIXER_ROW_TILE == 0
    assert all(w // dil == Q_BLOCK for w, dil in zip(WINDOWS, DILATIONS))

    mod3 = _modulation(c, w_ada, b_ada).reshape(bsz, 3, d)
    nw = norm_w.reshape(1, d)
    o_attn = 4 * conv_dim
    o_gate = o_attn + 3 * attn_dim + GROUP_DIM

    y_a, h = _mixer_a(x, mod3, nw, w_in, layer, conv_w)
    qkv, zs = _attn_proj(h, w_in, layer, o_attn, q_norm_w, k_norm_w)
    outs, maxes, sums = zip(*[_dilated_attention(t, dil) for t, dil in zip(qkv, DILATIONS)])
    return _merge(x, mod3, h, y_a, zs, outs, maxes, sums, w_in, layer, o_gate, w_br_conv, w_br_attn, w_out)


@jax.jit
def kernel(x, c, w_ada, b_ada, norm_w, w_in, conv_w, q_norm_w, k_norm_w, w_br_conv, w_br_attn, w_out):
    depth = w_ada.shape[0]
    for l in range(depth):
        x = _layer(x, c, l, w_ada[l], b_ada[l], norm_w[l], w_in, conv_w[l], q_norm_w[l],
                   k_norm_w[l], w_br_conv, w_br_attn, w_out)
    return x
```

```python
import functools

import jax
import jax.numpy as jnp
from jax import lax
from jax.experimental import pallas as pl
from jax.experimental.pallas import tpu as pltpu

F32 = jnp.float32
BF16 = jnp.bfloat16

HEAD_DIM = 64
ATTN_SLOTS = 8
WINDOWS = (128, 512, 2048)
DILATIONS = (1, 4, 16)
N_GROUPS = len(WINDOWS)
GROUP_DIM = ATTN_SLOTS * HEAD_DIM
Q_BLOCK = 128
CONV_WIDTH = 3
EPS = 1e-6
NEG_INF = -1e30
LOG2_E = 1.4426950408889634
LANES = 128
SUBLANES = 8
LSE_LANES = LANES // ATTN_SLOTS
VMEM_LIMIT = 56 * 1024 * 1024

MOD_COL_BLOCK = 1024
MIXER_ROW_TILE = 1024
STAGE_ROWS = 128
STAGE_BYTES_IN_FLIGHT = 6 * 1024 * 1024
ROW_TILE = 512
OUT_ROW_TILE = 512
ATTN_STEP_BLOCKS = 16


def _params(n_axes):
    return pltpu.CompilerParams(dimension_semantics=("arbitrary",) * n_axes,
                                vmem_limit_bytes=VMEM_LIMIT)


def _silu(t):
    return t * jax.nn.sigmoid(t)


def _modulated_norm(x, norm_w, mod):
    ms = jnp.mean(x * x, axis=-1, keepdims=True)
    xn = x * lax.rsqrt(ms + EPS) * norm_w
    return xn * (1.0 + mod[1:2, :]) + mod[0:1, :]


def _to_dilated(slab_ref, n_rows, dil):
    per = n_rows // dil
    return jnp.concatenate(
        [jnp.concatenate([slab_ref[j, pl.ds(r, per, stride=dil), :] for r in range(dil)], axis=0)
         for j in range(slab_ref.shape[0])], axis=-1)


def _fetch_bf16(pairs, stage_ref, sem):
    slots, rows, _ = stage_ref.shape
    chunks = [(src, dst, r0) for src, dst in pairs for r0 in range(0, dst.shape[0], rows)]

    def copy(i):
        src, dst, r0 = chunks[i]
        return pltpu.make_async_copy(src.at[pl.ds(r0, rows), :],
                                     stage_ref.at[i % slots, :, pl.ds(0, dst.shape[1])],
                                     sem.at[i % slots])

    for i in range(min(slots - 1, len(chunks))):
        copy(i).start()
    for i, (_, dst, r0) in enumerate(chunks):
        if i + slots - 1 < len(chunks):
            copy(i + slots - 1).start()
        copy(i).wait()
        dst[r0:r0 + rows, :] = stage_ref[i % slots, :, 0:dst.shape[1]].astype(dst.dtype)


def _first_step():
    return (pl.program_id(0) == 0) & (pl.program_id(1) == 0)


_HBM = pl.BlockSpec(memory_space=pl.ANY)


def _stage_scratch(width):
    chunk_bytes = STAGE_ROWS * width * 4
    slots = -(-STAGE_BYTES_IN_FLIGHT // chunk_bytes) + 1
    return [pltpu.VMEM((slots, STAGE_ROWS, width), F32), pltpu.SemaphoreType.DMA((slots,))]


def _mod_kernel(c_ref, w_ref, b_ref, o_ref):
    c = c_ref[...]
    o_ref[...] = jnp.dot(_silu(c), w_ref[...], preferred_element_type=F32) + b_ref[...]


def _modulation(c, w_ada, b_ada):
    bsz, d = c.shape
    n = w_ada.shape[1]
    bn = MOD_COL_BLOCK
    return pl.pallas_call(
        _mod_kernel,
        grid=(n // bn,),
        in_specs=[pl.BlockSpec((bsz, d), lambda j: (0, 0)),
                  pl.BlockSpec((d, bn), lambda j: (0, j)),
                  pl.BlockSpec((1, bn), lambda j: (0, j))],
        out_specs=pl.BlockSpec((bsz, bn), lambda j: (0, j)),
        out_shape=jax.ShapeDtypeStruct((bsz, n), F32),
        compiler_params=_params(1),
        name="adaln_mod",
    )(c, w_ada, b_ada.reshape(1, n))


def _mixer_a_kernel(x_ref, mod_ref, nw_ref, w_hbm, cw_ref, y_ref, h_ref, tail_ref,
                    w_ref, stage_ref, sem, *, layer):
    tm, d = x_ref.shape
    mod = mod_ref[...]
    nw = nw_ref[...]
    cw = cw_ref[...]

    @pl.when(_first_step())
    def _():
        _fetch_bf16([(w_hbm.at[layer, :, pl.ds(0, w_ref.shape[1])], w_ref)], stage_ref, sem)

    @pl.when(pl.program_id(1) == 0)
    def _():
        tail_ref[...] = jnp.zeros_like(tail_ref)

    prev1 = tail_ref[SUBLANES - 1:SUBLANES, :]
    prev2 = tail_ref[SUBLANES - 2:SUBLANES - 1, :]

    h = _modulated_norm(x_ref[...], nw, mod).astype(BF16)
    h_ref[...] = h
    proj = lambda j: jnp.dot(h, w_ref[:, j * d:(j + 1) * d], preferred_element_type=F32)
    z_a, c_a, x_a = proj(3), proj(1), proj(2)
    u = c_a * x_a
    row = lax.broadcasted_iota(jnp.int32, (tm, d), 0)
    u_m1 = jnp.where(row == 0, prev1, pltpu.roll(u, 1, axis=0))
    u_m2 = jnp.where(row == 0, prev2, jnp.where(row == 1, prev1, pltpu.roll(u, 2, axis=0)))
    conv = cw[0:1, :] * u_m2 + cw[1:2, :] * u_m1 + cw[2:3, :] * u
    gated = conv * _silu(z_a)
    y_ref[...] = (proj(0) * gated).astype(y_ref.dtype)
    tail_ref[...] = u[tm - SUBLANES:, :]


def _mixer_a(x, mod3, norm_w, w_in, layer, conv_w):
    bsz, s, d = x.shape
    tm = MIXER_ROW_TILE
    rows = pl.BlockSpec((None, tm, d), lambda b, i: (b, i, 0))
    return pl.pallas_call(
        functools.partial(_mixer_a_kernel, layer=layer),
        grid=(bsz, s // tm),
        in_specs=[rows,
                  pl.BlockSpec((None, 3, d), lambda b, i: (b, 0, 0)),
                  pl.BlockSpec((1, d), lambda b, i: (0, 0)),
                  _HBM,
                  pl.BlockSpec(conv_w.shape, lambda b, i: (0, 0))],
        out_specs=[rows, rows],
        out_shape=[jax.ShapeDtypeStruct((bsz, s, d), BF16)] * 2,
        scratch_shapes=[pltpu.VMEM((SUBLANES, d), F32),
                        pltpu.VMEM((d, 4 * d), BF16),
                        *_stage_scratch(4 * d)],
        compiler_params=_params(2),
        name="mixer_a",
    )(x, mod3, norm_w, w_in, conv_w)


def _head_rms(t, w):
    sq = t * t
    lane_lo = lax.broadcasted_iota(jnp.int32, (t.shape[0], LANES), 1) < HEAD_DIM
    parts = []
    for j in range(0, t.shape[-1], LANES):
        tile = sq[:, j:j + LANES]
        even = jnp.sum(jnp.where(lane_lo, tile, 0.0), axis=-1, keepdims=True)
        odd = jnp.sum(jnp.where(lane_lo, 0.0, tile), axis=-1, keepdims=True)
        parts.append(jnp.where(lane_lo, even, odd))
    ssq = jnp.concatenate(parts, axis=-1)
    return t * lax.rsqrt(ssq * (1.0 / HEAD_DIM) + EPS) * w


def _attn_proj_kernel(h_ref, w_hbm, qw_ref, kw_ref, qkv0_ref, qkv1_ref, qkv2_ref, zs_ref,
                      slab_ref, w_ref, stage_ref, sem, *, layer, w_start):
    @pl.when(_first_step())
    def _():
        _fetch_bf16([(w_hbm.at[layer, :, pl.ds(w_start, w_ref.shape[1])], w_ref)], stage_ref, sem)

    tm, d = h_ref.shape
    h = h_ref[...]
    for j in range(d // LANES):
        slab_ref[j] = h[:, j * LANES:(j + 1) * LANES].astype(F32)
    ad = N_GROUPS * GROUP_DIM
    for g, out_ref in enumerate((qkv0_ref, qkv1_ref, qkv2_ref)):
        hg = h if DILATIONS[g] == 1 else _to_dilated(slab_ref, tm, DILATIONS[g]).astype(BF16)
        q, k, v = [jnp.dot(hg, w_ref[:, part * ad + g * GROUP_DIM:part * ad + (g + 1) * GROUP_DIM],
                           preferred_element_type=F32) for part in range(3)]
        qkv = jnp.concatenate([_head_rms(q, qw_ref[...]), _head_rms(k, kw_ref[...]), v], axis=-1)
        out_ref[...] = qkv.astype(out_ref.dtype).reshape(out_ref.shape)
        if g == 0:
            z = jnp.dot(hg, w_ref[:, 3 * ad:], preferred_element_type=F32)
            zs_ref[...] = _silu(z)


def _attn_proj(h, w_in, layer, w_start, q_norm_w, k_norm_w):
    bsz, s, d = h.shape
    tm = ROW_TILE
    qw = jnp.tile(q_norm_w * (HEAD_DIM ** -0.5 * LOG2_E), ATTN_SLOTS).reshape(1, GROUP_DIM)
    kw = jnp.tile(k_norm_w, ATTN_SLOTS).reshape(1, GROUP_DIM)
    gw = 3 * GROUP_DIM
    w_cols = (3 * N_GROUPS + 1) * GROUP_DIM
    out_shapes, out_specs = [], []
    for dil in DILATIONS:
        tile = Q_BLOCK * dil
        assert dil == 1 or tile % tm == 0
        if tile <= tm:
            nblk = tm // Q_BLOCK
            out_shapes.append(jax.ShapeDtypeStruct((bsz, s // Q_BLOCK, Q_BLOCK, gw), BF16))
            out_specs.append(pl.BlockSpec((None, nblk, Q_BLOCK, gw), lambda b, i: (b, i, 0, 0)))
        else:
            per = tm // dil
            steps = tile // tm
            out_shapes.append(jax.ShapeDtypeStruct((bsz, s // tile, dil, Q_BLOCK, gw), BF16))
            out_specs.append(pl.BlockSpec((None, None, dil, per, gw),
                                          lambda b, i, steps=steps: (b, i // steps, 0, i % steps, 0)))
    out_shapes.append(jax.ShapeDtypeStruct((bsz, s, GROUP_DIM), F32))
    out_specs.append(pl.BlockSpec((None, tm, GROUP_DIM), lambda b, i: (b, i, 0)))
    outs = pl.pallas_call(
        functools.partial(_attn_proj_kernel, layer=layer, w_start=w_start),
        grid=(bsz, s // tm),
        in_specs=[pl.BlockSpec((None, tm, d), lambda b, i: (b, i, 0)),
                  _HBM,
                  pl.BlockSpec((1, GROUP_DIM), lambda b, i: (0, 0)),
                  pl.BlockSpec((1, GROUP_DIM), lambda b, i: (0, 0))],
        out_specs=out_specs,
        out_shape=out_shapes,
        scratch_shapes=[pltpu.VMEM((d // LANES, tm, LANES), F32),
                        pltpu.VMEM((d, w_cols), BF16),
                        *_stage_scratch(w_cols)],
        compiler_params=_params(2),
        name="attn_proj",
    )(h, w_in, qw, kw)
    qkv = [t.reshape(bsz, s // Q_BLOCK, Q_BLOCK, gw) for t in outs[:N_GROUPS]]
    return qkv, outs[N_GROUPS]


def _stat_lane(head):
    pair, odd = divmod(head, 2)
    return pair * LSE_LANES + (0 if odd else HEAD_DIM)


def _attn_block(q, kp, kc, vp, vc, bias, o_ref, m_ref, l_ref, j):
    blk = q.shape[0]
    lane = lax.broadcasted_iota(jnp.int32, (blk, LANES), 1)
    lane_lo = lane < HEAD_DIM
    lane2_lo = lax.broadcasted_iota(jnp.int32, (2 * blk, LANES), 1) < HEAD_DIM
    zero = jnp.zeros((blk, LANES), q.dtype)
    one = jnp.ones((2 * blk, LANES), q.dtype)
    m_tile = jnp.zeros((blk, LANES), F32)
    l_tile = jnp.ones((blk, LANES), F32)
    for p in range(GROUP_DIM // LANES):
        sl = slice(p * LANES, (p + 1) * LANES)
        q2 = q[:, sl]
        qs = jnp.concatenate([jnp.where(lane_lo, q2, zero), jnp.where(lane_lo, zero, q2)], axis=0)
        k2 = jnp.concatenate([kp[:, sl], kc[:, sl]], axis=0)
        v2 = jnp.concatenate([vp[:, sl], vc[:, sl]], axis=0)
        s = lax.dot_general(qs, k2, (((1,), (1,)), ((), ())), preferred_element_type=F32) + bias
        m = jnp.max(s, axis=-1, keepdims=True)
        e = jnp.exp2(s - m).astype(BF16)
        o_even = jnp.dot(e[:blk], jnp.where(lane2_lo, v2, one), preferred_element_type=F32)
        o_odd = jnp.dot(e[blk:], jnp.where(lane2_lo, one, v2), preferred_element_type=F32)
        o_ref[j + (slice(None), sl)] = jnp.where(lane_lo, o_even, o_odd)
        for head, m_h, o_h in ((2 * p, m[:blk], o_even), (2 * p + 1, m[blk:], o_odd)):
            slot = (lane >= _stat_lane(head)) & (lane < _stat_lane(head) + LSE_LANES)
            m_tile = jnp.where(slot, m_h, m_tile)
            l_tile = jnp.where(slot, o_h, l_tile)
    m_ref[j] = m_tile
    l_ref[j] = l_tile


def _attn_kernel(q_ref, kprev_ref, k_ref, vprev_ref, v_ref, o_ref, m_ref, l_ref):
    n_blocks, n_res, blk, _ = q_ref.shape
    span = 2 * blk
    row = lax.broadcasted_iota(jnp.int32, (span, span), 0) % blk
    col = lax.broadcasted_iota(jnp.int32, (span, span), 1)
    in_window = (col >= row) & (col <= row + blk)
    bias = jnp.where(in_window, 0.0, NEG_INF)
    run_has_prev = pl.program_id(2) > 0
    bias_first = jnp.where(in_window & ((col >= blk) | run_has_prev), 0.0, NEG_INF)

    for r in range(n_res):
        _attn_block(q_ref[0, r], kprev_ref[r], k_ref[0, r], vprev_ref[r], v_ref[0, r], bias_first,
                    o_ref, m_ref, l_ref, (0, r))
        for j in range(1, n_blocks):
            _attn_block(q_ref[j, r], k_ref[j - 1, r], k_ref[j, r], v_ref[j - 1, r], v_ref[j, r], bias,
                        o_ref, m_ref, l_ref, (j, r))


def _dilated_attention(qkv, dil):
    bsz, nb, blk, width = qkv.shape
    gd = GROUP_DIM
    tiles = nb // dil
    run = min(ATTN_STEP_BLOCKS, tiles)
    res = min(ATTN_STEP_BLOCKS // run, dil)
    assert tiles % run == 0 and dil % res == 0
    view = qkv.reshape(bsz, tiles, dil, blk, width)
    cur = lambda part: pl.BlockSpec((None, run, res, blk, gd), lambda b, r, c: (b, c, r, 0, part))
    prev = lambda part: pl.BlockSpec((None, None, res, blk, gd),
                                     lambda b, r, c: (b, jnp.maximum(c * run - 1, 0), r, 0, part))
    stat_spec = pl.BlockSpec((None, run, res, blk, LANES), lambda b, r, c: (b, c, r, 0, 0))
    stat_shape = jax.ShapeDtypeStruct((bsz, tiles, dil, blk, LANES), F32)
    o, m, l = pl.pallas_call(
        _attn_kernel,
        grid=(bsz, dil // res, tiles // run),
        in_specs=[cur(0), prev(1), cur(1), prev(2), cur(2)],
        out_specs=[pl.BlockSpec((None, run, res, blk, gd), lambda b, r, c: (b, c, r, 0, 0)),
                   stat_spec, stat_spec],
        out_shape=[jax.ShapeDtypeStruct((bsz, tiles, dil, blk, gd), F32), stat_shape, stat_shape],
        compiler_params=_params(3),
        name=f"dilated_attn_d{dil}",
    )(view, view, view, view, view)
    return (o.reshape(bsz, nb, blk, gd), m.reshape(bsz, nb, blk, LANES),
            l.reshape(bsz, nb, blk, LANES))


def _to_sequence(src_ref, slab_ref, dil):
    per = src_ref.shape[1]
    n_slabs = src_ref.shape[2] // LANES
    for j in range(n_slabs):
        for r in range(dil):
            slab_ref[j, pl.ds(r, per, stride=dil), :] = src_ref[r, :, j * LANES:(j + 1) * LANES]
    return jnp.concatenate([slab_ref[j] for j in range(n_slabs)], axis=-1)


def _merge_kernel(x_ref, mod_ref, h_ref, ya_ref, zs_ref,
                  o0_ref, o1_ref, o2_ref, m0_ref, m1_ref, m2_ref, l0_ref, l1_ref, l2_ref,
                  win_hbm, pa_hbm, pb_hbm, wo_hbm, out_ref,
                  slab_ref, wg_ref, pa_ref, pb_ref, wo_ref, stage_ref, sem, *, layer, w_start):
    d = x_ref.shape[-1]

    @pl.when(_first_step())
    def _():
        _fetch_bf16([(win_hbm.at[layer, :, pl.ds(w_start, 2 * d)], wg_ref),
                     (pa_hbm.at[layer], pa_ref), (pb_hbm.at[layer], pb_ref),
                     (wo_hbm.at[layer], wo_ref)], stage_ref, sem)

    h = h_ref[...]
    g_a = jax.nn.sigmoid(jnp.dot(h, wg_ref[:, :d], preferred_element_type=F32))
    g_b = jax.nn.sigmoid(jnp.dot(h, wg_ref[:, d:], preferred_element_type=F32))

    def in_sequence(refs):
        return [refs[0][...]] + [_to_sequence(r, slab_ref, dil) for r, dil in zip(refs[1:], DILATIONS[1:])]

    outs = in_sequence((o0_ref, o1_ref, o2_ref))
    maxes = in_sequence((m0_ref, m1_ref, m2_ref))
    sums = in_sequence((l0_ref, l1_ref, l2_ref))
    top = jnp.maximum(jnp.maximum(maxes[0], maxes[1]), maxes[2])
    wts = [jnp.exp2(t - top) for t in maxes]
    den = wts[0] * sums[0] + wts[1] * sums[1] + wts[2] * sums[2]
    attn = jnp.zeros_like(outs[0])
    lane_lo = lax.broadcasted_iota(jnp.int32, (h.shape[0], LANES), 1) < HEAD_DIM
    for w, o in zip(wts, outs):
        wn = w / den
        wide = jnp.concatenate(
            [jnp.where(lane_lo, wn[:, _stat_lane(2 * p):_stat_lane(2 * p) + 1],
                       wn[:, _stat_lane(2 * p + 1):_stat_lane(2 * p + 1) + 1])
             for p in range(GROUP_DIM // LANES)], axis=-1)
        attn = attn + wide * o
    y_b = (attn * zs_ref[...]).astype(BF16)

    merged = (g_a * jnp.dot(ya_ref[...], pa_ref[...], preferred_element_type=F32)
              + g_b * jnp.dot(y_b, pb_ref[...], preferred_element_type=F32))
    upd = jnp.dot(merged.astype(BF16), wo_ref[...], preferred_element_type=F32)
    out_ref[...] = x_ref[...] + mod_ref[2:3, :] * upd


def _merge(x, mod3, h, y_a, zs, outs, maxes, sums, w_in, layer, w_start, p_a, p_b, w_o):
    bsz, s, d = x.shape
    tm = OUT_ROW_TILE
    row_d = pl.BlockSpec((None, tm, d), lambda b, i: (b, i, 0))
    row_g = pl.BlockSpec((None, tm, GROUP_DIM), lambda b, i: (b, i, 0))

    def dilated(t, dil):
        width = t.shape[-1]
        tile = Q_BLOCK * dil
        per = tm // dil
        if tile % tm:
            raise NotImplementedError("a dilated-order tile must be a whole number of merge tiles")
        steps = tile // tm
        view = t.reshape(bsz, s // tile, dil, Q_BLOCK, width)
        spec = pl.BlockSpec((None, None, dil, per, width),
                            lambda b, i: (b, i // steps, 0, i % steps, 0))
        return view, spec

    views, specs = [], []
    for group in (outs, maxes, sums):
        for t, dil in zip(group, DILATIONS):
            width = t.shape[-1]
            if dil == 1:
                views.append(t.reshape(bsz, s, width))
                specs.append(pl.BlockSpec((None, tm, width), lambda b, i: (b, i, 0)))
            else:
                v, sp = dilated(t, dil)
                views.append(v)
                specs.append(sp)

    return pl.pallas_call(
        functools.partial(_merge_kernel, layer=layer, w_start=w_start),
        grid=(bsz, s // tm),
        in_specs=[row_d, pl.BlockSpec((None, 3, d), lambda b, i: (b, 0, 0)), row_d, row_d, row_g,
                  *specs,
                  _HBM, _HBM, _HBM, _HBM],
        out_specs=row_d,
        out_shape=jax.ShapeDtypeStruct((bsz, s, d), F32),
        scratch_shapes=[pltpu.VMEM((GROUP_DIM // LANES, tm, LANES), F32),
                        pltpu.VMEM((d, 2 * d), BF16),
                        pltpu.VMEM(p_a.shape[1:], BF16),
                        pltpu.VMEM(p_b.shape[1:], BF16),
                        pltpu.VMEM(w_o.shape[1:], BF16),
                        *_stage_scratch(2 * d)],
        compiler_params=_params(2),
        name="merge_out",
    )(x, mod3, h, y_a, zs, *views, w_in, p_a, p_b, w_o)


def _layer(x, c, layer, w_ada, b_ada, norm_w, w_in, conv_w, q_norm_w, k_norm_w, w_br_conv, w_br_attn, w_out):
    bsz, s, d = x.shape
    conv_dim = conv_w.shape[-1]
    attn_dim = N_GROUPS * GROUP_DIM
    assert conv_dim == d and conv_w.shape[0] == CONV_WIDTH and d % LANES == 0 and d % STAGE_ROWS == 0
    assert w_in.shape[2] == 4 * conv_dim + 3 * attn_dim + GROUP_DIM + 2 * d
    assert w_br_attn.shape[1] % STAGE_ROWS == 0
    assert s % (max(DILATIONS) * Q_BLOCK) == 0 and s % ROW_TILE == 0 and s % MIXER_ROW_TILE == 0
    assert all(w // dil == Q_BLOCK for w, dil in zip(WINDOWS, DILATIONS))

    mod3 = _modulation(c, w_ada, b_ada).reshape(bsz, 3, d)
    nw = norm_w.reshape(1, d)
    o_attn = 4 * conv_dim
    o_gate = o_attn + 3 * attn_dim + GROUP_DIM

    y_a, h = _mixer_a(x, mod3, nw, w_in, layer, conv_w)
    qkv, zs = _attn_proj(h, w_in, layer, o_attn, q_norm_w, k_norm_w)
    outs, maxes, sums = zip(*[_dilated_attention(t, dil) for t, dil in zip(qkv, DILATIONS)])
    return _merge(x, mod3, h, y_a, zs, outs, maxes, sums, w_in, layer, o_gate, w_br_conv, w_br_attn, w_out)


@jax.jit
def kernel(x, c, w_ada, b_ada, norm_w, w_in, conv_w, q_norm_w, k_norm_w, w_br_conv, w_br_attn, w_out):
    depth = w_ada.shape[0]
    for l in range(depth):
        x = _layer(x, c, l, w_ada[l], b_ada[l], norm_w[l], w_in, conv_w[l], q_norm_w[l],
                   k_norm_w[l], w_br_conv, w_br_attn, w_out)
    return x
```

```python
import functools

import jax
import jax.numpy as jnp
from jax import lax
from jax.experimental import pallas as pl
from jax.experimental.pallas import tpu as pltpu

F32 = jnp.float32
BF16 = jnp.bfloat16

HEAD_DIM = 64
ATTN_SLOTS = 8
WINDOWS = (128, 512, 2048)
DILATIONS = (1, 4, 16)
N_GROUPS = len(WINDOWS)
GROUP_DIM = ATTN_SLOTS * HEAD_DIM
Q_BLOCK = 128
CONV_WIDTH = 3
EPS = 1e-6
NEG_INF = -1e30
LOG2_E = 1.4426950408889634
LANES = 128
SUBLANES = 8
LSE_LANES = LANES // ATTN_SLOTS
VMEM_LIMIT = 56 * 1024 * 1024

MIXER_ROW_TILE = 1024
STAGE_ROWS = 128
STAGE_BYTES_IN_FLIGHT = 6 * 1024 * 1024
ROW_TILE = 512
OUT_ROW_TILE = 512
ATTN_STEP_BLOCKS = 16


def _params(n_axes):
    return pltpu.CompilerParams(dimension_semantics=("arbitrary",) * n_axes,
                                vmem_limit_bytes=VMEM_LIMIT)


def _silu(t):
    return t * jax.nn.sigmoid(t)


def _modulated_norm(x, norm_w, mod):
    ms = jnp.mean(x * x, axis=-1, keepdims=True)
    xn = x * lax.rsqrt(ms + EPS) * norm_w
    return xn * (1.0 + mod[1:2, :]) + mod[0:1, :]


def _to_dilated(slab_ref, n_rows, dil):
    per = n_rows // dil
    return jnp.concatenate(
        [jnp.concatenate([slab_ref[j, pl.ds(r, per, stride=dil), :] for r in range(dil)], axis=0)
         for j in range(slab_ref.shape[0])], axis=-1)


def _fetch_bf16(pairs, stage_ref, sem):
    slots, rows, _ = stage_ref.shape
    chunks = [(src, dst, r0) for src, dst in pairs for r0 in range(0, dst.shape[0], rows)]

    def copy(i):
        src, dst, r0 = chunks[i]
        return pltpu.make_async_copy(src.at[pl.ds(r0, rows), :],
                                     stage_ref.at[i % slots, :, pl.ds(0, dst.shape[1])],
                                     sem.at[i % slots])

    for i in range(min(slots - 1, len(chunks))):
        copy(i).start()
    for i, (_, dst, r0) in enumerate(chunks):
        if i + slots - 1 < len(chunks):
            copy(i + slots - 1).start()
        copy(i).wait()
        dst[r0:r0 + rows, :] = stage_ref[i % slots, :, 0:dst.shape[1]].astype(dst.dtype)


def _first_step():
    return (pl.program_id(0) == 0) & (pl.program_id(1) == 0)


_HBM = pl.BlockSpec(memory_space=pl.ANY)


def _stage_scratch(width):
    chunk_bytes = STAGE_ROWS * width * 4
    slots = -(-STAGE_BYTES_IN_FLIGHT // chunk_bytes) + 1
    return [pltpu.VMEM((slots, STAGE_ROWS, width), F32), pltpu.SemaphoreType.DMA((slots,))]


def _mod_kernel(c_ref, w_ref, b_ref, o_ref):
    d = c_ref.shape[-1]
    sc = _silu(c_ref[...])
    for t in range(o_ref.shape[1]):
        cols = slice(t * d, (t + 1) * d)
        o_ref[:, t, :] = jnp.dot(sc, w_ref[:, cols], preferred_element_type=F32) + b_ref[:, cols]


def _modulation(c, w_ada, b_ada):
    bsz, d = c.shape
    n = w_ada.shape[1]
    whole = lambda shape: pl.BlockSpec(shape, lambda j: (0,) * len(shape), pipeline_mode=pl.Buffered(1))
    return pl.pallas_call(
        _mod_kernel,
        grid=(1,),
        in_specs=[whole((bsz, d)), whole((d, n)), whole((1, n))],
        out_specs=pl.BlockSpec((bsz, n // d, d), lambda j: (0, 0, 0)),
        out_shape=jax.ShapeDtypeStruct((bsz, n // d, d), F32),
        compiler_params=_params(1),
        name="adaln_mod",
    )(c, w_ada, b_ada.reshape(1, n))


def _mixer_a_kernel(x_ref, mod_ref, nw_ref, w_hbm, cw_ref, y_ref, h_ref, tail_ref,
                    w_ref, stage_ref, sem, *, layer):
    tm, d = x_ref.shape
    mod = mod_ref[...]
    nw = nw_ref[...]
    cw = cw_ref[...]

    @pl.when(_first_step())
    def _():
        _fetch_bf16([(w_hbm.at[layer, :, pl.ds(0, w_ref.shape[1])], w_ref)], stage_ref, sem)

    @pl.when(pl.program_id(1) == 0)
    def _():
        tail_ref[...] = jnp.zeros_like(tail_ref)

    prev1 = tail_ref[SUBLANES - 1:SUBLANES, :]
    prev2 = tail_ref[SUBLANES - 2:SUBLANES - 1, :]

    h = _modulated_norm(x_ref[...], nw, mod).astype(BF16)
    h_ref[...] = h
    proj = lambda j: jnp.dot(h, w_ref[:, j * d:(j + 1) * d], preferred_element_type=F32)
    z_a, c_a, x_a = proj(3), proj(1), proj(2)
    u = c_a * x_a
    row = lax.broadcasted_iota(jnp.int32, (tm, d), 0)
    u_m1 = jnp.where(row == 0, prev1, pltpu.roll(u, 1, axis=0))
    u_m2 = jnp.where(row == 0, prev2, jnp.where(row == 1, prev1, pltpu.roll(u, 2, axis=0)))
    conv = cw[0:1, :] * u_m2 + cw[1:2, :] * u_m1 + cw[2:3, :] * u
    gated = conv * _silu(z_a)
    y_ref[...] = (proj(0) * gated).astype(y_ref.dtype)
    tail_ref[...] = u[tm - SUBLANES:, :]


def _mixer_a(x, mod3, norm_w, w_in, layer, conv_w):
    bsz, s, d = x.shape
    tm = MIXER_ROW_TILE
    rows = pl.BlockSpec((None, tm, d), lambda b, i: (b, i, 0))
    return pl.pallas_call(
        functools.partial(_mixer_a_kernel, layer=layer),
        grid=(bsz, s // tm),
        in_specs=[rows,
                  pl.BlockSpec((None, 3, d), lambda b, i: (b, 0, 0)),
                  pl.BlockSpec((1, d), lambda b, i: (0, 0)),
                  _HBM,
                  pl.BlockSpec(conv_w.shape, lambda b, i: (0, 0))],
        out_specs=[rows, rows],
        out_shape=[jax.ShapeDtypeStruct((bsz, s, d), BF16)] * 2,
        scratch_shapes=[pltpu.VMEM((SUBLANES, d), F32),
                        pltpu.VMEM((d, 4 * d), BF16),
                        *_stage_scratch(4 * d)],
        compiler_params=_params(2),
        name="mixer_a",
    )(x, mod3, norm_w, w_in, conv_w)


def _head_rms(t, w):
    sq = t * t
    lane_lo = lax.broadcasted_iota(jnp.int32, (t.shape[0], LANES), 1) < HEAD_DIM
    parts = []
    for j in range(0, t.shape[-1], LANES):
        tile = sq[:, j:j + LANES]
        even = jnp.sum(jnp.where(lane_lo, tile, 0.0), axis=-1, keepdims=True)
        odd = jnp.sum(jnp.where(lane_lo, 0.0, tile), axis=-1, keepdims=True)
        parts.append(jnp.where(lane_lo, even, odd))
    ssq = jnp.concatenate(parts, axis=-1)
    return t * lax.rsqrt(ssq * (1.0 / HEAD_DIM) + EPS) * w


def _attn_proj_kernel(h_ref, w_hbm, qw_ref, kw_ref, qkv0_ref, qkv1_ref, qkv2_ref, zs_ref,
                      slab_ref, w_ref, stage_ref, sem, *, layer, w_start):
    @pl.when(_first_step())
    def _():
        _fetch_bf16([(w_hbm.at[layer, :, pl.ds(w_start, w_ref.shape[1])], w_ref)], stage_ref, sem)

    tm, d = h_ref.shape
    h = h_ref[...]
    for j in range(d // LANES):
        slab_ref[j] = h[:, j * LANES:(j + 1) * LANES].astype(F32)
    ad = N_GROUPS * GROUP_DIM
    for g, out_ref in enumerate((qkv0_ref, qkv1_ref, qkv2_ref)):
        hg = h if DILATIONS[g] == 1 else _to_dilated(slab_ref, tm, DILATIONS[g]).astype(BF16)
        q, k, v = [jnp.dot(hg, w_ref[:, part * ad + g * GROUP_DIM:part * ad + (g + 1) * GROUP_DIM],
                           preferred_element_type=F32) for part in range(3)]
        qkv = jnp.concatenate([_head_rms(q, qw_ref[...]), _head_rms(k, kw_ref[...]), v], axis=-1)
        out_ref[...] = qkv.astype(out_ref.dtype).reshape(out_ref.shape)
        if g == 0:
            z = jnp.dot(hg, w_ref[:, 3 * ad:], preferred_element_type=F32)
            zs_ref[...] = _silu(z)


def _attn_proj(h, w_in, layer, w_start, q_norm_w, k_norm_w):
    bsz, s, d = h.shape
    tm = ROW_TILE
    qw = jnp.tile(q_norm_w * (HEAD_DIM ** -0.5 * LOG2_E), ATTN_SLOTS).reshape(1, GROUP_DIM)
    kw = jnp.tile(k_norm_w, ATTN_SLOTS).reshape(1, GROUP_DIM)
    gw = 3 * GROUP_DIM
    w_cols = (3 * N_GROUPS + 1) * GROUP_DIM
    out_shapes, out_specs = [], []
    for dil in DILATIONS:
        tile = Q_BLOCK * dil
        assert dil == 1 or tile % tm == 0
        if tile <= tm:
            nblk = tm // Q_BLOCK
            out_shapes.append(jax.ShapeDtypeStruct((bsz, s // Q_BLOCK, Q_BLOCK, gw), BF16))
            out_specs.append(pl.BlockSpec((None, nblk, Q_BLOCK, gw), lambda b, i: (b, i, 0, 0)))
        else:
            per = tm // dil
            steps = tile // tm
            out_shapes.append(jax.ShapeDtypeStruct((bsz, s // tile, dil, Q_BLOCK, gw), BF16))
            out_specs.append(pl.BlockSpec((None, None, dil, per, gw),
                                          lambda b, i, steps=steps: (b, i // steps, 0, i % steps, 0)))
    out_shapes.append(jax.ShapeDtypeStruct((bsz, s, GROUP_DIM), F32))
    out_specs.append(pl.BlockSpec((None, tm, GROUP_DIM), lambda b, i: (b, i, 0)))
    outs = pl.pallas_call(
        functools.partial(_attn_proj_kernel, layer=layer, w_start=w_start),
        grid=(bsz, s // tm),
        in_specs=[pl.BlockSpec((None, tm, d), lambda b, i: (b, i, 0)),
                  _HBM,
                  pl.BlockSpec((1, GROUP_DIM), lambda b, i: (0, 0)),
                  pl.BlockSpec((1, GROUP_DIM), lambda b, i: (0, 0))],
        out_specs=out_specs,
        out_shape=out_shapes,
        scratch_shapes=[pltpu.VMEM((d // LANES, tm, LANES), F32),
                        pltpu.VMEM((d, w_cols), BF16),
                        *_stage_scratch(w_cols)],
        compiler_params=_params(2),
        name="attn_proj",
    )(h, w_in, qw, kw)
    qkv = [t.reshape(bsz, s // Q_BLOCK, Q_BLOCK, gw) for t in outs[:N_GROUPS]]
    return qkv, outs[N_GROUPS]


def _stat_lane(head):
    pair, odd = divmod(head, 2)
    return pair * LSE_LANES + (0 if odd else HEAD_DIM)


def _attn_block(q, kp, kc, vp, vc, bias, o_ref, m_ref, l_ref, j):
    blk = q.shape[0]
    lane = lax.broadcasted_iota(jnp.int32, (blk, LANES), 1)
    lane_lo = lane < HEAD_DIM
    lane2_lo = lax.broadcasted_iota(jnp.int32, (2 * blk, LANES), 1) < HEAD_DIM
    zero = jnp.zeros((blk, LANES), q.dtype)
    one = jnp.ones((2 * blk, LANES), q.dtype)
    m_tile = jnp.zeros((blk, LANES), F32)
    l_tile = jnp.ones((blk, LANES), F32)
    for p in range(GROUP_DIM // LANES):
        sl = slice(p * LANES, (p + 1) * LANES)
        q2 = q[:, sl]
        qs = jnp.concatenate([jnp.where(lane_lo, q2, zero), jnp.where(lane_lo, zero, q2)], axis=0)
        k2 = jnp.concatenate([kp[:, sl], kc[:, sl]], axis=0)
        v2 = jnp.concatenate([vp[:, sl], vc[:, sl]], axis=0)
        s = lax.dot_general(qs, k2, (((1,), (1,)), ((), ())), preferred_element_type=F32) + bias
        m = jnp.max(s, axis=-1, keepdims=True)
        e = jnp.exp2(s - m).astype(BF16)
        o_even = jnp.dot(e[:blk], jnp.where(lane2_lo, v2, one), preferred_element_type=F32)
        o_odd = jnp.dot(e[blk:], jnp.where(lane2_lo, one, v2), preferred_element_type=F32)
        o_ref[j + (slice(None), sl)] = jnp.where(lane_lo, o_even, o_odd)
        for head, m_h, o_h in ((2 * p, m[:blk], o_even), (2 * p + 1, m[blk:], o_odd)):
            slot = (lane >= _stat_lane(head)) & (lane < _stat_lane(head) + LSE_LANES)
            m_tile = jnp.where(slot, m_h, m_tile)
            l_tile = jnp.where(slot, o_h, l_tile)
    m_ref[j] = m_tile
    l_ref[j] = l_tile


def _attn_kernel(q_ref, kprev_ref, k_ref, vprev_ref, v_ref, o_ref, m_ref, l_ref):
    n_blocks, n_res, blk, _ = q_ref.shape
    span = 2 * blk
    row = lax.broadcasted_iota(jnp.int32, (span, span), 0) % blk
    col = lax.broadcasted_iota(jnp.int32, (span, span), 1)
    in_window = (col >= row) & (col <= row + blk)
    bias = jnp.where(in_window, 0.0, NEG_INF)
    run_has_prev = pl.program_id(2) > 0
    bias_first = jnp.where(in_window & ((col >= blk) | run_has_prev), 0.0, NEG_INF)

    for r in range(n_res):
        _attn_block(q_ref[0, r], kprev_ref[r], k_ref[0, r], vprev_ref[r], v_ref[0, r], bias_first,
                    o_ref, m_ref, l_ref, (0, r))
        for j in range(1, n_blocks):
            _attn_block(q_ref[j, r], k_ref[j - 1, r], k_ref[j, r], v_ref[j - 1, r], v_ref[j, r], bias,
                        o_ref, m_ref, l_ref, (j, r))


def _dilated_attention(qkv, dil):
    bsz, nb, blk, width = qkv.shape
    gd = GROUP_DIM
    tiles = nb // dil
    run = min(ATTN_STEP_BLOCKS, tiles)
    res = min(ATTN_STEP_BLOCKS // run, dil)
    assert tiles % run == 0 and dil % res == 0
    view = qkv.reshape(bsz, tiles, dil, blk, width)
    cur = lambda part: pl.BlockSpec((None, run, res, blk, gd), lambda b, r, c: (b, c, r, 0, part))
    prev = lambda part: pl.BlockSpec((None, None, res, blk, gd),
                                     lambda b, r, c: (b, jnp.maximum(c * run - 1, 0), r, 0, part))
    stat_spec = pl.BlockSpec((None, run, res, blk, LANES), lambda b, r, c: (b, c, r, 0, 0))
    stat_shape = jax.ShapeDtypeStruct((bsz, tiles, dil, blk, LANES), F32)
    o, m, l = pl.pallas_call(
        _attn_kernel,
        grid=(bsz, dil // res, tiles // run),
        in_specs=[cur(0), prev(1), cur(1), prev(2), cur(2)],
        out_specs=[pl.BlockSpec((None, run, res, blk, gd), lambda b, r, c: (b, c, r, 0, 0)),
                   stat_spec, stat_spec],
        out_shape=[jax.ShapeDtypeStruct((bsz, tiles, dil, blk, gd), F32), stat_shape, stat_shape],
        compiler_params=_params(3),
        name=f"dilated_attn_d{dil}",
    )(view, view, view, view, view)
    return (o.reshape(bsz, nb, blk, gd), m.reshape(bsz, nb, blk, LANES),
            l.reshape(bsz, nb, blk, LANES))


def _to_sequence(src_ref, slab_ref, dil):
    per = src_ref.shape[1]
    n_slabs = src_ref.shape[2] // LANES
    for j in range(n_slabs):
        for r in range(dil):
            slab_ref[j, pl.ds(r, per, stride=dil), :] = src_ref[r, :, j * LANES:(j + 1) * LANES]
    return jnp.concatenate([slab_ref[j] for j in range(n_slabs)], axis=-1)


def _merge_kernel(x_ref, mod_ref, h_ref, ya_ref, zs_ref,
                  o0_ref, o1_ref, o2_ref, m0_ref, m1_ref, m2_ref, l0_ref, l1_ref, l2_ref,
                  win_hbm, pa_hbm, pb_hbm, wo_hbm, out_ref,
                  slab_ref, wg_ref, pa_ref, pb_ref, wo_ref, stage_ref, sem, *, layer, w_start):
    d = x_ref.shape[-1]

    @pl.when(_first_step())
    def _():
        _fetch_bf16([(win_hbm.at[layer, :, pl.ds(w_start, 2 * d)], wg_ref),
                     (pa_hbm.at[layer], pa_ref), (pb_hbm.at[layer], pb_ref),
                     (wo_hbm.at[layer], wo_ref)], stage_ref, sem)

    h = h_ref[...]
    g_a = jax.nn.sigmoid(jnp.dot(h, wg_ref[:, :d], preferred_element_type=F32))
    g_b = jax.nn.sigmoid(jnp.dot(h, wg_ref[:, d:], preferred_element_type=F32))

    def in_sequence(refs):
        return [refs[0][...]] + [_to_sequence(r, slab_ref, dil) for r, dil in zip(refs[1:], DILATIONS[1:])]

    outs = in_sequence((o0_ref, o1_ref, o2_ref))
    maxes = in_sequence((m0_ref, m1_ref, m2_ref))
    sums = in_sequence((l0_ref, l1_ref, l2_ref))
    top = jnp.maximum(jnp.maximum(maxes[0], maxes[1]), maxes[2])
    wts = [jnp.exp2(t - top) for t in maxes]
    den = wts[0] * sums[0] + wts[1] * sums[1] + wts[2] * sums[2]
    attn = jnp.zeros_like(outs[0])
    lane_lo = lax.broadcasted_iota(jnp.int32, (h.shape[0], LANES), 1) < HEAD_DIM
    for w, o in zip(wts, outs):
        wn = w / den
        wide = jnp.concatenate(
            [jnp.where(lane_lo, wn[:, _stat_lane(2 * p):_stat_lane(2 * p) + 1],
                       wn[:, _stat_lane(2 * p + 1):_stat_lane(2 * p + 1) + 1])
             for p in range(GROUP_DIM // LANES)], axis=-1)
        attn = attn + wide * o
    y_b = (attn * zs_ref[...]).astype(BF16)

    merged = (g_a * jnp.dot(ya_ref[...], pa_ref[...], preferred_element_type=F32)
              + g_b * jnp.dot(y_b, pb_ref[...], preferred_element_type=F32))
    upd = jnp.dot(merged.astype(BF16), wo_ref[...], preferred_element_type=F32)
    out_ref[...] = x_ref[...] + mod_ref[2:3, :] * upd


def _merge(x, mod3, h, y_a, zs, outs, maxes, sums, w_in, layer, w_start, p_a, p_b, w_o):
    bsz, s, d = x.shape
    tm = OUT_ROW_TILE
    row_d = pl.BlockSpec((None, tm, d), lambda b, i: (b, i, 0))
    row_g = pl.BlockSpec((None, tm, GROUP_DIM), lambda b, i: (b, i, 0))

    def dilated(t, dil):
        width = t.shape[-1]
        tile = Q_BLOCK * dil
        per = tm // dil
        if tile % tm:
            raise NotImplementedError("a dilated-order tile must be a whole number of merge tiles")
        steps = tile // tm
        view = t.reshape(bsz, s // tile, dil, Q_BLOCK, width)
        spec = pl.BlockSpec((None, None, dil, per, width),
                            lambda b, i: (b, i // steps, 0, i % steps, 0))
        return view, spec

    views, specs = [], []
    for group in (outs, maxes, sums):
        for t, dil in zip(group, DILATIONS):
            width = t.shape[-1]
            if dil == 1:
                views.append(t.reshape(bsz, s, width))
                specs.append(pl.BlockSpec((None, tm, width), lambda b, i: (b, i, 0)))
            else:
                v, sp = dilated(t, dil)
                views.append(v)
                specs.append(sp)

    return pl.pallas_call(
        functools.partial(_merge_kernel, layer=layer, w_start=w_start),
        grid=(bsz, s // tm),
        in_specs=[row_d, pl.BlockSpec((None, 3, d), lambda b, i: (b, 0, 0)), row_d, row_d, row_g,
                  *specs,
                  _HBM, _HBM, _HBM, _HBM],
        out_specs=row_d,
        out_shape=jax.ShapeDtypeStruct((bsz, s, d), F32),
        scratch_shapes=[pltpu.VMEM((GROUP_DIM // LANES, tm, LANES), F32),
                        pltpu.VMEM((d, 2 * d), BF16),
                        pltpu.VMEM(p_a.shape[1:], BF16),
                        pltpu.VMEM(p_b.shape[1:], BF16),
                        pltpu.VMEM(w_o.shape[1:], BF16),
                        *_stage_scratch(2 * d)],
        compiler_params=_params(2),
        name="merge_out",
    )(x, mod3, h, y_a, zs, *views, w_in, p_a, p_b, w_o)


def _layer(x, c, layer, w_ada, b_ada, norm_w, w_in, conv_w, q_norm_w, k_norm_w, w_br_conv, w_br_attn, w_out):
    bsz, s, d = x.shape
    conv_dim = conv_w.shape[-1]
    attn_dim = N_GROUPS * GROUP_DIM
    assert conv_dim == d and conv_w.shape[0] == CONV_WIDTH and d % LANES == 0 and d % STAGE_ROWS == 0
    assert w_in.shape[2] == 4 * conv_dim + 3 * attn_dim + GROUP_DIM + 2 * d
    assert w_br_attn.shape[1] % STAGE_ROWS == 0
    assert s % (max(DILATIONS) * Q_BLOCK) == 0 and s % ROW_TILE == 0 and s % MIXER_ROW_TILE == 0
    assert all(w // dil == Q_BLOCK for w, dil in zip(WINDOWS, DILATIONS))

    mod3 = _modulation(c, w_ada, b_ada)
    nw = norm_w.reshape(1, d)
    o_attn = 4 * conv_dim
    o_gate = o_attn + 3 * attn_dim + GROUP_DIM

    y_a, h = _mixer_a(x, mod3, nw, w_in, layer, conv_w)
    qkv, zs = _attn_proj(h, w_in, layer, o_attn, q_norm_w, k_norm_w)
    outs, maxes, sums = zip(*[_dilated_attention(t, dil) for t, dil in zip(qkv, DILATIONS)])
    return _merge(x, mod3, h, y_a, zs, outs, maxes, sums, w_in, layer, o_gate, w_br_conv, w_br_attn, w_out)


@jax.jit
def kernel(x, c, w_ada, b_ada, norm_w, w_in, conv_w, q_norm_w, k_norm_w, w_br_conv, w_br_attn, w_out):
    depth = w_ada.shape[0]
    for l in range(depth):
        x = _layer(x, c, l, w_ada[l], b_ada[l], norm_w[l], w_in, conv_w[l], q_norm_w[l],
                   k_norm_w[l], w_br_conv, w_br_attn, w_out)
    return x
```

```python
import functools

import jax
import jax.numpy as jnp
from jax import lax
from jax.experimental import pallas as pl
from jax.experimental.pallas import tpu as pltpu

F32 = jnp.float32
BF16 = jnp.bfloat16

HEAD_DIM = 64
ATTN_SLOTS = 8
WINDOWS = (128, 512, 2048)
DILATIONS = (1, 4, 16)
N_GROUPS = len(WINDOWS)
GROUP_DIM = ATTN_SLOTS * HEAD_DIM
Q_BLOCK = 128
CONV_WIDTH = 3
EPS = 1e-6
NEG_INF = -1e30
LOG2_E = 1.4426950408889634
LANES = 128
SUBLANES = 8
LSE_LANES = LANES // ATTN_SLOTS
VMEM_LIMIT = 56 * 1024 * 1024

MIXER_ROW_TILE = 1024
STAGE_ROWS = 128
STAGE_BYTES_IN_FLIGHT = 6 * 1024 * 1024
ROW_TILE = 512
OUT_ROW_TILE = 512
ATTN_STEP_BLOCKS = 16


def _params(n_axes):
    return pltpu.CompilerParams(dimension_semantics=("arbitrary",) * n_axes,
                                vmem_limit_bytes=VMEM_LIMIT)


def _silu(t):
    return t * jax.nn.sigmoid(t)


def _modulated_norm(x, norm_w, mod):
    ms = jnp.mean(x * x, axis=-1, keepdims=True)
    xn = x * lax.rsqrt(ms + EPS) * norm_w
    return xn * (1.0 + mod[1:2, :]) + mod[0:1, :]


def _to_dilated(slab_ref, n_rows, dil):
    per = n_rows // dil
    return jnp.concatenate(
        [jnp.concatenate([slab_ref[j, pl.ds(r, per, stride=dil), :] for r in range(dil)], axis=0)
         for j in range(slab_ref.shape[0])], axis=-1)


def _fetch_bf16(pairs, stage_ref, sem):
    slots, rows, _ = stage_ref.shape
    chunks = [(src, dst, r0) for src, dst in pairs for r0 in range(0, dst.shape[0], rows)]

    def copy(i):
        src, dst, r0 = chunks[i]
        return pltpu.make_async_copy(src.at[pl.ds(r0, rows), :],
                                     stage_ref.at[i % slots, :, pl.ds(0, dst.shape[1])],
                                     sem.at[i % slots])

    for i in range(min(slots - 1, len(chunks))):
        copy(i).start()
    for i, (_, dst, r0) in enumerate(chunks):
        if i + slots - 1 < len(chunks):
            copy(i + slots - 1).start()
        copy(i).wait()
        dst[r0:r0 + rows, :] = stage_ref[i % slots, :, 0:dst.shape[1]].astype(dst.dtype)


def _first_step():
    return (pl.program_id(0) == 0) & (pl.program_id(1) == 0)


_HBM = pl.BlockSpec(memory_space=pl.ANY)


def _resident(a):
    return pl.BlockSpec(a.shape, lambda *_: (0,) * a.ndim, pipeline_mode=pl.Buffered(1))


def _stage_scratch(width):
    chunk_bytes = STAGE_ROWS * width * 4
    slots = -(-STAGE_BYTES_IN_FLIGHT // chunk_bytes) + 1
    return [pltpu.VMEM((slots, STAGE_ROWS, width), F32), pltpu.SemaphoreType.DMA((slots,))]


def _mod_kernel(c_ref, w_ref, b_ref, o_ref):
    d = c_ref.shape[-1]
    sc = _silu(c_ref[...])
    for t in range(o_ref.shape[1]):
        cols = slice(t * d, (t + 1) * d)
        o_ref[:, t, :] = jnp.dot(sc, w_ref[:, cols], preferred_element_type=F32) + b_ref[:, cols]


def _modulation(c, w_ada, b_ada):
    bsz, d = c.shape
    n = w_ada.shape[1]
    whole = lambda shape: pl.BlockSpec(shape, lambda j: (0,) * len(shape), pipeline_mode=pl.Buffered(1))
    return pl.pallas_call(
        _mod_kernel,
        grid=(1,),
        in_specs=[whole((bsz, d)), whole((d, n)), whole((1, n))],
        out_specs=pl.BlockSpec((bsz, n // d, d), lambda j: (0, 0, 0)),
        out_shape=jax.ShapeDtypeStruct((bsz, n // d, d), F32),
        compiler_params=_params(1),
        name="adaln_mod",
    )(c, w_ada, b_ada.reshape(1, n))


def _mixer_a_kernel(x_ref, mod_ref, nw_ref, w_hbm, cw_ref, win_rows_ref, pa_rows_ref, pb_rows_ref,
                    wo_rows_ref, y_ref, h_ref, wattn_ref, wgate_ref, pa_ref, pb_ref, wo_ref,
                    tail_ref, w_ref, stage_ref, sem, *, layer):
    tm, d = x_ref.shape
    mod = mod_ref[...]
    nw = nw_ref[...]
    cw = cw_ref[...]

    @pl.when(_first_step())
    def _():
        _fetch_bf16([(w_hbm.at[layer, :, pl.ds(0, w_ref.shape[1])], w_ref)], stage_ref, sem)

    @pl.when(pl.program_id(1) == 0)
    def _():
        tail_ref[...] = jnp.zeros_like(tail_ref)

    n_attn = wattn_ref.shape[1]
    wattn_ref[...] = win_rows_ref[:, 4 * d:4 * d + n_attn].astype(wattn_ref.dtype)
    wgate_ref[...] = win_rows_ref[:, 4 * d + n_attn:].astype(wgate_ref.dtype)
    pa_ref[...] = pa_rows_ref[...].astype(pa_ref.dtype)
    pb_ref[...] = pb_rows_ref[...].astype(pb_ref.dtype)
    wo_ref[...] = wo_rows_ref[...].astype(wo_ref.dtype)

    prev1 = tail_ref[SUBLANES - 1:SUBLANES, :]
    prev2 = tail_ref[SUBLANES - 2:SUBLANES - 1, :]

    h = _modulated_norm(x_ref[...], nw, mod).astype(BF16)
    h_ref[...] = h
    proj = lambda j: jnp.dot(h, w_ref[:, j * d:(j + 1) * d], preferred_element_type=F32)
    z_a, c_a, x_a = proj(3), proj(1), proj(2)
    u = c_a * x_a
    row = lax.broadcasted_iota(jnp.int32, (tm, d), 0)
    u_m1 = jnp.where(row == 0, prev1, pltpu.roll(u, 1, axis=0))
    u_m2 = jnp.where(row == 0, prev2, jnp.where(row == 1, prev1, pltpu.roll(u, 2, axis=0)))
    conv = cw[0:1, :] * u_m2 + cw[1:2, :] * u_m1 + cw[2:3, :] * u
    gated = conv * _silu(z_a)
    y_ref[...] = (proj(0) * gated).astype(y_ref.dtype)
    tail_ref[...] = u[tm - SUBLANES:, :]


def _mixer_a(x, mod3, norm_w, w_in, layer, conv_w, p_a, p_b, w_o):
    bsz, s, d = x.shape
    tm = MIXER_ROW_TILE
    tiles = s // tm
    n_steps = bsz * tiles
    n_attn = (3 * N_GROUPS + 1) * GROUP_DIM
    rows = pl.BlockSpec((None, tm, d), lambda b, i: (b, i, 0))

    def slab(a, layered):
        k, n = a.shape[-2:]
        assert k % (n_steps * 2 * SUBLANES) == 0
        step = lambda b, i: b * tiles + i
        if layered:
            return pl.BlockSpec((None, k // n_steps, n), lambda b, i: (layer, step(b, i), 0))
        return pl.BlockSpec((k // n_steps, n), lambda b, i: (step(b, i), 0))

    bf16 = lambda k, n: jax.ShapeDtypeStruct((k, n), BF16)
    out_shapes = [jax.ShapeDtypeStruct((bsz, s, d), BF16)] * 2 + [
        bf16(d, n_attn), bf16(d, 2 * d), bf16(*p_a.shape[1:]), bf16(*p_b.shape[1:]), bf16(*w_o.shape[1:])]
    return pl.pallas_call(
        functools.partial(_mixer_a_kernel, layer=layer),
        grid=(bsz, tiles),
        in_specs=[rows,
                  pl.BlockSpec((None, 3, d), lambda b, i: (b, 0, 0)),
                  pl.BlockSpec((1, d), lambda b, i: (0, 0)),
                  _HBM,
                  pl.BlockSpec(conv_w.shape, lambda b, i: (0, 0)),
                  slab(w_in, True), slab(p_a, True), slab(p_b, True), slab(w_o, True)],
        out_specs=[rows, rows] + [slab(t, False) for t in out_shapes[2:]],
        out_shape=out_shapes,
        scratch_shapes=[pltpu.VMEM((SUBLANES, d), F32),
                        pltpu.VMEM((d, 4 * d), BF16),
                        *_stage_scratch(4 * d)],
        compiler_params=_params(2),
        name="mixer_a",
    )(x, mod3, norm_w, w_in, conv_w, w_in, p_a, p_b, w_o)


def _head_rms(t, w):
    sq = t * t
    lane_lo = lax.broadcasted_iota(jnp.int32, (t.shape[0], LANES), 1) < HEAD_DIM
    parts = []
    for j in range(0, t.shape[-1], LANES):
        tile = sq[:, j:j + LANES]
        even = jnp.sum(jnp.where(lane_lo, tile, 0.0), axis=-1, keepdims=True)
        odd = jnp.sum(jnp.where(lane_lo, 0.0, tile), axis=-1, keepdims=True)
        parts.append(jnp.where(lane_lo, even, odd))
    ssq = jnp.concatenate(parts, axis=-1)
    return t * lax.rsqrt(ssq * (1.0 / HEAD_DIM) + EPS) * w


def _attn_proj_kernel(h_ref, w_ref, qw_ref, kw_ref, qkv0_ref, qkv1_ref, qkv2_ref, zs_ref, slab_ref):
    tm, d = h_ref.shape
    h = h_ref[...]
    for j in range(d // LANES):
        slab_ref[j] = h[:, j * LANES:(j + 1) * LANES].astype(F32)
    ad = N_GROUPS * GROUP_DIM
    for g, out_ref in enumerate((qkv0_ref, qkv1_ref, qkv2_ref)):
        hg = h if DILATIONS[g] == 1 else _to_dilated(slab_ref, tm, DILATIONS[g]).astype(BF16)
        q, k, v = [jnp.dot(hg, w_ref[:, part * ad + g * GROUP_DIM:part * ad + (g + 1) * GROUP_DIM],
                           preferred_element_type=F32) for part in range(3)]
        qkv = jnp.concatenate([_head_rms(q, qw_ref[...]), _head_rms(k, kw_ref[...]), v], axis=-1)
        out_ref[...] = qkv.astype(out_ref.dtype).reshape(out_ref.shape)
        if g == 0:
            z = jnp.dot(hg, w_ref[:, 3 * ad:], preferred_element_type=F32)
            zs_ref[...] = _silu(z)


def _attn_proj(h, w_attn, q_norm_w, k_norm_w):
    bsz, s, d = h.shape
    tm = ROW_TILE
    qw = jnp.tile(q_norm_w * (HEAD_DIM ** -0.5 * LOG2_E), ATTN_SLOTS).reshape(1, GROUP_DIM)
    kw = jnp.tile(k_norm_w, ATTN_SLOTS).reshape(1, GROUP_DIM)
    gw = 3 * GROUP_DIM
    out_shapes, out_specs = [], []
    for dil in DILATIONS:
        tile = Q_BLOCK * dil
        assert dil == 1 or tile % tm == 0
        if tile <= tm:
            nblk = tm // Q_BLOCK
            out_shapes.append(jax.ShapeDtypeStruct((bsz, s // Q_BLOCK, Q_BLOCK, gw), BF16))
            out_specs.append(pl.BlockSpec((None, nblk, Q_BLOCK, gw), lambda b, i: (b, i, 0, 0)))
        else:
            per = tm // dil
            steps = tile // tm
            out_shapes.append(jax.ShapeDtypeStruct((bsz, s // tile, dil, Q_BLOCK, gw), BF16))
            out_specs.append(pl.BlockSpec((None, None, dil, per, gw),
                                          lambda b, i, steps=steps: (b, i // steps, 0, i % steps, 0)))
    out_shapes.append(jax.ShapeDtypeStruct((bsz, s, GROUP_DIM), F32))
    out_specs.append(pl.BlockSpec((None, tm, GROUP_DIM), lambda b, i: (b, i, 0)))
    outs = pl.pallas_call(
        _attn_proj_kernel,
        grid=(bsz, s // tm),
        in_specs=[pl.BlockSpec((None, tm, d), lambda b, i: (b, i, 0)),
                  _resident(w_attn),
                  pl.BlockSpec((1, GROUP_DIM), lambda b, i: (0, 0)),
                  pl.BlockSpec((1, GROUP_DIM), lambda b, i: (0, 0))],
        out_specs=out_specs,
        out_shape=out_shapes,
        scratch_shapes=[pltpu.VMEM((d // LANES, tm, LANES), F32)],
        compiler_params=_params(2),
        name="attn_proj",
    )(h, w_attn, qw, kw)
    qkv = [t.reshape(bsz, s // Q_BLOCK, Q_BLOCK, gw) for t in outs[:N_GROUPS]]
    return qkv, outs[N_GROUPS]


def _stat_lane(head):
    pair, odd = divmod(head, 2)
    return pair * LSE_LANES + (0 if odd else HEAD_DIM)


def _attn_block(q, kp, kc, vp, vc, bias, o_ref, m_ref, l_ref, j):
    blk = q.shape[0]
    lane = lax.broadcasted_iota(jnp.int32, (blk, LANES), 1)
    lane_lo = lane < HEAD_DIM
    lane2_lo = lax.broadcasted_iota(jnp.int32, (2 * blk, LANES), 1) < HEAD_DIM
    zero = jnp.zeros((blk, LANES), q.dtype)
    one = jnp.ones((2 * blk, LANES), q.dtype)
    m_tile = jnp.zeros((blk, LANES), F32)
    l_tile = jnp.ones((blk, LANES), F32)
    for p in range(GROUP_DIM // LANES):
        sl = slice(p * LANES, (p + 1) * LANES)
        q2 = q[:, sl]
        qs = jnp.concatenate([jnp.where(lane_lo, q2, zero), jnp.where(lane_lo, zero, q2)], axis=0)
        k2 = jnp.concatenate([kp[:, sl], kc[:, sl]], axis=0)
        v2 = jnp.concatenate([vp[:, sl], vc[:, sl]], axis=0)
        s = lax.dot_general(qs, k2, (((1,), (1,)), ((), ())), preferred_element_type=F32) + bias
        m = jnp.max(s, axis=-1, keepdims=True)
        e = jnp.exp2(s - m).astype(BF16)
        o_even = jnp.dot(e[:blk], jnp.where(lane2_lo, v2, one), preferred_element_type=F32)
        o_odd = jnp.dot(e[blk:], jnp.where(lane2_lo, one, v2), preferred_element_type=F32)
        o_ref[j + (slice(None), sl)] = jnp.where(lane_lo, o_even, o_odd)
        for head, m_h, o_h in ((2 * p, m[:blk], o_even), (2 * p + 1, m[blk:], o_odd)):
            slot = (lane >= _stat_lane(head)) & (lane < _stat_lane(head) + LSE_LANES)
            m_tile = jnp.where(slot, m_h, m_tile)
            l_tile = jnp.where(slot, o_h, l_tile)
    m_ref[j] = m_tile
    l_ref[j] = l_tile


def _attn_kernel(q_ref, kprev_ref, k_ref, vprev_ref, v_ref, o_ref, m_ref, l_ref):
    n_blocks, n_res, blk, _ = q_ref.shape
    span = 2 * blk
    row = lax.broadcasted_iota(jnp.int32, (span, span), 0) % blk
    col = lax.broadcasted_iota(jnp.int32, (span, span), 1)
    in_window = (col >= row) & (col <= row + blk)
    bias = jnp.where(in_window, 0.0, NEG_INF)
    run_has_prev = pl.program_id(2) > 0
    bias_first = jnp.where(in_window & ((col >= blk) | run_has_prev), 0.0, NEG_INF)

    for r in range(n_res):
        _attn_block(q_ref[0, r], kprev_ref[r], k_ref[0, r], vprev_ref[r], v_ref[0, r], bias_first,
                    o_ref, m_ref, l_ref, (0, r))
        for j in range(1, n_blocks):
            _attn_block(q_ref[j, r], k_ref[j - 1, r], k_ref[j, r], v_ref[j - 1, r], v_ref[j, r], bias,
                        o_ref, m_ref, l_ref, (j, r))


def _dilated_attention(qkv, dil):
    bsz, nb, blk, width = qkv.shape
    gd = GROUP_DIM
    tiles = nb // dil
    run = min(ATTN_STEP_BLOCKS, tiles)
    res = min(ATTN_STEP_BLOCKS // run, dil)
    assert tiles % run == 0 and dil % res == 0
    view = qkv.reshape(bsz, tiles, dil, blk, width)
    cur = lambda part: pl.BlockSpec((None, run, res, blk, gd), lambda b, r, c: (b, c, r, 0, part))
    prev = lambda part: pl.BlockSpec((None, None, res, blk, gd),
                                     lambda b, r, c: (b, jnp.maximum(c * run - 1, 0), r, 0, part))
    stat_spec = pl.BlockSpec((None, run, res, blk, LANES), lambda b, r, c: (b, c, r, 0, 0))
    stat_shape = jax.ShapeDtypeStruct((bsz, tiles, dil, blk, LANES), F32)
    o, m, l = pl.pallas_call(
        _attn_kernel,
        grid=(bsz, dil // res, tiles // run),
        in_specs=[cur(0), prev(1), cur(1), prev(2), cur(2)],
        out_specs=[pl.BlockSpec((None, run, res, blk, gd), lambda b, r, c: (b, c, r, 0, 0)),
                   stat_spec, stat_spec],
        out_shape=[jax.ShapeDtypeStruct((bsz, tiles, dil, blk, gd), F32), stat_shape, stat_shape],
        compiler_params=_params(3),
        name=f"dilated_attn_d{dil}",
    )(view, view, view, view, view)
    return (o.reshape(bsz, nb, blk, gd), m.reshape(bsz, nb, blk, LANES),
            l.reshape(bsz, nb, blk, LANES))


def _to_sequence(src_ref, slab_ref, dil):
    per = src_ref.shape[1]
    n_slabs = src_ref.shape[2] // LANES
    for j in range(n_slabs):
        for r in range(dil):
            slab_ref[j, pl.ds(r, per, stride=dil), :] = src_ref[r, :, j * LANES:(j + 1) * LANES]
    return jnp.concatenate([slab_ref[j] for j in range(n_slabs)], axis=-1)


def _merge_kernel(x_ref, mod_ref, h_ref, ya_ref, zs_ref,
                  o0_ref, o1_ref, o2_ref, m0_ref, m1_ref, m2_ref, l0_ref, l1_ref, l2_ref,
                  wg_ref, pa_ref, pb_ref, wo_ref, out_ref, slab_ref):
    d = x_ref.shape[-1]
    h = h_ref[...]
    g_a = jax.nn.sigmoid(jnp.dot(h, wg_ref[:, :d], preferred_element_type=F32))
    g_b = jax.nn.sigmoid(jnp.dot(h, wg_ref[:, d:], preferred_element_type=F32))

    def in_sequence(refs):
        return [refs[0][...]] + [_to_sequence(r, slab_ref, dil) for r, dil in zip(refs[1:], DILATIONS[1:])]

    outs = in_sequence((o0_ref, o1_ref, o2_ref))
    maxes = in_sequence((m0_ref, m1_ref, m2_ref))
    sums = in_sequence((l0_ref, l1_ref, l2_ref))
    top = jnp.maximum(jnp.maximum(maxes[0], maxes[1]), maxes[2])
    wts = [jnp.exp2(t - top) for t in maxes]
    den = wts[0] * sums[0] + wts[1] * sums[1] + wts[2] * sums[2]
    attn = jnp.zeros_like(outs[0])
    lane_lo = lax.broadcasted_iota(jnp.int32, (h.shape[0], LANES), 1) < HEAD_DIM
    for w, o in zip(wts, outs):
        wn = w / den
        wide = jnp.concatenate(
            [jnp.where(lane_lo, wn[:, _stat_lane(2 * p):_stat_lane(2 * p) + 1],
                       wn[:, _stat_lane(2 * p + 1):_stat_lane(2 * p + 1) + 1])
             for p in range(GROUP_DIM // LANES)], axis=-1)
        attn = attn + wide * o
    y_b = (attn * zs_ref[...]).astype(BF16)

    merged = (g_a * jnp.dot(ya_ref[...], pa_ref[...], preferred_element_type=F32)
              + g_b * jnp.dot(y_b, pb_ref[...], preferred_element_type=F32))
    upd = jnp.dot(merged.astype(BF16), wo_ref[...], preferred_element_type=F32)
    out_ref[...] = x_ref[...] + mod_ref[2:3, :] * upd


def _merge(x, mod3, h, y_a, zs, outs, maxes, sums, w_gate, p_a, p_b, w_o):
    bsz, s, d = x.shape
    tm = OUT_ROW_TILE
    row_d = pl.BlockSpec((None, tm, d), lambda b, i: (b, i, 0))
    row_g = pl.BlockSpec((None, tm, GROUP_DIM), lambda b, i: (b, i, 0))

    def dilated(t, dil):
        width = t.shape[-1]
        tile = Q_BLOCK * dil
        per = tm // dil
        if tile % tm:
            raise NotImplementedError("a dilated-order tile must be a whole number of merge tiles")
        steps = tile // tm
        view = t.reshape(bsz, s // tile, dil, Q_BLOCK, width)
        spec = pl.BlockSpec((None, None, dil, per, width),
                            lambda b, i: (b, i // steps, 0, i % steps, 0))
        return view, spec

    views, specs = [], []
    for group in (outs, maxes, sums):
        for t, dil in zip(group, DILATIONS):
            width = t.shape[-1]
            if dil == 1:
                views.append(t.reshape(bsz, s, width))
                specs.append(pl.BlockSpec((None, tm, width), lambda b, i: (b, i, 0)))
            else:
                v, sp = dilated(t, dil)
                views.append(v)
                specs.append(sp)

    return pl.pallas_call(
        _merge_kernel,
        grid=(bsz, s // tm),
        in_specs=[row_d, pl.BlockSpec((None, 3, d), lambda b, i: (b, 0, 0)), row_d, row_d, row_g,
                  *specs,
                  _resident(w_gate), _resident(p_a), _resident(p_b), _resident(w_o)],
        out_specs=row_d,
        out_shape=jax.ShapeDtypeStruct((bsz, s, d), F32),
        scratch_shapes=[pltpu.VMEM((GROUP_DIM // LANES, tm, LANES), F32)],
        compiler_params=_params(2),
        name="merge_out",
    )(x, mod3, h, y_a, zs, *views, w_gate, p_a, p_b, w_o)


def _layer(x, c, layer, w_ada, b_ada, norm_w, w_in, conv_w, q_norm_w, k_norm_w, w_br_conv, w_br_attn, w_out):
    bsz, s, d = x.shape
    conv_dim = conv_w.shape[-1]
    attn_dim = N_GROUPS * GROUP_DIM
    assert conv_dim == d and conv_w.shape[0] == CONV_WIDTH and d % LANES == 0 and d % STAGE_ROWS == 0
    assert w_in.shape[2] == 4 * conv_dim + 3 * attn_dim + GROUP_DIM + 2 * d
    assert s % (max(DILATIONS) * Q_BLOCK) == 0 and s % ROW_TILE == 0 and s % MIXER_ROW_TILE == 0
    assert all(w // dil == Q_BLOCK for w, dil in zip(WINDOWS, DILATIONS))

    mod3 = _modulation(c, w_ada, b_ada)
    nw = norm_w.reshape(1, d)

    y_a, h, w_attn, w_gate, p_a, p_b, w_o = _mixer_a(x, mod3, nw, w_in, layer, conv_w,
                                                     w_br_conv, w_br_attn, w_out)
    qkv, zs = _attn_proj(h, w_attn, q_norm_w, k_norm_w)
    outs, maxes, sums = zip(*[_dilated_attention(t, dil) for t, dil in zip(qkv, DILATIONS)])
    return _merge(x, mod3, h, y_a, zs, outs, maxes, sums, w_gate, p_a, p_b, w_o)


@jax.jit
def kernel(x, c, w_ada, b_ada, norm_w, w_in, conv_w, q_norm_w, k_norm_w, w_br_conv, w_br_attn, w_out):
    depth = w_ada.shape[0]
    for l in range(depth):
        x = _layer(x, c, l, w_ada[l], b_ada[l], norm_w[l], w_in, conv_w[l], q_norm_w[l],
                   k_norm_w[l], w_br_conv, w_br_attn, w_out)
    return x
```

```python
import functools

import jax
import jax.numpy as jnp
from jax import lax
from jax.experimental import pallas as pl
from jax.experimental.pallas import tpu as pltpu

F32 = jnp.float32
BF16 = jnp.bfloat16

HEAD_DIM = 64
ATTN_SLOTS = 8
WINDOWS = (128, 512, 2048)
DILATIONS = (1, 4, 16)
N_GROUPS = len(WINDOWS)
GROUP_DIM = ATTN_SLOTS * HEAD_DIM
Q_BLOCK = 128
CONV_WIDTH = 3
EPS = 1e-6
NEG_INF = -1e30
LOG2_E = 1.4426950408889634
LANES = 128
SUBLANES = 8
LSE_LANES = LANES // ATTN_SLOTS
VMEM_LIMIT = 56 * 1024 * 1024

MIXER_ROW_TILE = 1024
STAGE_ROWS = 128
STAGE_BYTES_IN_FLIGHT = 6 * 1024 * 1024
ROW_TILE = 512
OUT_ROW_TILE = 512
ATTN_STEP_BLOCKS = 16


def _params(n_axes):
    return pltpu.CompilerParams(dimension_semantics=("arbitrary",) * n_axes,
                                vmem_limit_bytes=VMEM_LIMIT)


def _silu(t):
    return t * jax.nn.sigmoid(t)


def _modulated_norm(x, norm_w, mod):
    ms = jnp.mean(x * x, axis=-1, keepdims=True)
    xn = x * lax.rsqrt(ms + EPS) * norm_w
    return xn * (1.0 + mod[1:2, :]) + mod[0:1, :]


def _to_dilated(slab_ref, n_rows, dil):
    per = n_rows // dil
    return jnp.concatenate(
        [jnp.concatenate([slab_ref[j, pl.ds(r, per, stride=dil), :] for r in range(dil)], axis=0)
         for j in range(slab_ref.shape[0])], axis=-1)


def _fetch_bf16(pairs, stage_ref, sem):
    slots, rows, _ = stage_ref.shape
    chunks = [(src, dst, r0) for src, dst in pairs for r0 in range(0, dst.shape[0], rows)]

    def copy(i):
        src, dst, r0 = chunks[i]
        return pltpu.make_async_copy(src.at[pl.ds(r0, rows), :],
                                     stage_ref.at[i % slots, :, pl.ds(0, dst.shape[1])],
                                     sem.at[i % slots])

    for i in range(min(slots - 1, len(chunks))):
        copy(i).start()
    for i, (_, dst, r0) in enumerate(chunks):
        if i + slots - 1 < len(chunks):
            copy(i + slots - 1).start()
        copy(i).wait()
        dst[r0:r0 + rows, :] = stage_ref[i % slots, :, 0:dst.shape[1]].astype(dst.dtype)


def _first_step():
    return (pl.program_id(0) == 0) & (pl.program_id(1) == 0)


_HBM = pl.BlockSpec(memory_space=pl.ANY)


def _resident(a):
    return pl.BlockSpec(a.shape, lambda *_: (0,) * a.ndim, pipeline_mode=pl.Buffered(1))


def _stage_scratch(width):
    chunk_bytes = STAGE_ROWS * width * 4
    slots = -(-STAGE_BYTES_IN_FLIGHT // chunk_bytes) + 1
    return [pltpu.VMEM((slots, STAGE_ROWS, width), F32), pltpu.SemaphoreType.DMA((slots,))]


def _mod_kernel(c_ref, w_ref, b_ref, o_ref):
    d = c_ref.shape[-1]
    sc = _silu(c_ref[...])
    for t in range(o_ref.shape[1]):
        cols = slice(t * d, (t + 1) * d)
        o_ref[:, t, :] = jnp.dot(sc, w_ref[:, cols], preferred_element_type=F32) + b_ref[:, cols]


def _modulation(c, w_ada, b_ada):
    bsz, d = c.shape
    n = w_ada.shape[1]
    whole = lambda shape: pl.BlockSpec(shape, lambda j: (0,) * len(shape), pipeline_mode=pl.Buffered(1))
    return pl.pallas_call(
        _mod_kernel,
        grid=(1,),
        in_specs=[whole((bsz, d)), whole((d, n)), whole((1, n))],
        out_specs=pl.BlockSpec((bsz, n // d, d), lambda j: (0, 0, 0)),
        out_shape=jax.ShapeDtypeStruct((bsz, n // d, d), F32),
        compiler_params=_params(1),
        name="adaln_mod",
    )(c, w_ada, b_ada.reshape(1, n))


def _mixer_a_kernel(x_ref, mod_ref, nw_ref, w_hbm, cw_ref, win_rows_ref, pa_rows_ref, pb_rows_ref,
                    wo_rows_ref, y_ref, h_ref, wattn_ref, wgate_ref, pa_ref, pb_ref, wo_ref,
                    tail_ref, w_ref, stage_ref, sem, *, layer):
    tm, d = x_ref.shape
    mod = mod_ref[...]
    nw = nw_ref[...]
    cw = cw_ref[...]

    @pl.when(_first_step())
    def _():
        _fetch_bf16([(w_hbm.at[layer, :, pl.ds(0, w_ref.shape[1])], w_ref)], stage_ref, sem)

    @pl.when(pl.program_id(1) == 0)
    def _():
        tail_ref[...] = jnp.zeros_like(tail_ref)

    n_attn = wattn_ref.shape[1]
    wattn_ref[...] = win_rows_ref[:, 4 * d:4 * d + n_attn].astype(wattn_ref.dtype)
    wgate_ref[...] = win_rows_ref[:, 4 * d + n_attn:].astype(wgate_ref.dtype)
    pa_ref[...] = pa_rows_ref[...].astype(pa_ref.dtype)
    pb_ref[...] = pb_rows_ref[...].astype(pb_ref.dtype)
    wo_ref[...] = wo_rows_ref[...].astype(wo_ref.dtype)

    prev1 = tail_ref[SUBLANES - 1:SUBLANES, :]
    prev2 = tail_ref[SUBLANES - 2:SUBLANES - 1, :]

    h = _modulated_norm(x_ref[...], nw, mod).astype(BF16)
    h_ref[...] = h
    proj = lambda j: jnp.dot(h, w_ref[:, j * d:(j + 1) * d], preferred_element_type=F32)
    z_a, c_a, x_a = proj(3), proj(1), proj(2)
    u = c_a * x_a
    row = lax.broadcasted_iota(jnp.int32, (tm, d), 0)
    u_m1 = jnp.where(row == 0, prev1, pltpu.roll(u, 1, axis=0))
    u_m2 = jnp.where(row == 0, prev2, jnp.where(row == 1, prev1, pltpu.roll(u, 2, axis=0)))
    conv = cw[0:1, :] * u_m2 + cw[1:2, :] * u_m1 + cw[2:3, :] * u
    gated = conv * _silu(z_a)
    y_ref[...] = (proj(0) * gated).astype(y_ref.dtype)
    tail_ref[...] = u[tm - SUBLANES:, :]


def _mixer_a(x, mod3, norm_w, w_in, layer, conv_w, p_a, p_b, w_o):
    bsz, s, d = x.shape
    tm = MIXER_ROW_TILE
    tiles = s // tm
    n_steps = bsz * tiles
    n_attn = (3 * N_GROUPS + 1) * GROUP_DIM
    rows = pl.BlockSpec((None, tm, d), lambda b, i: (b, i, 0))

    def slab(a, layered):
        k, n = a.shape[-2:]
        assert k % (n_steps * 2 * SUBLANES) == 0
        step = lambda b, i: b * tiles + i
        if layered:
            return pl.BlockSpec((None, k // n_steps, n), lambda b, i: (layer, step(b, i), 0))
        return pl.BlockSpec((k // n_steps, n), lambda b, i: (step(b, i), 0))

    bf16 = lambda k, n: jax.ShapeDtypeStruct((k, n), BF16)
    out_shapes = [jax.ShapeDtypeStruct((bsz, s, d), BF16)] * 2 + [
        bf16(d, n_attn), bf16(d, 2 * d), bf16(*p_a.shape[1:]), bf16(*p_b.shape[1:]), bf16(*w_o.shape[1:])]
    return pl.pallas_call(
        functools.partial(_mixer_a_kernel, layer=layer),
        grid=(bsz, tiles),
        in_specs=[rows,
                  pl.BlockSpec((None, 3, d), lambda b, i: (b, 0, 0)),
                  pl.BlockSpec((1, d), lambda b, i: (0, 0)),
                  _HBM,
                  pl.BlockSpec(conv_w.shape, lambda b, i: (0, 0)),
                  slab(w_in, True), slab(p_a, True), slab(p_b, True), slab(w_o, True)],
        out_specs=[rows, rows] + [slab(t, False) for t in out_shapes[2:]],
        out_shape=out_shapes,
        scratch_shapes=[pltpu.VMEM((SUBLANES, d), F32),
                        pltpu.VMEM((d, 4 * d), BF16),
                        *_stage_scratch(4 * d)],
        compiler_params=_params(2),
        name="mixer_a",
    )(x, mod3, norm_w, w_in, conv_w, w_in, p_a, p_b, w_o)


def _head_rms(t, w):
    sq = t * t
    lane_lo = lax.broadcasted_iota(jnp.int32, (t.shape[0], LANES), 1) < HEAD_DIM
    parts = []
    for j in range(0, t.shape[-1], LANES):
        tile = sq[:, j:j + LANES]
        even = jnp.sum(jnp.where(lane_lo, tile, 0.0), axis=-1, keepdims=True)
        odd = jnp.sum(jnp.where(lane_lo, 0.0, tile), axis=-1, keepdims=True)
        parts.append(jnp.where(lane_lo, even, odd))
    ssq = jnp.concatenate(parts, axis=-1)
    return t * lax.rsqrt(ssq * (1.0 / HEAD_DIM) + EPS) * w


def _attn_proj_kernel(h_ref, w_ref, qkw_ref, qkv0_ref, qkv1_ref, qkv2_ref, zs_ref, slab_ref):
    tm, d = h_ref.shape
    h = h_ref[...]
    for j in range(d // LANES):
        slab_ref[j] = h[:, j * LANES:(j + 1) * LANES].astype(F32)
    ad = N_GROUPS * GROUP_DIM
    for g, out_ref in enumerate((qkv0_ref, qkv1_ref, qkv2_ref)):
        hg = h if DILATIONS[g] == 1 else _to_dilated(slab_ref, tm, DILATIONS[g]).astype(BF16)
        q, k, v = [jnp.dot(hg, w_ref[:, part * ad + g * GROUP_DIM:part * ad + (g + 1) * GROUP_DIM],
                           preferred_element_type=F32) for part in range(3)]
        qkv = jnp.concatenate([_head_rms(q, qkw_ref[0:1, :]), _head_rms(k, qkw_ref[1:2, :]), v], axis=-1)
        out_ref[...] = qkv.astype(out_ref.dtype).reshape(out_ref.shape)
        if g == 0:
            z = jnp.dot(hg, w_ref[:, 3 * ad:], preferred_element_type=F32)
            zs_ref[...] = _silu(z)


def _attn_proj(h, w_attn, q_norm_w, k_norm_w):
    bsz, s, d = h.shape
    tm = ROW_TILE
    qkw = jnp.tile(jnp.stack([q_norm_w * (HEAD_DIM ** -0.5 * LOG2_E), k_norm_w]), (1, ATTN_SLOTS))
    gw = 3 * GROUP_DIM
    out_shapes, out_specs = [], []
    for dil in DILATIONS:
        tile = Q_BLOCK * dil
        assert dil == 1 or tile % tm == 0
        if tile <= tm:
            nblk = tm // Q_BLOCK
            out_shapes.append(jax.ShapeDtypeStruct((bsz, s // Q_BLOCK, Q_BLOCK, gw), BF16))
            out_specs.append(pl.BlockSpec((None, nblk, Q_BLOCK, gw), lambda b, i: (b, i, 0, 0)))
        else:
            per = tm // dil
            steps = tile // tm
            out_shapes.append(jax.ShapeDtypeStruct((bsz, s // tile, dil, Q_BLOCK, gw), BF16))
            out_specs.append(pl.BlockSpec((None, None, dil, per, gw),
                                          lambda b, i, steps=steps: (b, i // steps, 0, i % steps, 0)))
    out_shapes.append(jax.ShapeDtypeStruct((bsz, s, GROUP_DIM), F32))
    out_specs.append(pl.BlockSpec((None, tm, GROUP_DIM), lambda b, i: (b, i, 0)))
    outs = pl.pallas_call(
        _attn_proj_kernel,
        grid=(bsz, s // tm),
        in_specs=[pl.BlockSpec((None, tm, d), lambda b, i: (b, i, 0)),
                  _resident(w_attn),
                  pl.BlockSpec(qkw.shape, lambda b, i: (0, 0))],
        out_specs=out_specs,
        out_shape=out_shapes,
        scratch_shapes=[pltpu.VMEM((d // LANES, tm, LANES), F32)],
        compiler_params=_params(2),
        name="attn_proj",
    )(h, w_attn, qkw)
    qkv = [t.reshape(bsz, s // Q_BLOCK, Q_BLOCK, gw) for t in outs[:N_GROUPS]]
    return qkv, outs[N_GROUPS]


def _stat_lane(head):
    pair, odd = divmod(head, 2)
    return pair * LSE_LANES + (0 if odd else HEAD_DIM)


def _attn_block(q, kp, kc, vp, vc, bias, o_ref, stat_ref, j):
    blk = q.shape[0]
    lane = lax.broadcasted_iota(jnp.int32, (blk, LANES), 1)
    lane_lo = lane < HEAD_DIM
    lane2_lo = lax.broadcasted_iota(jnp.int32, (2 * blk, LANES), 1) < HEAD_DIM
    zero = jnp.zeros((blk, LANES), q.dtype)
    one = jnp.ones((2 * blk, LANES), q.dtype)
    m_tile = jnp.zeros((blk, LANES), F32)
    l_tile = jnp.ones((blk, LANES), F32)
    for p in range(GROUP_DIM // LANES):
        sl = slice(p * LANES, (p + 1) * LANES)
        q2 = q[:, sl]
        qs = jnp.concatenate([jnp.where(lane_lo, q2, zero), jnp.where(lane_lo, zero, q2)], axis=0)
        k2 = jnp.concatenate([kp[:, sl], kc[:, sl]], axis=0)
        v2 = jnp.concatenate([vp[:, sl], vc[:, sl]], axis=0)
        s = lax.dot_general(qs, k2, (((1,), (1,)), ((), ())), preferred_element_type=F32) + bias
        m = jnp.max(s, axis=-1, keepdims=True)
        e = jnp.exp2(s - m).astype(BF16)
        o_even = jnp.dot(e[:blk], jnp.where(lane2_lo, v2, one), preferred_element_type=F32)
        o_odd = jnp.dot(e[blk:], jnp.where(lane2_lo, one, v2), preferred_element_type=F32)
        o_ref[j + (slice(None), sl)] = jnp.where(lane_lo, o_even, o_odd)
        for head, m_h, o_h in ((2 * p, m[:blk], o_even), (2 * p + 1, m[blk:], o_odd)):
            slot = (lane >= _stat_lane(head)) & (lane < _stat_lane(head) + LSE_LANES)
            m_tile = jnp.where(slot, m_h, m_tile)
            l_tile = jnp.where(slot, o_h, l_tile)
    stat_ref[j] = jnp.concatenate([m_tile, l_tile], axis=-1)


def _attn_kernel(q_ref, kprev_ref, k_ref, vprev_ref, v_ref, o_ref, stat_ref):
    n_blocks, n_res, blk, _ = q_ref.shape
    span = 2 * blk
    row = lax.broadcasted_iota(jnp.int32, (span, span), 0) % blk
    col = lax.broadcasted_iota(jnp.int32, (span, span), 1)
    in_window = (col >= row) & (col <= row + blk)
    bias = jnp.where(in_window, 0.0, NEG_INF)
    run_has_prev = pl.program_id(2) > 0
    bias_first = jnp.where(in_window & ((col >= blk) | run_has_prev), 0.0, NEG_INF)

    for r in range(n_res):
        _attn_block(q_ref[0, r], kprev_ref[r], k_ref[0, r], vprev_ref[r], v_ref[0, r], bias_first,
                    o_ref, stat_ref, (0, r))
        for j in range(1, n_blocks):
            _attn_block(q_ref[j, r], k_ref[j - 1, r], k_ref[j, r], v_ref[j - 1, r], v_ref[j, r], bias,
                        o_ref, stat_ref, (j, r))


def _dilated_attention(qkv, dil):
    bsz, nb, blk, width = qkv.shape
    gd = GROUP_DIM
    tiles = nb // dil
    run = min(ATTN_STEP_BLOCKS, tiles)
    res = min(ATTN_STEP_BLOCKS // run, dil)
    assert tiles % run == 0 and dil % res == 0
    view = qkv.reshape(bsz, tiles, dil, blk, width)
    cur = lambda part: pl.BlockSpec((None, run, res, blk, gd), lambda b, r, c: (b, c, r, 0, part))
    prev = lambda part: pl.BlockSpec((None, None, res, blk, gd),
                                     lambda b, r, c: (b, jnp.maximum(c * run - 1, 0), r, 0, part))
    stat_spec = pl.BlockSpec((None, run, res, blk, 2 * LANES), lambda b, r, c: (b, c, r, 0, 0))
    stat_shape = jax.ShapeDtypeStruct((bsz, tiles, dil, blk, 2 * LANES), F32)
    o, stats = pl.pallas_call(
        _attn_kernel,
        grid=(bsz, dil // res, tiles // run),
        in_specs=[cur(0), prev(1), cur(1), prev(2), cur(2)],
        out_specs=[pl.BlockSpec((None, run, res, blk, gd), lambda b, r, c: (b, c, r, 0, 0)),
                   stat_spec],
        out_shape=[jax.ShapeDtypeStruct((bsz, tiles, dil, blk, gd), F32), stat_shape],
        compiler_params=_params(3),
        name=f"dilated_attn_d{dil}",
    )(view, view, view, view, view)
    return o.reshape(bsz, nb, blk, gd), stats.reshape(bsz, nb, blk, 2 * LANES)


def _to_sequence(src_ref, slab_ref, dil):
    per = src_ref.shape[1]
    n_slabs = src_ref.shape[2] // LANES
    for j in range(n_slabs):
        for r in range(dil):
            slab_ref[j, pl.ds(r, per, stride=dil), :] = src_ref[r, :, j * LANES:(j + 1) * LANES]
    return jnp.concatenate([slab_ref[j] for j in range(n_slabs)], axis=-1)


def _merge_kernel(x_ref, mod_ref, h_ref, ya_ref, zs_ref,
                  o0_ref, o1_ref, o2_ref, st0_ref, st1_ref, st2_ref,
                  wg_ref, pa_ref, pb_ref, wo_ref, out_ref, slab_ref):
    d = x_ref.shape[-1]
    h = h_ref[...]
    g_a = jax.nn.sigmoid(jnp.dot(h, wg_ref[:, :d], preferred_element_type=F32))
    g_b = jax.nn.sigmoid(jnp.dot(h, wg_ref[:, d:], preferred_element_type=F32))

    def in_sequence(refs):
        return [refs[0][...]] + [_to_sequence(r, slab_ref, dil) for r, dil in zip(refs[1:], DILATIONS[1:])]

    outs = in_sequence((o0_ref, o1_ref, o2_ref))
    stats = in_sequence((st0_ref, st1_ref, st2_ref))
    maxes = [t[:, :LANES] for t in stats]
    sums = [t[:, LANES:] for t in stats]
    top = jnp.maximum(jnp.maximum(maxes[0], maxes[1]), maxes[2])
    wts = [jnp.exp2(t - top) for t in maxes]
    den = wts[0] * sums[0] + wts[1] * sums[1] + wts[2] * sums[2]
    attn = jnp.zeros_like(outs[0])
    lane_lo = lax.broadcasted_iota(jnp.int32, (h.shape[0], LANES), 1) < HEAD_DIM
    for w, o in zip(wts, outs):
        wn = w / den
        wide = jnp.concatenate(
            [jnp.where(lane_lo, wn[:, _stat_lane(2 * p):_stat_lane(2 * p) + 1],
                       wn[:, _stat_lane(2 * p + 1):_stat_lane(2 * p + 1) + 1])
             for p in range(GROUP_DIM // LANES)], axis=-1)
        attn = attn + wide * o
    y_b = (attn * zs_ref[...]).astype(BF16)

    merged = (g_a * jnp.dot(ya_ref[...], pa_ref[...], preferred_element_type=F32)
              + g_b * jnp.dot(y_b, pb_ref[...], preferred_element_type=F32))
    upd = jnp.dot(merged.astype(BF16), wo_ref[...], preferred_element_type=F32)
    out_ref[...] = x_ref[...] + mod_ref[2:3, :] * upd


def _merge(x, mod3, h, y_a, zs, outs, stats, w_gate, p_a, p_b, w_o):
    bsz, s, d = x.shape
    tm = OUT_ROW_TILE
    row_d = pl.BlockSpec((None, tm, d), lambda b, i: (b, i, 0))
    row_g = pl.BlockSpec((None, tm, GROUP_DIM), lambda b, i: (b, i, 0))

    def dilated(t, dil):
        width = t.shape[-1]
        tile = Q_BLOCK * dil
        per = tm // dil
        if tile % tm:
            raise NotImplementedError("a dilated-order tile must be a whole number of merge tiles")
        steps = tile // tm
        view = t.reshape(bsz, s // tile, dil, Q_BLOCK, width)
        spec = pl.BlockSpec((None, None, dil, per, width),
                            lambda b, i: (b, i // steps, 0, i % steps, 0))
        return view, spec

    views, specs = [], []
    for group in (outs, stats):
        for t, dil in zip(group, DILATIONS):
            width = t.shape[-1]
            if dil == 1:
                views.append(t.reshape(bsz, s, width))
                specs.append(pl.BlockSpec((None, tm, width), lambda b, i: (b, i, 0)))
            else:
                v, sp = dilated(t, dil)
                views.append(v)
                specs.append(sp)

    return pl.pallas_call(
        _merge_kernel,
        grid=(bsz, s // tm),
        in_specs=[row_d, pl.BlockSpec((None, 3, d), lambda b, i: (b, 0, 0)), row_d, row_d, row_g,
                  *specs,
                  _resident(w_gate), _resident(p_a), _resident(p_b), _resident(w_o)],
        out_specs=row_d,
        out_shape=jax.ShapeDtypeStruct((bsz, s, d), F32),
        scratch_shapes=[pltpu.VMEM((GROUP_DIM // LANES, tm, LANES), F32)],
        compiler_params=_params(2),
        name="merge_out",
    )(x, mod3, h, y_a, zs, *views, w_gate, p_a, p_b, w_o)


def _layer(x, c, layer, w_ada, b_ada, norm_w, w_in, conv_w, q_norm_w, k_norm_w, w_br_conv, w_br_attn, w_out):
    bsz, s, d = x.shape
    conv_dim = conv_w.shape[-1]
    attn_dim = N_GROUPS * GROUP_DIM
    assert conv_dim == d and conv_w.shape[0] == CONV_WIDTH and d % LANES == 0 and d % STAGE_ROWS == 0
    assert w_in.shape[2] == 4 * conv_dim + 3 * attn_dim + GROUP_DIM + 2 * d
    assert s % (max(DILATIONS) * Q_BLOCK) == 0 and s % ROW_TILE == 0 and s % MIXER_ROW_TILE == 0
    assert all(w // dil == Q_BLOCK for w, dil in zip(WINDOWS, DILATIONS))

    mod3 = _modulation(c, w_ada, b_ada)
    nw = norm_w.reshape(1, d)

    y_a, h, w_attn, w_gate, p_a, p_b, w_o = _mixer_a(x, mod3, nw, w_in, layer, conv_w,
                                                     w_br_conv, w_br_attn, w_out)
    qkv, zs = _attn_proj(h, w_attn, q_norm_w, k_norm_w)
    outs, stats = zip(*[_dilated_attention(t, dil) for t, dil in zip(qkv, DILATIONS)])
    return _merge(x, mod3, h, y_a, zs, outs, stats, w_gate, p_a, p_b, w_o)


@jax.jit
def kernel(x, c, w_ada, b_ada, norm_w, w_in, conv_w, q_norm_w, k_norm_w, w_br_conv, w_br_attn, w_out):
    depth = w_ada.shape[0]
    for l in range(depth):
        x = _layer(x, c, l, w_ada[l], b_ada[l], norm_w[l], w_in, conv_w[l], q_norm_w[l],
                   k_norm_w[l], w_br_conv, w_br_attn, w_out)
    return x
```

```python
import functools

import jax
import jax.numpy as jnp
from jax import lax
from jax.experimental import pallas as pl
from jax.experimental.pallas import tpu as pltpu

F32 = jnp.float32
BF16 = jnp.bfloat16

HEAD_DIM = 64
ATTN_SLOTS = 8
WINDOWS = (128, 512, 2048)
DILATIONS = (1, 4, 16)
N_GROUPS = len(WINDOWS)
GROUP_DIM = ATTN_SLOTS * HEAD_DIM
Q_BLOCK = 128
CONV_WIDTH = 3
EPS = 1e-6
NEG_INF = -1e30
LOG2_E = 1.4426950408889634
LANES = 128
SUBLANES = 8
LSE_LANES = LANES // ATTN_SLOTS
PASS_STRIDE = 4
VMEM_LIMIT = 56 * 1024 * 1024

MIXER_ROW_TILE = 1024
STAGE_ROWS = 128
STAGE_BYTES_IN_FLIGHT = 6 * 1024 * 1024
ROW_TILE = 512
OUT_ROW_TILE = 512
ATTN_STEP_BLOCKS = 16


def _params(n_axes):
    return pltpu.CompilerParams(dimension_semantics=("arbitrary",) * n_axes,
                                vmem_limit_bytes=VMEM_LIMIT)


def _silu(t):
    return t * jax.nn.sigmoid(t)


def _modulated_norm(x, norm_w, mod):
    ms = jnp.mean(x * x, axis=-1, keepdims=True)
    xn = x * lax.rsqrt(ms + EPS) * norm_w
    return xn * (1.0 + mod[1:2, :]) + mod[0:1, :]


def _to_dilated(slab_ref, tmp_ref, n_rows, dil):
    per = n_rows // dil
    n_slabs = slab_ref.shape[0]
    if dil <= PASS_STRIDE:
        piece = lambda j, r: slab_ref[j, pl.ds(r, per, stride=dil), :]
    else:
        n_hi = dil // PASS_STRIDE
        assert n_hi <= PASS_STRIDE and dil % PASS_STRIDE == 0
        part = n_hi * per
        for j in range(n_slabs):
            for r_lo in range(PASS_STRIDE):
                tmp_ref[j, r_lo * part:(r_lo + 1) * part, :] = slab_ref[j, pl.ds(r_lo, part, stride=PASS_STRIDE), :]
        piece = lambda j, r: tmp_ref[j, pl.ds((r % PASS_STRIDE) * part + r // PASS_STRIDE, per, stride=n_hi), :]
    return jnp.concatenate(
        [jnp.concatenate([piece(j, r) for r in range(dil)], axis=0) for j in range(n_slabs)], axis=-1)


def _fetch_bf16(pairs, stage_ref, sem):
    slots, rows, _ = stage_ref.shape
    chunks = [(src, dst, r0) for src, dst in pairs for r0 in range(0, dst.shape[0], rows)]

    def copy(i):
        src, dst, r0 = chunks[i]
        return pltpu.make_async_copy(src.at[pl.ds(r0, rows), :],
                                     stage_ref.at[i % slots, :, pl.ds(0, dst.shape[1])],
                                     sem.at[i % slots])

    for i in range(min(slots - 1, len(chunks))):
        copy(i).start()
    for i, (_, dst, r0) in enumerate(chunks):
        if i + slots - 1 < len(chunks):
            copy(i + slots - 1).start()
        copy(i).wait()
        dst[r0:r0 + rows, :] = stage_ref[i % slots, :, 0:dst.shape[1]].astype(dst.dtype)


def _first_step():
    return (pl.program_id(0) == 0) & (pl.program_id(1) == 0)


_HBM = pl.BlockSpec(memory_space=pl.ANY)


def _resident(a):
    return pl.BlockSpec(a.shape, lambda *_: (0,) * a.ndim, pipeline_mode=pl.Buffered(1))


def _stage_scratch(width):
    chunk_bytes = STAGE_ROWS * width * 4
    slots = -(-STAGE_BYTES_IN_FLIGHT // chunk_bytes) + 1
    return [pltpu.VMEM((slots, STAGE_ROWS, width), F32), pltpu.SemaphoreType.DMA((slots,))]


def _mod_kernel(c_ref, w_ref, b_ref, o_ref):
    d = c_ref.shape[-1]
    sc = _silu(c_ref[...])
    for t in range(o_ref.shape[1]):
        cols = slice(t * d, (t + 1) * d)
        o_ref[:, t, :] = jnp.dot(sc, w_ref[:, cols], preferred_element_type=F32) + b_ref[:, cols]


def _modulation(c, w_ada, b_ada):
    bsz, d = c.shape
    n = w_ada.shape[1]
    whole = lambda shape: pl.BlockSpec(shape, lambda j: (0,) * len(shape), pipeline_mode=pl.Buffered(1))
    return pl.pallas_call(
        _mod_kernel,
        grid=(1,),
        in_specs=[whole((bsz, d)), whole((d, n)), whole((1, n))],
        out_specs=pl.BlockSpec((bsz, n // d, d), lambda j: (0, 0, 0)),
        out_shape=jax.ShapeDtypeStruct((bsz, n // d, d), F32),
        compiler_params=_params(1),
        name="adaln_mod",
    )(c, w_ada, b_ada.reshape(1, n))


def _mixer_a_kernel(x_ref, mod_ref, nw_ref, w_hbm, cw_ref, win_rows_ref, pa_rows_ref, pb_rows_ref,
                    wo_rows_ref, y_ref, h_ref, wattn_ref, wgate_ref, pa_ref, pb_ref, wo_ref,
                    tail_ref, w_ref, stage_ref, sem, *, layer):
    tm, d = x_ref.shape
    mod = mod_ref[...]
    nw = nw_ref[...]
    cw = cw_ref[...]

    @pl.when(_first_step())
    def _():
        _fetch_bf16([(w_hbm.at[layer, :, pl.ds(0, w_ref.shape[1])], w_ref)], stage_ref, sem)

    @pl.when(pl.program_id(1) == 0)
    def _():
        tail_ref[...] = jnp.zeros_like(tail_ref)

    n_attn = wattn_ref.shape[1]
    wattn_ref[...] = win_rows_ref[:, 4 * d:4 * d + n_attn].astype(wattn_ref.dtype)
    wgate_ref[...] = win_rows_ref[:, 4 * d + n_attn:].astype(wgate_ref.dtype)
    pa_ref[...] = pa_rows_ref[...].astype(pa_ref.dtype)
    pb_ref[...] = pb_rows_ref[...].astype(pb_ref.dtype)
    wo_ref[...] = wo_rows_ref[...].astype(wo_ref.dtype)

    prev1 = tail_ref[SUBLANES - 1:SUBLANES, :]
    prev2 = tail_ref[SUBLANES - 2:SUBLANES - 1, :]

    h = _modulated_norm(x_ref[...], nw, mod).astype(BF16)
    h_ref[...] = h
    proj = lambda j: jnp.dot(h, w_ref[:, j * d:(j + 1) * d], preferred_element_type=F32)
    z_a, c_a, x_a = proj(3), proj(1), proj(2)
    u = c_a * x_a
    row = lax.broadcasted_iota(jnp.int32, (tm, d), 0)
    u_m1 = jnp.where(row == 0, prev1, pltpu.roll(u, 1, axis=0))
    u_m2 = jnp.where(row == 0, prev2, jnp.where(row == 1, prev1, pltpu.roll(u, 2, axis=0)))
    conv = cw[0:1, :] * u_m2 + cw[1:2, :] * u_m1 + cw[2:3, :] * u
    gated = conv * _silu(z_a)
    y_ref[...] = (proj(0) * gated).astype(y_ref.dtype)
    tail_ref[...] = u[tm - SUBLANES:, :]


def _mixer_a(x, mod3, norm_w, w_in, layer, conv_w, p_a, p_b, w_o):
    bsz, s, d = x.shape
    tm = MIXER_ROW_TILE
    tiles = s // tm
    n_steps = bsz * tiles
    n_attn = (3 * N_GROUPS + 1) * GROUP_DIM
    rows = pl.BlockSpec((None, tm, d), lambda b, i: (b, i, 0))

    def slab(a, layered):
        k, n = a.shape[-2:]
        assert k % (n_steps * 2 * SUBLANES) == 0
        step = lambda b, i: b * tiles + i
        if layered:
            return pl.BlockSpec((None, k // n_steps, n), lambda b, i: (layer, step(b, i), 0))
        return pl.BlockSpec((k // n_steps, n), lambda b, i: (step(b, i), 0))

    bf16 = lambda k, n: jax.ShapeDtypeStruct((k, n), BF16)
    out_shapes = [jax.ShapeDtypeStruct((bsz, s, d), BF16)] * 2 + [
        bf16(d, n_attn), bf16(d, 2 * d), bf16(*p_a.shape[1:]), bf16(*p_b.shape[1:]), bf16(*w_o.shape[1:])]
    return pl.pallas_call(
        functools.partial(_mixer_a_kernel, layer=layer),
        grid=(bsz, tiles),
        in_specs=[rows,
                  pl.BlockSpec((None, 3, d), lambda b, i: (b, 0, 0)),
                  pl.BlockSpec((1, d), lambda b, i: (0, 0)),
                  _HBM,
                  pl.BlockSpec(conv_w.shape, lambda b, i: (0, 0)),
                  slab(w_in, True), slab(p_a, True), slab(p_b, True), slab(w_o, True)],
        out_specs=[rows, rows] + [slab(t, False) for t in out_shapes[2:]],
        out_shape=out_shapes,
        scratch_shapes=[pltpu.VMEM((SUBLANES, d), F32),
                        pltpu.VMEM((d, 4 * d), BF16),
                        *_stage_scratch(4 * d)],
        compiler_params=_params(2),
        name="mixer_a",
    )(x, mod3, norm_w, w_in, conv_w, w_in, p_a, p_b, w_o)


def _head_rms(t, w):
    sq = t * t
    lane_lo = lax.broadcasted_iota(jnp.int32, (t.shape[0], LANES), 1) < HEAD_DIM
    parts = []
    for j in range(0, t.shape[-1], LANES):
        tile = sq[:, j:j + LANES]
        even = jnp.sum(jnp.where(lane_lo, tile, 0.0), axis=-1, keepdims=True)
        odd = jnp.sum(jnp.where(lane_lo, 0.0, tile), axis=-1, keepdims=True)
        parts.append(jnp.where(lane_lo, even, odd))
    ssq = jnp.concatenate(parts, axis=-1)
    return t * lax.rsqrt(ssq * (1.0 / HEAD_DIM) + EPS) * w


def _attn_proj_kernel(h_ref, w_ref, qw_ref, kw_ref, qkv0_ref, qkv1_ref, qkv2_ref, zs_ref, slab_ref,
                      tmp_ref):
    tm, d = h_ref.shape
    h = h_ref[...]
    for j in range(d // LANES):
        slab_ref[j] = h[:, j * LANES:(j + 1) * LANES].astype(F32)
    ad = N_GROUPS * GROUP_DIM
    for g, out_ref in enumerate((qkv0_ref, qkv1_ref, qkv2_ref)):
        hg = h if DILATIONS[g] == 1 else _to_dilated(slab_ref, tmp_ref, tm, DILATIONS[g]).astype(BF16)
        q, k, v = [jnp.dot(hg, w_ref[:, part * ad + g * GROUP_DIM:part * ad + (g + 1) * GROUP_DIM],
                           preferred_element_type=F32) for part in range(3)]
        qkv = jnp.concatenate([_head_rms(q, qw_ref[...]), _head_rms(k, kw_ref[...]), v], axis=-1)
        out_ref[...] = qkv.astype(out_ref.dtype).reshape(out_ref.shape)
        if g == 0:
            z = jnp.dot(hg, w_ref[:, 3 * ad:], preferred_element_type=F32)
            zs_ref[...] = _silu(z)


def _attn_proj(h, w_attn, q_norm_w, k_norm_w):
    bsz, s, d = h.shape
    tm = ROW_TILE
    qw = jnp.tile(q_norm_w * (HEAD_DIM ** -0.5 * LOG2_E), ATTN_SLOTS).reshape(1, GROUP_DIM)
    kw = jnp.tile(k_norm_w, ATTN_SLOTS).reshape(1, GROUP_DIM)
    gw = 3 * GROUP_DIM
    out_shapes, out_specs = [], []
    for dil in DILATIONS:
        tile = Q_BLOCK * dil
        assert dil == 1 or tile % tm == 0
        if tile <= tm:
            nblk = tm // Q_BLOCK
            out_shapes.append(jax.ShapeDtypeStruct((bsz, s // Q_BLOCK, Q_BLOCK, gw), BF16))
            out_specs.append(pl.BlockSpec((None, nblk, Q_BLOCK, gw), lambda b, i: (b, i, 0, 0)))
        else:
            per = tm // dil
            steps = tile // tm
            out_shapes.append(jax.ShapeDtypeStruct((bsz, s // tile, dil, Q_BLOCK, gw), BF16))
            out_specs.append(pl.BlockSpec((None, None, dil, per, gw),
                                          lambda b, i, steps=steps: (b, i // steps, 0, i % steps, 0)))
    out_shapes.append(jax.ShapeDtypeStruct((bsz, s, GROUP_DIM), F32))
    out_specs.append(pl.BlockSpec((None, tm, GROUP_DIM), lambda b, i: (b, i, 0)))
    outs = pl.pallas_call(
        _attn_proj_kernel,
        grid=(bsz, s // tm),
        in_specs=[pl.BlockSpec((None, tm, d), lambda b, i: (b, i, 0)),
                  _resident(w_attn),
                  pl.BlockSpec((1, GROUP_DIM), lambda b, i: (0, 0)),
                  pl.BlockSpec((1, GROUP_DIM), lambda b, i: (0, 0))],
        out_specs=out_specs,
        out_shape=out_shapes,
        scratch_shapes=[pltpu.VMEM((d // LANES, tm, LANES), F32)] * 2,
        compiler_params=_params(2),
        name="attn_proj",
    )(h, w_attn, qw, kw)
    qkv = [t.reshape(bsz, s // Q_BLOCK, Q_BLOCK, gw) for t in outs[:N_GROUPS]]
    return qkv, outs[N_GROUPS]


def _stat_lane(head):
    pair, odd = divmod(head, 2)
    return pair * LSE_LANES + (0 if odd else HEAD_DIM)


def _attn_block(q, kp, kc, vp, vc, bias, o_ref, m_ref, l_ref, j):
    blk = q.shape[0]
    lane = lax.broadcasted_iota(jnp.int32, (blk, LANES), 1)
    lane_lo = lane < HEAD_DIM
    lane2_lo = lax.broadcasted_iota(jnp.int32, (2 * blk, LANES), 1) < HEAD_DIM
    zero = jnp.zeros((blk, LANES), q.dtype)
    one = jnp.ones((2 * blk, LANES), q.dtype)
    m_tile = jnp.zeros((blk, LANES), F32)
    l_tile = jnp.ones((blk, LANES), F32)
    for p in range(GROUP_DIM // LANES):
        sl = slice(p * LANES, (p + 1) * LANES)
        q2 = q[:, sl]
        qs = jnp.concatenate([jnp.where(lane_lo, q2, zero), jnp.where(lane_lo, zero, q2)], axis=0)
        k2 = jnp.concatenate([kp[:, sl], kc[:, sl]], axis=0)
        v2 = jnp.concatenate([vp[:, sl], vc[:, sl]], axis=0)
        s = lax.dot_general(qs, k2, (((1,), (1,)), ((), ())), preferred_element_type=F32) + bias
        m = jnp.max(s, axis=-1, keepdims=True)
        e = jnp.exp2(s - m).astype(BF16)
        o_even = jnp.dot(e[:blk], jnp.where(lane2_lo, v2, one), preferred_element_type=F32)
        o_odd = jnp.dot(e[blk:], jnp.where(lane2_lo, one, v2), preferred_element_type=F32)
        o_ref[j + (slice(None), sl)] = jnp.where(lane_lo, o_even, o_odd)
        for head, m_h, o_h in ((2 * p, m[:blk], o_even), (2 * p + 1, m[blk:], o_odd)):
            slot = (lane >= _stat_lane(head)) & (lane < _stat_lane(head) + LSE_LANES)
            m_tile = jnp.where(slot, m_h, m_tile)
            l_tile = jnp.where(slot, o_h, l_tile)
    m_ref[j] = m_tile
    l_ref[j] = l_tile


def _attn_kernel(q_ref, kprev_ref, k_ref, vprev_ref, v_ref, o_ref, m_ref, l_ref):
    n_blocks, n_res, blk, _ = q_ref.shape
    span = 2 * blk
    row = lax.broadcasted_iota(jnp.int32, (span, span), 0) % blk
    col = lax.broadcasted_iota(jnp.int32, (span, span), 1)
    in_window = (col >= row) & (col <= row + blk)
    bias = jnp.where(in_window, 0.0, NEG_INF)
    run_has_prev = pl.program_id(2) > 0
    bias_first = jnp.where(in_window & ((col >= blk) | run_has_prev), 0.0, NEG_INF)

    for r in range(n_res):
        _attn_block(q_ref[0, r], kprev_ref[r], k_ref[0, r], vprev_ref[r], v_ref[0, r], bias_first,
                    o_ref, m_ref, l_ref, (0, r))
        for j in range(1, n_blocks):
            _attn_block(q_ref[j, r], k_ref[j - 1, r], k_ref[j, r], v_ref[j - 1, r], v_ref[j, r], bias,
                        o_ref, m_ref, l_ref, (j, r))


def _dilated_attention(qkv, dil):
    bsz, nb, blk, width = qkv.shape
    gd = GROUP_DIM
    tiles = nb // dil
    run = min(ATTN_STEP_BLOCKS, tiles)
    res = min(ATTN_STEP_BLOCKS // run, dil)
    assert tiles % run == 0 and dil % res == 0
    view = qkv.reshape(bsz, tiles, dil, blk, width)
    cur = lambda part: pl.BlockSpec((None, run, res, blk, gd), lambda b, r, c: (b, c, r, 0, part))
    prev = lambda part: pl.BlockSpec((None, None, res, blk, gd),
                                     lambda b, r, c: (b, jnp.maximum(c * run - 1, 0), r, 0, part))
    stat_spec = pl.BlockSpec((None, run, res, blk, LANES), lambda b, r, c: (b, c, r, 0, 0))
    stat_shape = jax.ShapeDtypeStruct((bsz, tiles, dil, blk, LANES), F32)
    o, m, l = pl.pallas_call(
        _attn_kernel,
        grid=(bsz, dil // res, tiles // run),
        in_specs=[cur(0), prev(1), cur(1), prev(2), cur(2)],
        out_specs=[pl.BlockSpec((None, run, res, blk, gd), lambda b, r, c: (b, c, r, 0, 0)),
                   stat_spec, stat_spec],
        out_shape=[jax.ShapeDtypeStruct((bsz, tiles, dil, blk, gd), F32), stat_shape, stat_shape],
        compiler_params=_params(3),
        name=f"dilated_attn_d{dil}",
    )(view, view, view, view, view)
    return (o.reshape(bsz, nb, blk, gd), m.reshape(bsz, nb, blk, LANES),
            l.reshape(bsz, nb, blk, LANES))


def _to_sequence(src_ref, lane0, seq_ref, slab, tmp_ref, dil):
    per = src_ref.shape[1]
    src = lambda r: src_ref[r, :, lane0:lane0 + LANES]
    if dil <= PASS_STRIDE:
        for r in range(dil):
            seq_ref[slab, pl.ds(r, per, stride=dil), :] = src(r)
        return
    n_hi = dil // PASS_STRIDE
    assert n_hi <= PASS_STRIDE and dil % PASS_STRIDE == 0
    part = n_hi * per
    for r in range(dil):
        r_hi, r_lo = divmod(r, PASS_STRIDE)
        tmp_ref[slab, pl.ds(r_lo * part + r_hi, per, stride=n_hi), :] = src(r)
    for r_lo in range(PASS_STRIDE):
        seq_ref[slab, pl.ds(r_lo, part, stride=PASS_STRIDE), :] = tmp_ref[slab, r_lo * part:(r_lo + 1) * part, :]


def _merge_kernel(x_ref, mod_ref, h_ref, ya_ref, zs_ref,
                  o0_ref, o1_ref, o2_ref, m0_ref, m1_ref, m2_ref, l0_ref, l1_ref, l2_ref,
                  wg_ref, pa_ref, pb_ref, wo_ref, out_ref, seq_ref, tmp_ref):
    d = x_ref.shape[-1]
    n_tiles = GROUP_DIM // LANES
    h = h_ref[...]
    g_a = jax.nn.sigmoid(jnp.dot(h, wg_ref[:, :d], preferred_element_type=F32))
    g_b = jax.nn.sigmoid(jnp.dot(h, wg_ref[:, d:], preferred_element_type=F32))

    groups = ((o0_ref, m0_ref, l0_ref), (o1_ref, m1_ref, l1_ref), (o2_ref, m2_ref, l2_ref))
    for g in range(1, N_GROUPS):
        o_ref, m_ref, l_ref = groups[g]
        sources = [(o_ref, p * LANES) for p in range(n_tiles)] + [(m_ref, 0), (l_ref, 0)]
        for slab, (src_ref, lane0) in enumerate(sources):
            _to_sequence(src_ref, lane0, seq_ref.at[g - 1], slab, tmp_ref, DILATIONS[g])

    def out_tile(g, p):
        return groups[0][0][:, p * LANES:(p + 1) * LANES] if g == 0 else seq_ref[g - 1, p]

    def stat_tile(g, which):
        return groups[0][1 + which][...] if g == 0 else seq_ref[g - 1, n_tiles + which]

    maxes = [stat_tile(g, 0) for g in range(N_GROUPS)]
    sums = [stat_tile(g, 1) for g in range(N_GROUPS)]
    top = jnp.maximum(jnp.maximum(maxes[0], maxes[1]), maxes[2])
    wts = [jnp.exp2(t - top) for t in maxes]
    den = wts[0] * sums[0] + wts[1] * sums[1] + wts[2] * sums[2]
    wns = [w / den for w in wts]
    lane_lo = lax.broadcasted_iota(jnp.int32, (h.shape[0], LANES), 1) < HEAD_DIM
    tiles = []
    for p in range(n_tiles):
        attn = None
        for g, wn in enumerate(wns):
            wide = jnp.where(lane_lo, wn[:, _stat_lane(2 * p):_stat_lane(2 * p) + 1],
                             wn[:, _stat_lane(2 * p + 1):_stat_lane(2 * p + 1) + 1])
            term = wide * out_tile(g, p)
            attn = term if attn is None else attn + term
        tiles.append(attn * zs_ref[:, p * LANES:(p + 1) * LANES])
    y_b = jnp.concatenate(tiles, axis=-1).astype(BF16)

    merged = (g_a * jnp.dot(ya_ref[...], pa_ref[...], preferred_element_type=F32)
              + g_b * jnp.dot(y_b, pb_ref[...], preferred_element_type=F32))
    upd = jnp.dot(merged.astype(BF16), wo_ref[...], preferred_element_type=F32)
    out_ref[...] = x_ref[...] + mod_ref[2:3, :] * upd


def _merge(x, mod3, h, y_a, zs, outs, maxes, sums, w_gate, p_a, p_b, w_o):
    bsz, s, d = x.shape
    tm = OUT_ROW_TILE
    row_d = pl.BlockSpec((None, tm, d), lambda b, i: (b, i, 0))
    row_g = pl.BlockSpec((None, tm, GROUP_DIM), lambda b, i: (b, i, 0))

    def dilated(t, dil):
        width = t.shape[-1]
        tile = Q_BLOCK * dil
        per = tm // dil
        if tile % tm:
            raise NotImplementedError("a dilated-order tile must be a whole number of merge tiles")
        steps = tile // tm
        view = t.reshape(bsz, s // tile, dil, Q_BLOCK, width)
        spec = pl.BlockSpec((None, None, dil, per, width),
                            lambda b, i: (b, i // steps, 0, i % steps, 0))
        return view, spec

    views, specs = [], []
    for group in (outs, maxes, sums):
        for t, dil in zip(group, DILATIONS):
            width = t.shape[-1]
            if dil == 1:
                views.append(t.reshape(bsz, s, width))
                specs.append(pl.BlockSpec((None, tm, width), lambda b, i: (b, i, 0)))
            else:
                v, sp = dilated(t, dil)
                views.append(v)
                specs.append(sp)

    return pl.pallas_call(
        _merge_kernel,
        grid=(bsz, s // tm),
        in_specs=[row_d, pl.BlockSpec((None, 3, d), lambda b, i: (b, 0, 0)), row_d, row_d, row_g,
                  *specs,
                  _resident(w_gate), _resident(p_a), _resident(p_b), _resident(w_o)],
        out_specs=row_d,
        out_shape=jax.ShapeDtypeStruct((bsz, s, d), F32),
        scratch_shapes=[pltpu.VMEM((N_GROUPS - 1, GROUP_DIM // LANES + 2, tm, LANES), F32),
                        pltpu.VMEM((GROUP_DIM // LANES + 2, tm, LANES), F32)],
        compiler_params=_params(2),
        name="merge_out",
    )(x, mod3, h, y_a, zs, *views, w_gate, p_a, p_b, w_o)


def _layer(x, c, layer, w_ada, b_ada, norm_w, w_in, conv_w, q_norm_w, k_norm_w, w_br_conv, w_br_attn, w_out):
    bsz, s, d = x.shape
    conv_dim = conv_w.shape[-1]
    attn_dim = N_GROUPS * GROUP_DIM
    assert conv_dim == d and conv_w.shape[0] == CONV_WIDTH and d % LANES == 0 and d % STAGE_ROWS == 0
    assert w_in.shape[2] == 4 * conv_dim + 3 * attn_dim + GROUP_DIM + 2 * d
    assert s % (max(DILATIONS) * Q_BLOCK) == 0 and s % ROW_TILE == 0 and s % MIXER_ROW_TILE == 0
    assert all(w // dil == Q_BLOCK for w, dil in zip(WINDOWS, DILATIONS))

    mod3 = _modulation(c, w_ada, b_ada)
    nw = norm_w.reshape(1, d)

    y_a, h, w_attn, w_gate, p_a, p_b, w_o = _mixer_a(x, mod3, nw, w_in, layer, conv_w,
                                                     w_br_conv, w_br_attn, w_out)
    qkv, zs = _attn_proj(h, w_attn, q_norm_w, k_norm_w)
    outs, maxes, sums = zip(*[_dilated_attention(t, dil) for t, dil in zip(qkv, DILATIONS)])
    return _merge(x, mod3, h, y_a, zs, outs, maxes, sums, w_gate, p_a, p_b, w_o)


@jax.jit
def kernel(x, c, w_ada, b_ada, norm_w, w_in, conv_w, q_norm_w, k_norm_w, w_br_conv, w_br_attn, w_out):
    depth = w_ada.shape[0]
    for l in range(depth):
        x = _layer(x, c, l, w_ada[l], b_ada[l], norm_w[l], w_in, conv_w[l], q_norm_w[l],
                   k_norm_w[l], w_br_conv, w_br_attn, w_out)
    return x
```

```python
import functools

import jax
import jax.numpy as jnp
from jax import lax
from jax.experimental import pallas as pl
from jax.experimental.pallas import tpu as pltpu

F32 = jnp.float32
BF16 = jnp.bfloat16

HEAD_DIM = 64
ATTN_SLOTS = 8
WINDOWS = (128, 512, 2048)
DILATIONS = (1, 4, 16)
N_GROUPS = len(WINDOWS)
GROUP_DIM = ATTN_SLOTS * HEAD_DIM
Q_BLOCK = 128
CONV_WIDTH = 3
EPS = 1e-6
NEG_INF = -1e30
LOG2_E = 1.4426950408889634
LANES = 128
SUBLANES = 8
LSE_LANES = LANES // ATTN_SLOTS
PASS_STRIDE = 4
VMEM_LIMIT = 56 * 1024 * 1024

MIXER_ROW_TILE = 1024
STAGE_ROWS = 128
STAGE_BYTES_IN_FLIGHT = 6 * 1024 * 1024
ROW_TILE = 512
OUT_ROW_TILE = 512
ATTN_STEP_BLOCKS = 16


def _params(n_axes):
    return pltpu.CompilerParams(dimension_semantics=("arbitrary",) * n_axes,
                                vmem_limit_bytes=VMEM_LIMIT)


def _silu(t):
    return t * jax.nn.sigmoid(t)


def _modulated_norm(x, norm_w, mod):
    ms = jnp.mean(x * x, axis=-1, keepdims=True)
    xn = x * lax.rsqrt(ms + EPS) * norm_w
    return xn * (1.0 + mod[1:2, :]) + mod[0:1, :]


def _to_dilated(slab_ref, tmp_ref, n_rows, dil):
    per = n_rows // dil
    n_slabs = slab_ref.shape[0]
    if dil <= PASS_STRIDE:
        piece = lambda j, r: slab_ref[j, pl.ds(r, per, stride=dil), :]
    else:
        n_hi = dil // PASS_STRIDE
        assert n_hi <= PASS_STRIDE and dil % PASS_STRIDE == 0
        part = n_hi * per
        for j in range(n_slabs):
            for r_lo in range(PASS_STRIDE):
                tmp_ref[j, r_lo * part:(r_lo + 1) * part, :] = slab_ref[j, pl.ds(r_lo, part, stride=PASS_STRIDE), :]
        piece = lambda j, r: tmp_ref[j, pl.ds((r % PASS_STRIDE) * part + r // PASS_STRIDE, per, stride=n_hi), :]
    return jnp.concatenate(
        [jnp.concatenate([piece(j, r) for r in range(dil)], axis=0) for j in range(n_slabs)], axis=-1)


def _fetch_bf16(pairs, stage_ref, sem):
    slots, rows, _ = stage_ref.shape
    chunks = [(src, dst, r0) for src, dst in pairs for r0 in range(0, dst.shape[0], rows)]

    def copy(i):
        src, dst, r0 = chunks[i]
        return pltpu.make_async_copy(src.at[pl.ds(r0, rows), :],
                                     stage_ref.at[i % slots, :, pl.ds(0, dst.shape[1])],
                                     sem.at[i % slots])

    for i in range(min(slots - 1, len(chunks))):
        copy(i).start()
    for i, (_, dst, r0) in enumerate(chunks):
        if i + slots - 1 < len(chunks):
            copy(i + slots - 1).start()
        copy(i).wait()
        dst[r0:r0 + rows, :] = stage_ref[i % slots, :, 0:dst.shape[1]].astype(dst.dtype)


def _first_step():
    return (pl.program_id(0) == 0) & (pl.program_id(1) == 0)


_HBM = pl.BlockSpec(memory_space=pl.ANY)


def _resident(a):
    return pl.BlockSpec(a.shape, lambda *_: (0,) * a.ndim, pipeline_mode=pl.Buffered(1))


def _stage_scratch(width):
    chunk_bytes = STAGE_ROWS * width * 4
    slots = -(-STAGE_BYTES_IN_FLIGHT // chunk_bytes) + 1
    return [pltpu.VMEM((slots, STAGE_ROWS, width), F32), pltpu.SemaphoreType.DMA((slots,))]


def _mod_kernel(c_ref, w_ref, b_ref, o_ref):
    d = c_ref.shape[-1]
    sc = _silu(c_ref[...])
    for t in range(o_ref.shape[1]):
        cols = slice(t * d, (t + 1) * d)
        o_ref[:, t, :] = jnp.dot(sc, w_ref[:, cols], preferred_element_type=F32) + b_ref[:, cols]


def _modulation(c, w_ada, b_ada):
    bsz, d = c.shape
    n = w_ada.shape[1]
    whole = lambda shape: pl.BlockSpec(shape, lambda j: (0,) * len(shape), pipeline_mode=pl.Buffered(1))
    return pl.pallas_call(
        _mod_kernel,
        grid=(1,),
        in_specs=[whole((bsz, d)), whole((d, n)), whole((1, n))],
        out_specs=pl.BlockSpec((bsz, n // d, d), lambda j: (0, 0, 0)),
        out_shape=jax.ShapeDtypeStruct((bsz, n // d, d), F32),
        compiler_params=_params(1),
        name="adaln_mod",
    )(c, w_ada, b_ada.reshape(1, n))


def _mixer_a_kernel(x_ref, mod_ref, nw_ref, w_hbm, cw_ref, win_rows_ref, pa_rows_ref, pb_rows_ref,
                    wo_rows_ref, y_ref, h_ref, wattn_ref, wgate_ref, pa_ref, pb_ref, wo_ref,
                    u_ref, w_ref, stage_ref, sem, *, layer):
    tm, d = x_ref.shape
    mod = mod_ref[...]
    nw = nw_ref[...]
    cw = cw_ref[...]

    @pl.when(_first_step())
    def _():
        _fetch_bf16([(w_hbm.at[layer, :, pl.ds(0, w_ref.shape[1])], w_ref)], stage_ref, sem)

    @pl.when(pl.program_id(1) == 0)
    def _():
        u_ref[0:SUBLANES, :] = jnp.zeros((SUBLANES, d), F32)

    n_attn = wattn_ref.shape[1]
    wattn_ref[...] = win_rows_ref[:, 4 * d:4 * d + n_attn].astype(wattn_ref.dtype)
    wgate_ref[...] = win_rows_ref[:, 4 * d + n_attn:].astype(wgate_ref.dtype)
    pa_ref[...] = pa_rows_ref[...].astype(pa_ref.dtype)
    pb_ref[...] = pb_rows_ref[...].astype(pb_ref.dtype)
    wo_ref[...] = wo_rows_ref[...].astype(wo_ref.dtype)

    h = _modulated_norm(x_ref[...], nw, mod).astype(BF16)
    h_ref[...] = h
    proj = lambda j: jnp.dot(h, w_ref[:, j * d:(j + 1) * d], preferred_element_type=F32)
    z_a, c_a, x_a = proj(3), proj(1), proj(2)
    u = c_a * x_a
    u_ref[SUBLANES:, :] = u
    u_m1 = u_ref[pl.ds(SUBLANES - 1, tm), :]
    u_m2 = u_ref[pl.ds(SUBLANES - 2, tm), :]
    conv = cw[0:1, :] * u_m2 + cw[1:2, :] * u_m1 + cw[2:3, :] * u
    gated = conv * _silu(z_a)
    y_ref[...] = (proj(0) * gated).astype(y_ref.dtype)
    u_ref[0:SUBLANES, :] = u[tm - SUBLANES:, :]


def _mixer_a(x, mod3, norm_w, w_in, layer, conv_w, p_a, p_b, w_o):
    bsz, s, d = x.shape
    tm = MIXER_ROW_TILE
    tiles = s // tm
    n_steps = bsz * tiles
    n_attn = (3 * N_GROUPS + 1) * GROUP_DIM
    rows = pl.BlockSpec((None, tm, d), lambda b, i: (b, i, 0))

    def slab(a, layered):
        k, n = a.shape[-2:]
        assert k % (n_steps * 2 * SUBLANES) == 0
        step = lambda b, i: b * tiles + i
        if layered:
            return pl.BlockSpec((None, k // n_steps, n), lambda b, i: (layer, step(b, i), 0))
        return pl.BlockSpec((k // n_steps, n), lambda b, i: (step(b, i), 0))

    bf16 = lambda k, n: jax.ShapeDtypeStruct((k, n), BF16)
    out_shapes = [jax.ShapeDtypeStruct((bsz, s, d), BF16)] * 2 + [
        bf16(d, n_attn), bf16(d, 2 * d), bf16(*p_a.shape[1:]), bf16(*p_b.shape[1:]), bf16(*w_o.shape[1:])]
    return pl.pallas_call(
        functools.partial(_mixer_a_kernel, layer=layer),
        grid=(bsz, tiles),
        in_specs=[rows,
                  pl.BlockSpec((None, 3, d), lambda b, i: (b, 0, 0)),
                  pl.BlockSpec((1, d), lambda b, i: (0, 0)),
                  _HBM,
                  pl.BlockSpec(conv_w.shape, lambda b, i: (0, 0)),
                  slab(w_in, True), slab(p_a, True), slab(p_b, True), slab(w_o, True)],
        out_specs=[rows, rows] + [slab(t, False) for t in out_shapes[2:]],
        out_shape=out_shapes,
        scratch_shapes=[pltpu.VMEM((SUBLANES + tm, d), F32),
                        pltpu.VMEM((d, 4 * d), BF16),
                        *_stage_scratch(4 * d)],
        compiler_params=_params(2),
        name="mixer_a",
    )(x, mod3, norm_w, w_in, conv_w, w_in, p_a, p_b, w_o)


def _head_rms(t, w):
    sq = t * t
    lane_lo = lax.broadcasted_iota(jnp.int32, (t.shape[0], LANES), 1) < HEAD_DIM
    parts = []
    for j in range(0, t.shape[-1], LANES):
        tile = sq[:, j:j + LANES]
        even = jnp.sum(jnp.where(lane_lo, tile, 0.0), axis=-1, keepdims=True)
        odd = jnp.sum(jnp.where(lane_lo, 0.0, tile), axis=-1, keepdims=True)
        parts.append(jnp.where(lane_lo, even, odd))
    ssq = jnp.concatenate(parts, axis=-1)
    return t * lax.rsqrt(ssq * (1.0 / HEAD_DIM) + EPS) * w


def _attn_proj_kernel(h_ref, w_ref, qw_ref, kw_ref, qkv0_ref, qkv1_ref, qkv2_ref, zs_ref, slab_ref,
                      tmp_ref):
    tm, d = h_ref.shape
    h = h_ref[...]
    for j in range(d // LANES):
        slab_ref[j] = h[:, j * LANES:(j + 1) * LANES].astype(F32)
    ad = N_GROUPS * GROUP_DIM
    for g, out_ref in enumerate((qkv0_ref, qkv1_ref, qkv2_ref)):
        hg = h if DILATIONS[g] == 1 else _to_dilated(slab_ref, tmp_ref, tm, DILATIONS[g]).astype(BF16)
        q, k, v = [jnp.dot(hg, w_ref[:, part * ad + g * GROUP_DIM:part * ad + (g + 1) * GROUP_DIM],
                           preferred_element_type=F32) for part in range(3)]
        qkv = jnp.concatenate([_head_rms(q, qw_ref[...]), _head_rms(k, kw_ref[...]), v], axis=-1)
        out_ref[...] = qkv.astype(out_ref.dtype).reshape(out_ref.shape)
        if g == 0:
            z = jnp.dot(hg, w_ref[:, 3 * ad:], preferred_element_type=F32)
            zs_ref[...] = _silu(z)


def _attn_proj(h, w_attn, q_norm_w, k_norm_w):
    bsz, s, d = h.shape
    tm = ROW_TILE
    qw = jnp.tile(q_norm_w * (HEAD_DIM ** -0.5 * LOG2_E), ATTN_SLOTS).reshape(1, GROUP_DIM)
    kw = jnp.tile(k_norm_w, ATTN_SLOTS).reshape(1, GROUP_DIM)
    gw = 3 * GROUP_DIM
    out_shapes, out_specs = [], []
    for dil in DILATIONS:
        tile = Q_BLOCK * dil
        assert dil == 1 or tile % tm == 0
        if tile <= tm:
            nblk = tm // Q_BLOCK
            out_shapes.append(jax.ShapeDtypeStruct((bsz, s // Q_BLOCK, Q_BLOCK, gw), BF16))
            out_specs.append(pl.BlockSpec((None, nblk, Q_BLOCK, gw), lambda b, i: (b, i, 0, 0)))
        else:
            per = tm // dil
            steps = tile // tm
            out_shapes.append(jax.ShapeDtypeStruct((bsz, s // tile, dil, Q_BLOCK, gw), BF16))
            out_specs.append(pl.BlockSpec((None, None, dil, per, gw),
                                          lambda b, i, steps=steps: (b, i // steps, 0, i % steps, 0)))
    out_shapes.append(jax.ShapeDtypeStruct((bsz, s, GROUP_DIM), F32))
    out_specs.append(pl.BlockSpec((None, tm, GROUP_DIM), lambda b, i: (b, i, 0)))
    outs = pl.pallas_call(
        _attn_proj_kernel,
        grid=(bsz, s // tm),
        in_specs=[pl.BlockSpec((None, tm, d), lambda b, i: (b, i, 0)),
                  _resident(w_attn),
                  pl.BlockSpec((1, GROUP_DIM), lambda b, i: (0, 0)),
                  pl.BlockSpec((1, GROUP_DIM), lambda b, i: (0, 0))],
        out_specs=out_specs,
        out_shape=out_shapes,
        scratch_shapes=[pltpu.VMEM((d // LANES, tm, LANES), F32)] * 2,
        compiler_params=_params(2),
        name="attn_proj",
    )(h, w_attn, qw, kw)
    qkv = [t.reshape(bsz, s // Q_BLOCK, Q_BLOCK, gw) for t in outs[:N_GROUPS]]
    return qkv, outs[N_GROUPS]


def _stat_lane(head):
    pair, odd = divmod(head, 2)
    return pair * LSE_LANES + (0 if odd else HEAD_DIM)


def _attn_block(q, kp, kc, vp, vc, bias, o_ref, m_ref, l_ref, j):
    blk = q.shape[0]
    lane = lax.broadcasted_iota(jnp.int32, (blk, LANES), 1)
    lane_lo = lane < HEAD_DIM
    lane2_lo = lax.broadcasted_iota(jnp.int32, (2 * blk, LANES), 1) < HEAD_DIM
    zero = jnp.zeros((blk, LANES), q.dtype)
    one = jnp.ones((2 * blk, LANES), q.dtype)
    for p in range(GROUP_DIM // LANES):
        sl = slice(p * LANES, (p + 1) * LANES)
        q2 = q[:, sl]
        qs = jnp.concatenate([jnp.where(lane_lo, q2, zero), jnp.where(lane_lo, zero, q2)], axis=0)
        k2 = jnp.concatenate([kp[:, sl], kc[:, sl]], axis=0)
        v2 = jnp.concatenate([vp[:, sl], vc[:, sl]], axis=0)
        s = lax.dot_general(qs, k2, (((1,), (1,)), ((), ())), preferred_element_type=F32) + bias
        m = jnp.max(s, axis=-1, keepdims=True)
        e = jnp.exp2(s - m).astype(BF16)
        o_even = jnp.dot(e[:blk], jnp.where(lane2_lo, v2, one), preferred_element_type=F32)
        o_odd = jnp.dot(e[blk:], jnp.where(lane2_lo, one, v2), preferred_element_type=F32)
        o_ref[j + (slice(None), sl)] = jnp.where(lane_lo, o_even, o_odd)
        for head, m_h, o_h in ((2 * p, m[:blk], o_even), (2 * p + 1, m[blk:], o_odd)):
            slot = slice(_stat_lane(head), _stat_lane(head) + LSE_LANES)
            m_ref[j + (slice(None), slot)] = jnp.broadcast_to(m_h, (blk, LSE_LANES))
            l_ref[j + (slice(None), slot)] = o_h[:, slot]


def _attn_kernel(q_ref, kprev_ref, k_ref, vprev_ref, v_ref, o_ref, m_ref, l_ref):
    n_blocks, n_res, blk, _ = q_ref.shape
    span = 2 * blk
    row = lax.broadcasted_iota(jnp.int32, (span, span), 0) % blk
    col = lax.broadcasted_iota(jnp.int32, (span, span), 1)
    in_window = (col >= row) & (col <= row + blk)
    bias = jnp.where(in_window, 0.0, NEG_INF)
    run_has_prev = pl.program_id(2) > 0
    bias_first = jnp.where(in_window & ((col >= blk) | run_has_prev), 0.0, NEG_INF)

    for r in range(n_res):
        _attn_block(q_ref[0, r], kprev_ref[r], k_ref[0, r], vprev_ref[r], v_ref[0, r], bias_first,
                    o_ref, m_ref, l_ref, (0, r))
        for j in range(1, n_blocks):
            _attn_block(q_ref[j, r], k_ref[j - 1, r], k_ref[j, r], v_ref[j - 1, r], v_ref[j, r], bias,
                        o_ref, m_ref, l_ref, (j, r))


def _dilated_attention(qkv, dil):
    bsz, nb, blk, width = qkv.shape
    gd = GROUP_DIM
    tiles = nb // dil
    run = min(ATTN_STEP_BLOCKS, tiles)
    res = min(ATTN_STEP_BLOCKS // run, dil)
    assert tiles % run == 0 and dil % res == 0
    view = qkv.reshape(bsz, tiles, dil, blk, width)
    cur = lambda part: pl.BlockSpec((None, run, res, blk, gd), lambda b, r, c: (b, c, r, 0, part))
    prev = lambda part: pl.BlockSpec((None, None, res, blk, gd),
                                     lambda b, r, c: (b, jnp.maximum(c * run - 1, 0), r, 0, part))
    stat_spec = pl.BlockSpec((None, run, res, blk, LANES), lambda b, r, c: (b, c, r, 0, 0))
    stat_shape = jax.ShapeDtypeStruct((bsz, tiles, dil, blk, LANES), F32)
    o, m, l = pl.pallas_call(
        _attn_kernel,
        grid=(bsz, dil // res, tiles // run),
        in_specs=[cur(0), prev(1), cur(1), prev(2), cur(2)],
        out_specs=[pl.BlockSpec((None, run, res, blk, gd), lambda b, r, c: (b, c, r, 0, 0)),
                   stat_spec, stat_spec],
        out_shape=[jax.ShapeDtypeStruct((bsz, tiles, dil, blk, gd), F32), stat_shape, stat_shape],
        compiler_params=_params(3),
        name=f"dilated_attn_d{dil}",
    )(view, view, view, view, view)
    return (o.reshape(bsz, nb, blk, gd), m.reshape(bsz, nb, blk, LANES),
            l.reshape(bsz, nb, blk, LANES))


def _to_sequence(src_ref, lane0, seq_ref, slab, tmp_ref, dil):
    per = src_ref.shape[1]
    src = lambda r: src_ref[r, :, lane0:lane0 + LANES]
    if dil <= PASS_STRIDE:
        for r in range(dil):
            seq_ref[slab, pl.ds(r, per, stride=dil), :] = src(r)
        return
    n_hi = dil // PASS_STRIDE
    assert n_hi <= PASS_STRIDE and dil % PASS_STRIDE == 0
    part = n_hi * per
    for r in range(dil):
        r_hi, r_lo = divmod(r, PASS_STRIDE)
        tmp_ref[slab, pl.ds(r_lo * part + r_hi, per, stride=n_hi), :] = src(r)
    for r_lo in range(PASS_STRIDE):
        seq_ref[slab, pl.ds(r_lo, part, stride=PASS_STRIDE), :] = tmp_ref[slab, r_lo * part:(r_lo + 1) * part, :]


def _merge_kernel(x_ref, mod_ref, h_ref, ya_ref, zs_ref,
                  o0_ref, o1_ref, o2_ref, m0_ref, m1_ref, m2_ref, l0_ref, l1_ref, l2_ref,
                  wg_ref, pa_ref, pb_ref, wo_ref, out_ref, seq_ref, tmp_ref):
    d = x_ref.shape[-1]
    n_tiles = GROUP_DIM // LANES
    h = h_ref[...]
    g_a = jax.nn.sigmoid(jnp.dot(h, wg_ref[:, :d], preferred_element_type=F32))
    g_b = jax.nn.sigmoid(jnp.dot(h, wg_ref[:, d:], preferred_element_type=F32))

    groups = ((o0_ref, m0_ref, l0_ref), (o1_ref, m1_ref, l1_ref), (o2_ref, m2_ref, l2_ref))
    for g in range(1, N_GROUPS):
        o_ref, m_ref, l_ref = groups[g]
        sources = [(o_ref, p * LANES) for p in range(n_tiles)] + [(m_ref, 0), (l_ref, 0)]
        for slab, (src_ref, lane0) in enumerate(sources):
            _to_sequence(src_ref, lane0, seq_ref.at[g - 1], slab, tmp_ref, DILATIONS[g])

    def out_tile(g, p):
        return groups[0][0][:, p * LANES:(p + 1) * LANES] if g == 0 else seq_ref[g - 1, p]

    def stat_tile(g, which):
        return groups[0][1 + which][...] if g == 0 else seq_ref[g - 1, n_tiles + which]

    maxes = [stat_tile(g, 0) for g in range(N_GROUPS)]
    sums = [stat_tile(g, 1) for g in range(N_GROUPS)]
    top = jnp.maximum(jnp.maximum(maxes[0], maxes[1]), maxes[2])
    wts = [jnp.exp2(t - top) for t in maxes]
    den = wts[0] * sums[0] + wts[1] * sums[1] + wts[2] * sums[2]
    wns = [w / den for w in wts]
    lane_lo = lax.broadcasted_iota(jnp.int32, (h.shape[0], LANES), 1) < HEAD_DIM
    tiles = []
    for p in range(n_tiles):
        attn = None
        for g, wn in enumerate(wns):
            wide = jnp.where(lane_lo, wn[:, _stat_lane(2 * p):_stat_lane(2 * p) + 1],
                             wn[:, _stat_lane(2 * p + 1):_stat_lane(2 * p + 1) + 1])
            term = wide * out_tile(g, p)
            attn = term if attn is None else attn + term
        tiles.append(attn * zs_ref[:, p * LANES:(p + 1) * LANES])
    y_b = jnp.concatenate(tiles, axis=-1).astype(BF16)

    merged = (g_a * jnp.dot(ya_ref[...], pa_ref[...], preferred_element_type=F32)
              + g_b * jnp.dot(y_b, pb_ref[...], preferred_element_type=F32))
    upd = jnp.dot(merged.astype(BF16), wo_ref[...], preferred_element_type=F32)
    out_ref[...] = x_ref[...] + mod_ref[2:3, :] * upd


def _merge(x, mod3, h, y_a, zs, outs, maxes, sums, w_gate, p_a, p_b, w_o):
    bsz, s, d = x.shape
    tm = OUT_ROW_TILE
    row_d = pl.BlockSpec((None, tm, d), lambda b, i: (b, i, 0))
    row_g = pl.BlockSpec((None, tm, GROUP_DIM), lambda b, i: (b, i, 0))

    def dilated(t, dil):
        width = t.shape[-1]
        tile = Q_BLOCK * dil
        per = tm // dil
        if tile % tm:
            raise NotImplementedError("a dilated-order tile must be a whole number of merge tiles")
        steps = tile // tm
        view = t.reshape(bsz, s // tile, dil, Q_BLOCK, width)
        spec = pl.BlockSpec((None, None, dil, per, width),
                            lambda b, i: (b, i // steps, 0, i % steps, 0))
        return view, spec

    views, specs = [], []
    for group in (outs, maxes, sums):
        for t, dil in zip(group, DILATIONS):
            width = t.shape[-1]
            if dil == 1:
                views.append(t.reshape(bsz, s, width))
                specs.append(pl.BlockSpec((None, tm, width), lambda b, i: (b, i, 0)))
            else:
                v, sp = dilated(t, dil)
                views.append(v)
                specs.append(sp)

    return pl.pallas_call(
        _merge_kernel,
        grid=(bsz, s // tm),
        in_specs=[row_d, pl.BlockSpec((None, 3, d), lambda b, i: (b, 0, 0)), row_d, row_d, row_g,
                  *specs,
                  _resident(w_gate), _resident(p_a), _resident(p_b), _resident(w_o)],
        out_specs=row_d,
        out_shape=jax.ShapeDtypeStruct((bsz, s, d), F32),
        scratch_shapes=[pltpu.VMEM((N_GROUPS - 1, GROUP_DIM // LANES + 2, tm, LANES), F32),
                        pltpu.VMEM((GROUP_DIM // LANES + 2, tm, LANES), F32)],
        compiler_params=_params(2),
        name="merge_out",
    )(x, mod3, h, y_a, zs, *views, w_gate, p_a, p_b, w_o)


def _layer(x, c, layer, w_ada, b_ada, norm_w, w_in, conv_w, q_norm_w, k_norm_w, w_br_conv, w_br_attn, w_out):
    bsz, s, d = x.shape
    conv_dim = conv_w.shape[-1]
    attn_dim = N_GROUPS * GROUP_DIM
    assert conv_dim == d and conv_w.shape[0] == CONV_WIDTH and d % LANES == 0 and d % STAGE_ROWS == 0
    assert w_in.shape[2] == 4 * conv_dim + 3 * attn_dim + GROUP_DIM + 2 * d
    assert s % (max(DILATIONS) * Q_BLOCK) == 0 and s % ROW_TILE == 0 and s % MIXER_ROW_TILE == 0
    assert all(w // dil == Q_BLOCK for w, dil in zip(WINDOWS, DILATIONS))

    mod3 = _modulation(c, w_ada, b_ada)
    nw = norm_w.reshape(1, d)

    y_a, h, w_attn, w_gate, p_a, p_b, w_o = _mixer_a(x, mod3, nw, w_in, layer, conv_w,
                                                     w_br_conv, w_br_attn, w_out)
    qkv, zs = _attn_proj(h, w_attn, q_norm_w, k_norm_w)
    outs, maxes, sums = zip(*[_dilated_attention(t, dil) for t, dil in zip(qkv, DILATIONS)])
    return _merge(x, mod3, h, y_a, zs, outs, maxes, sums, w_gate, p_a, p_b, w_o)


@jax.jit
def kernel(x, c, w_ada, b_ada, norm_w, w_in, conv_w, q_norm_w, k_norm_w, w_br_conv, w_br_attn, w_out):
    depth = w_ada.shape[0]
    for l in range(depth):
        x = _layer(x, c, l, w_ada[l], b_ada[l], norm_w[l], w_in, conv_w[l], q_norm_w[l],
                   k_norm_w[l], w_br_conv, w_br_attn, w_out)
    return x
```

```python
import functools

import jax
import jax.numpy as jnp
from jax import lax
from jax.experimental import pallas as pl
from jax.experimental.pallas import tpu as pltpu

F32 = jnp.float32
BF16 = jnp.bfloat16

HEAD_DIM = 64
ATTN_SLOTS = 8
WINDOWS = (128, 512, 2048)
DILATIONS = (1, 4, 16)
N_GROUPS = len(WINDOWS)
GROUP_DIM = ATTN_SLOTS * HEAD_DIM
Q_BLOCK = 128
CONV_WIDTH = 3
EPS = 1e-6
NEG_INF = -1e30
LOG2_E = 1.4426950408889634
LANES = 128
SUBLANES = 8
LSE_LANES = LANES // ATTN_SLOTS
PASS_STRIDE = 4
VMEM_LIMIT = 56 * 1024 * 1024

MIXER_ROW_TILE = 1024
STAGE_ROWS = 128
STAGE_BYTES_IN_FLIGHT = 6 * 1024 * 1024
ROW_TILE = 512
OUT_ROW_TILE = 512
ATTN_STEP_BLOCKS = 16


def _params(n_axes):
    return pltpu.CompilerParams(dimension_semantics=("arbitrary",) * n_axes,
                                vmem_limit_bytes=VMEM_LIMIT)


def _silu(t):
    return t * jax.nn.sigmoid(t)


def _modulated_norm(x, norm_w, mod):
    ms = jnp.mean(x * x, axis=-1, keepdims=True)
    xn = x * lax.rsqrt(ms + EPS) * norm_w
    return xn * (1.0 + mod[1:2, :]) + mod[0:1, :]


def _to_dilated(slab_ref, tmp_ref, n_rows, dil):
    per = n_rows // dil
    n_slabs = slab_ref.shape[0]
    if dil <= PASS_STRIDE:
        piece = lambda j, r: slab_ref[j, pl.ds(r, per, stride=dil), :]
    else:
        n_hi = dil // PASS_STRIDE
        assert n_hi <= PASS_STRIDE and dil % PASS_STRIDE == 0
        part = n_hi * per
        for j in range(n_slabs):
            for r_lo in range(PASS_STRIDE):
                tmp_ref[j, r_lo * part:(r_lo + 1) * part, :] = slab_ref[j, pl.ds(r_lo, part, stride=PASS_STRIDE), :]
        piece = lambda j, r: tmp_ref[j, pl.ds((r % PASS_STRIDE) * part + r // PASS_STRIDE, per, stride=n_hi), :]
    return jnp.concatenate(
        [jnp.concatenate([piece(j, r) for r in range(dil)], axis=0) for j in range(n_slabs)], axis=-1)


def _fetch_bf16(pairs, stage_ref, sem):
    slots, rows, _ = stage_ref.shape
    chunks = [(src, dst, r0) for src, dst in pairs for r0 in range(0, dst.shape[0], rows)]

    def copy(i):
        src, dst, r0 = chunks[i]
        return pltpu.make_async_copy(src.at[pl.ds(r0, rows), :],
                                     stage_ref.at[i % slots, :, pl.ds(0, dst.shape[1])],
                                     sem.at[i % slots])

    for i in range(min(slots - 1, len(chunks))):
        copy(i).start()
    for i, (_, dst, r0) in enumerate(chunks):
        if i + slots - 1 < len(chunks):
            copy(i + slots - 1).start()
        copy(i).wait()
        dst[r0:r0 + rows, :] = stage_ref[i % slots, :, 0:dst.shape[1]].astype(dst.dtype)


def _first_step():
    return (pl.program_id(0) == 0) & (pl.program_id(1) == 0)


_HBM = pl.BlockSpec(memory_space=pl.ANY)


def _resident(a):
    return pl.BlockSpec(a.shape, lambda *_: (0,) * a.ndim, pipeline_mode=pl.Buffered(1))


def _stage_scratch(width):
    chunk_bytes = STAGE_ROWS * width * 4
    slots = -(-STAGE_BYTES_IN_FLIGHT // chunk_bytes) + 1
    return [pltpu.VMEM((slots, STAGE_ROWS, width), F32), pltpu.SemaphoreType.DMA((slots,))]


def _mod_kernel(c_ref, w_ref, b_ref, o_ref):
    d = c_ref.shape[-1]
    sc = _silu(c_ref[...])
    for t in range(o_ref.shape[1]):
        cols = slice(t * d, (t + 1) * d)
        o_ref[:, t, :] = jnp.dot(sc, w_ref[:, cols], preferred_element_type=F32) + b_ref[:, cols]


def _modulation(c, w_ada, b_ada):
    bsz, d = c.shape
    n = w_ada.shape[1]
    whole = lambda shape: pl.BlockSpec(shape, lambda j: (0,) * len(shape), pipeline_mode=pl.Buffered(1))
    return pl.pallas_call(
        _mod_kernel,
        grid=(1,),
        in_specs=[whole((bsz, d)), whole((d, n)), whole((1, n))],
        out_specs=pl.BlockSpec((bsz, n // d, d), lambda j: (0, 0, 0)),
        out_shape=jax.ShapeDtypeStruct((bsz, n // d, d), F32),
        compiler_params=_params(1),
        name="adaln_mod",
    )(c, w_ada, b_ada.reshape(1, n))


def _mixer_a_kernel(x_ref, mod_ref, nw_ref, w_hbm, cw_ref, win_rows_ref, pa_rows_ref, pb_rows_ref,
                    wo_rows_ref, y_ref, h_ref, wattn_ref, wgate_ref, pa_ref, pb_ref, wo_ref,
                    u_ref, w_ref, stage_ref, sem, *, layer):
    tm, d = x_ref.shape
    mod = mod_ref[...]
    nw = nw_ref[...]
    cw = cw_ref[...]

    @pl.when(_first_step())
    def _():
        _fetch_bf16([(w_hbm.at[layer, :, pl.ds(0, w_ref.shape[1])], w_ref)], stage_ref, sem)

    @pl.when(pl.program_id(1) == 0)
    def _():
        u_ref[0:SUBLANES, :] = jnp.zeros((SUBLANES, d), F32)

    n_attn = wattn_ref.shape[1]
    wattn_ref[...] = win_rows_ref[:, 4 * d:4 * d + n_attn].astype(wattn_ref.dtype)
    wgate_ref[...] = win_rows_ref[:, 4 * d + n_attn:].astype(wgate_ref.dtype)
    pa_ref[...] = pa_rows_ref[...].astype(pa_ref.dtype)
    pb_ref[...] = pb_rows_ref[...].astype(pb_ref.dtype)
    wo_ref[...] = wo_rows_ref[...].astype(wo_ref.dtype)

    h = _modulated_norm(x_ref[...], nw, mod).astype(BF16)
    h_ref[...] = h
    proj = lambda j: jnp.dot(h, w_ref[:, j * d:(j + 1) * d], preferred_element_type=F32)
    z_a, c_a, x_a = proj(3), proj(1), proj(2)
    u = c_a * x_a
    u_ref[SUBLANES:, :] = u
    u_m1 = u_ref[pl.ds(SUBLANES - 1, tm), :]
    u_m2 = u_ref[pl.ds(SUBLANES - 2, tm), :]
    conv = cw[0:1, :] * u_m2 + cw[1:2, :] * u_m1 + cw[2:3, :] * u
    gated = conv * _silu(z_a)
    y_ref[...] = (proj(0) * gated).astype(y_ref.dtype)
    u_ref[0:SUBLANES, :] = u[tm - SUBLANES:, :]


def _mixer_a(x, mod3, norm_w, w_in, layer, conv_w, p_a, p_b, w_o):
    bsz, s, d = x.shape
    tm = MIXER_ROW_TILE
    tiles = s // tm
    n_steps = bsz * tiles
    n_attn = (3 * N_GROUPS + 1) * GROUP_DIM
    rows = pl.BlockSpec((None, tm, d), lambda b, i: (b, i, 0))

    def slab(a, layered):
        k, n = a.shape[-2:]
        assert k % (n_steps * 2 * SUBLANES) == 0
        step = lambda b, i: b * tiles + i
        if layered:
            return pl.BlockSpec((None, k // n_steps, n), lambda b, i: (layer, step(b, i), 0))
        return pl.BlockSpec((k // n_steps, n), lambda b, i: (step(b, i), 0))

    bf16 = lambda k, n: jax.ShapeDtypeStruct((k, n), BF16)
    out_shapes = [jax.ShapeDtypeStruct((bsz, s, d), BF16)] * 2 + [
        bf16(d, n_attn), bf16(d, 2 * d), bf16(*p_a.shape[1:]), bf16(*p_b.shape[1:]), bf16(*w_o.shape[1:])]
    return pl.pallas_call(
        functools.partial(_mixer_a_kernel, layer=layer),
        grid=(bsz, tiles),
        in_specs=[rows,
                  pl.BlockSpec((None, 3, d), lambda b, i: (b, 0, 0)),
                  pl.BlockSpec((1, d), lambda b, i: (0, 0)),
                  _HBM,
                  pl.BlockSpec(conv_w.shape, lambda b, i: (0, 0)),
                  slab(w_in, True), slab(p_a, True), slab(p_b, True), slab(w_o, True)],
        out_specs=[rows, rows] + [slab(t, False) for t in out_shapes[2:]],
        out_shape=out_shapes,
        scratch_shapes=[pltpu.VMEM((SUBLANES + tm, d), F32),
                        pltpu.VMEM((d, 4 * d), BF16),
                        *_stage_scratch(4 * d)],
        compiler_params=_params(2),
        name="mixer_a",
    )(x, mod3, norm_w, w_in, conv_w, w_in, p_a, p_b, w_o)


def _head_rms(t, w):
    sq = t * t
    lane_lo = lax.broadcasted_iota(jnp.int32, (t.shape[0], LANES), 1) < HEAD_DIM
    parts = []
    for j in range(0, t.shape[-1], LANES):
        tile = sq[:, j:j + LANES]
        even = jnp.sum(jnp.where(lane_lo, tile, 0.0), axis=-1, keepdims=True)
        odd = jnp.sum(jnp.where(lane_lo, 0.0, tile), axis=-1, keepdims=True)
        parts.append(jnp.where(lane_lo, even, odd))
    ssq = jnp.concatenate(parts, axis=-1)
    return t * lax.rsqrt(ssq * (1.0 / HEAD_DIM) + EPS) * w


def _attn_proj_kernel(h_ref, w_ref, qw_ref, kw_ref, qkv0_ref, qkv1_ref, qkv2_ref, zs_ref, slab_ref,
                      tmp_ref):
    tm, d = h_ref.shape
    h = h_ref[...]
    for j in range(d // LANES):
        slab_ref[j] = h[:, j * LANES:(j + 1) * LANES].astype(F32)
    ad = N_GROUPS * GROUP_DIM
    for g, out_ref in enumerate((qkv0_ref, qkv1_ref, qkv2_ref)):
        hg = h if DILATIONS[g] == 1 else _to_dilated(slab_ref, tmp_ref, tm, DILATIONS[g]).astype(BF16)
        q, k, v = [jnp.dot(hg, w_ref[:, part * ad + g * GROUP_DIM:part * ad + (g + 1) * GROUP_DIM],
                           preferred_element_type=F32) for part in range(3)]
        qkv = jnp.concatenate([_head_rms(q, qw_ref[...]), _head_rms(k, kw_ref[...]), v], axis=-1)
        out_ref[...] = qkv.astype(out_ref.dtype).reshape(out_ref.shape)
        if g == 0:
            z = jnp.dot(hg, w_ref[:, 3 * ad:], preferred_element_type=F32)
            zs_ref[...] = _silu(z)


def _attn_proj(h, w_attn, q_norm_w, k_norm_w):
    bsz, s, d = h.shape
    tm = ROW_TILE
    qw = jnp.tile(q_norm_w * (HEAD_DIM ** -0.5 * LOG2_E), ATTN_SLOTS).reshape(1, GROUP_DIM)
    kw = jnp.tile(k_norm_w, ATTN_SLOTS).reshape(1, GROUP_DIM)
    gw = 3 * GROUP_DIM
    out_shapes, out_specs = [], []
    for dil in DILATIONS:
        tile = Q_BLOCK * dil
        assert dil == 1 or tile % tm == 0
        if tile <= tm:
            nblk = tm // Q_BLOCK
            out_shapes.append(jax.ShapeDtypeStruct((bsz, s // Q_BLOCK, Q_BLOCK, gw), BF16))
            out_specs.append(pl.BlockSpec((None, nblk, Q_BLOCK, gw), lambda b, i: (b, i, 0, 0)))
        else:
            per = tm // dil
            steps = tile // tm
            out_shapes.append(jax.ShapeDtypeStruct((bsz, s // tile, dil, Q_BLOCK, gw), BF16))
            out_specs.append(pl.BlockSpec((None, None, dil, per, gw),
                                          lambda b, i, steps=steps: (b, i // steps, 0, i % steps, 0)))
    out_shapes.append(jax.ShapeDtypeStruct((bsz, s, GROUP_DIM), F32))
    out_specs.append(pl.BlockSpec((None, tm, GROUP_DIM), lambda b, i: (b, i, 0)))
    outs = pl.pallas_call(
        _attn_proj_kernel,
        grid=(bsz, s // tm),
        in_specs=[pl.BlockSpec((None, tm, d), lambda b, i: (b, i, 0)),
                  _resident(w_attn),
                  pl.BlockSpec((1, GROUP_DIM), lambda b, i: (0, 0)),
                  pl.BlockSpec((1, GROUP_DIM), lambda b, i: (0, 0))],
        out_specs=out_specs,
        out_shape=out_shapes,
        scratch_shapes=[pltpu.VMEM((d // LANES, tm, LANES), F32)] * 2,
        compiler_params=_params(2),
        name="attn_proj",
    )(h, w_attn, qw, kw)
    qkv = [t.reshape(bsz, s // Q_BLOCK, Q_BLOCK, gw) for t in outs[:N_GROUPS]]
    return qkv, outs[N_GROUPS]


def _stat_lane(head):
    pair, odd = divmod(head, 2)
    return pair * LSE_LANES + (0 if odd else HEAD_DIM)


def _attn_block(q, kp, kc, vp, vc, bias, o_ref, m_ref, l_ref, j):
    blk = q.shape[0]
    lane = lax.broadcasted_iota(jnp.int32, (blk, LANES), 1)
    lane_lo = lane < HEAD_DIM
    zero = jnp.zeros((blk, LANES), q.dtype)
    one = jnp.ones((2 * blk, LANES), q.dtype)
    for p in range(GROUP_DIM // LANES):
        sl = slice(p * LANES, (p + 1) * LANES)
        q2 = q[:, sl]
        qs = jnp.concatenate([jnp.where(lane_lo, q2, zero), jnp.where(lane_lo, zero, q2)], axis=0)
        k2 = jnp.concatenate([kp[:, sl], kc[:, sl]], axis=0)
        v2 = jnp.concatenate([vp[:, sl], vc[:, sl]], axis=0)
        s = lax.dot_general(qs, k2, (((1,), (1,)), ((), ())), preferred_element_type=F32) + bias
        m = jnp.max(s, axis=-1, keepdims=True)
        e = jnp.exp2(s - m).astype(BF16)
        res = jnp.dot(e, jnp.concatenate([v2, one], axis=-1), preferred_element_type=F32)
        o_ref[j + (slice(None), sl)] = jnp.where(lane_lo, res[:blk, :LANES], res[blk:, :LANES])
        for head, rows in ((2 * p, slice(0, blk)), (2 * p + 1, slice(blk, 2 * blk))):
            slot = slice(_stat_lane(head), _stat_lane(head) + LSE_LANES)
            m_ref[j + (slice(None), slot)] = jnp.broadcast_to(m[rows], (blk, LSE_LANES))
            l_ref[j + (slice(None), slot)] = res[rows, LANES + slot.start:LANES + slot.stop]


def _attn_kernel(q_ref, kprev_ref, k_ref, vprev_ref, v_ref, o_ref, m_ref, l_ref):
    n_blocks, n_res, blk, _ = q_ref.shape
    span = 2 * blk
    row = lax.broadcasted_iota(jnp.int32, (span, span), 0) % blk
    col = lax.broadcasted_iota(jnp.int32, (span, span), 1)
    in_window = (col >= row) & (col <= row + blk)
    bias = jnp.where(in_window, 0.0, NEG_INF)
    run_has_prev = pl.program_id(2) > 0
    bias_first = jnp.where(in_window & ((col >= blk) | run_has_prev), 0.0, NEG_INF)

    for r in range(n_res):
        _attn_block(q_ref[0, r], kprev_ref[r], k_ref[0, r], vprev_ref[r], v_ref[0, r], bias_first,
                    o_ref, m_ref, l_ref, (0, r))
        for j in range(1, n_blocks):
            _attn_block(q_ref[j, r], k_ref[j - 1, r], k_ref[j, r], v_ref[j - 1, r], v_ref[j, r], bias,
                        o_ref, m_ref, l_ref, (j, r))


def _dilated_attention(qkv, dil):
    bsz, nb, blk, width = qkv.shape
    gd = GROUP_DIM
    tiles = nb // dil
    run = min(ATTN_STEP_BLOCKS, tiles)
    res = min(ATTN_STEP_BLOCKS // run, dil)
    assert tiles % run == 0 and dil % res == 0
    view = qkv.reshape(bsz, tiles, dil, blk, width)
    cur = lambda part: pl.BlockSpec((None, run, res, blk, gd), lambda b, r, c: (b, c, r, 0, part))
    prev = lambda part: pl.BlockSpec((None, None, res, blk, gd),
                                     lambda b, r, c: (b, jnp.maximum(c * run - 1, 0), r, 0, part))
    stat_spec = pl.BlockSpec((None, run, res, blk, LANES), lambda b, r, c: (b, c, r, 0, 0))
    stat_shape = jax.ShapeDtypeStruct((bsz, tiles, dil, blk, LANES), F32)
    o, m, l = pl.pallas_call(
        _attn_kernel,
        grid=(bsz, dil // res, tiles // run),
        in_specs=[cur(0), prev(1), cur(1), prev(2), cur(2)],
        out_specs=[pl.BlockSpec((None, run, res, blk, gd), lambda b, r, c: (b, c, r, 0, 0)),
                   stat_spec, stat_spec],
        out_shape=[jax.ShapeDtypeStruct((bsz, tiles, dil, blk, gd), F32), stat_shape, stat_shape],
        compiler_params=_params(3),
        name=f"dilated_attn_d{dil}",
    )(view, view, view, view, view)
    return (o.reshape(bsz, nb, blk, gd), m.reshape(bsz, nb, blk, LANES),
            l.reshape(bsz, nb, blk, LANES))


def _to_sequence(src_ref, lane0, seq_ref, slab, tmp_ref, dil):
    per = src_ref.shape[1]
    src = lambda r: src_ref[r, :, lane0:lane0 + LANES]
    if dil <= PASS_STRIDE:
        for r in range(dil):
            seq_ref[slab, pl.ds(r, per, stride=dil), :] = src(r)
        return
    n_hi = dil // PASS_STRIDE
    assert n_hi <= PASS_STRIDE and dil % PASS_STRIDE == 0
    part = n_hi * per
    for r in range(dil):
        r_hi, r_lo = divmod(r, PASS_STRIDE)
        tmp_ref[slab, pl.ds(r_lo * part + r_hi, per, stride=n_hi), :] = src(r)
    for r_lo in range(PASS_STRIDE):
        seq_ref[slab, pl.ds(r_lo, part, stride=PASS_STRIDE), :] = tmp_ref[slab, r_lo * part:(r_lo + 1) * part, :]


def _merge_kernel(x_ref, mod_ref, h_ref, ya_ref, zs_ref,
                  o0_ref, o1_ref, o2_ref, m0_ref, m1_ref, m2_ref, l0_ref, l1_ref, l2_ref,
                  wg_ref, pa_ref, pb_ref, wo_ref, out_ref, seq_ref, tmp_ref):
    d = x_ref.shape[-1]
    n_tiles = GROUP_DIM // LANES
    h = h_ref[...]
    g_a = jax.nn.sigmoid(jnp.dot(h, wg_ref[:, :d], preferred_element_type=F32))
    g_b = jax.nn.sigmoid(jnp.dot(h, wg_ref[:, d:], preferred_element_type=F32))

    groups = ((o0_ref, m0_ref, l0_ref), (o1_ref, m1_ref, l1_ref), (o2_ref, m2_ref, l2_ref))
    for g in range(1, N_GROUPS):
        o_ref, m_ref, l_ref = groups[g]
        sources = [(o_ref, p * LANES) for p in range(n_tiles)] + [(m_ref, 0), (l_ref, 0)]
        for slab, (src_ref, lane0) in enumerate(sources):
            _to_sequence(src_ref, lane0, seq_ref.at[g - 1], slab, tmp_ref, DILATIONS[g])

    def out_tile(g, p):
        return groups[0][0][:, p * LANES:(p + 1) * LANES] if g == 0 else seq_ref[g - 1, p]

    def stat_tile(g, which):
        return groups[0][1 + which][...] if g == 0 else seq_ref[g - 1, n_tiles + which]

    maxes = [stat_tile(g, 0) for g in range(N_GROUPS)]
    sums = [stat_tile(g, 1) for g in range(N_GROUPS)]
    top = jnp.maximum(jnp.maximum(maxes[0], maxes[1]), maxes[2])
    wts = [jnp.exp2(t - top) for t in maxes]
    den = wts[0] * sums[0] + wts[1] * sums[1] + wts[2] * sums[2]
    wns = [w / den for w in wts]
    lane_lo = lax.broadcasted_iota(jnp.int32, (h.shape[0], LANES), 1) < HEAD_DIM
    tiles = []
    for p in range(n_tiles):
        attn = None
        for g, wn in enumerate(wns):
            wide = jnp.where(lane_lo, wn[:, _stat_lane(2 * p):_stat_lane(2 * p) + 1],
                             wn[:, _stat_lane(2 * p + 1):_stat_lane(2 * p + 1) + 1])
            term = wide * out_tile(g, p)
            attn = term if attn is None else attn + term
        tiles.append(attn * zs_ref[:, p * LANES:(p + 1) * LANES])
    y_b = jnp.concatenate(tiles, axis=-1).astype(BF16)

    merged = (g_a * jnp.dot(ya_ref[...], pa_ref[...], preferred_element_type=F32)
              + g_b * jnp.dot(y_b, pb_ref[...], preferred_element_type=F32))
    upd = jnp.dot(merged.astype(BF16), wo_ref[...], preferred_element_type=F32)
    out_ref[...] = x_ref[...] + mod_ref[2:3, :] * upd


def _merge(x, mod3, h, y_a, zs, outs, maxes, sums, w_gate, p_a, p_b, w_o):
    bsz, s, d = x.shape
    tm = OUT_ROW_TILE
    row_d = pl.BlockSpec((None, tm, d), lambda b, i: (b, i, 0))
    row_g = pl.BlockSpec((None, tm, GROUP_DIM), lambda b, i: (b, i, 0))

    def dilated(t, dil):
        width = t.shape[-1]
        tile = Q_BLOCK * dil
        per = tm // dil
        if tile % tm:
            raise NotImplementedError("a dilated-order tile must be a whole number of merge tiles")
        steps = tile // tm
        view = t.reshape(bsz, s // tile, dil, Q_BLOCK, width)
        spec = pl.BlockSpec((None, None, dil, per, width),
                            lambda b, i: (b, i // steps, 0, i % steps, 0))
        return view, spec

    views, specs = [], []
    for group in (outs, maxes, sums):
        for t, dil in zip(group, DILATIONS):
            width = t.shape[-1]
            if dil == 1:
                views.append(t.reshape(bsz, s, width))
                specs.append(pl.BlockSpec((None, tm, width), lambda b, i: (b, i, 0)))
            else:
                v, sp = dilated(t, dil)
                views.append(v)
                specs.append(sp)

    return pl.pallas_call(
        _merge_kernel,
        grid=(bsz, s // tm),
        in_specs=[row_d, pl.BlockSpec((None, 3, d), lambda b, i: (b, 0, 0)), row_d, row_d, row_g,
                  *specs,
                  _resident(w_gate), _resident(p_a), _resident(p_b), _resident(w_o)],
        out_specs=row_d,
        out_shape=jax.ShapeDtypeStruct((bsz, s, d), F32),
        scratch_shapes=[pltpu.VMEM((N_GROUPS - 1, GROUP_DIM // LANES + 2, tm, LANES), F32),
                        pltpu.VMEM((GROUP_DIM // LANES + 2, tm, LANES), F32)],
        compiler_params=_params(2),
        name="merge_out",
    )(x, mod3, h, y_a, zs, *views, w_gate, p_a, p_b, w_o)


def _layer(x, c, layer, w_ada, b_ada, norm_w, w_in, conv_w, q_norm_w, k_norm_w, w_br_conv, w_br_attn, w_out):
    bsz, s, d = x.shape
    conv_dim = conv_w.shape[-1]
    attn_dim = N_GROUPS * GROUP_DIM
    assert conv_dim == d and conv_w.shape[0] == CONV_WIDTH and d % LANES == 0 and d % STAGE_ROWS == 0
    assert w_in.shape[2] == 4 * conv_dim + 3 * attn_dim + GROUP_DIM + 2 * d
    assert s % (max(DILATIONS) * Q_BLOCK) == 0 and s % ROW_TILE == 0 and s % MIXER_ROW_TILE == 0
    assert all(w // dil == Q_BLOCK for w, dil in zip(WINDOWS, DILATIONS))

    mod3 = _modulation(c, w_ada, b_ada)
    nw = norm_w.reshape(1, d)

    y_a, h, w_attn, w_gate, p_a, p_b, w_o = _mixer_a(x, mod3, nw, w_in, layer, conv_w,
                                                     w_br_conv, w_br_attn, w_out)
    qkv, zs = _attn_proj(h, w_attn, q_norm_w, k_norm_w)
    outs, maxes, sums = zip(*[_dilated_attention(t, dil) for t, dil in zip(qkv, DILATIONS)])
    return _merge(x, mod3, h, y_a, zs, outs, maxes, sums, w_gate, p_a, p_b, w_o)


@jax.jit
def kernel(x, c, w_ada, b_ada, norm_w, w_in, conv_w, q_norm_w, k_norm_w, w_br_conv, w_br_attn, w_out):
    depth = w_ada.shape[0]
    for l in range(depth):
        x = _layer(x, c, l, w_ada[l], b_ada[l], norm_w[l], w_in, conv_w[l], q_norm_w[l],
                   k_norm_w[l], w_br_conv, w_br_attn, w_out)
    return x
```

```python
import functools

import jax
import jax.numpy as jnp
from jax import lax
from jax.experimental import pallas as pl
from jax.experimental.pallas import tpu as pltpu

F32 = jnp.float32
BF16 = jnp.bfloat16

HEAD_DIM = 64
ATTN_SLOTS = 8
WINDOWS = (128, 512, 2048)
DILATIONS = (1, 4, 16)
N_GROUPS = len(WINDOWS)
GROUP_DIM = ATTN_SLOTS * HEAD_DIM
Q_BLOCK = 128
CONV_WIDTH = 3
EPS = 1e-6
NEG_INF = -1e30
LOG2_E = 1.4426950408889634
LANES = 128
SUBLANES = 8
LSE_LANES = LANES // ATTN_SLOTS
PASS_STRIDE = 4
VMEM_LIMIT = 56 * 1024 * 1024

MIXER_ROW_TILE = 1024
STAGE_ROWS = 128
STAGE_BYTES_IN_FLIGHT = 6 * 1024 * 1024
ROW_TILE = 512
OUT_ROW_TILE = 512
ATTN_STEP_BLOCKS = 16


def _params(n_axes):
    return pltpu.CompilerParams(dimension_semantics=("arbitrary",) * n_axes,
                                vmem_limit_bytes=VMEM_LIMIT)


def _silu(t):
    return t * jax.nn.sigmoid(t)


def _modulated_norm(x, norm_w, mod):
    ms = jnp.mean(x * x, axis=-1, keepdims=True)
    xn = x * lax.rsqrt(ms + EPS) * norm_w
    return xn * (1.0 + mod[1:2, :]) + mod[0:1, :]


def _to_dilated(slab_ref, tmp_ref, n_rows, dil):
    per = n_rows // dil
    n_slabs = slab_ref.shape[0]
    if dil <= PASS_STRIDE:
        piece = lambda j, r: slab_ref[j, pl.ds(r, per, stride=dil), :]
    else:
        n_hi = dil // PASS_STRIDE
        assert n_hi <= PASS_STRIDE and dil % PASS_STRIDE == 0
        part = n_hi * per
        for j in range(n_slabs):
            for r_lo in range(PASS_STRIDE):
                tmp_ref[j, r_lo * part:(r_lo + 1) * part, :] = slab_ref[j, pl.ds(r_lo, part, stride=PASS_STRIDE), :]
        piece = lambda j, r: tmp_ref[j, pl.ds((r % PASS_STRIDE) * part + r // PASS_STRIDE, per, stride=n_hi), :]
    return jnp.concatenate(
        [jnp.concatenate([piece(j, r) for r in range(dil)], axis=0) for j in range(n_slabs)], axis=-1)


def _fetch_bf16(pairs, stage_ref, sem):
    slots, rows, _ = stage_ref.shape
    chunks = [(src, dst, r0) for src, dst in pairs for r0 in range(0, dst.shape[0], rows)]

    def copy(i):
        src, dst, r0 = chunks[i]
        return pltpu.make_async_copy(src.at[pl.ds(r0, rows), :],
                                     stage_ref.at[i % slots, :, pl.ds(0, dst.shape[1])],
                                     sem.at[i % slots])

    for i in range(min(slots - 1, len(chunks))):
        copy(i).start()
    for i, (_, dst, r0) in enumerate(chunks):
        if i + slots - 1 < len(chunks):
            copy(i + slots - 1).start()
        copy(i).wait()
        dst[r0:r0 + rows, :] = stage_ref[i % slots, :, 0:dst.shape[1]].astype(dst.dtype)


def _first_step():
    return (pl.program_id(0) == 0) & (pl.program_id(1) == 0)


_HBM = pl.BlockSpec(memory_space=pl.ANY)


def _resident(a):
    return pl.BlockSpec(a.shape, lambda *_: (0,) * a.ndim, pipeline_mode=pl.Buffered(1))


def _stage_scratch(width):
    chunk_bytes = STAGE_ROWS * width * 4
    slots = -(-STAGE_BYTES_IN_FLIGHT // chunk_bytes) + 1
    return [pltpu.VMEM((slots, STAGE_ROWS, width), F32), pltpu.SemaphoreType.DMA((slots,))]


def _mod_kernel(c_ref, b_ref, *refs):
    *w_refs, o_ref = refs
    d = c_ref.shape[-1]
    sc = _silu(c_ref[...])
    for t, w_ref in enumerate(w_refs):
        o_ref[:, t, :] = (jnp.dot(sc, w_ref[...], preferred_element_type=F32)
                          + b_ref[:, t * d:(t + 1) * d])


def _modulation(c, w_ada, b_ada):
    bsz, d = c.shape
    n = w_ada.shape[1]
    n_rows = n // d
    whole = lambda shape: pl.BlockSpec(shape, lambda j: (0,) * len(shape), pipeline_mode=pl.Buffered(1))
    cols = lambda t: pl.BlockSpec((d, d), lambda j: (0, t), pipeline_mode=pl.Buffered(1))
    return pl.pallas_call(
        _mod_kernel,
        grid=(1,),
        in_specs=[whole((bsz, d)), whole((1, n))] + [cols(t) for t in range(n_rows)],
        out_specs=pl.BlockSpec((bsz, n_rows, d), lambda j: (0, 0, 0)),
        out_shape=jax.ShapeDtypeStruct((bsz, n_rows, d), F32),
        compiler_params=_params(1),
        name="adaln_mod",
    )(c, b_ada.reshape(1, n), *([w_ada] * n_rows))


def _mixer_a_kernel(x_ref, mod_ref, nw_ref, w_hbm, cw_ref, win_rows_ref, pa_rows_ref, pb_rows_ref,
                    wo_rows_ref, y_ref, h_ref, wattn_ref, wgate_ref, pa_ref, pb_ref, wo_ref,
                    u_ref, w_ref, stage_ref, sem, *, layer):
    tm, d = x_ref.shape
    mod = mod_ref[...]
    nw = nw_ref[...]
    cw = cw_ref[...]

    @pl.when(_first_step())
    def _():
        _fetch_bf16([(w_hbm.at[layer, :, pl.ds(0, w_ref.shape[1])], w_ref)], stage_ref, sem)

    @pl.when(pl.program_id(1) == 0)
    def _():
        u_ref[0:SUBLANES, :] = jnp.zeros((SUBLANES, d), F32)

    n_attn = wattn_ref.shape[1]
    wattn_ref[...] = win_rows_ref[:, 4 * d:4 * d + n_attn].astype(wattn_ref.dtype)
    wgate_ref[...] = win_rows_ref[:, 4 * d + n_attn:].astype(wgate_ref.dtype)
    pa_ref[...] = pa_rows_ref[...].astype(pa_ref.dtype)
    pb_ref[...] = pb_rows_ref[...].astype(pb_ref.dtype)
    wo_ref[...] = wo_rows_ref[...].astype(wo_ref.dtype)

    h = _modulated_norm(x_ref[...], nw, mod).astype(BF16)
    h_ref[...] = h
    proj = lambda j: jnp.dot(h, w_ref[:, j * d:(j + 1) * d], preferred_element_type=F32)
    z_a, c_a, x_a = proj(3), proj(1), proj(2)
    u = c_a * x_a
    u_ref[SUBLANES:, :] = u
    u_m1 = u_ref[pl.ds(SUBLANES - 1, tm), :]
    u_m2 = u_ref[pl.ds(SUBLANES - 2, tm), :]
    conv = cw[0:1, :] * u_m2 + cw[1:2, :] * u_m1 + cw[2:3, :] * u
    gated = conv * _silu(z_a)
    y_ref[...] = (proj(0) * gated).astype(y_ref.dtype)
    u_ref[0:SUBLANES, :] = u[tm - SUBLANES:, :]


def _mixer_a(x, mod3, norm_w, w_in, layer, conv_w, p_a, p_b, w_o):
    bsz, s, d = x.shape
    tm = MIXER_ROW_TILE
    tiles = s // tm
    n_steps = bsz * tiles
    n_attn = (3 * N_GROUPS + 1) * GROUP_DIM
    rows = pl.BlockSpec((None, tm, d), lambda b, i: (b, i, 0))

    def slab(a, layered):
        k, n = a.shape[-2:]
        assert k % (n_steps * 2 * SUBLANES) == 0
        step = lambda b, i: b * tiles + i
        if layered:
            return pl.BlockSpec((None, k // n_steps, n), lambda b, i: (layer, step(b, i), 0))
        return pl.BlockSpec((k // n_steps, n), lambda b, i: (step(b, i), 0))

    bf16 = lambda k, n: jax.ShapeDtypeStruct((k, n), BF16)
    out_shapes = [jax.ShapeDtypeStruct((bsz, s, d), BF16)] * 2 + [
        bf16(d, n_attn), bf16(d, 2 * d), bf16(*p_a.shape[1:]), bf16(*p_b.shape[1:]), bf16(*w_o.shape[1:])]
    return pl.pallas_call(
        functools.partial(_mixer_a_kernel, layer=layer),
        grid=(bsz, tiles),
        in_specs=[rows,
                  pl.BlockSpec((None, 3, d), lambda b, i: (b, 0, 0)),
                  pl.BlockSpec((1, d), lambda b, i: (0, 0)),
                  _HBM,
                  pl.BlockSpec(conv_w.shape, lambda b, i: (0, 0)),
                  slab(w_in, True), slab(p_a, True), slab(p_b, True), slab(w_o, True)],
        out_specs=[rows, rows] + [slab(t, False) for t in out_shapes[2:]],
        out_shape=out_shapes,
        scratch_shapes=[pltpu.VMEM((SUBLANES + tm, d), F32),
                        pltpu.VMEM((d, 4 * d), BF16),
                        *_stage_scratch(4 * d)],
        compiler_params=_params(2),
        name="mixer_a",
    )(x, mod3, norm_w, w_in, conv_w, w_in, p_a, p_b, w_o)


def _head_rms(t, w):
    sq = t * t
    lane_lo = lax.broadcasted_iota(jnp.int32, (t.shape[0], LANES), 1) < HEAD_DIM
    parts = []
    for j in range(0, t.shape[-1], LANES):
        tile = sq[:, j:j + LANES]
        even = jnp.sum(jnp.where(lane_lo, tile, 0.0), axis=-1, keepdims=True)
        odd = jnp.sum(jnp.where(lane_lo, 0.0, tile), axis=-1, keepdims=True)
        parts.append(jnp.where(lane_lo, even, odd))
    ssq = jnp.concatenate(parts, axis=-1)
    return t * lax.rsqrt(ssq * (1.0 / HEAD_DIM) + EPS) * w


def _attn_proj_kernel(h_ref, w_ref, qw_ref, kw_ref, qkv0_ref, qkv1_ref, qkv2_ref, zs_ref, slab_ref,
                      tmp_ref):
    tm, d = h_ref.shape
    h = h_ref[...]
    for j in range(d // LANES):
        slab_ref[j] = h[:, j * LANES:(j + 1) * LANES].astype(F32)
    ad = N_GROUPS * GROUP_DIM
    for g, out_ref in enumerate((qkv0_ref, qkv1_ref, qkv2_ref)):
        hg = h if DILATIONS[g] == 1 else _to_dilated(slab_ref, tmp_ref, tm, DILATIONS[g]).astype(BF16)
        q, k, v = [jnp.dot(hg, w_ref[:, part * ad + g * GROUP_DIM:part * ad + (g + 1) * GROUP_DIM],
                           preferred_element_type=F32) for part in range(3)]
        qkv = jnp.concatenate([_head_rms(q, qw_ref[...]), _head_rms(k, kw_ref[...]), v], axis=-1)
        out_ref[...] = qkv.astype(out_ref.dtype).reshape(out_ref.shape)
        if g == 0:
            z = jnp.dot(hg, w_ref[:, 3 * ad:], preferred_element_type=F32)
            zs_ref[...] = _silu(z)


def _attn_proj(h, w_attn, q_norm_w, k_norm_w):
    bsz, s, d = h.shape
    tm = ROW_TILE
    qw = jnp.tile(q_norm_w * (HEAD_DIM ** -0.5 * LOG2_E), ATTN_SLOTS).reshape(1, GROUP_DIM)
    kw = jnp.tile(k_norm_w, ATTN_SLOTS).reshape(1, GROUP_DIM)
    gw = 3 * GROUP_DIM
    out_shapes, out_specs = [], []
    for dil in DILATIONS:
        tile = Q_BLOCK * dil
        assert dil == 1 or tile % tm == 0
        if tile <= tm:
            nblk = tm // Q_BLOCK
            out_shapes.append(jax.ShapeDtypeStruct((bsz, s // Q_BLOCK, Q_BLOCK, gw), BF16))
            out_specs.append(pl.BlockSpec((None, nblk, Q_BLOCK, gw), lambda b, i: (b, i, 0, 0)))
        else:
            per = tm // dil
            steps = tile // tm
            out_shapes.append(jax.ShapeDtypeStruct((bsz, s // tile, dil, Q_BLOCK, gw), BF16))
            out_specs.append(pl.BlockSpec((None, None, dil, per, gw),
                                          lambda b, i, steps=steps: (b, i // steps, 0, i % steps, 0)))
    out_shapes.append(jax.ShapeDtypeStruct((bsz, s, GROUP_DIM), F32))
    out_specs.append(pl.BlockSpec((None, tm, GROUP_DIM), lambda b, i: (b, i, 0)))
    outs = pl.pallas_call(
        _attn_proj_kernel,
        grid=(bsz, s // tm),
        in_specs=[pl.BlockSpec((None, tm, d), lambda b, i: (b, i, 0)),
                  _resident(w_attn),
                  pl.BlockSpec((1, GROUP_DIM), lambda b, i: (0, 0)),
                  pl.BlockSpec((1, GROUP_DIM), lambda b, i: (0, 0))],
        out_specs=out_specs,
        out_shape=out_shapes,
        scratch_shapes=[pltpu.VMEM((d // LANES, tm, LANES), F32)] * 2,
        compiler_params=_params(2),
        name="attn_proj",
    )(h, w_attn, qw, kw)
    qkv = [t.reshape(bsz, s // Q_BLOCK, Q_BLOCK, gw) for t in outs[:N_GROUPS]]
    return qkv, outs[N_GROUPS]


def _stat_lane(head):
    return head * LSE_LANES


def _attn_block(q, kp, kc, vp, vc, bias, o_ref, m_ref, l_ref, j):
    blk = q.shape[0]
    lane = lax.broadcasted_iota(jnp.int32, (blk, LANES), 1)
    lane_lo = lane < HEAD_DIM
    zero = jnp.zeros((blk, LANES), q.dtype)
    one = jnp.ones((2 * blk, LANES), q.dtype)
    for p in range(GROUP_DIM // LANES):
        sl = slice(p * LANES, (p + 1) * LANES)
        q2 = q[:, sl]
        qs = jnp.concatenate([jnp.where(lane_lo, q2, zero), jnp.where(lane_lo, zero, q2)], axis=0)
        k2 = jnp.concatenate([kp[:, sl], kc[:, sl]], axis=0)
        v2 = jnp.concatenate([vp[:, sl], vc[:, sl]], axis=0)
        s = lax.dot_general(qs, k2, (((1,), (1,)), ((), ())), preferred_element_type=F32) + bias
        m = jnp.max(s, axis=-1, keepdims=True)
        e = jnp.exp2(s - m).astype(BF16)
        res = jnp.dot(e, jnp.concatenate([v2, one], axis=-1), preferred_element_type=F32)
        o_ref[j + (slice(None), sl)] = jnp.where(lane_lo, res[:blk, :LANES], res[blk:, :LANES])
        for head, rows in ((2 * p, slice(0, blk)), (2 * p + 1, slice(blk, 2 * blk))):
            slot = slice(_stat_lane(head), _stat_lane(head) + LSE_LANES)
            m_ref[j + (slice(None), slot)] = jnp.broadcast_to(m[rows], (blk, LSE_LANES))
            l_ref[j + (slice(None), slot)] = res[rows, LANES + slot.start:LANES + slot.stop]


def _attn_kernel(q_ref, kprev_ref, k_ref, vprev_ref, v_ref, o_ref, m_ref, l_ref):
    n_blocks, n_res, blk, _ = q_ref.shape
    span = 2 * blk
    row = lax.broadcasted_iota(jnp.int32, (span, span), 0) % blk
    col = lax.broadcasted_iota(jnp.int32, (span, span), 1)
    in_window = (col >= row) & (col <= row + blk)
    bias = jnp.where(in_window, 0.0, NEG_INF)
    run_has_prev = pl.program_id(2) > 0
    bias_first = jnp.where(in_window & ((col >= blk) | run_has_prev), 0.0, NEG_INF)

    for r in range(n_res):
        _attn_block(q_ref[0, r], kprev_ref[r], k_ref[0, r], vprev_ref[r], v_ref[0, r], bias_first,
                    o_ref, m_ref, l_ref, (0, r))
        for j in range(1, n_blocks):
            _attn_block(q_ref[j, r], k_ref[j - 1, r], k_ref[j, r], v_ref[j - 1, r], v_ref[j, r], bias,
                        o_ref, m_ref, l_ref, (j, r))


def _dilated_attention(qkv, dil):
    bsz, nb, blk, width = qkv.shape
    gd = GROUP_DIM
    tiles = nb // dil
    run = min(ATTN_STEP_BLOCKS, tiles)
    res = min(ATTN_STEP_BLOCKS // run, dil)
    assert tiles % run == 0 and dil % res == 0
    view = qkv.reshape(bsz, tiles, dil, blk, width)
    cur = lambda part: pl.BlockSpec((None, run, res, blk, gd), lambda b, r, c: (b, c, r, 0, part))
    prev = lambda part: pl.BlockSpec((None, None, res, blk, gd),
                                     lambda b, r, c: (b, jnp.maximum(c * run - 1, 0), r, 0, part))
    stat_spec = pl.BlockSpec((None, run, res, blk, LANES), lambda b, r, c: (b, c, r, 0, 0))
    stat_shape = jax.ShapeDtypeStruct((bsz, tiles, dil, blk, LANES), F32)
    o, m, l = pl.pallas_call(
        _attn_kernel,
        grid=(bsz, dil // res, tiles // run),
        in_specs=[cur(0), prev(1), cur(1), prev(2), cur(2)],
        out_specs=[pl.BlockSpec((None, run, res, blk, gd), lambda b, r, c: (b, c, r, 0, 0)),
                   stat_spec, stat_spec],
        out_shape=[jax.ShapeDtypeStruct((bsz, tiles, dil, blk, gd), F32), stat_shape, stat_shape],
        compiler_params=_params(3),
        name=f"dilated_attn_d{dil}",
    )(view, view, view, view, view)
    return (o.reshape(bsz, nb, blk, gd), m.reshape(bsz, nb, blk, LANES),
            l.reshape(bsz, nb, blk, LANES))


def _to_sequence(src_ref, lane0, seq_ref, slab, tmp_ref, dil):
    per = src_ref.shape[1]
    src = lambda r: src_ref[r, :, lane0:lane0 + LANES]
    if dil <= PASS_STRIDE:
        for r in range(dil):
            seq_ref[slab, pl.ds(r, per, stride=dil), :] = src(r)
        return
    n_hi = dil // PASS_STRIDE
    assert n_hi <= PASS_STRIDE and dil % PASS_STRIDE == 0
    part = n_hi * per
    for r in range(dil):
        r_hi, r_lo = divmod(r, PASS_STRIDE)
        tmp_ref[slab, pl.ds(r_lo * part + r_hi, per, stride=n_hi), :] = src(r)
    for r_lo in range(PASS_STRIDE):
        seq_ref[slab, pl.ds(r_lo, part, stride=PASS_STRIDE), :] = tmp_ref[slab, r_lo * part:(r_lo + 1) * part, :]


def _merge_kernel(x_ref, mod_ref, h_ref, ya_ref, zs_ref,
                  o0_ref, o1_ref, o2_ref, m0_ref, m1_ref, m2_ref, l0_ref, l1_ref, l2_ref,
                  wg_ref, pa_ref, pb_ref, wo_ref, out_ref, seq_ref, tmp_ref):
    d = x_ref.shape[-1]
    n_tiles = GROUP_DIM // LANES
    h = h_ref[...]
    g_a = jax.nn.sigmoid(jnp.dot(h, wg_ref[:, :d], preferred_element_type=F32))
    g_b = jax.nn.sigmoid(jnp.dot(h, wg_ref[:, d:], preferred_element_type=F32))

    groups = ((o0_ref, m0_ref, l0_ref), (o1_ref, m1_ref, l1_ref), (o2_ref, m2_ref, l2_ref))
    for g in range(1, N_GROUPS):
        o_ref, m_ref, l_ref = groups[g]
        sources = [(o_ref, p * LANES) for p in range(n_tiles)] + [(m_ref, 0), (l_ref, 0)]
        for slab, (src_ref, lane0) in enumerate(sources):
            _to_sequence(src_ref, lane0, seq_ref.at[g - 1], slab, tmp_ref, DILATIONS[g])

    def out_tile(g, p):
        return groups[0][0][:, p * LANES:(p + 1) * LANES] if g == 0 else seq_ref[g - 1, p]

    def stat_tile(g, which):
        return groups[0][1 + which][...] if g == 0 else seq_ref[g - 1, n_tiles + which]

    maxes = [stat_tile(g, 0) for g in range(N_GROUPS)]
    sums = [stat_tile(g, 1) for g in range(N_GROUPS)]
    top = jnp.maximum(jnp.maximum(maxes[0], maxes[1]), maxes[2])
    wts = [jnp.exp2(t - top) for t in maxes]
    den = wts[0] * sums[0] + wts[1] * sums[1] + wts[2] * sums[2]
    wns = [w / den for w in wts]
    lane_lo = lax.broadcasted_iota(jnp.int32, (h.shape[0], LANES), 1) < HEAD_DIM
    tiles = []
    for p in range(n_tiles):
        attn = None
        for g, wn in enumerate(wns):
            wide = jnp.where(lane_lo, wn[:, _stat_lane(2 * p):_stat_lane(2 * p) + 1],
                             wn[:, _stat_lane(2 * p + 1):_stat_lane(2 * p + 1) + 1])
            term = wide * out_tile(g, p)
            attn = term if attn is None else attn + term
        tiles.append(attn * zs_ref[:, p * LANES:(p + 1) * LANES])
    y_b = jnp.concatenate(tiles, axis=-1).astype(BF16)

    merged = (g_a * jnp.dot(ya_ref[...], pa_ref[...], preferred_element_type=F32)
              + g_b * jnp.dot(y_b, pb_ref[...], preferred_element_type=F32))
    upd = jnp.dot(merged.astype(BF16), wo_ref[...], preferred_element_type=F32)
    out_ref[...] = x_ref[...] + mod_ref[2:3, :] * upd


def _merge(x, mod3, h, y_a, zs, outs, maxes, sums, w_gate, p_a, p_b, w_o):
    bsz, s, d = x.shape
    tm = OUT_ROW_TILE
    row_d = pl.BlockSpec((None, tm, d), lambda b, i: (b, i, 0))
    row_g = pl.BlockSpec((None, tm, GROUP_DIM), lambda b, i: (b, i, 0))

    def dilated(t, dil):
        width = t.shape[-1]
        tile = Q_BLOCK * dil
        per = tm // dil
        if tile % tm:
            raise NotImplementedError("a dilated-order tile must be a whole number of merge tiles")
        steps = tile // tm
        view = t.reshape(bsz, s // tile, dil, Q_BLOCK, width)
        spec = pl.BlockSpec((None, None, dil, per, width),
                            lambda b, i: (b, i // steps, 0, i % steps, 0))
        return view, spec

    views, specs = [], []
    for group in (outs, maxes, sums):
        for t, dil in zip(group, DILATIONS):
            width = t.shape[-1]
            if dil == 1:
                views.append(t.reshape(bsz, s, width))
                specs.append(pl.BlockSpec((None, tm, width), lambda b, i: (b, i, 0)))
            else:
                v, sp = dilated(t, dil)
                views.append(v)
                specs.append(sp)

    return pl.pallas_call(
        _merge_kernel,
        grid=(bsz, s // tm),
        in_specs=[row_d, pl.BlockSpec((None, 3, d), lambda b, i: (b, 0, 0)), row_d, row_d, row_g,
                  *specs,
                  _resident(w_gate), _resident(p_a), _resident(p_b), _resident(w_o)],
        out_specs=row_d,
        out_shape=jax.ShapeDtypeStruct((bsz, s, d), F32),
        scratch_shapes=[pltpu.VMEM((N_GROUPS - 1, GROUP_DIM // LANES + 2, tm, LANES), F32),
                        pltpu.VMEM((GROUP_DIM // LANES + 2, tm, LANES), F32)],
        compiler_params=_params(2),
        name="merge_out",
    )(x, mod3, h, y_a, zs, *views, w_gate, p_a, p_b, w_o)


def _layer(x, c, layer, w_ada, b_ada, norm_w, w_in, conv_w, q_norm_w, k_norm_w, w_br_conv, w_br_attn, w_out):
    bsz, s, d = x.shape
    conv_dim = conv_w.shape[-1]
    attn_dim = N_GROUPS * GROUP_DIM
    assert conv_dim == d and conv_w.shape[0] == CONV_WIDTH and d % LANES == 0 and d % STAGE_ROWS == 0
    assert w_in.shape[2] == 4 * conv_dim + 3 * attn_dim + GROUP_DIM + 2 * d
    assert s % (max(DILATIONS) * Q_BLOCK) == 0 and s % ROW_TILE == 0 and s % MIXER_ROW_TILE == 0
    assert all(w // dil == Q_BLOCK for w, dil in zip(WINDOWS, DILATIONS))

    mod3 = _modulation(c, w_ada, b_ada)
    nw = norm_w.reshape(1, d)

    y_a, h, w_attn, w_gate, p_a, p_b, w_o = _mixer_a(x, mod3, nw, w_in, layer, conv_w,
                                                     w_br_conv, w_br_attn, w_out)
    qkv, zs = _attn_proj(h, w_attn, q_norm_w, k_norm_w)
    outs, maxes, sums = zip(*[_dilated_attention(t, dil) for t, dil in zip(qkv, DILATIONS)])
    return _merge(x, mod3, h, y_a, zs, outs, maxes, sums, w_gate, p_a, p_b, w_o)


@jax.jit
def kernel(x, c, w_ada, b_ada, norm_w, w_in, conv_w, q_norm_w, k_norm_w, w_br_conv, w_br_attn, w_out):
    depth = w_ada.shape[0]
    for l in range(depth):
        x = _layer(x, c, l, w_ada[l], b_ada[l], norm_w[l], w_in, conv_w[l], q_norm_w[l],
                   k_norm_w[l], w_br_conv, w_br_attn, w_out)
    return x
```

```python
import functools

import jax
import jax.numpy as jnp
from jax import lax
from jax.experimental import pallas as pl
from jax.experimental.pallas import tpu as pltpu

F32 = jnp.float32
BF16 = jnp.bfloat16

HEAD_DIM = 64
ATTN_SLOTS = 8
WINDOWS = (128, 512, 2048)
DILATIONS = (1, 4, 16)
N_GROUPS = len(WINDOWS)
GROUP_DIM = ATTN_SLOTS * HEAD_DIM
Q_BLOCK = 128
CONV_WIDTH = 3
EPS = 1e-6
NEG_INF = -1e30
LOG2_E = 1.4426950408889634
LANES = 128
SUBLANES = 8
LSE_LANES = LANES // ATTN_SLOTS
PASS_STRIDE = 4
VMEM_LIMIT = 56 * 1024 * 1024

MIXER_ROW_TILE = 1024
STAGE_ROWS = 128
STAGE_BYTES_IN_FLIGHT = 6 * 1024 * 1024
ROW_TILE = 512
OUT_ROW_TILE = 512
ATTN_STEP_BLOCKS = 16


def _params(n_axes):
    return pltpu.CompilerParams(dimension_semantics=("arbitrary",) * n_axes,
                                vmem_limit_bytes=VMEM_LIMIT)


def _silu(t):
    return t * jax.nn.sigmoid(t)


def _modulated_norm(x, norm_w, mod):
    ms = jnp.mean(x * x, axis=-1, keepdims=True)
    xn = x * lax.rsqrt(ms + EPS) * norm_w
    return xn * (1.0 + mod[1:2, :]) + mod[0:1, :]


def _to_dilated(slab_ref, tmp_ref, n_rows, dil):
    per = n_rows // dil
    n_slabs = slab_ref.shape[0]
    if dil <= PASS_STRIDE:
        piece = lambda j, r: slab_ref[j, pl.ds(r, per, stride=dil), :]
    else:
        n_hi = dil // PASS_STRIDE
        assert n_hi <= PASS_STRIDE and dil % PASS_STRIDE == 0
        part = n_hi * per
        for j in range(n_slabs):
            for r_lo in range(PASS_STRIDE):
                tmp_ref[j, r_lo * part:(r_lo + 1) * part, :] = slab_ref[j, pl.ds(r_lo, part, stride=PASS_STRIDE), :]
        piece = lambda j, r: tmp_ref[j, pl.ds((r % PASS_STRIDE) * part + r // PASS_STRIDE, per, stride=n_hi), :]
    return jnp.concatenate(
        [jnp.concatenate([piece(j, r) for r in range(dil)], axis=0) for j in range(n_slabs)], axis=-1)


def _fetch_bf16(pairs, stage_ref, sem):
    slots, rows, _ = stage_ref.shape
    chunks = [(src, dst, r0) for src, dst in pairs for r0 in range(0, dst.shape[0], rows)]

    def copy(i):
        src, dst, r0 = chunks[i]
        return pltpu.make_async_copy(src.at[pl.ds(r0, rows), :],
                                     stage_ref.at[i % slots, :, pl.ds(0, dst.shape[1])],
                                     sem.at[i % slots])

    for i in range(min(slots - 1, len(chunks))):
        copy(i).start()
    for i, (_, dst, r0) in enumerate(chunks):
        if i + slots - 1 < len(chunks):
            copy(i + slots - 1).start()
        copy(i).wait()
        dst[r0:r0 + rows, :] = stage_ref[i % slots, :, 0:dst.shape[1]].astype(dst.dtype)


def _first_step():
    return (pl.program_id(0) == 0) & (pl.program_id(1) == 0)


_HBM = pl.BlockSpec(memory_space=pl.ANY)


def _resident(a):
    return pl.BlockSpec(a.shape, lambda *_: (0,) * a.ndim, pipeline_mode=pl.Buffered(1))


def _stage_scratch(width):
    chunk_bytes = STAGE_ROWS * width * 4
    slots = -(-STAGE_BYTES_IN_FLIGHT // chunk_bytes) + 1
    return [pltpu.VMEM((slots, STAGE_ROWS, width), F32), pltpu.SemaphoreType.DMA((slots,))]


def _mod_kernel(c_ref, w_ref, b_ref, o_ref):
    d = c_ref.shape[-1]
    sc = _silu(c_ref[...])
    for t in range(o_ref.shape[1]):
        cols = slice(t * d, (t + 1) * d)
        o_ref[:, t, :] = jnp.dot(sc, w_ref[:, cols], preferred_element_type=F32) + b_ref[:, cols]


def _modulation(c, w_ada, b_ada):
    bsz, d = c.shape
    n = w_ada.shape[1]
    whole = lambda shape: pl.BlockSpec(shape, lambda j: (0,) * len(shape), pipeline_mode=pl.Buffered(1))
    return pl.pallas_call(
        _mod_kernel,
        grid=(1,),
        in_specs=[whole((bsz, d)), whole((d, n)), whole((1, n))],
        out_specs=pl.BlockSpec((bsz, n // d, d), lambda j: (0, 0, 0)),
        out_shape=jax.ShapeDtypeStruct((bsz, n // d, d), F32),
        compiler_params=_params(1),
        name="adaln_mod",
    )(c, w_ada, b_ada.reshape(1, n))


def _mixer_a_kernel(x_ref, mod_ref, nw_ref, w_hbm, cw_ref, win_rows_ref, pa_rows_ref, pb_rows_ref,
                    wo_rows_ref, y_ref, h_ref, wattn_ref, wgate_ref, pa_ref, pb_ref, wo_ref,
                    u_ref, w_ref, stage_ref, sem, *, layer):
    tm, d = x_ref.shape
    mod = mod_ref[...]
    nw = nw_ref[...]
    cw = cw_ref[...]

    @pl.when(_first_step())
    def _():
        _fetch_bf16([(w_hbm.at[layer, :, pl.ds(0, w_ref.shape[1])], w_ref)], stage_ref, sem)

    @pl.when(pl.program_id(1) == 0)
    def _():
        u_ref[0:SUBLANES, :] = jnp.zeros((SUBLANES, d), F32)

    n_attn = wattn_ref.shape[1]
    wattn_ref[...] = win_rows_ref[:, 4 * d:4 * d + n_attn].astype(wattn_ref.dtype)
    wgate_ref[...] = win_rows_ref[:, 4 * d + n_attn:].astype(wgate_ref.dtype)
    pa_ref[...] = pa_rows_ref[...].astype(pa_ref.dtype)
    pb_ref[...] = pb_rows_ref[...].astype(pb_ref.dtype)
    wo_ref[...] = wo_rows_ref[...].astype(wo_ref.dtype)

    h = _modulated_norm(x_ref[...], nw, mod).astype(BF16)
    h_ref[...] = h
    proj = lambda j: jnp.dot(h, w_ref[:, j * d:(j + 1) * d], preferred_element_type=F32)
    z_a, c_a, x_a = proj(3), proj(1), proj(2)
    u = c_a * x_a
    u_ref[SUBLANES:, :] = u
    u_m1 = u_ref[pl.ds(SUBLANES - 1, tm), :]
    u_m2 = u_ref[pl.ds(SUBLANES - 2, tm), :]
    conv = cw[0:1, :] * u_m2 + cw[1:2, :] * u_m1 + cw[2:3, :] * u
    gated = conv * _silu(z_a)
    y_ref[...] = (proj(0) * gated).astype(y_ref.dtype)
    u_ref[0:SUBLANES, :] = u[tm - SUBLANES:, :]


def _mixer_a(x, mod3, norm_w, w_in, layer, conv_w, p_a, p_b, w_o):
    bsz, s, d = x.shape
    tm = MIXER_ROW_TILE
    tiles = s // tm
    n_steps = bsz * tiles
    n_attn = (3 * N_GROUPS + 1) * GROUP_DIM
    rows = pl.BlockSpec((None, tm, d), lambda b, i: (b, i, 0))

    def slab(a, layered):
        k, n = a.shape[-2:]
        assert k % (n_steps * 2 * SUBLANES) == 0
        step = lambda b, i: b * tiles + i
        if layered:
            return pl.BlockSpec((None, k // n_steps, n), lambda b, i: (layer, step(b, i), 0))
        return pl.BlockSpec((k // n_steps, n), lambda b, i: (step(b, i), 0))

    bf16 = lambda k, n: jax.ShapeDtypeStruct((k, n), BF16)
    out_shapes = [jax.ShapeDtypeStruct((bsz, s, d), BF16)] * 2 + [
        bf16(d, n_attn), bf16(d, 2 * d), bf16(*p_a.shape[1:]), bf16(*p_b.shape[1:]), bf16(*w_o.shape[1:])]
    return pl.pallas_call(
        functools.partial(_mixer_a_kernel, layer=layer),
        grid=(bsz, tiles),
        in_specs=[rows,
                  pl.BlockSpec((None, 3, d), lambda b, i: (b, 0, 0)),
                  pl.BlockSpec((1, d), lambda b, i: (0, 0)),
                  _HBM,
                  pl.BlockSpec(conv_w.shape, lambda b, i: (0, 0)),
                  slab(w_in, True), slab(p_a, True), slab(p_b, True), slab(w_o, True)],
        out_specs=[rows, rows] + [slab(t, False) for t in out_shapes[2:]],
        out_shape=out_shapes,
        scratch_shapes=[pltpu.VMEM((SUBLANES + tm, d), F32),
                        pltpu.VMEM((d, 4 * d), BF16),
                        *_stage_scratch(4 * d)],
        compiler_params=_params(2),
        name="mixer_a",
    )(x, mod3, norm_w, w_in, conv_w, w_in, p_a, p_b, w_o)


def _head_rms(t, w):
    sq = t * t
    lane_lo = lax.broadcasted_iota(jnp.int32, (t.shape[0], LANES), 1) < HEAD_DIM
    parts = []
    for j in range(0, t.shape[-1], LANES):
        tile = sq[:, j:j + LANES]
        even = jnp.sum(jnp.where(lane_lo, tile, 0.0), axis=-1, keepdims=True)
        odd = jnp.sum(jnp.where(lane_lo, 0.0, tile), axis=-1, keepdims=True)
        parts.append(jnp.where(lane_lo, even, odd))
    ssq = jnp.concatenate(parts, axis=-1)
    return t * lax.rsqrt(ssq * (1.0 / HEAD_DIM) + EPS) * w


def _attn_proj_kernel(h_ref, w_ref, qw_ref, kw_ref, qkv0_ref, qkv1_ref, qkv2_ref, zs_ref, slab_ref,
                      tmp_ref):
    tm, d = h_ref.shape
    h = h_ref[...]
    for j in range(d // LANES):
        slab_ref[j] = h[:, j * LANES:(j + 1) * LANES].astype(F32)
    ad = N_GROUPS * GROUP_DIM
    for g, out_ref in enumerate((qkv0_ref, qkv1_ref, qkv2_ref)):
        hg = h if DILATIONS[g] == 1 else _to_dilated(slab_ref, tmp_ref, tm, DILATIONS[g]).astype(BF16)
        q, k, v = [jnp.dot(hg, w_ref[:, part * ad + g * GROUP_DIM:part * ad + (g + 1) * GROUP_DIM],
                           preferred_element_type=F32) for part in range(3)]
        qkv = jnp.concatenate([_head_rms(q, qw_ref[...]), _head_rms(k, kw_ref[...]), v], axis=-1)
        out_ref[...] = qkv.astype(out_ref.dtype).reshape(out_ref.shape)
        if g == 0:
            z = jnp.dot(hg, w_ref[:, 3 * ad:], preferred_element_type=F32)
            zs_ref[...] = _silu(z)


def _attn_proj(h, w_attn, q_norm_w, k_norm_w):
    bsz, s, d = h.shape
    tm = ROW_TILE
    qw = jnp.tile(q_norm_w * (HEAD_DIM ** -0.5 * LOG2_E), ATTN_SLOTS).reshape(1, GROUP_DIM)
    kw = jnp.tile(k_norm_w, ATTN_SLOTS).reshape(1, GROUP_DIM)
    gw = 3 * GROUP_DIM
    out_shapes, out_specs = [], []
    for dil in DILATIONS:
        tile = Q_BLOCK * dil
        assert dil == 1 or tile % tm == 0
        if tile <= tm:
            nblk = tm // Q_BLOCK
            out_shapes.append(jax.ShapeDtypeStruct((bsz, s // Q_BLOCK, Q_BLOCK, gw), BF16))
            out_specs.append(pl.BlockSpec((None, nblk, Q_BLOCK, gw), lambda b, i: (b, i, 0, 0)))
        else:
            per = tm // dil
            steps = tile // tm
            out_shapes.append(jax.ShapeDtypeStruct((bsz, s // tile, dil, Q_BLOCK, gw), BF16))
            out_specs.append(pl.BlockSpec((None, None, dil, per, gw),
                                          lambda b, i, steps=steps: (b, i // steps, 0, i % steps, 0)))
    out_shapes.append(jax.ShapeDtypeStruct((bsz, s, GROUP_DIM), F32))
    out_specs.append(pl.BlockSpec((None, tm, GROUP_DIM), lambda b, i: (b, i, 0)))
    outs = pl.pallas_call(
        _attn_proj_kernel,
        grid=(bsz, s // tm),
        in_specs=[pl.BlockSpec((None, tm, d), lambda b, i: (b, i, 0)),
                  _resident(w_attn),
                  pl.BlockSpec((1, GROUP_DIM), lambda b, i: (0, 0)),
                  pl.BlockSpec((1, GROUP_DIM), lambda b, i: (0, 0))],
        out_specs=out_specs,
        out_shape=out_shapes,
        scratch_shapes=[pltpu.VMEM((d // LANES, tm, LANES), F32)] * 2,
        compiler_params=_params(2),
        name="attn_proj",
    )(h, w_attn, qw, kw)
    qkv = [t.reshape(bsz, s // Q_BLOCK, Q_BLOCK, gw) for t in outs[:N_GROUPS]]
    return qkv, outs[N_GROUPS]


def _stat_lane(head):
    return head * LSE_LANES


def _attn_block(q, kp, kc, vp, vc, bias, o_ref, m_ref, l_ref, j):
    blk = q.shape[0]
    lane = lax.broadcasted_iota(jnp.int32, (blk, LANES), 1)
    lane_lo = lane < HEAD_DIM
    zero = jnp.zeros((blk, LANES), q.dtype)
    one = jnp.ones((2 * blk, LANES), q.dtype)
    for p in range(GROUP_DIM // LANES):
        sl = slice(p * LANES, (p + 1) * LANES)
        q2 = q[:, sl]
        qs = jnp.concatenate([jnp.where(lane_lo, q2, zero), jnp.where(lane_lo, zero, q2)], axis=0)
        k2 = jnp.concatenate([kp[:, sl], kc[:, sl]], axis=0)
        v2 = jnp.concatenate([vp[:, sl], vc[:, sl]], axis=0)
        s = lax.dot_general(qs, k2, (((1,), (1,)), ((), ())), preferred_element_type=F32) + bias
        m = jnp.max(s, axis=-1, keepdims=True)
        e = jnp.exp2(s - m).astype(BF16)
        res = jnp.dot(e, jnp.concatenate([v2, one], axis=-1), preferred_element_type=F32)
        o_ref[j + (slice(None), sl)] = jnp.where(lane_lo, res[:blk, :LANES], res[blk:, :LANES])
        for head, rows in ((2 * p, slice(0, blk)), (2 * p + 1, slice(blk, 2 * blk))):
            slot = slice(_stat_lane(head), _stat_lane(head) + LSE_LANES)
            m_ref[j + (slice(None), slot)] = jnp.broadcast_to(m[rows], (blk, LSE_LANES))
            l_ref[j + (slice(None), slot)] = res[rows, LANES + slot.start:LANES + slot.stop]


def _attn_kernel(q_ref, kprev_ref, k_ref, vprev_ref, v_ref, o_ref, m_ref, l_ref):
    n_blocks, n_res, blk, _ = q_ref.shape
    span = 2 * blk
    row = lax.broadcasted_iota(jnp.int32, (span, span), 0) % blk
    col = lax.broadcasted_iota(jnp.int32, (span, span), 1)
    in_window = (col >= row) & (col <= row + blk)
    bias = jnp.where(in_window, 0.0, NEG_INF)
    run_has_prev = pl.program_id(2) > 0
    bias_first = jnp.where(in_window & ((col >= blk) | run_has_prev), 0.0, NEG_INF)

    for r in range(n_res):
        _attn_block(q_ref[0, r], kprev_ref[r], k_ref[0, r], vprev_ref[r], v_ref[0, r], bias_first,
                    o_ref, m_ref, l_ref, (0, r))
        for j in range(1, n_blocks):
            _attn_block(q_ref[j, r], k_ref[j - 1, r], k_ref[j, r], v_ref[j - 1, r], v_ref[j, r], bias,
                        o_ref, m_ref, l_ref, (j, r))


def _dilated_attention(qkv, dil):
    bsz, nb, blk, width = qkv.shape
    gd = GROUP_DIM
    tiles = nb // dil
    run = min(ATTN_STEP_BLOCKS, tiles)
    res = min(ATTN_STEP_BLOCKS // run, dil)
    assert tiles % run == 0 and dil % res == 0
    view = qkv.reshape(bsz, tiles, dil, blk, width)
    cur = lambda part: pl.BlockSpec((None, run, res, blk, gd), lambda b, r, c: (b, c, r, 0, part))
    prev = lambda part: pl.BlockSpec((None, None, res, blk, gd),
                                     lambda b, r, c: (b, jnp.maximum(c * run - 1, 0), r, 0, part))
    stat_spec = pl.BlockSpec((None, run, res, blk, LANES), lambda b, r, c: (b, c, r, 0, 0))
    stat_shape = jax.ShapeDtypeStruct((bsz, tiles, dil, blk, LANES), F32)
    o, m, l = pl.pallas_call(
        _attn_kernel,
        grid=(bsz, dil // res, tiles // run),
        in_specs=[cur(0), prev(1), cur(1), prev(2), cur(2)],
        out_specs=[pl.BlockSpec((None, run, res, blk, gd), lambda b, r, c: (b, c, r, 0, 0)),
                   stat_spec, stat_spec],
        out_shape=[jax.ShapeDtypeStruct((bsz, tiles, dil, blk, gd), F32), stat_shape, stat_shape],
        compiler_params=_params(3),
        name=f"dilated_attn_d{dil}",
    )(view, view, view, view, view)
    return (o.reshape(bsz, nb, blk, gd), m.reshape(bsz, nb, blk, LANES),
            l.reshape(bsz, nb, blk, LANES))


def _to_sequence(src_ref, lane0, seq_ref, slab, tmp_ref, dil):
    per = src_ref.shape[1]
    src = lambda r: src_ref[r, :, lane0:lane0 + LANES]
    if dil <= PASS_STRIDE:
        for r in range(dil):
            seq_ref[slab, pl.ds(r, per, stride=dil), :] = src(r)
        return
    n_hi = dil // PASS_STRIDE
    assert n_hi <= PASS_STRIDE and dil % PASS_STRIDE == 0
    part = n_hi * per
    for r in range(dil):
        r_hi, r_lo = divmod(r, PASS_STRIDE)
        tmp_ref[slab, pl.ds(r_lo * part + r_hi, per, stride=n_hi), :] = src(r)
    for r_lo in range(PASS_STRIDE):
        seq_ref[slab, pl.ds(r_lo, part, stride=PASS_STRIDE), :] = tmp_ref[slab, r_lo * part:(r_lo + 1) * part, :]


def _merge_kernel(x_ref, mod_ref, h_ref, ya_ref, zs_ref,
                  o0_ref, o1_ref, o2_ref, m0_ref, m1_ref, m2_ref, l0_ref, l1_ref, l2_ref,
                  wg_ref, pa_ref, pb_ref, wo_ref, out_ref, seq_ref, tmp_ref):
    d = x_ref.shape[-1]
    n_tiles = GROUP_DIM // LANES
    h = h_ref[...]
    g_a = jax.nn.sigmoid(jnp.dot(h, wg_ref[:, :d], preferred_element_type=F32))
    g_b = jax.nn.sigmoid(jnp.dot(h, wg_ref[:, d:], preferred_element_type=F32))

    groups = ((o0_ref, m0_ref, l0_ref), (o1_ref, m1_ref, l1_ref), (o2_ref, m2_ref, l2_ref))
    for g in range(1, N_GROUPS):
        o_ref, m_ref, l_ref = groups[g]
        sources = [(o_ref, p * LANES) for p in range(n_tiles)] + [(m_ref, 0), (l_ref, 0)]
        for slab, (src_ref, lane0) in enumerate(sources):
            _to_sequence(src_ref, lane0, seq_ref.at[g - 1], slab, tmp_ref, DILATIONS[g])

    def out_tile(g, p):
        return groups[0][0][:, p * LANES:(p + 1) * LANES] if g == 0 else seq_ref[g - 1, p]

    def stat_tile(g, which):
        return groups[0][1 + which][...] if g == 0 else seq_ref[g - 1, n_tiles + which]

    maxes = [stat_tile(g, 0) for g in range(N_GROUPS)]
    sums = [stat_tile(g, 1) for g in range(N_GROUPS)]
    top = jnp.maximum(jnp.maximum(maxes[0], maxes[1]), maxes[2])
    wts = [jnp.exp2(t - top) for t in maxes]
    den = wts[0] * sums[0] + wts[1] * sums[1] + wts[2] * sums[2]
    wns = [w / den for w in wts]
    lane_lo = lax.broadcasted_iota(jnp.int32, (h.shape[0], LANES), 1) < HEAD_DIM
    tiles = []
    for p in range(n_tiles):
        attn = None
        for g, wn in enumerate(wns):
            wide = jnp.where(lane_lo, wn[:, _stat_lane(2 * p):_stat_lane(2 * p) + 1],
                             wn[:, _stat_lane(2 * p + 1):_stat_lane(2 * p + 1) + 1])
            term = wide * out_tile(g, p)
            attn = term if attn is None else attn + term
        tiles.append(attn * zs_ref[:, p * LANES:(p + 1) * LANES])
    y_b = jnp.concatenate(tiles, axis=-1).astype(BF16)

    merged = (g_a * jnp.dot(ya_ref[...], pa_ref[...], preferred_element_type=F32)
              + g_b * jnp.dot(y_b, pb_ref[...], preferred_element_type=F32))
    upd = jnp.dot(merged.astype(BF16), wo_ref[...], preferred_element_type=F32)
    out_ref[...] = x_ref[...] + mod_ref[2:3, :] * upd


def _merge(x, mod3, h, y_a, zs, outs, maxes, sums, w_gate, p_a, p_b, w_o):
    bsz, s, d = x.shape
    tm = OUT_ROW_TILE
    row_d = pl.BlockSpec((None, tm, d), lambda b, i: (b, i, 0))
    row_g = pl.BlockSpec((None, tm, GROUP_DIM), lambda b, i: (b, i, 0))

    def dilated(t, dil):
        width = t.shape[-1]
        tile = Q_BLOCK * dil
        per = tm // dil
        if tile % tm:
            raise NotImplementedError("a dilated-order tile must be a whole number of merge tiles")
        steps = tile // tm
        view = t.reshape(bsz, s // tile, dil, Q_BLOCK, width)
        spec = pl.BlockSpec((None, None, dil, per, width),
                            lambda b, i: (b, i // steps, 0, i % steps, 0))
        return view, spec

    views, specs = [], []
    for group in (outs, maxes, sums):
        for t, dil in zip(group, DILATIONS):
            width = t.shape[-1]
            if dil == 1:
                views.append(t.reshape(bsz, s, width))
                specs.append(pl.BlockSpec((None, tm, width), lambda b, i: (b, i, 0)))
            else:
                v, sp = dilated(t, dil)
                views.append(v)
                specs.append(sp)

    return pl.pallas_call(
        _merge_kernel,
        grid=(bsz, s // tm),
        in_specs=[row_d, pl.BlockSpec((None, 3, d), lambda b, i: (b, 0, 0)), row_d, row_d, row_g,
                  *specs,
                  _resident(w_gate), _resident(p_a), _resident(p_b), _resident(w_o)],
        out_specs=row_d,
        out_shape=jax.ShapeDtypeStruct((bsz, s, d), F32),
        scratch_shapes=[pltpu.VMEM((N_GROUPS - 1, GROUP_DIM // LANES + 2, tm, LANES), F32),
                        pltpu.VMEM((GROUP_DIM // LANES + 2, tm, LANES), F32)],
        compiler_params=_params(2),
        name="merge_out",
    )(x, mod3, h, y_a, zs, *views, w_gate, p_a, p_b, w_o)


def _layer(x, c, layer, w_ada, b_ada, norm_w, w_in, conv_w, q_norm_w, k_norm_w, w_br_conv, w_br_attn, w_out):
    bsz, s, d = x.shape
    conv_dim = conv_w.shape[-1]
    attn_dim = N_GROUPS * GROUP_DIM
    assert conv_dim == d and conv_w.shape[0] == CONV_WIDTH and d % LANES == 0 and d % STAGE_ROWS == 0
    assert w_in.shape[2] == 4 * conv_dim + 3 * attn_dim + GROUP_DIM + 2 * d
    assert s % (max(DILATIONS) * Q_BLOCK) == 0 and s % ROW_TILE == 0 and s % MIXER_ROW_TILE == 0
    assert all(w // dil == Q_BLOCK for w, dil in zip(WINDOWS, DILATIONS))

    mod3 = _modulation(c, w_ada, b_ada)
    nw = norm_w.reshape(1, d)

    y_a, h, w_attn, w_gate, p_a, p_b, w_o = _mixer_a(x, mod3, nw, w_in, layer, conv_w,
                                                     w_br_conv, w_br_attn, w_out)
    qkv, zs = _attn_proj(h, w_attn, q_norm_w, k_norm_w)
    outs, maxes, sums = zip(*[_dilated_attention(t, dil) for t, dil in zip(qkv, DILATIONS)])
    return _merge(x, mod3, h, y_a, zs, outs, maxes, sums, w_gate, p_a, p_b, w_o)


@jax.jit
def kernel(x, c, w_ada, b_ada, norm_w, w_in, conv_w, q_norm_w, k_norm_w, w_br_conv, w_br_attn, w_out):
    depth = w_ada.shape[0]
    for l in range(depth):
        x = _layer(x, c, l, w_ada[l], b_ada[l], norm_w[l], w_in, conv_w[l], q_norm_w[l],
                   k_norm_w[l], w_br_conv, w_br_attn, w_out)
    return x
```

```python
import functools

import jax
import jax.numpy as jnp
from jax import lax
from jax.experimental import pallas as pl
from jax.experimental.pallas import tpu as pltpu

F32 = jnp.float32
BF16 = jnp.bfloat16

HEAD_DIM = 64
ATTN_SLOTS = 8
WINDOWS = (128, 512, 2048)
DILATIONS = (1, 4, 16)
N_GROUPS = len(WINDOWS)
GROUP_DIM = ATTN_SLOTS * HEAD_DIM
Q_BLOCK = 128
CONV_WIDTH = 3
EPS = 1e-6
NEG_INF = -1e30
LOG2_E = 1.4426950408889634
LANES = 128
SUBLANES = 8
LSE_LANES = LANES // ATTN_SLOTS
PASS_STRIDE = 4
VMEM_LIMIT = 56 * 1024 * 1024

MIXER_ROW_TILE = 1024
STAGE_ROWS = 128
STAGE_BYTES_IN_FLIGHT = 6 * 1024 * 1024
ROW_TILE = 512
OUT_ROW_TILE = 512
ATTN_STEP_BLOCKS = 16


def _params(n_axes):
    return pltpu.CompilerParams(dimension_semantics=("arbitrary",) * n_axes,
                                vmem_limit_bytes=VMEM_LIMIT)


def _silu(t):
    return t * jax.nn.sigmoid(t)


def _modulated_norm(x, norm_w, mod):
    ms = jnp.mean(x * x, axis=-1, keepdims=True)
    xn = x * lax.rsqrt(ms + EPS) * norm_w
    return xn * (1.0 + mod[1:2, :]) + mod[0:1, :]


def _to_dilated(slab_ref, tmp_ref, n_rows, dil):
    per = n_rows // dil
    n_slabs = slab_ref.shape[0]
    if dil <= PASS_STRIDE:
        piece = lambda j, r: slab_ref[j, pl.ds(r, per, stride=dil), :]
    else:
        n_hi = dil // PASS_STRIDE
        assert n_hi <= PASS_STRIDE and dil % PASS_STRIDE == 0
        part = n_hi * per
        for j in range(n_slabs):
            for r_lo in range(PASS_STRIDE):
                tmp_ref[j, r_lo * part:(r_lo + 1) * part, :] = slab_ref[j, pl.ds(r_lo, part, stride=PASS_STRIDE), :]
        piece = lambda j, r: tmp_ref[j, pl.ds((r % PASS_STRIDE) * part + r // PASS_STRIDE, per, stride=n_hi), :]
    return jnp.concatenate(
        [jnp.concatenate([piece(j, r) for r in range(dil)], axis=0) for j in range(n_slabs)], axis=-1)


def _fetch_bf16(pairs, stage_ref, sem):
    slots, rows, _ = stage_ref.shape
    chunks = [(src, dst, r0) for src, dst in pairs for r0 in range(0, dst.shape[0], rows)]

    def copy(i):
        src, dst, r0 = chunks[i]
        return pltpu.make_async_copy(src.at[pl.ds(r0, rows), :],
                                     stage_ref.at[i % slots, :, pl.ds(0, dst.shape[1])],
                                     sem.at[i % slots])

    for i in range(min(slots - 1, len(chunks))):
        copy(i).start()
    for i, (_, dst, r0) in enumerate(chunks):
        if i + slots - 1 < len(chunks):
            copy(i + slots - 1).start()
        copy(i).wait()
        dst[r0:r0 + rows, :] = stage_ref[i % slots, :, 0:dst.shape[1]].astype(dst.dtype)


def _first_step():
    return (pl.program_id(0) == 0) & (pl.program_id(1) == 0)


_HBM = pl.BlockSpec(memory_space=pl.ANY)


def _resident(a):
    return pl.BlockSpec(a.shape, lambda *_: (0,) * a.ndim, pipeline_mode=pl.Buffered(1))


def _stage_scratch(width):
    chunk_bytes = STAGE_ROWS * width * 4
    slots = -(-STAGE_BYTES_IN_FLIGHT // chunk_bytes) + 1
    return [pltpu.VMEM((slots, STAGE_ROWS, width), F32), pltpu.SemaphoreType.DMA((slots,))]


def _mod_kernel(c_ref, w_ref, b_ref, o_ref):
    d = c_ref.shape[-1]
    sc = _silu(c_ref[...])
    for t in range(o_ref.shape[1]):
        cols = slice(t * d, (t + 1) * d)
        o_ref[:, t, :] = jnp.dot(sc, w_ref[:, cols], preferred_element_type=F32) + b_ref[:, cols]


def _modulation(c, w_ada, b_ada):
    bsz, d = c.shape
    n = w_ada.shape[1]
    whole = lambda shape: pl.BlockSpec(shape, lambda j: (0,) * len(shape), pipeline_mode=pl.Buffered(1))
    return pl.pallas_call(
        _mod_kernel,
        grid=(1,),
        in_specs=[whole((bsz, d)), whole((d, n)), whole((1, n))],
        out_specs=pl.BlockSpec((bsz, n // d, d), lambda j: (0, 0, 0)),
        out_shape=jax.ShapeDtypeStruct((bsz, n // d, d), F32),
        compiler_params=_params(1),
        name="adaln_mod",
    )(c, w_ada, b_ada.reshape(1, n))


def _mixer_a_kernel(x_ref, mod_ref, nw_ref, w_hbm, cw_ref, win_rows_ref, pa_rows_ref, pb_rows_ref,
                    wo_rows_ref, y_ref, h_ref, wattn_ref, wgate_ref, pa_ref, pb_ref, wo_ref,
                    u_ref, w_ref, stage_ref, sem, *, layer):
    tm, d = x_ref.shape
    mod = mod_ref[...]
    nw = nw_ref[...]
    cw = cw_ref[...]

    @pl.when(_first_step())
    def _():
        _fetch_bf16([(w_hbm.at[layer, :, pl.ds(0, w_ref.shape[1])], w_ref)], stage_ref, sem)

    @pl.when(pl.program_id(1) == 0)
    def _():
        u_ref[0:SUBLANES, :] = jnp.zeros((SUBLANES, d), F32)

    n_attn = wattn_ref.shape[1]
    wattn_ref[...] = win_rows_ref[:, 4 * d:4 * d + n_attn].astype(wattn_ref.dtype)
    wgate_ref[...] = win_rows_ref[:, 4 * d + n_attn:].astype(wgate_ref.dtype)
    pa_ref[...] = pa_rows_ref[...].astype(pa_ref.dtype)
    pb_ref[...] = pb_rows_ref[...].astype(pb_ref.dtype)
    wo_ref[...] = wo_rows_ref[...].astype(wo_ref.dtype)

    h = _modulated_norm(x_ref[...], nw, mod).astype(BF16)
    h_ref[...] = h
    proj = lambda j: jnp.dot(h, w_ref[:, j * d:(j + 1) * d], preferred_element_type=F32)
    z_a, c_a, x_a = proj(3), proj(1), proj(2)
    u = c_a * x_a
    u_ref[SUBLANES:, :] = u
    u_m1 = u_ref[pl.ds(SUBLANES - 1, tm), :]
    u_m2 = u_ref[pl.ds(SUBLANES - 2, tm), :]
    conv = cw[0:1, :] * u_m2 + cw[1:2, :] * u_m1 + cw[2:3, :] * u
    gated = conv * _silu(z_a)
    y_ref[...] = (proj(0) * gated).astype(y_ref.dtype)
    u_ref[0:SUBLANES, :] = u[tm - SUBLANES:, :]


def _mixer_a(x, mod3, norm_w, w_in, layer, conv_w, p_a, p_b, w_o):
    bsz, s, d = x.shape
    tm = MIXER_ROW_TILE
    tiles = s // tm
    n_steps = bsz * tiles
    n_attn = (3 * N_GROUPS + 1) * GROUP_DIM
    rows = pl.BlockSpec((None, tm, d), lambda b, i: (b, i, 0))

    def slab(a, layered):
        k, n = a.shape[-2:]
        assert k % (n_steps * 2 * SUBLANES) == 0
        step = lambda b, i: b * tiles + i
        if layered:
            return pl.BlockSpec((None, k // n_steps, n), lambda b, i: (layer, step(b, i), 0))
        return pl.BlockSpec((k // n_steps, n), lambda b, i: (step(b, i), 0))

    bf16 = lambda k, n: jax.ShapeDtypeStruct((k, n), BF16)
    out_shapes = [jax.ShapeDtypeStruct((bsz, s, d), BF16)] * 2 + [
        bf16(d, n_attn), bf16(d, 2 * d), bf16(*p_a.shape[1:]), bf16(*p_b.shape[1:]), bf16(*w_o.shape[1:])]
    return pl.pallas_call(
        functools.partial(_mixer_a_kernel, layer=layer),
        grid=(bsz, tiles),
        in_specs=[rows,
                  pl.BlockSpec((None, 3, d), lambda b, i: (b, 0, 0)),
                  pl.BlockSpec((1, d), lambda b, i: (0, 0)),
                  _HBM,
                  pl.BlockSpec((None,) + conv_w.shape[1:], lambda b, i: (layer, 0, 0)),
                  slab(w_in, True), slab(p_a, True), slab(p_b, True), slab(w_o, True)],
        out_specs=[rows, rows] + [slab(t, False) for t in out_shapes[2:]],
        out_shape=out_shapes,
        scratch_shapes=[pltpu.VMEM((SUBLANES + tm, d), F32),
                        pltpu.VMEM((d, 4 * d), BF16),
                        *_stage_scratch(4 * d)],
        compiler_params=_params(2),
        name="mixer_a",
    )(x, mod3, norm_w, w_in, conv_w, w_in, p_a, p_b, w_o)


def _head_rms(t, w):
    sq = t * t
    lane_lo = lax.broadcasted_iota(jnp.int32, (t.shape[0], LANES), 1) < HEAD_DIM
    parts = []
    for j in range(0, t.shape[-1], LANES):
        tile = sq[:, j:j + LANES]
        even = jnp.sum(jnp.where(lane_lo, tile, 0.0), axis=-1, keepdims=True)
        odd = jnp.sum(jnp.where(lane_lo, 0.0, tile), axis=-1, keepdims=True)
        parts.append(jnp.where(lane_lo, even, odd))
    ssq = jnp.concatenate(parts, axis=-1)
    return t * lax.rsqrt(ssq * (1.0 / HEAD_DIM) + EPS) * w


def _attn_proj_kernel(h_ref, w_ref, qw_ref, kw_ref, qkv0_ref, qkv1_ref, qkv2_ref, zs_ref, slab_ref,
                      tmp_ref):
    tm, d = h_ref.shape
    h = h_ref[...]
    qw = jnp.tile(qw_ref[...] * (HEAD_DIM ** -0.5 * LOG2_E), (1, ATTN_SLOTS))
    kw = jnp.tile(kw_ref[...], (1, ATTN_SLOTS))
    for j in range(d // LANES):
        slab_ref[j] = h[:, j * LANES:(j + 1) * LANES].astype(F32)
    ad = N_GROUPS * GROUP_DIM
    for g, out_ref in enumerate((qkv0_ref, qkv1_ref, qkv2_ref)):
        hg = h if DILATIONS[g] == 1 else _to_dilated(slab_ref, tmp_ref, tm, DILATIONS[g]).astype(BF16)
        q, k, v = [jnp.dot(hg, w_ref[:, part * ad + g * GROUP_DIM:part * ad + (g + 1) * GROUP_DIM],
                           preferred_element_type=F32) for part in range(3)]
        qkv = jnp.concatenate([_head_rms(q, qw), _head_rms(k, kw), v], axis=-1)
        out_ref[...] = qkv.astype(out_ref.dtype).reshape(out_ref.shape)
        if g == 0:
            z = jnp.dot(hg, w_ref[:, 3 * ad:], preferred_element_type=F32)
            zs_ref[...] = _silu(z)


def _attn_proj(h, w_attn, q_norm_w, k_norm_w, layer):
    bsz, s, d = h.shape
    tm = ROW_TILE
    depth = q_norm_w.shape[0]
    head_w = pl.BlockSpec((None, 1, HEAD_DIM), lambda b, i: (layer, 0, 0))
    gw = 3 * GROUP_DIM
    out_shapes, out_specs = [], []
    for dil in DILATIONS:
        tile = Q_BLOCK * dil
        assert dil == 1 or tile % tm == 0
        if tile <= tm:
            nblk = tm // Q_BLOCK
            out_shapes.append(jax.ShapeDtypeStruct((bsz, s // Q_BLOCK, Q_BLOCK, gw), BF16))
            out_specs.append(pl.BlockSpec((None, nblk, Q_BLOCK, gw), lambda b, i: (b, i, 0, 0)))
        else:
            per = tm // dil
            steps = tile // tm
            out_shapes.append(jax.ShapeDtypeStruct((bsz, s // tile, dil, Q_BLOCK, gw), BF16))
            out_specs.append(pl.BlockSpec((None, None, dil, per, gw),
                                          lambda b, i, steps=steps: (b, i // steps, 0, i % steps, 0)))
    out_shapes.append(jax.ShapeDtypeStruct((bsz, s, GROUP_DIM), F32))
    out_specs.append(pl.BlockSpec((None, tm, GROUP_DIM), lambda b, i: (b, i, 0)))
    outs = pl.pallas_call(
        _attn_proj_kernel,
        grid=(bsz, s // tm),
        in_specs=[pl.BlockSpec((None, tm, d), lambda b, i: (b, i, 0)),
                  _resident(w_attn),
                  head_w, head_w],
        out_specs=out_specs,
        out_shape=out_shapes,
        scratch_shapes=[pltpu.VMEM((d // LANES, tm, LANES), F32)] * 2,
        compiler_params=_params(2),
        name="attn_proj",
    )(h, w_attn, q_norm_w.reshape(depth, 1, HEAD_DIM), k_norm_w.reshape(depth, 1, HEAD_DIM))
    qkv = [t.reshape(bsz, s // Q_BLOCK, Q_BLOCK, gw) for t in outs[:N_GROUPS]]
    return qkv, outs[N_GROUPS]


def _stat_lane(head):
    return head * LSE_LANES


def _attn_block(q, kp, kc, vp, vc, bias, o_ref, m_ref, l_ref, j):
    blk = q.shape[0]
    lane = lax.broadcasted_iota(jnp.int32, (blk, LANES), 1)
    lane_lo = lane < HEAD_DIM
    zero = jnp.zeros((blk, LANES), q.dtype)
    one = jnp.ones((2 * blk, LANES), q.dtype)
    for p in range(GROUP_DIM // LANES):
        sl = slice(p * LANES, (p + 1) * LANES)
        q2 = q[:, sl]
        qs = jnp.concatenate([jnp.where(lane_lo, q2, zero), jnp.where(lane_lo, zero, q2)], axis=0)
        k2 = jnp.concatenate([kp[:, sl], kc[:, sl]], axis=0)
        v2 = jnp.concatenate([vp[:, sl], vc[:, sl]], axis=0)
        s = lax.dot_general(qs, k2, (((1,), (1,)), ((), ())), preferred_element_type=F32) + bias
        m = jnp.max(s, axis=-1, keepdims=True)
        e = jnp.exp2(s - m).astype(BF16)
        res = jnp.dot(e, jnp.concatenate([v2, one], axis=-1), preferred_element_type=F32)
        o_ref[j + (slice(None), sl)] = jnp.where(lane_lo, res[:blk, :LANES], res[blk:, :LANES])
        for head, rows in ((2 * p, slice(0, blk)), (2 * p + 1, slice(blk, 2 * blk))):
            slot = slice(_stat_lane(head), _stat_lane(head) + LSE_LANES)
            m_ref[j + (slice(None), slot)] = jnp.broadcast_to(m[rows], (blk, LSE_LANES))
            l_ref[j + (slice(None), slot)] = res[rows, LANES + slot.start:LANES + slot.stop]


def _attn_kernel(q_ref, kprev_ref, k_ref, vprev_ref, v_ref, o_ref, m_ref, l_ref):
    n_blocks, n_res, blk, _ = q_ref.shape
    span = 2 * blk
    row = lax.broadcasted_iota(jnp.int32, (span, span), 0) % blk
    col = lax.broadcasted_iota(jnp.int32, (span, span), 1)
    in_window = (col >= row) & (col <= row + blk)
    bias = jnp.where(in_window, 0.0, NEG_INF)
    run_has_prev = pl.program_id(2) > 0
    bias_first = jnp.where(in_window & ((col >= blk) | run_has_prev), 0.0, NEG_INF)

    for r in range(n_res):
        _attn_block(q_ref[0, r], kprev_ref[r], k_ref[0, r], vprev_ref[r], v_ref[0, r], bias_first,
                    o_ref, m_ref, l_ref, (0, r))
        for j in range(1, n_blocks):
            _attn_block(q_ref[j, r], k_ref[j - 1, r], k_ref[j, r], v_ref[j - 1, r], v_ref[j, r], bias,
                        o_ref, m_ref, l_ref, (j, r))


def _dilated_attention(qkv, dil):
    bsz, nb, blk, width = qkv.shape
    gd = GROUP_DIM
    tiles = nb // dil
    run = min(ATTN_STEP_BLOCKS, tiles)
    res = min(ATTN_STEP_BLOCKS // run, dil)
    assert tiles % run == 0 and dil % res == 0
    view = qkv.reshape(bsz, tiles, dil, blk, width)
    cur = lambda part: pl.BlockSpec((None, run, res, blk, gd), lambda b, r, c: (b, c, r, 0, part))
    prev = lambda part: pl.BlockSpec((None, None, res, blk, gd),
                                     lambda b, r, c: (b, jnp.maximum(c * run - 1, 0), r, 0, part))
    stat_spec = pl.BlockSpec((None, run, res, blk, LANES), lambda b, r, c: (b, c, r, 0, 0))
    stat_shape = jax.ShapeDtypeStruct((bsz, tiles, dil, blk, LANES), F32)
    o, m, l = pl.pallas_call(
        _attn_kernel,
        grid=(bsz, dil // res, tiles // run),
        in_specs=[cur(0), prev(1), cur(1), prev(2), cur(2)],
        out_specs=[pl.BlockSpec((None, run, res, blk, gd), lambda b, r, c: (b, c, r, 0, 0)),
                   stat_spec, stat_spec],
        out_shape=[jax.ShapeDtypeStruct((bsz, tiles, dil, blk, gd), F32), stat_shape, stat_shape],
        compiler_params=_params(3),
        name=f"dilated_attn_d{dil}",
    )(view, view, view, view, view)
    return (o.reshape(bsz, nb, blk, gd), m.reshape(bsz, nb, blk, LANES),
            l.reshape(bsz, nb, blk, LANES))


def _to_sequence(src_ref, lane0, seq_ref, slab, tmp_ref, dil):
    per = src_ref.shape[1]
    src = lambda r: src_ref[r, :, lane0:lane0 + LANES]
    if dil <= PASS_STRIDE:
        for r in range(dil):
            seq_ref[slab, pl.ds(r, per, stride=dil), :] = src(r)
        return
    n_hi = dil // PASS_STRIDE
    assert n_hi <= PASS_STRIDE and dil % PASS_STRIDE == 0
    part = n_hi * per
    for r in range(dil):
        r_hi, r_lo = divmod(r, PASS_STRIDE)
        tmp_ref[slab, pl.ds(r_lo * part + r_hi, per, stride=n_hi), :] = src(r)
    for r_lo in range(PASS_STRIDE):
        seq_ref[slab, pl.ds(r_lo, part, stride=PASS_STRIDE), :] = tmp_ref[slab, r_lo * part:(r_lo + 1) * part, :]


def _merge_kernel(x_ref, mod_ref, h_ref, ya_ref, zs_ref,
                  o0_ref, o1_ref, o2_ref, m0_ref, m1_ref, m2_ref, l0_ref, l1_ref, l2_ref,
                  wg_ref, pa_ref, pb_ref, wo_ref, out_ref, seq_ref, tmp_ref):
    d = x_ref.shape[-1]
    n_tiles = GROUP_DIM // LANES
    h = h_ref[...]
    g_a = jax.nn.sigmoid(jnp.dot(h, wg_ref[:, :d], preferred_element_type=F32))
    g_b = jax.nn.sigmoid(jnp.dot(h, wg_ref[:, d:], preferred_element_type=F32))

    groups = ((o0_ref, m0_ref, l0_ref), (o1_ref, m1_ref, l1_ref), (o2_ref, m2_ref, l2_ref))
    for g in range(1, N_GROUPS):
        o_ref, m_ref, l_ref = groups[g]
        sources = [(o_ref, p * LANES) for p in range(n_tiles)] + [(m_ref, 0), (l_ref, 0)]
        for slab, (src_ref, lane0) in enumerate(sources):
            _to_sequence(src_ref, lane0, seq_ref.at[g - 1], slab, tmp_ref, DILATIONS[g])

    def out_tile(g, p):
        return groups[0][0][:, p * LANES:(p + 1) * LANES] if g == 0 else seq_ref[g - 1, p]

    def stat_tile(g, which):
        return groups[0][1 + which][...] if g == 0 else seq_ref[g - 1, n_tiles + which]

    maxes = [stat_tile(g, 0) for g in range(N_GROUPS)]
    sums = [stat_tile(g, 1) for g in range(N_GROUPS)]
    top = jnp.maximum(jnp.maximum(maxes[0], maxes[1]), maxes[2])
    wts = [jnp.exp2(t - top) for t in maxes]
    den = wts[0] * sums[0] + wts[1] * sums[1] + wts[2] * sums[2]
    wns = [w / den for w in wts]
    lane_lo = lax.broadcasted_iota(jnp.int32, (h.shape[0], LANES), 1) < HEAD_DIM
    tiles = []
    for p in range(n_tiles):
        attn = None
        for g, wn in enumerate(wns):
            wide = jnp.where(lane_lo, wn[:, _stat_lane(2 * p):_stat_lane(2 * p) + 1],
                             wn[:, _stat_lane(2 * p + 1):_stat_lane(2 * p + 1) + 1])
            term = wide * out_tile(g, p)
            attn = term if attn is None else attn + term
        tiles.append(attn * zs_ref[:, p * LANES:(p + 1) * LANES])
    y_b = jnp.concatenate(tiles, axis=-1).astype(BF16)

    merged = (g_a * jnp.dot(ya_ref[...], pa_ref[...], preferred_element_type=F32)
              + g_b * jnp.dot(y_b, pb_ref[...], preferred_element_type=F32))
    upd = jnp.dot(merged.astype(BF16), wo_ref[...], preferred_element_type=F32)
    out_ref[...] = x_ref[...] + mod_ref[2:3, :] * upd


def _merge(x, mod3, h, y_a, zs, outs, maxes, sums, w_gate, p_a, p_b, w_o):
    bsz, s, d = x.shape
    tm = OUT_ROW_TILE
    row_d = pl.BlockSpec((None, tm, d), lambda b, i: (b, i, 0))
    row_g = pl.BlockSpec((None, tm, GROUP_DIM), lambda b, i: (b, i, 0))

    def dilated(t, dil):
        width = t.shape[-1]
        tile = Q_BLOCK * dil
        per = tm // dil
        if tile % tm:
            raise NotImplementedError("a dilated-order tile must be a whole number of merge tiles")
        steps = tile // tm
        view = t.reshape(bsz, s // tile, dil, Q_BLOCK, width)
        spec = pl.BlockSpec((None, None, dil, per, width),
                            lambda b, i: (b, i // steps, 0, i % steps, 0))
        return view, spec

    views, specs = [], []
    for group in (outs, maxes, sums):
        for t, dil in zip(group, DILATIONS):
            width = t.shape[-1]
            if dil == 1:
                views.append(t.reshape(bsz, s, width))
                specs.append(pl.BlockSpec((None, tm, width), lambda b, i: (b, i, 0)))
            else:
                v, sp = dilated(t, dil)
                views.append(v)
                specs.append(sp)

    return pl.pallas_call(
        _merge_kernel,
        grid=(bsz, s // tm),
        in_specs=[row_d, pl.BlockSpec((None, 3, d), lambda b, i: (b, 0, 0)), row_d, row_d, row_g,
                  *specs,
                  _resident(w_gate), _resident(p_a), _resident(p_b), _resident(w_o)],
        out_specs=row_d,
        out_shape=jax.ShapeDtypeStruct((bsz, s, d), F32),
        scratch_shapes=[pltpu.VMEM((N_GROUPS - 1, GROUP_DIM // LANES + 2, tm, LANES), F32),
                        pltpu.VMEM((GROUP_DIM // LANES + 2, tm, LANES), F32)],
        compiler_params=_params(2),
        name="merge_out",
    )(x, mod3, h, y_a, zs, *views, w_gate, p_a, p_b, w_o)


def _layer(x, c, layer, w_ada, b_ada, norm_w, w_in, conv_w, q_norm_w, k_norm_w, w_br_conv, w_br_attn, w_out):
    bsz, s, d = x.shape
    conv_dim = conv_w.shape[-1]
    attn_dim = N_GROUPS * GROUP_DIM
    assert conv_dim == d and conv_w.shape[1] == CONV_WIDTH and d % LANES == 0 and d % STAGE_ROWS == 0
    assert q_norm_w.shape[1:] == k_norm_w.shape[1:] == (HEAD_DIM,)
    assert w_in.shape[2] == 4 * conv_dim + 3 * attn_dim + GROUP_DIM + 2 * d
    assert s % (max(DILATIONS) * Q_BLOCK) == 0 and s % ROW_TILE == 0 and s % MIXER_ROW_TILE == 0
    assert all(w // dil == Q_BLOCK for w, dil in zip(WINDOWS, DILATIONS))

    mod3 = _modulation(c, w_ada, b_ada)
    nw = norm_w.reshape(1, d)

    y_a, h, w_attn, w_gate, p_a, p_b, w_o = _mixer_a(x, mod3, nw, w_in, layer, conv_w,
                                                     w_br_conv, w_br_attn, w_out)
    qkv, zs = _attn_proj(h, w_attn, q_norm_w, k_norm_w, layer)
    outs, maxes, sums = zip(*[_dilated_attention(t, dil) for t, dil in zip(qkv, DILATIONS)])
    return _merge(x, mod3, h, y_a, zs, outs, maxes, sums, w_gate, p_a, p_b, w_o)


@jax.jit
def kernel(x, c, w_ada, b_ada, norm_w, w_in, conv_w, q_norm_w, k_norm_w, w_br_conv, w_br_attn, w_out):
    depth = w_ada.shape[0]
    for l in range(depth):
        x = _layer(x, c, l, w_ada[l], b_ada[l], norm_w[l], w_in, conv_w, q_norm_w, k_norm_w,
                   w_br_conv, w_br_attn, w_out)
    return x
```

```python
import functools

import jax
import jax.numpy as jnp
from jax import lax
from jax.experimental import pallas as pl
from jax.experimental.pallas import tpu as pltpu

F32 = jnp.float32
BF16 = jnp.bfloat16

HEAD_DIM = 64
ATTN_SLOTS = 8
WINDOWS = (128, 512, 2048)
DILATIONS = (1, 4, 16)
N_GROUPS = len(WINDOWS)
GROUP_DIM = ATTN_SLOTS * HEAD_DIM
Q_BLOCK = 128
CONV_WIDTH = 3
EPS = 1e-6
NEG_INF = -1e30
LOG2_E = 1.4426950408889634
LANES = 128
SUBLANES = 8
LSE_LANES = LANES // ATTN_SLOTS
PASS_STRIDE = 4
VMEM_LIMIT = 56 * 1024 * 1024

MIXER_ROW_TILE = 1024
STAGE_ROWS = 128
STAGE_BYTES_IN_FLIGHT = 6 * 1024 * 1024
ROW_TILE = 512
OUT_ROW_TILE = 512
ATTN_STEP_BLOCKS = 16


def _params(n_axes):
    return pltpu.CompilerParams(dimension_semantics=("arbitrary",) * n_axes,
                                vmem_limit_bytes=VMEM_LIMIT)


def _silu(t):
    return t * jax.nn.sigmoid(t)


def _modulated_norm(x, norm_w, mod):
    ms = jnp.mean(x * x, axis=-1, keepdims=True)
    xn = x * lax.rsqrt(ms + EPS) * norm_w
    return xn * (1.0 + mod[1:2, :]) + mod[0:1, :]


def _to_dilated(slab_ref, tmp_ref, n_rows, dil):
    per = n_rows // dil
    n_slabs = slab_ref.shape[0]
    if dil <= PASS_STRIDE:
        piece = lambda j, r: slab_ref[j, pl.ds(r, per, stride=dil), :]
    else:
        n_hi = dil // PASS_STRIDE
        assert n_hi <= PASS_STRIDE and dil % PASS_STRIDE == 0
        part = n_hi * per
        for j in range(n_slabs):
            for r_lo in range(PASS_STRIDE):
                tmp_ref[j, r_lo * part:(r_lo + 1) * part, :] = slab_ref[j, pl.ds(r_lo, part, stride=PASS_STRIDE), :]
        piece = lambda j, r: tmp_ref[j, pl.ds((r % PASS_STRIDE) * part + r // PASS_STRIDE, per, stride=n_hi), :]
    return jnp.concatenate(
        [jnp.concatenate([piece(j, r) for r in range(dil)], axis=0) for j in range(n_slabs)], axis=-1)


def _fetch_bf16(pairs, stage_ref, sem):
    slots, rows, _ = stage_ref.shape
    chunks = [(src, dst, r0) for src, dst in pairs for r0 in range(0, dst.shape[0], rows)]

    def copy(i):
        src, dst, r0 = chunks[i]
        return pltpu.make_async_copy(src.at[pl.ds(r0, rows), :],
                                     stage_ref.at[i % slots, :, pl.ds(0, dst.shape[1])],
                                     sem.at[i % slots])

    for i in range(min(slots - 1, len(chunks))):
        copy(i).start()
    for i, (_, dst, r0) in enumerate(chunks):
        if i + slots - 1 < len(chunks):
            copy(i + slots - 1).start()
        copy(i).wait()
        dst[r0:r0 + rows, :] = stage_ref[i % slots, :, 0:dst.shape[1]].astype(dst.dtype)


def _first_step():
    return (pl.program_id(0) == 0) & (pl.program_id(1) == 0)


_HBM = pl.BlockSpec(memory_space=pl.ANY)


def _resident(a):
    return pl.BlockSpec(a.shape, lambda *_: (0,) * a.ndim, pipeline_mode=pl.Buffered(1))


def _stage_scratch(width):
    chunk_bytes = STAGE_ROWS * width * 4
    slots = -(-STAGE_BYTES_IN_FLIGHT // chunk_bytes) + 1
    return [pltpu.VMEM((slots, STAGE_ROWS, width), F32), pltpu.SemaphoreType.DMA((slots,))]


def _mod_kernel(c_ref, w_ref, b_ref, o_ref):
    d = c_ref.shape[-1]
    sc = _silu(c_ref[...])
    for t in range(o_ref.shape[1]):
        cols = slice(t * d, (t + 1) * d)
        o_ref[:, t, :] = jnp.dot(sc, w_ref[:, cols], preferred_element_type=F32) + b_ref[:, cols]


def _modulation(c, w_ada, b_ada):
    bsz, d = c.shape
    n = w_ada.shape[1]
    whole = lambda shape: pl.BlockSpec(shape, lambda j: (0,) * len(shape), pipeline_mode=pl.Buffered(1))
    return pl.pallas_call(
        _mod_kernel,
        grid=(1,),
        in_specs=[whole((bsz, d)), whole((d, n)), whole((1, n))],
        out_specs=pl.BlockSpec((bsz, n // d, d), lambda j: (0, 0, 0)),
        out_shape=jax.ShapeDtypeStruct((bsz, n // d, d), F32),
        compiler_params=_params(1),
        name="adaln_mod",
    )(c, w_ada, b_ada.reshape(1, n))


def _mixer_a_kernel(x_ref, mod_ref, nw_ref, w_hbm, cw_ref, win_rows_ref, pa_rows_ref, pb_rows_ref,
                    wo_rows_ref, y_ref, h_ref, wattn_ref, wgate_ref, pa_ref, pb_ref, wo_ref,
                    u_ref, w_ref, stage_ref, sem, *, layer):
    tm, d = x_ref.shape
    mod = mod_ref[...]
    nw = nw_ref[...]
    cw = cw_ref[...]

    @pl.when(_first_step())
    def _():
        _fetch_bf16([(w_hbm.at[layer, :, pl.ds(0, w_ref.shape[1])], w_ref)], stage_ref, sem)

    @pl.when(pl.program_id(1) == 0)
    def _():
        u_ref[0:SUBLANES, :] = jnp.zeros((SUBLANES, d), F32)

    n_attn = wattn_ref.shape[1]
    wattn_ref[...] = win_rows_ref[:, 4 * d:4 * d + n_attn].astype(wattn_ref.dtype)
    wgate_ref[...] = win_rows_ref[:, 4 * d + n_attn:].astype(wgate_ref.dtype)
    pa_ref[...] = pa_rows_ref[...].astype(pa_ref.dtype)
    pb_ref[...] = pb_rows_ref[...].astype(pb_ref.dtype)
    wo_ref[...] = wo_rows_ref[...].astype(wo_ref.dtype)

    h = _modulated_norm(x_ref[...], nw, mod).astype(BF16)
    h_ref[...] = h
    proj = lambda j: jnp.dot(h, w_ref[:, j * d:(j + 1) * d], preferred_element_type=F32)
    z_a, c_a, x_a = proj(3), proj(1), proj(2)
    u = c_a * x_a
    u_ref[SUBLANES:, :] = u
    u_m1 = u_ref[pl.ds(SUBLANES - 1, tm), :]
    u_m2 = u_ref[pl.ds(SUBLANES - 2, tm), :]
    conv = cw[0:1, :] * u_m2 + cw[1:2, :] * u_m1 + cw[2:3, :] * u
    gated = conv * _silu(z_a)
    y_ref[...] = (proj(0) * gated).astype(y_ref.dtype)
    u_ref[0:SUBLANES, :] = u[tm - SUBLANES:, :]


def _mixer_a(x, mod3, norm_w, w_in, layer, conv_w, p_a, p_b, w_o):
    bsz, s, d = x.shape
    tm = MIXER_ROW_TILE
    tiles = s // tm
    n_steps = bsz * tiles
    n_attn = (3 * N_GROUPS + 1) * GROUP_DIM
    rows = pl.BlockSpec((None, tm, d), lambda b, i: (b, i, 0))

    def slab(a, layered):
        k, n = a.shape[-2:]
        assert k % (n_steps * 2 * SUBLANES) == 0
        step = lambda b, i: b * tiles + i
        if layered:
            return pl.BlockSpec((None, k // n_steps, n), lambda b, i: (layer, step(b, i), 0))
        return pl.BlockSpec((k // n_steps, n), lambda b, i: (step(b, i), 0))

    bf16 = lambda k, n: jax.ShapeDtypeStruct((k, n), BF16)
    out_shapes = [jax.ShapeDtypeStruct((bsz, s, d), BF16)] * 2 + [
        bf16(d, n_attn), bf16(d, 2 * d), bf16(*p_a.shape[1:]), bf16(*p_b.shape[1:]), bf16(*w_o.shape[1:])]
    return pl.pallas_call(
        functools.partial(_mixer_a_kernel, layer=layer),
        grid=(bsz, tiles),
        in_specs=[rows,
                  pl.BlockSpec((None, 3, d), lambda b, i: (b, 0, 0)),
                  pl.BlockSpec((1, d), lambda b, i: (0, 0)),
                  _HBM,
                  pl.BlockSpec((None,) + conv_w.shape[1:], lambda b, i: (layer, 0, 0)),
                  slab(w_in, True), slab(p_a, True), slab(p_b, True), slab(w_o, True)],
        out_specs=[rows, rows] + [slab(t, False) for t in out_shapes[2:]],
        out_shape=out_shapes,
        scratch_shapes=[pltpu.VMEM((SUBLANES + tm, d), F32),
                        pltpu.VMEM((d, 4 * d), BF16),
                        *_stage_scratch(4 * d)],
        compiler_params=_params(2),
        name="mixer_a",
    )(x, mod3, norm_w, w_in, conv_w, w_in, p_a, p_b, w_o)


def _head_rms(t, w):
    sq = t * t
    lane_lo = lax.broadcasted_iota(jnp.int32, (t.shape[0], LANES), 1) < HEAD_DIM
    parts = []
    for j in range(0, t.shape[-1], LANES):
        tile = sq[:, j:j + LANES]
        even = jnp.sum(jnp.where(lane_lo, tile, 0.0), axis=-1, keepdims=True)
        odd = jnp.sum(jnp.where(lane_lo, 0.0, tile), axis=-1, keepdims=True)
        parts.append(jnp.where(lane_lo, even, odd))
    ssq = jnp.concatenate(parts, axis=-1)
    return t * lax.rsqrt(ssq * (1.0 / HEAD_DIM) + EPS) * w


def _attn_proj_kernel(h_ref, w_ref, qw_ref, kw_ref, qkv0_ref, qkv1_ref, qkv2_ref, zs_ref, slab_ref,
                      tmp_ref):
    tm, d = h_ref.shape
    h = h_ref[...]
    qw = jnp.tile(qw_ref[...] * (HEAD_DIM ** -0.5 * LOG2_E), (1, ATTN_SLOTS))
    kw = jnp.tile(kw_ref[...], (1, ATTN_SLOTS))
    for j in range(d // LANES):
        slab_ref[j] = h[:, j * LANES:(j + 1) * LANES].astype(F32)
    ad = N_GROUPS * GROUP_DIM
    for g, out_ref in enumerate((qkv0_ref, qkv1_ref, qkv2_ref)):
        hg = h if DILATIONS[g] == 1 else _to_dilated(slab_ref, tmp_ref, tm, DILATIONS[g]).astype(BF16)
        q, k, v = [jnp.dot(hg, w_ref[:, part * ad + g * GROUP_DIM:part * ad + (g + 1) * GROUP_DIM],
                           preferred_element_type=F32) for part in range(3)]
        qkv = jnp.concatenate([_head_rms(q, qw), _head_rms(k, kw), v], axis=-1)
        out_ref[...] = qkv.astype(out_ref.dtype).reshape(out_ref.shape)
        if g == 0:
            z = jnp.dot(hg, w_ref[:, 3 * ad:], preferred_element_type=F32)
            zs_ref[...] = _silu(z)


def _attn_proj(h, w_attn, q_norm_w, k_norm_w, layer):
    bsz, s, d = h.shape
    tm = ROW_TILE
    depth = q_norm_w.shape[0]
    head_w = pl.BlockSpec((None, 1, HEAD_DIM), lambda b, i: (layer, 0, 0))
    gw = 3 * GROUP_DIM
    out_shapes, out_specs = [], []
    for dil in DILATIONS:
        tile = Q_BLOCK * dil
        assert dil == 1 or tile % tm == 0
        if tile <= tm:
            nblk = tm // Q_BLOCK
            out_shapes.append(jax.ShapeDtypeStruct((bsz, s // Q_BLOCK, Q_BLOCK, gw), BF16))
            out_specs.append(pl.BlockSpec((None, nblk, Q_BLOCK, gw), lambda b, i: (b, i, 0, 0)))
        else:
            per = tm // dil
            steps = tile // tm
            out_shapes.append(jax.ShapeDtypeStruct((bsz, s // tile, dil, Q_BLOCK, gw), BF16))
            out_specs.append(pl.BlockSpec((None, None, dil, per, gw),
                                          lambda b, i, steps=steps: (b, i // steps, 0, i % steps, 0)))
    out_shapes.append(jax.ShapeDtypeStruct((bsz, s, GROUP_DIM), F32))
    out_specs.append(pl.BlockSpec((None, tm, GROUP_DIM), lambda b, i: (b, i, 0)))
    outs = pl.pallas_call(
        _attn_proj_kernel,
        grid=(bsz, s // tm),
        in_specs=[pl.BlockSpec((None, tm, d), lambda b, i: (b, i, 0)),
                  _resident(w_attn),
                  head_w, head_w],
        out_specs=out_specs,
        out_shape=out_shapes,
        scratch_shapes=[pltpu.VMEM((d // LANES, tm, LANES), F32)] * 2,
        compiler_params=_params(2),
        name="attn_proj",
    )(h, w_attn, q_norm_w.reshape(depth, 1, HEAD_DIM), k_norm_w.reshape(depth, 1, HEAD_DIM))
    qkv = [t.reshape(bsz, s // Q_BLOCK, Q_BLOCK, gw) for t in outs[:N_GROUPS]]
    return qkv, outs[N_GROUPS]


def _stat_lane(head):
    return head * LSE_LANES


def _attn_block(q, kp, kc, vp, vc, bias, o_ref, m_ref, l_ref, j):
    blk = q.shape[0]
    lane = lax.broadcasted_iota(jnp.int32, (blk, LANES), 1)
    lane_lo = lane < HEAD_DIM
    zero = jnp.zeros((blk, LANES), q.dtype)
    one = jnp.ones((2 * blk, LANES), q.dtype)
    for p in range(GROUP_DIM // LANES):
        sl = slice(p * LANES, (p + 1) * LANES)
        q2 = q[:, sl]
        qs = jnp.concatenate([jnp.where(lane_lo, q2, zero), jnp.where(lane_lo, zero, q2)], axis=0)
        k2 = jnp.concatenate([kp[:, sl], kc[:, sl]], axis=0)
        v2 = jnp.concatenate([vp[:, sl], vc[:, sl]], axis=0)
        s = lax.dot_general(qs, k2, (((1,), (1,)), ((), ())), preferred_element_type=F32) + bias
        m = jnp.max(s, axis=-1, keepdims=True)
        e = jnp.exp2(s - m).astype(BF16)
        res = jnp.dot(e, jnp.concatenate([v2, one], axis=-1), preferred_element_type=F32)
        o_ref[j + (slice(None), sl)] = jnp.where(lane_lo, res[:blk, :LANES], res[blk:, :LANES])
        for head, rows in ((2 * p, slice(0, blk)), (2 * p + 1, slice(blk, 2 * blk))):
            slot = slice(_stat_lane(head), _stat_lane(head) + LSE_LANES)
            m_ref[j + (slice(None), slot)] = jnp.broadcast_to(m[rows], (blk, LSE_LANES))
            l_ref[j + (slice(None), slot)] = res[rows, LANES + slot.start:LANES + slot.stop]


def _attn_kernel(q_ref, kprev_ref, k_ref, vprev_ref, v_ref, o_ref, m_ref, l_ref):
    n_blocks, n_res, blk, _ = q_ref.shape
    span = 2 * blk
    row = lax.broadcasted_iota(jnp.int32, (span, span), 0) % blk
    col = lax.broadcasted_iota(jnp.int32, (span, span), 1)
    in_window = (col >= row) & (col <= row + blk)
    bias = jnp.where(in_window, 0.0, NEG_INF)
    run_has_prev = pl.program_id(2) > 0
    bias_first = jnp.where(in_window & ((col >= blk) | run_has_prev), 0.0, NEG_INF)

    for r in range(n_res):
        _attn_block(q_ref.at[0, r], kprev_ref.at[r], k_ref.at[0, r], vprev_ref.at[r], v_ref.at[0, r],
                    bias_first, o_ref, m_ref, l_ref, (0, r))
        for j in range(1, n_blocks):
            _attn_block(q_ref.at[j, r], k_ref.at[j - 1, r], k_ref.at[j, r], v_ref.at[j - 1, r],
                        v_ref.at[j, r], bias, o_ref, m_ref, l_ref, (j, r))


def _dilated_attention(qkv, dil):
    bsz, nb, blk, width = qkv.shape
    gd = GROUP_DIM
    tiles = nb // dil
    run = min(ATTN_STEP_BLOCKS, tiles)
    res = min(ATTN_STEP_BLOCKS // run, dil)
    assert tiles % run == 0 and dil % res == 0
    view = qkv.reshape(bsz, tiles, dil, blk, width)
    cur = lambda part: pl.BlockSpec((None, run, res, blk, gd), lambda b, r, c: (b, c, r, 0, part))
    prev = lambda part: pl.BlockSpec((None, None, res, blk, gd),
                                     lambda b, r, c: (b, jnp.maximum(c * run - 1, 0), r, 0, part))
    stat_spec = pl.BlockSpec((None, run, res, blk, LANES), lambda b, r, c: (b, c, r, 0, 0))
    stat_shape = jax.ShapeDtypeStruct((bsz, tiles, dil, blk, LANES), F32)
    o, m, l = pl.pallas_call(
        _attn_kernel,
        grid=(bsz, dil // res, tiles // run),
        in_specs=[cur(0), prev(1), cur(1), prev(2), cur(2)],
        out_specs=[pl.BlockSpec((None, run, res, blk, gd), lambda b, r, c: (b, c, r, 0, 0)),
                   stat_spec, stat_spec],
        out_shape=[jax.ShapeDtypeStruct((bsz, tiles, dil, blk, gd), F32), stat_shape, stat_shape],
        compiler_params=_params(3),
        name=f"dilated_attn_d{dil}",
    )(view, view, view, view, view)
    return (o.reshape(bsz, nb, blk, gd), m.reshape(bsz, nb, blk, LANES),
            l.reshape(bsz, nb, blk, LANES))


def _to_sequence(src_ref, lane0, seq_ref, slab, tmp_ref, dil):
    per = src_ref.shape[1]
    src = lambda r: src_ref[r, :, lane0:lane0 + LANES]
    if dil <= PASS_STRIDE:
        for r in range(dil):
            seq_ref[slab, pl.ds(r, per, stride=dil), :] = src(r)
        return
    n_hi = dil // PASS_STRIDE
    assert n_hi <= PASS_STRIDE and dil % PASS_STRIDE == 0
    part = n_hi * per
    for r in range(dil):
        r_hi, r_lo = divmod(r, PASS_STRIDE)
        tmp_ref[slab, pl.ds(r_lo * part + r_hi, per, stride=n_hi), :] = src(r)
    for r_lo in range(PASS_STRIDE):
        seq_ref[slab, pl.ds(r_lo, part, stride=PASS_STRIDE), :] = tmp_ref[slab, r_lo * part:(r_lo + 1) * part, :]


def _merge_kernel(x_ref, mod_ref, h_ref, ya_ref, zs_ref,
                  o0_ref, o1_ref, o2_ref, m0_ref, m1_ref, m2_ref, l0_ref, l1_ref, l2_ref,
                  wg_ref, pa_ref, pb_ref, wo_ref, out_ref, seq_ref, tmp_ref):
    d = x_ref.shape[-1]
    n_tiles = GROUP_DIM // LANES
    h = h_ref[...]
    g_a = jax.nn.sigmoid(jnp.dot(h, wg_ref[:, :d], preferred_element_type=F32))
    g_b = jax.nn.sigmoid(jnp.dot(h, wg_ref[:, d:], preferred_element_type=F32))

    groups = ((o0_ref, m0_ref, l0_ref), (o1_ref, m1_ref, l1_ref), (o2_ref, m2_ref, l2_ref))
    for g in range(1, N_GROUPS):
        o_ref, m_ref, l_ref = groups[g]
        sources = [(o_ref, p * LANES) for p in range(n_tiles)] + [(m_ref, 0), (l_ref, 0)]
        for slab, (src_ref, lane0) in enumerate(sources):
            _to_sequence(src_ref, lane0, seq_ref.at[g - 1], slab, tmp_ref, DILATIONS[g])

    def out_tile(g, p):
        return groups[0][0][:, p * LANES:(p + 1) * LANES] if g == 0 else seq_ref[g - 1, p]

    def stat_tile(g, which):
        return groups[0][1 + which][...] if g == 0 else seq_ref[g - 1, n_tiles + which]

    maxes = [stat_tile(g, 0) for g in range(N_GROUPS)]
    sums = [stat_tile(g, 1) for g in range(N_GROUPS)]
    top = jnp.maximum(jnp.maximum(maxes[0], maxes[1]), maxes[2])
    wts = [jnp.exp2(t - top) for t in maxes]
    den = wts[0] * sums[0] + wts[1] * sums[1] + wts[2] * sums[2]
    wns = [w / den for w in wts]
    lane_lo = lax.broadcasted_iota(jnp.int32, (h.shape[0], LANES), 1) < HEAD_DIM
    tiles = []
    for p in range(n_tiles):
        attn = None
        for g, wn in enumerate(wns):
            wide = jnp.where(lane_lo, wn[:, _stat_lane(2 * p):_stat_lane(2 * p) + 1],
                             wn[:, _stat_lane(2 * p + 1):_stat_lane(2 * p + 1) + 1])
            term = wide * out_tile(g, p)
            attn = term if attn is None else attn + term
        tiles.append(attn * zs_ref[:, p * LANES:(p + 1) * LANES])
    y_b = jnp.concatenate(tiles, axis=-1).astype(BF16)

    merged = (g_a * jnp.dot(ya_ref[...], pa_ref[...], preferred_element_type=F32)
              + g_b * jnp.dot(y_b, pb_ref[...], preferred_element_type=F32))
    upd = jnp.dot(merged.astype(BF16), wo_ref[...], preferred_element_type=F32)
    out_ref[...] = x_ref[...] + mod_ref[2:3, :] * upd


def _merge(x, mod3, h, y_a, zs, outs, maxes, sums, w_gate, p_a, p_b, w_o):
    bsz, s, d = x.shape
    tm = OUT_ROW_TILE
    row_d = pl.BlockSpec((None, tm, d), lambda b, i: (b, i, 0))
    row_g = pl.BlockSpec((None, tm, GROUP_DIM), lambda b, i: (b, i, 0))

    def dilated(t, dil):
        width = t.shape[-1]
        tile = Q_BLOCK * dil
        per = tm // dil
        if tile % tm:
            raise NotImplementedError("a dilated-order tile must be a whole number of merge tiles")
        steps = tile // tm
        view = t.reshape(bsz, s // tile, dil, Q_BLOCK, width)
        spec = pl.BlockSpec((None, None, dil, per, width),
                            lambda b, i: (b, i // steps, 0, i % steps, 0))
        return view, spec

    views, specs = [], []
    for group in (outs, maxes, sums):
        for t, dil in zip(group, DILATIONS):
            width = t.shape[-1]
            if dil == 1:
                views.append(t.reshape(bsz, s, width))
                specs.append(pl.BlockSpec((None, tm, width), lambda b, i: (b, i, 0)))
            else:
                v, sp = dilated(t, dil)
                views.append(v)
                specs.append(sp)

    return pl.pallas_call(
        _merge_kernel,
        grid=(bsz, s // tm),
        in_specs=[row_d, pl.BlockSpec((None, 3, d), lambda b, i: (b, 0, 0)), row_d, row_d, row_g,
                  *specs,
                  _resident(w_gate), _resident(p_a), _resident(p_b), _resident(w_o)],
        out_specs=row_d,
        out_shape=jax.ShapeDtypeStruct((bsz, s, d), F32),
        scratch_shapes=[pltpu.VMEM((N_GROUPS - 1, GROUP_DIM // LANES + 2, tm, LANES), F32),
                        pltpu.VMEM((GROUP_DIM // LANES + 2, tm, LANES), F32)],
        compiler_params=_params(2),
        name="merge_out",
    )(x, mod3, h, y_a, zs, *views, w_gate, p_a, p_b, w_o)


def _layer(x, c, layer, w_ada, b_ada, norm_w, w_in, conv_w, q_norm_w, k_norm_w, w_br_conv, w_br_attn, w_out):
    bsz, s, d = x.shape
    conv_dim = conv_w.shape[-1]
    attn_dim = N_GROUPS * GROUP_DIM
    assert conv_dim == d and conv_w.shape[1] == CONV_WIDTH and d % LANES == 0 and d % STAGE_ROWS == 0
    assert q_norm_w.shape[1:] == k_norm_w.shape[1:] == (HEAD_DIM,)
    assert w_in.shape[2] == 4 * conv_dim + 3 * attn_dim + GROUP_DIM + 2 * d
    assert s % (max(DILATIONS) * Q_BLOCK) == 0 and s % ROW_TILE == 0 and s % MIXER_ROW_TILE == 0
    assert all(w // dil == Q_BLOCK for w, dil in zip(WINDOWS, DILATIONS))

    mod3 = _modulation(c, w_ada, b_ada)
    nw = norm_w.reshape(1, d)

    y_a, h, w_attn, w_gate, p_a, p_b, w_o = _mixer_a(x, mod3, nw, w_in, layer, conv_w,
                                                     w_br_conv, w_br_attn, w_out)
    qkv, zs = _attn_proj(h, w_attn, q_norm_w, k_norm_w, layer)
    outs, maxes, sums = zip(*[_dilated_attention(t, dil) for t, dil in zip(qkv, DILATIONS)])
    return _merge(x, mod3, h, y_a, zs, outs, maxes, sums, w_gate, p_a, p_b, w_o)


@jax.jit
def kernel(x, c, w_ada, b_ada, norm_w, w_in, conv_w, q_norm_w, k_norm_w, w_br_conv, w_br_attn, w_out):
    depth = w_ada.shape[0]
    for l in range(depth):
        x = _layer(x, c, l, w_ada[l], b_ada[l], norm_w[l], w_in, conv_w, q_norm_w, k_norm_w,
                   w_br_conv, w_br_attn, w_out)
    return x
```

```python
import functools

import jax
import jax.numpy as jnp
from jax import lax
from jax.experimental import pallas as pl
from jax.experimental.pallas import tpu as pltpu

F32 = jnp.float32
BF16 = jnp.bfloat16

HEAD_DIM = 64
ATTN_SLOTS = 8
WINDOWS = (128, 512, 2048)
DILATIONS = (1, 4, 16)
N_GROUPS = len(WINDOWS)
GROUP_DIM = ATTN_SLOTS * HEAD_DIM
Q_BLOCK = 128
CONV_WIDTH = 3
EPS = 1e-6
NEG_INF = -1e30
LOG2_E = 1.4426950408889634
LANES = 128
SUBLANES = 8
LSE_LANES = LANES // ATTN_SLOTS
PASS_STRIDE = 4
VMEM_LIMIT = 56 * 1024 * 1024

MIXER_ROW_TILE = 1024
STAGE_ROWS = 128
STAGE_BYTES_IN_FLIGHT = 6 * 1024 * 1024
ROW_TILE = 512
OUT_ROW_TILE = 512
ATTN_STEP_BLOCKS = 16


def _params(n_axes):
    return pltpu.CompilerParams(dimension_semantics=("arbitrary",) * n_axes,
                                vmem_limit_bytes=VMEM_LIMIT)


def _silu(t):
    return t * jax.nn.sigmoid(t)


def _modulated_norm(x, norm_w, mod):
    ms = jnp.mean(x * x, axis=-1, keepdims=True)
    xn = x * lax.rsqrt(ms + EPS) * norm_w
    return xn * (1.0 + mod[1:2, :]) + mod[0:1, :]


def _to_dilated(slab_ref, tmp_ref, n_rows, dil):
    per = n_rows // dil
    n_slabs = slab_ref.shape[0]
    if dil <= PASS_STRIDE:
        piece = lambda j, r: slab_ref[j, pl.ds(r, per, stride=dil), :]
    else:
        n_hi = dil // PASS_STRIDE
        assert n_hi <= PASS_STRIDE and dil % PASS_STRIDE == 0
        part = n_hi * per
        for j in range(n_slabs):
            for r_lo in range(PASS_STRIDE):
                tmp_ref[j, r_lo * part:(r_lo + 1) * part, :] = slab_ref[j, pl.ds(r_lo, part, stride=PASS_STRIDE), :]
        piece = lambda j, r: tmp_ref[j, pl.ds((r % PASS_STRIDE) * part + r // PASS_STRIDE, per, stride=n_hi), :]
    return jnp.concatenate(
        [jnp.concatenate([piece(j, r) for r in range(dil)], axis=0) for j in range(n_slabs)], axis=-1)


def _fetch_bf16(pairs, stage_ref, sem):
    slots, rows, _ = stage_ref.shape
    chunks = [(src, dst, r0) for src, dst in pairs for r0 in range(0, dst.shape[0], rows)]

    def copy(i):
        src, dst, r0 = chunks[i]
        return pltpu.make_async_copy(src.at[pl.ds(r0, rows), :],
                                     stage_ref.at[i % slots, :, pl.ds(0, dst.shape[1])],
                                     sem.at[i % slots])

    for i in range(min(slots - 1, len(chunks))):
        copy(i).start()
    for i, (_, dst, r0) in enumerate(chunks):
        if i + slots - 1 < len(chunks):
            copy(i + slots - 1).start()
        copy(i).wait()
        dst[r0:r0 + rows, :] = stage_ref[i % slots, :, 0:dst.shape[1]].astype(dst.dtype)


def _first_step():
    return (pl.program_id(0) == 0) & (pl.program_id(1) == 0)


_HBM = pl.BlockSpec(memory_space=pl.ANY)


def _resident(a):
    return pl.BlockSpec(a.shape, lambda *_: (0,) * a.ndim, pipeline_mode=pl.Buffered(1))


def _stage_scratch(width):
    chunk_bytes = STAGE_ROWS * width * 4
    slots = -(-STAGE_BYTES_IN_FLIGHT // chunk_bytes) + 1
    return [pltpu.VMEM((slots, STAGE_ROWS, width), F32), pltpu.SemaphoreType.DMA((slots,))]


def _mod_kernel(c_ref, w_ref, b_ref, o_ref):
    d = c_ref.shape[-1]
    sc = _silu(c_ref[...])
    for t in range(o_ref.shape[1]):
        cols = slice(t * d, (t + 1) * d)
        o_ref[:, t, :] = jnp.dot(sc, w_ref[:, cols], preferred_element_type=F32) + b_ref[:, cols]


def _modulation(c, w_ada, b_ada):
    bsz, d = c.shape
    n = w_ada.shape[1]
    whole = lambda shape: pl.BlockSpec(shape, lambda j: (0,) * len(shape), pipeline_mode=pl.Buffered(1))
    return pl.pallas_call(
        _mod_kernel,
        grid=(1,),
        in_specs=[whole((bsz, d)), whole((d, n)), whole((1, n))],
        out_specs=pl.BlockSpec((bsz, n // d, d), lambda j: (0, 0, 0)),
        out_shape=jax.ShapeDtypeStruct((bsz, n // d, d), F32),
        compiler_params=_params(1),
        name="adaln_mod",
    )(c, w_ada, b_ada.reshape(1, n))


def _mixer_a_kernel(x_ref, mod_ref, nw_ref, w_hbm, cw_hbm, win_rows_ref, pa_rows_ref, pb_rows_ref,
                    wo_rows_ref, y_ref, h_ref, wattn_ref, wgate_ref, pa_ref, pb_ref, wo_ref,
                    u_ref, w_ref, stage_ref, sem, cw_ref, cw_sem, *, layer):
    tm, d = x_ref.shape
    mod = mod_ref[...]
    nw = nw_ref[...]

    @pl.when(_first_step())
    def _():
        conv_copy = pltpu.make_async_copy(cw_hbm.at[layer], cw_ref, cw_sem.at[0])
        conv_copy.start()
        _fetch_bf16([(w_hbm.at[layer, :, pl.ds(0, w_ref.shape[1])], w_ref)], stage_ref, sem)
        conv_copy.wait()

    cw = cw_ref[...]

    @pl.when(pl.program_id(1) == 0)
    def _():
        u_ref[0:SUBLANES, :] = jnp.zeros((SUBLANES, d), F32)

    n_attn = wattn_ref.shape[1]
    wattn_ref[...] = win_rows_ref[:, 4 * d:4 * d + n_attn].astype(wattn_ref.dtype)
    wgate_ref[...] = win_rows_ref[:, 4 * d + n_attn:].astype(wgate_ref.dtype)
    pa_ref[...] = pa_rows_ref[...].astype(pa_ref.dtype)
    pb_ref[...] = pb_rows_ref[...].astype(pb_ref.dtype)
    wo_ref[...] = wo_rows_ref[...].astype(wo_ref.dtype)

    h = _modulated_norm(x_ref[...], nw, mod).astype(BF16)
    h_ref[...] = h
    proj = lambda j: jnp.dot(h, w_ref[:, j * d:(j + 1) * d], preferred_element_type=F32)
    z_a, c_a, x_a = proj(3), proj(1), proj(2)
    u = c_a * x_a
    u_ref[SUBLANES:, :] = u
    u_m1 = u_ref[pl.ds(SUBLANES - 1, tm), :]
    u_m2 = u_ref[pl.ds(SUBLANES - 2, tm), :]
    conv = cw[0:1, :] * u_m2 + cw[1:2, :] * u_m1 + cw[2:3, :] * u
    gated = conv * _silu(z_a)
    y_ref[...] = (proj(0) * gated).astype(y_ref.dtype)
    u_ref[0:SUBLANES, :] = u[tm - SUBLANES:, :]


def _mixer_a(x, mod3, norm_w, w_in, layer, conv_w, p_a, p_b, w_o):
    bsz, s, d = x.shape
    tm = MIXER_ROW_TILE
    tiles = s // tm
    n_steps = bsz * tiles
    n_attn = (3 * N_GROUPS + 1) * GROUP_DIM
    rows = pl.BlockSpec((None, tm, d), lambda b, i: (b, i, 0))

    def slab(a, layered):
        k, n = a.shape[-2:]
        assert k % (n_steps * 2 * SUBLANES) == 0
        step = lambda b, i: b * tiles + i
        if layered:
            return pl.BlockSpec((None, k // n_steps, n), lambda b, i: (layer, step(b, i), 0))
        return pl.BlockSpec((k // n_steps, n), lambda b, i: (step(b, i), 0))

    bf16 = lambda k, n: jax.ShapeDtypeStruct((k, n), BF16)
    out_shapes = [jax.ShapeDtypeStruct((bsz, s, d), BF16)] * 2 + [
        bf16(d, n_attn), bf16(d, 2 * d), bf16(*p_a.shape[1:]), bf16(*p_b.shape[1:]), bf16(*w_o.shape[1:])]
    return pl.pallas_call(
        functools.partial(_mixer_a_kernel, layer=layer),
        grid=(bsz, tiles),
        in_specs=[rows,
                  pl.BlockSpec((None, 3, d), lambda b, i: (b, 0, 0)),
                  pl.BlockSpec((1, d), lambda b, i: (0, 0)),
                  _HBM,
                  _HBM,
                  slab(w_in, True), slab(p_a, True), slab(p_b, True), slab(w_o, True)],
        out_specs=[rows, rows] + [slab(t, False) for t in out_shapes[2:]],
        out_shape=out_shapes,
        scratch_shapes=[pltpu.VMEM((SUBLANES + tm, d), F32),
                        pltpu.VMEM((d, 4 * d), BF16),
                        *_stage_scratch(4 * d),
                        pltpu.VMEM(conv_w.shape[1:], F32),
                        pltpu.SemaphoreType.DMA((1,))],
        compiler_params=_params(2),
        name="mixer_a",
    )(x, mod3, norm_w, w_in, conv_w, w_in, p_a, p_b, w_o)


def _head_rms(t, w):
    sq = t * t
    lane_lo = lax.broadcasted_iota(jnp.int32, (t.shape[0], LANES), 1) < HEAD_DIM
    parts = []
    for j in range(0, t.shape[-1], LANES):
        tile = sq[:, j:j + LANES]
        even = jnp.sum(jnp.where(lane_lo, tile, 0.0), axis=-1, keepdims=True)
        odd = jnp.sum(jnp.where(lane_lo, 0.0, tile), axis=-1, keepdims=True)
        parts.append(jnp.where(lane_lo, even, odd))
    ssq = jnp.concatenate(parts, axis=-1)
    return t * lax.rsqrt(ssq * (1.0 / HEAD_DIM) + EPS) * w


def _attn_proj_kernel(h_ref, w_ref, qw_ref, kw_ref, qkv0_ref, qkv1_ref, qkv2_ref, zs_ref, slab_ref,
                      tmp_ref):
    tm, d = h_ref.shape
    h = h_ref[...]
    qw = jnp.tile(qw_ref[...] * (HEAD_DIM ** -0.5 * LOG2_E), (1, ATTN_SLOTS))
    kw = jnp.tile(kw_ref[...], (1, ATTN_SLOTS))
    for j in range(d // LANES):
        slab_ref[j] = h[:, j * LANES:(j + 1) * LANES].astype(F32)
    ad = N_GROUPS * GROUP_DIM
    for g, out_ref in enumerate((qkv0_ref, qkv1_ref, qkv2_ref)):
        hg = h if DILATIONS[g] == 1 else _to_dilated(slab_ref, tmp_ref, tm, DILATIONS[g]).astype(BF16)
        q, k, v = [jnp.dot(hg, w_ref[:, part * ad + g * GROUP_DIM:part * ad + (g + 1) * GROUP_DIM],
                           preferred_element_type=F32) for part in range(3)]
        qkv = jnp.concatenate([_head_rms(q, qw), _head_rms(k, kw), v], axis=-1)
        out_ref[...] = qkv.astype(out_ref.dtype).reshape(out_ref.shape)
        if g == 0:
            z = jnp.dot(hg, w_ref[:, 3 * ad:], preferred_element_type=F32)
            zs_ref[...] = _silu(z)


def _attn_proj(h, w_attn, q_norm_w, k_norm_w, layer):
    bsz, s, d = h.shape
    tm = ROW_TILE
    depth = q_norm_w.shape[0]
    head_w = pl.BlockSpec((None, 1, HEAD_DIM), lambda b, i: (layer, 0, 0))
    gw = 3 * GROUP_DIM
    out_shapes, out_specs = [], []
    for dil in DILATIONS:
        tile = Q_BLOCK * dil
        assert dil == 1 or tile % tm == 0
        if tile <= tm:
            nblk = tm // Q_BLOCK
            out_shapes.append(jax.ShapeDtypeStruct((bsz, s // Q_BLOCK, Q_BLOCK, gw), BF16))
            out_specs.append(pl.BlockSpec((None, nblk, Q_BLOCK, gw), lambda b, i: (b, i, 0, 0)))
        else:
            per = tm // dil
            steps = tile // tm
            out_shapes.append(jax.ShapeDtypeStruct((bsz, s // tile, dil, Q_BLOCK, gw), BF16))
            out_specs.append(pl.BlockSpec((None, None, dil, per, gw),
                                          lambda b, i, steps=steps: (b, i // steps, 0, i % steps, 0)))
    out_shapes.append(jax.ShapeDtypeStruct((bsz, s, GROUP_DIM), F32))
    out_specs.append(pl.BlockSpec((None, tm, GROUP_DIM), lambda b, i: (b, i, 0)))
    outs = pl.pallas_call(
        _attn_proj_kernel,
        grid=(bsz, s // tm),
        in_specs=[pl.BlockSpec((None, tm, d), lambda b, i: (b, i, 0)),
                  _resident(w_attn),
                  head_w, head_w],
        out_specs=out_specs,
        out_shape=out_shapes,
        scratch_shapes=[pltpu.VMEM((d // LANES, tm, LANES), F32)] * 2,
        compiler_params=_params(2),
        name="attn_proj",
    )(h, w_attn, q_norm_w.reshape(depth, 1, HEAD_DIM), k_norm_w.reshape(depth, 1, HEAD_DIM))
    qkv = [t.reshape(bsz, s // Q_BLOCK, Q_BLOCK, gw) for t in outs[:N_GROUPS]]
    return qkv, outs[N_GROUPS]


def _stat_lane(head):
    return head * LSE_LANES


def _attn_block(q, kp, kc, vp, vc, bias, o_ref, m_ref, l_ref, j):
    blk = q.shape[0]
    lane = lax.broadcasted_iota(jnp.int32, (blk, LANES), 1)
    lane_lo = lane < HEAD_DIM
    zero = jnp.zeros((blk, LANES), q.dtype)
    one = jnp.ones((2 * blk, LANES), q.dtype)
    for p in range(GROUP_DIM // LANES):
        sl = slice(p * LANES, (p + 1) * LANES)
        q2 = q[:, sl]
        qs = jnp.concatenate([jnp.where(lane_lo, q2, zero), jnp.where(lane_lo, zero, q2)], axis=0)
        k2 = jnp.concatenate([kp[:, sl], kc[:, sl]], axis=0)
        v2 = jnp.concatenate([vp[:, sl], vc[:, sl]], axis=0)
        s = lax.dot_general(qs, k2, (((1,), (1,)), ((), ())), preferred_element_type=F32) + bias
        m = jnp.max(s, axis=-1, keepdims=True)
        e = jnp.exp2(s - m).astype(BF16)
        res = jnp.dot(e, jnp.concatenate([v2, one], axis=-1), preferred_element_type=F32)
        o_ref[j + (slice(None), sl)] = jnp.where(lane_lo, res[:blk, :LANES], res[blk:, :LANES])
        for head, rows in ((2 * p, slice(0, blk)), (2 * p + 1, slice(blk, 2 * blk))):
            slot = slice(_stat_lane(head), _stat_lane(head) + LSE_LANES)
            m_ref[j + (slice(None), slot)] = jnp.broadcast_to(m[rows], (blk, LSE_LANES))
            l_ref[j + (slice(None), slot)] = res[rows, LANES + slot.start:LANES + slot.stop]


def _attn_kernel(q_ref, kprev_ref, k_ref, vprev_ref, v_ref, o_ref, m_ref, l_ref):
    n_blocks, n_res, blk, _ = q_ref.shape
    span = 2 * blk
    row = lax.broadcasted_iota(jnp.int32, (span, span), 0) % blk
    col = lax.broadcasted_iota(jnp.int32, (span, span), 1)
    in_window = (col >= row) & (col <= row + blk)
    bias = jnp.where(in_window, 0.0, NEG_INF)
    run_has_prev = pl.program_id(2) > 0
    bias_first = jnp.where(in_window & ((col >= blk) | run_has_prev), 0.0, NEG_INF)

    for r in range(n_res):
        _attn_block(q_ref.at[0, r], kprev_ref.at[r], k_ref.at[0, r], vprev_ref.at[r], v_ref.at[0, r],
                    bias_first, o_ref, m_ref, l_ref, (0, r))
        for j in range(1, n_blocks):
            _attn_block(q_ref.at[j, r], k_ref.at[j - 1, r], k_ref.at[j, r], v_ref.at[j - 1, r],
                        v_ref.at[j, r], bias, o_ref, m_ref, l_ref, (j, r))


def _dilated_attention(qkv, dil):
    bsz, nb, blk, width = qkv.shape
    gd = GROUP_DIM
    tiles = nb // dil
    run = min(ATTN_STEP_BLOCKS, tiles)
    res = min(ATTN_STEP_BLOCKS // run, dil)
    assert tiles % run == 0 and dil % res == 0
    view = qkv.reshape(bsz, tiles, dil, blk, width)
    cur = lambda part: pl.BlockSpec((None, run, res, blk, gd), lambda b, r, c: (b, c, r, 0, part))
    prev = lambda part: pl.BlockSpec((None, None, res, blk, gd),
                                     lambda b, r, c: (b, jnp.maximum(c * run - 1, 0), r, 0, part))
    stat_spec = pl.BlockSpec((None, run, res, blk, LANES), lambda b, r, c: (b, c, r, 0, 0))
    stat_shape = jax.ShapeDtypeStruct((bsz, tiles, dil, blk, LANES), F32)
    o, m, l = pl.pallas_call(
        _attn_kernel,
        grid=(bsz, dil // res, tiles // run),
        in_specs=[cur(0), prev(1), cur(1), prev(2), cur(2)],
        out_specs=[pl.BlockSpec((None, run, res, blk, gd), lambda b, r, c: (b, c, r, 0, 0)),
                   stat_spec, stat_spec],
        out_shape=[jax.ShapeDtypeStruct((bsz, tiles, dil, blk, gd), F32), stat_shape, stat_shape],
        compiler_params=_params(3),
        name=f"dilated_attn_d{dil}",
    )(view, view, view, view, view)
    return (o.reshape(bsz, nb, blk, gd), m.reshape(bsz, nb, blk, LANES),
            l.reshape(bsz, nb, blk, LANES))


def _to_sequence(src_ref, lane0, seq_ref, slab, tmp_ref, dil):
    per = src_ref.shape[1]
    src = lambda r: src_ref[r, :, lane0:lane0 + LANES]
    if dil <= PASS_STRIDE:
        for r in range(dil):
            seq_ref[slab, pl.ds(r, per, stride=dil), :] = src(r)
        return
    n_hi = dil // PASS_STRIDE
    assert n_hi <= PASS_STRIDE and dil % PASS_STRIDE == 0
    part = n_hi * per
    for r in range(dil):
        r_hi, r_lo = divmod(r, PASS_STRIDE)
        tmp_ref[slab, pl.ds(r_lo * part + r_hi, per, stride=n_hi), :] = src(r)
    for r_lo in range(PASS_STRIDE):
        seq_ref[slab, pl.ds(r_lo, part, stride=PASS_STRIDE), :] = tmp_ref[slab, r_lo * part:(r_lo + 1) * part, :]


def _merge_kernel(x_ref, mod_ref, h_ref, ya_ref, zs_ref,
                  o0_ref, o1_ref, o2_ref, m0_ref, m1_ref, m2_ref, l0_ref, l1_ref, l2_ref,
                  wg_ref, pa_ref, pb_ref, wo_ref, out_ref, seq_ref, tmp_ref):
    d = x_ref.shape[-1]
    n_tiles = GROUP_DIM // LANES
    h = h_ref[...]
    g_a = jax.nn.sigmoid(jnp.dot(h, wg_ref[:, :d], preferred_element_type=F32))
    g_b = jax.nn.sigmoid(jnp.dot(h, wg_ref[:, d:], preferred_element_type=F32))

    groups = ((o0_ref, m0_ref, l0_ref), (o1_ref, m1_ref, l1_ref), (o2_ref, m2_ref, l2_ref))
    for g in range(1, N_GROUPS):
        o_ref, m_ref, l_ref = groups[g]
        sources = [(o_ref, p * LANES) for p in range(n_tiles)] + [(m_ref, 0), (l_ref, 0)]
        for slab, (src_ref, lane0) in enumerate(sources):
            _to_sequence(src_ref, lane0, seq_ref.at[g - 1], slab, tmp_ref, DILATIONS[g])

    def out_tile(g, p):
        return groups[0][0][:, p * LANES:(p + 1) * LANES] if g == 0 else seq_ref[g - 1, p]

    def stat_tile(g, which):
        return groups[0][1 + which][...] if g == 0 else seq_ref[g - 1, n_tiles + which]

    maxes = [stat_tile(g, 0) for g in range(N_GROUPS)]
    sums = [stat_tile(g, 1) for g in range(N_GROUPS)]
    top = jnp.maximum(jnp.maximum(maxes[0], maxes[1]), maxes[2])
    wts = [jnp.exp2(t - top) for t in maxes]
    den = wts[0] * sums[0] + wts[1] * sums[1] + wts[2] * sums[2]
    wns = [w / den for w in wts]
    lane_lo = lax.broadcasted_iota(jnp.int32, (h.shape[0], LANES), 1) < HEAD_DIM
    tiles = []
    for p in range(n_tiles):
        attn = None
        for g, wn in enumerate(wns):
            wide = jnp.where(lane_lo, wn[:, _stat_lane(2 * p):_stat_lane(2 * p) + 1],
                             wn[:, _stat_lane(2 * p + 1):_stat_lane(2 * p + 1) + 1])
            term = wide * out_tile(g, p)
            attn = term if attn is None else attn + term
        tiles.append(attn * zs_ref[:, p * LANES:(p + 1) * LANES])
    y_b = jnp.concatenate(tiles, axis=-1).astype(BF16)

    merged = (g_a * jnp.dot(ya_ref[...], pa_ref[...], preferred_element_type=F32)
              + g_b * jnp.dot(y_b, pb_ref[...], preferred_element_type=F32))
    upd = jnp.dot(merged.astype(BF16), wo_ref[...], preferred_element_type=F32)
    out_ref[...] = x_ref[...] + mod_ref[2:3, :] * upd


def _merge(x, mod3, h, y_a, zs, outs, maxes, sums, w_gate, p_a, p_b, w_o):
    bsz, s, d = x.shape
    tm = OUT_ROW_TILE
    row_d = pl.BlockSpec((None, tm, d), lambda b, i: (b, i, 0))
    row_g = pl.BlockSpec((None, tm, GROUP_DIM), lambda b, i: (b, i, 0))

    def dilated(t, dil):
        width = t.shape[-1]
        tile = Q_BLOCK * dil
        per = tm // dil
        if tile % tm:
            raise NotImplementedError("a dilated-order tile must be a whole number of merge tiles")
        steps = tile // tm
        view = t.reshape(bsz, s // tile, dil, Q_BLOCK, width)
        spec = pl.BlockSpec((None, None, dil, per, width),
                            lambda b, i: (b, i // steps, 0, i % steps, 0))
        return view, spec

    views, specs = [], []
    for group in (outs, maxes, sums):
        for t, dil in zip(group, DILATIONS):
            width = t.shape[-1]
            if dil == 1:
                views.append(t.reshape(bsz, s, width))
                specs.append(pl.BlockSpec((None, tm, width), lambda b, i: (b, i, 0)))
            else:
                v, sp = dilated(t, dil)
                views.append(v)
                specs.append(sp)

    return pl.pallas_call(
        _merge_kernel,
        grid=(bsz, s // tm),
        in_specs=[row_d, pl.BlockSpec((None, 3, d), lambda b, i: (b, 0, 0)), row_d, row_d, row_g,
                  *specs,
                  _resident(w_gate), _resident(p_a), _resident(p_b), _resident(w_o)],
        out_specs=row_d,
        out_shape=jax.ShapeDtypeStruct((bsz, s, d), F32),
        scratch_shapes=[pltpu.VMEM((N_GROUPS - 1, GROUP_DIM // LANES + 2, tm, LANES), F32),
                        pltpu.VMEM((GROUP_DIM // LANES + 2, tm, LANES), F32)],
        compiler_params=_params(2),
        name="merge_out",
    )(x, mod3, h, y_a, zs, *views, w_gate, p_a, p_b, w_o)


def _layer(x, c, layer, w_ada, b_ada, norm_w, w_in, conv_w, q_norm_w, k_norm_w, w_br_conv, w_br_attn, w_out):
    bsz, s, d = x.shape
    conv_dim = conv_w.shape[-1]
    attn_dim = N_GROUPS * GROUP_DIM
    assert conv_dim == d and conv_w.shape[1] == CONV_WIDTH and d % LANES == 0 and d % STAGE_ROWS == 0
    assert q_norm_w.shape[1:] == k_norm_w.shape[1:] == (HEAD_DIM,)
    assert w_in.shape[2] == 4 * conv_dim + 3 * attn_dim + GROUP_DIM + 2 * d
    assert s % (max(DILATIONS) * Q_BLOCK) == 0 and s % ROW_TILE == 0 and s % MIXER_ROW_TILE == 0
    assert all(w // dil == Q_BLOCK for w, dil in zip(WINDOWS, DILATIONS))

    mod3 = _modulation(c, w_ada, b_ada)
    nw = norm_w.reshape(1, d)

    y_a, h, w_attn, w_gate, p_a, p_b, w_o = _mixer_a(x, mod3, nw, w_in, layer, conv_w,
                                                     w_br_conv, w_br_attn, w_out)
    qkv, zs = _attn_proj(h, w_attn, q_norm_w, k_norm_w, layer)
    outs, maxes, sums = zip(*[_dilated_attention(t, dil) for t, dil in zip(qkv, DILATIONS)])
    return _merge(x, mod3, h, y_a, zs, outs, maxes, sums, w_gate, p_a, p_b, w_o)


@jax.jit
def kernel(x, c, w_ada, b_ada, norm_w, w_in, conv_w, q_norm_w, k_norm_w, w_br_conv, w_br_attn, w_out):
    depth = w_ada.shape[0]
    for l in range(depth):
        x = _layer(x, c, l, w_ada[l], b_ada[l], norm_w[l], w_in, conv_w, q_norm_w, k_norm_w,
                   w_br_conv, w_br_attn, w_out)
    return x
```
